```python
import math
import jax
import jax.numpy as jnp
from jax import lax
import numpy as np

D_MODEL = 2048
BATCH = 4
SEQ = 2048
DEPTH = 4
DEC_BATCH = 8
DEC_SEQ = 4
PAST_LEN = 16384
PAGE_SIZE = 128

HEAD_DIM = 64
GROUP_W = D_MODEL // 4
A_HEADS = GROUP_W // (2 * HEAD_DIM)
A_VD = 2 * HEAD_DIM
B_WIDTH = GROUP_W
B_BLOCKS = GROUP_W // HEAD_DIM
B_BD = B_WIDTH // B_BLOCKS
CONV_W = 4
LRU_C = 8.0
C_HEADS = 4
C_DK = GROUP_W // C_HEADS
C_DV = GROUP_W // C_HEADS
C_CHUNK = 64
D_HEADS = GROUP_W // HEAD_DIM
D_KV_HEADS = 2
D_GROUP = D_HEADS // D_KV_HEADS
IDX_HEADS = 8
IDX_DIM = 64
TOPK_MAX = 256
N_BUCKETS = 32
MAX_DIST = 128
N_BIAS = 2 * A_HEADS + D_HEADS
D_FF = 4 * D_MODEL
Q_BLOCK = 128
EPS = 1e-6
IN_SIZES = (A_HEADS * 2 * HEAD_DIM, A_HEADS * 2 * HEAD_DIM, A_HEADS * A_VD,
            B_WIDTH, B_WIDTH,
            C_HEADS * C_DK, C_HEADS * C_DK, C_HEADS * C_DV, GROUP_W,
            D_HEADS * HEAD_DIM, D_KV_HEADS * HEAD_DIM, D_KV_HEADS * HEAD_DIM,
            IDX_HEADS * IDX_DIM, IDX_DIM, IDX_HEADS)
IN_WIDTH = sum(IN_SIZES)

kernel_name = 'hymba_diff_rglru_hgrn2_dsa_step'


def _rms(x, g):
    xf = x.astype(jnp.float32)
    y = xf * lax.rsqrt(jnp.mean(xf * xf, axis=-1, keepdims=True) + EPS)
    return (y * g.astype(jnp.float32)).astype(x.dtype)


def _rel_bucket(dist):
    n = jnp.maximum(dist, 0)
    exact = N_BUCKETS // 2
    large = exact + (jnp.log(jnp.maximum(n, 1).astype(jnp.float32) / exact)
                     / math.log(MAX_DIST / exact) * (N_BUCKETS - exact)).astype(jnp.int32)
    return jnp.where(n < exact, n, jnp.minimum(large, N_BUCKETS - 1))


def _mixer_inputs(x, ln1, w_in, a_qn, a_kn, d_qn, d_kn):
    B, T, _ = x.shape
    p = _rms(x, ln1) @ w_in
    splits = np.cumsum(IN_SIZES)[:-1].tolist()
    aq, ak, av, bx, bg, cq, cf, ci, cg, dq, dk, dv, iq, ik, iw = jnp.split(p, splits, axis=-1)
    aq = _rms(aq.reshape(B, T, A_HEADS, 2, HEAD_DIM), a_qn)
    ak = _rms(ak.reshape(B, T, A_HEADS, 2, HEAD_DIM), a_kn)
    av = av.reshape(B, T, A_HEADS, A_VD)
    cq = cq.reshape(B, T, C_HEADS, C_DK)
    cf = cf.reshape(B, T, C_HEADS, C_DK)
    ci = ci.reshape(B, T, C_HEADS, C_DV)
    cg = cg.reshape(B, T, C_HEADS, C_DV)
    dq = _rms(dq.reshape(B, T, D_KV_HEADS, D_GROUP, HEAD_DIM), d_qn)
    dk = _rms(dk.reshape(B, T, D_KV_HEADS, HEAD_DIM), d_kn)
    dv = dv.reshape(B, T, D_KV_HEADS, HEAD_DIM)
    iq = iq.reshape(B, T, IDX_HEADS, IDX_DIM)
    return aq, ak, av, bx, bg, cq, cf, ci, cg, dq, dk, dv, iq, ik, iw


def _diff_core(q, k, v, q_pos, k_pos, tab_a, lam):
    logits = jnp.einsum('bqhmd,bshmd->bhmqs', q, k).astype(jnp.float32) * HEAD_DIM ** -0.5
    dist = q_pos[:, None] - k_pos[None, :]
    bias = tab_a[_rel_bucket(dist)]
    logits = logits + jnp.transpose(bias, (2, 3, 0, 1))
    logits = jnp.where(dist >= 0, logits, -jnp.inf)
    p = jax.nn.softmax(logits, axis=-1)
    w = p[:, :, 0] - lam * p[:, :, 1]
    return jnp.einsum('bhqs,bshv->bqhv', w.astype(v.dtype), v)


def _dsa_select(iq, iw, ik, q_pos, k_pos, n_top):
    s = jnp.einsum('bqhd,bsd->bqhs', iq, ik).astype(jnp.float32) * IDX_DIM ** -0.5
    score = jnp.einsum('bqh,bqhs->bqs', iw.astype(jnp.float32) * IDX_HEADS ** -0.5, jax.nn.relu(s))
    score = jnp.where(k_pos[None, None, :] <= q_pos[None, :, None], score, -jnp.inf)
    _, sel = lax.top_k(score, n_top)
    return sel


def _dsa_attend(q, k_sel, v_sel, q_pos, sel, tab_d):
    logits = jnp.einsum('bqhgd,bqkhd->bqhgk', q, k_sel).astype(jnp.float32) * HEAD_DIM ** -0.5
    dist = q_pos[None, :, None] - sel
    bias = tab_d[_rel_bucket(dist)]
    logits = logits + jnp.transpose(bias, (0, 1, 3, 4, 2))
    logits = jnp.where((dist >= 0)[:, :, None, None, :], logits, -jnp.inf)
    p = jax.nn.softmax(logits, axis=-1)
    return jnp.einsum('bqhgk,bqkhd->bqhgd', p.astype(v_sel.dtype), v_sel)


def _rglru(x, pos, h0, buf0, conv_w, conv_b, wa, ba, wx, bx, lam):
    f32 = jnp.float32
    B, T, W = x.shape
    xp = jnp.concatenate([buf0.astype(x.dtype), x], axis=1).astype(f32)
    xc = conv_b.astype(f32) + sum(xp[:, j:j + T] * conv_w[j].astype(f32) for j in range(CONV_W))
    new_buf = xp[:, T:]
    xb = xc.reshape(B, T, B_BLOCKS, B_BD)
    r = jax.nn.sigmoid(jnp.einsum('btnd,nde->btne', xb, wa.astype(f32)).reshape(B, T, W) + ba.astype(f32))
    i = jax.nn.sigmoid(jnp.einsum('btnd,nde->btne', xb, wx.astype(f32)).reshape(B, T, W) + bx.astype(f32))
    log_a = -LRU_C * r * jax.nn.softplus(-lam.astype(f32))
    a = jnp.exp(log_a)
    mult = jnp.where((pos == 0)[None, :, None], 1.0, jnp.sqrt(-jnp.expm1(2.0 * log_a)))
    b = mult * (i * xc)
    b = b.at[:, 0].add(a[:, 0] * h0.astype(f32))

    def comb(lhs, rhs):
        return (lhs[0] * rhs[0], rhs[0] * lhs[1] + rhs[1])

    _, h = lax.associative_scan(comb, (a, b), axis=1)
    return h, h[:, -1], new_buf


def _hgrn2(q, f_pre, v, s0, lb):
    f32 = jnp.float32
    q = jax.nn.silu(q.astype(f32))
    v = v.astype(f32)
    log_f = jnp.logaddexp(jnp.log(lb), jnp.log1p(-lb) + jax.nn.log_sigmoid(f_pre.astype(f32)))
    k = -jnp.expm1(log_f)
    B, T, H, DK = q.shape
    DV = v.shape[-1]
    C = math.gcd(T, C_CHUNK)
    nc = T // C

    def chunks(z):
        return z.reshape(B, nc, C, H, z.shape[-1]).transpose(1, 0, 3, 2, 4)

    causal = jnp.tril(jnp.ones((C, C), dtype=bool))[:, :, None]

    def step(s, inp):
        qc, kc, vc, gc = inp
        G = jnp.cumsum(gc, axis=2)
        o = jnp.einsum('bhtk,bhkv->bhtv', qc * jnp.exp(G), s)
        decay = jnp.exp(jnp.where(causal, G[:, :, :, None, :] - G[:, :, None, :, :], -jnp.inf))
        att = jnp.einsum('bhtk,bhtsk,bhsk->bhts', qc, decay, kc)
        o = o + jnp.einsum('bhts,bhsv->bhtv', att, vc)
        g_last = G[:, :, -1:, :]
        s = jnp.exp(g_last[:, :, 0, :])[..., None] * s + jnp.einsum('bhsk,bhsv->bhkv', kc * jnp.exp(g_last - G), vc)
        return s, o

    s_fin, o = lax.scan(step, s0.astype(f32), (chunks(q), chunks(k), chunks(v), chunks(log_f)))
    o = o.transpose(1, 0, 3, 2, 4).reshape(B, T, H, DV)
    return o, s_fin


def setup_inputs(seed: int = 0) -> dict:
    key = jax.random.key(seed)
    ks = iter(jax.random.split(key, 48))
    f32 = jnp.float32

    def nrm(shape, scale):
        return jax.random.normal(next(ks), shape, f32) * scale

    n_pages = PAST_LEN // PAGE_SIZE
    n_pool = (DEC_BATCH * n_pages * 5) // 4
    perm = jax.random.permutation(next(ks), n_pool)
    page_table = perm[:DEC_BATCH * n_pages].reshape(DEC_BATCH, n_pages).astype(jnp.int32)
    u = jax.random.uniform(next(ks), (DEPTH, B_WIDTH), f32, 0.9, 0.999)
    a = u ** (1.0 / LRU_C)
    b_lambda = jnp.log(a) - jnp.log1p(-a)
    return {
        'x_prompt': nrm((BATCH, SEQ, D_MODEL), 1.0),
        'x_sample': nrm((DEC_BATCH, DEC_SEQ, D_MODEL), 1.0),
        'cache_a_k': nrm((n_pool, DEPTH, PAGE_SIZE, A_HEADS, 2 * HEAD_DIM), 1.0),
        'cache_a_v': nrm((n_pool, DEPTH, PAGE_SIZE, A_HEADS, A_VD), 1.0),
        'cache_d_k': nrm((n_pool, DEPTH, PAGE_SIZE, D_KV_HEADS, HEAD_DIM), 1.0),
        'cache_d_v': nrm((n_pool, DEPTH, PAGE_SIZE, D_KV_HEADS, HEAD_DIM), 1.0),
        'cache_d_idx': nrm((n_pool, DEPTH, PAGE_SIZE, IDX_DIM), 1.0),
        'state_b_h': nrm((DEC_BATCH, DEPTH, B_WIDTH), 0.5),
        'state_b_conv': nrm((DEC_BATCH, DEPTH, CONV_W - 1, B_WIDTH), 1.0),
        'state_c_s': nrm((DEC_BATCH, DEPTH, C_HEADS, C_DK, C_DV), 0.5),
        'page_table': page_table,
        'rel_bias': nrm((N_BUCKETS, N_BIAS), 0.5),
        'ln1': 1.0 + nrm((DEPTH, D_MODEL), 0.1),
        'w_in': nrm((DEPTH, D_MODEL, IN_WIDTH), D_MODEL ** -0.5),
        'a_q_norm': 1.0 + nrm((DEPTH, HEAD_DIM), 0.1),
        'a_k_norm': 1.0 + nrm((DEPTH, HEAD_DIM), 0.1),
        'a_lam_q1': nrm((DEPTH, HEAD_DIM), 0.1),
        'a_lam_k1': nrm((DEPTH, HEAD_DIM), 0.1),
        'a_lam_q2': nrm((DEPTH, HEAD_DIM), 0.1),
        'a_lam_k2': nrm((DEPTH, HEAD_DIM), 0.1),
        'a_out_norm': 1.0 + nrm((DEPTH, A_VD), 0.1),
        'b_conv_w': nrm((DEPTH, CONV_W, B_WIDTH), CONV_W ** -0.5),
        'b_conv_b': nrm((DEPTH, B_WIDTH), 0.02),
        'b_wa': nrm((DEPTH, B_BLOCKS, B_BD, B_BD), B_BD ** -0.5),
        'b_ba': nrm((DEPTH, B_WIDTH), 0.1),
        'b_wx': nrm((DEPTH, B_BLOCKS, B_BD, B_BD), B_BD ** -0.5),
        'b_bx': nrm((DEPTH, B_WIDTH), 0.1),
        'b_lambda': b_lambda,
        'c_lb_logits': nrm((DEPTH, C_HEADS * C_DK), 0.5),
        'c_out_norm': 1.0 + nrm((DEPTH, C_DV), 0.1),
        'd_q_norm': 1.0 + nrm((DEPTH, HEAD_DIM), 0.1),
        'd_k_norm': 1.0 + nrm((DEPTH, HEAD_DIM), 0.1),
        'w_out': nrm((DEPTH, D_MODEL, D_MODEL), D_MODEL ** -0.5),
        'ln2': 1.0 + nrm((DEPTH, D_MODEL), 0.1),
        'w_up': nrm((DEPTH, D_MODEL, D_FF), D_MODEL ** -0.5),
        'w_down': nrm((DEPTH, D_FF, D_MODEL), D_FF ** -0.5),
    }


def reference(x_prompt, x_sample, cache_a_k, cache_a_v, cache_d_k, cache_d_v, cache_d_idx,
              state_b_h, state_b_conv, state_c_s, page_table, rel_bias, ln1, w_in,
              a_q_norm, a_k_norm, a_lam_q1, a_lam_k1, a_lam_q2, a_lam_k2, a_out_norm,
              b_conv_w, b_conv_b, b_wa, b_ba, b_wx, b_bx, b_lambda, c_lb_logits, c_out_norm,
              d_q_norm, d_k_norm, w_out, ln2, w_up, w_down):
    f32 = jnp.float32
    past = page_table.shape[1] * PAGE_SIZE
    tab = rel_bias.astype(f32)
    tab_a = tab[:, :2 * A_HEADS].reshape(N_BUCKETS, A_HEADS, 2)
    tab_d = tab[:, 2 * A_HEADS:].reshape(N_BUCKETS, D_KV_HEADS, D_GROUP)
    lb_cum = jnp.cumsum(jax.nn.softmax(c_lb_logits.astype(f32), axis=0), axis=0)
    lb_all = lb_cum - lb_cum[0]

    def layer(x, l, pos, sample):
        B, T, _ = x.shape
        aq, ak, av, bx, bg, cq, cf, ci, cg, dq, dk, dv, iq, ik, iw = _mixer_inputs(
            x, ln1[l], w_in[l], a_q_norm[l], a_k_norm[l], d_q_norm[l], d_k_norm[l])
        lam_init = 0.8 - 0.6 * math.exp(-0.3 * l)
        lam = (jnp.exp(jnp.sum(a_lam_q1[l].astype(f32) * a_lam_k1[l].astype(f32)))
               - jnp.exp(jnp.sum(a_lam_q2[l].astype(f32) * a_lam_k2[l].astype(f32))) + lam_init)
        bidx = jnp.arange(B)[:, None, None]
        if sample:
            k_pos = jnp.arange(past + T)
            ak_all = jnp.concatenate([cache_a_k[page_table, l].reshape(B, past, A_HEADS, 2, HEAD_DIM).astype(ak.dtype), ak], axis=1)
            av_all = jnp.concatenate([cache_a_v[page_table, l].reshape(B, past, A_HEADS, A_VD).astype(av.dtype), av], axis=1)
            o_a = _diff_core(aq, ak_all, av_all, pos, k_pos, tab_a, lam)
            ik_all = jnp.concatenate([cache_d_idx[page_table, l].reshape(B, past, IDX_DIM).astype(ik.dtype), ik], axis=1)
            n_top = min(TOPK_MAX, (past + T) // 4)
            sel = _dsa_select(iq, iw, ik_all, pos, k_pos, n_top)
            in_past = (sel < past)[..., None, None]
            sp = jnp.minimum(sel, past - 1)
            phys = page_table[bidx, sp // PAGE_SIZE]
            off = sp % PAGE_SIZE
            sn = jnp.clip(sel - past, 0, T - 1)
            k_sel = jnp.where(in_past, cache_d_k[phys, l, off].astype(dk.dtype), dk[bidx, sn])
            v_sel = jnp.where(in_past, cache_d_v[phys, l, off].astype(dv.dtype), dv[bidx, sn])
            o_d = _dsa_attend(dq, k_sel, v_sel, pos, sel, tab_d)
            h0, buf0, s0 = state_b_h[:, l], state_b_conv[:, l], state_c_s[:, l]
        else:
            nb = T // Q_BLOCK

            def blocks(z):
                return z.reshape(B, nb, Q_BLOCK, *z.shape[2:]).swapaxes(0, 1)

            def unblocks(z):
                return z.swapaxes(0, 1).reshape(B, T, *z.shape[3:])

            pos_b = pos.reshape(nb, Q_BLOCK)
            o_a = unblocks(lax.map(lambda a: _diff_core(a[0], ak, av, a[1], pos, tab_a, lam), (blocks(aq), pos_b)))
            n_top = min(TOPK_MAX, T // 4)

            def dsa_blk(a):
                q_b, iq_b, iw_b, qp = a
                sel_b = _dsa_select(iq_b, iw_b, ik, qp, pos, n_top)
                return _dsa_attend(q_b, dk[bidx, sel_b], dv[bidx, sel_b], qp, sel_b, tab_d)

            o_d = unblocks(lax.map(dsa_blk, (blocks(dq), blocks(iq), blocks(iw), pos_b)))
            h0 = jnp.zeros((B, B_WIDTH), f32)
            buf0 = jnp.zeros((B, CONV_W - 1, B_WIDTH), x.dtype)
            s0 = jnp.zeros((B, C_HEADS, C_DK, C_DV), f32)
        o_a = (_rms(o_a, a_out_norm[l]) * (1.0 - lam_init)).reshape(B, T, GROUP_W)
        h, h_fin, buf = _rglru(bx, pos, h0, buf0, b_conv_w[l], b_conv_b[l], b_wa[l], b_ba[l], b_wx[l], b_bx[l], b_lambda[l])
        o_b = h * jax.nn.gelu(bg.astype(f32))
        o_c, s_fin = _hgrn2(cq, cf, ci, s0, lb_all[l].reshape(C_HEADS, C_DK))
        o_c = (_rms(o_c, c_out_norm[l]) * jax.nn.silu(cg.astype(f32))).reshape(B, T, GROUP_W)
        o_d = o_d.reshape(B, T, GROUP_W)
        mix = jnp.concatenate([o_a.astype(x.dtype), o_b.astype(x.dtype), o_c.astype(x.dtype), o_d.astype(x.dtype)], axis=-1)
        x = x + mix @ w_out[l]
        x = x + jnp.square(jax.nn.relu(_rms(x, ln2[l]) @ w_up[l])) @ w_down[l]
        new = (ak.reshape(B, T, A_HEADS, 2 * HEAD_DIM), av, dk, dv, ik,
               h_fin.astype(x.dtype), buf.astype(x.dtype), s_fin.astype(x.dtype))
        return x, new

    def run(x, pos, sample):
        news = []
        for l in range(DEPTH):
            x, new = layer(x, l, pos, sample)
            news.append(new)
        return x, [jnp.stack(s, axis=1) for s in zip(*news)]

    y_prompt, (p_a_k, p_a_v, p_d_k, p_d_v, p_d_idx, p_b_h, p_b_conv, p_c_s) = run(
        x_prompt, jnp.arange(x_prompt.shape[1]), False)
    y_sample, (s_a_k, s_a_v, s_d_k, s_d_v, s_d_idx, s_b_h, s_b_conv, s_c_s) = run(
        x_sample, past + jnp.arange(x_sample.shape[1]), True)
    return (y_prompt, y_sample, p_a_k, p_a_v, p_d_k, p_d_v, p_d_idx, p_b_h, p_b_conv, p_c_s,
            s_a_k, s_a_v, s_d_k, s_d_v, s_d_idx, s_b_h, s_b_conv, s_c_s)
```

```python
import functools
import math

import jax
import jax.numpy as jnp
import numpy as np
from jax import lax
from jax.experimental import pallas as pl
from jax.experimental.pallas import tpu as pltpu

F32 = jnp.float32
BF16 = jnp.bfloat16
I32 = jnp.int32

EPS = 1e-6
HEAD_DIM = 64
PAGE = 128
CONV_W = 4
LRU_C = 8.0
IDX_HEADS = 8
TOPK_MAX = 256
N_BUCKETS = 32
MAX_DIST = 128
LANE = 128
SUBLANE = 8
T_PAD = 8
INT_MIN = -(2 ** 31)
KEY_NEG_INF = -2139095041
VMEM_LIMIT = 56 * 1024 * 1024
NEG_INF = float("-inf")

C_AQ, C_AK, C_AV, C_BX, C_BG, C_CQ, C_CF, C_CI, C_CG, C_DQ, C_IQ = (4 * i for i in range(11))
C_DK, C_DV, C_TAIL = 44, 45, 46
P_WIDTH = 48 * LANE
IW_OFF = 64


def _cparams(sem):
    return pltpu.CompilerParams(dimension_semantics=sem, vmem_limit_bytes=VMEM_LIMIT)


def _nt(a, b):
    return lax.dot_general(a, b, (((1,), (1,)), ((), ())), preferred_element_type=F32)


def _tn(a, b):
    return lax.dot_general(a, b, (((0,), (0,)), ((), ())), preferred_element_type=F32)


def _rel_bucket(dist):
    n = jnp.maximum(dist, 0)
    exact = N_BUCKETS // 2
    large = exact + (jnp.log(jnp.maximum(n, 1).astype(F32) / exact)
                     / math.log(MAX_DIST / exact) * (N_BUCKETS - exact)).astype(I32)
    return jnp.where(n < exact, n, jnp.minimum(large, N_BUCKETS - 1))


def _in_proj_body(x_ref, g_ref, w_ref, gain_ref, flag_ref, seg_ref, o_ref, xn_ref, *, tn, norm_tiles):
    j = pl.program_id(1)

    @pl.when(j == 0)
    def _():
        x = x_ref[...]
        ms = jnp.mean(x * x, axis=-1, keepdims=True)
        xn_ref[...] = (x * lax.rsqrt(ms + EPS) * g_ref[...]).astype(BF16)

    y = jnp.dot(xn_ref[...], w_ref[...], preferred_element_type=F32)
    is_norm = functools.reduce(jnp.logical_or, [j == t for t in norm_tiles])

    @pl.when(is_norm)
    def _():
        seg = seg_ref[...]
        for c in range(tn // LANE):
            sl = slice(c * LANE, (c + 1) * LANE)
            yc = y[:, sl]
            y2 = yc * yc
            hi = y2.astype(BF16)
            lo = (y2 - hi.astype(F32)).astype(BF16)
            ms = (jnp.dot(hi, seg, preferred_element_type=F32)
                  + jnp.dot(lo, seg, preferred_element_type=F32))
            yn = yc * lax.rsqrt(ms + EPS) * gain_ref[:, sl]
            o_ref[:, sl] = jnp.where(flag_ref[:, sl] > 0, yn, yc)

    @pl.when(jnp.logical_not(is_norm))
    def _():
        o_ref[...] = y


def _in_proj(x, g, w, gain, flag, seg, *, tm, tn, norm_tiles):
    M, K = x.shape
    N = w.shape[1]
    return pl.pallas_call(
        functools.partial(_in_proj_body, tn=tn, norm_tiles=norm_tiles),
        grid=(M // tm, N // tn),
        in_specs=[pl.BlockSpec((tm, K), lambda i, j: (i, 0)),
                  pl.BlockSpec((1, K), lambda i, j: (0, 0)),
                  pl.BlockSpec((K, tn), lambda i, j: (0, j)),
                  pl.BlockSpec((1, tn), lambda i, j: (0, j)),
                  pl.BlockSpec((1, tn), lambda i, j: (0, j)),
                  pl.BlockSpec((LANE, LANE), lambda i, j: (0, 0))],
        out_specs=pl.BlockSpec((tm, tn), lambda i, j: (i, j)),
        out_shape=jax.ShapeDtypeStruct((M, N), F32),
        scratch_shapes=[pltpu.VMEM((tm, K), BF16)],
        compiler_params=_cparams(("parallel", "arbitrary")),
        name="in_proj")(x, g, w, gain, flag, seg)


def _out_proj_body(a_ref, b_ref, c_ref, d_ref, w_ref, x_ref, o_ref, *, gw):
    acc = x_ref[...]
    for g, m_ref in enumerate((a_ref, b_ref, c_ref, d_ref)):
        acc = acc + jnp.dot(m_ref[...].astype(BF16), w_ref[g * gw:(g + 1) * gw, :],
                            preferred_element_type=F32)
    o_ref[...] = acc


def _out_proj(mixes, w, x, *, tm, tn):
    M, D = x.shape
    gw = mixes[0].shape[1]
    mix_spec = pl.BlockSpec((tm, gw), lambda i, j: (i, 0))
    return pl.pallas_call(
        functools.partial(_out_proj_body, gw=gw),
        grid=(M // tm, D // tn),
        in_specs=[mix_spec, mix_spec, mix_spec, mix_spec,
                  pl.BlockSpec((w.shape[0], tn), lambda i, j: (0, j)),
                  pl.BlockSpec((tm, tn), lambda i, j: (i, j))],
        out_specs=pl.BlockSpec((tm, tn), lambda i, j: (i, j)),
        out_shape=jax.ShapeDtypeStruct((M, D), F32),
        compiler_params=_cparams(("parallel", "arbitrary")),
        name="out_proj")(*mixes, w, x)


def _ffn_body(x_ref, g_ref, wu_ref, wd_ref, o_ref, xn_ref):
    f = pl.program_id(1)

    @pl.when(f == 0)
    def _():
        x = x_ref[...]
        ms = jnp.mean(x * x, axis=-1, keepdims=True)
        xn_ref[...] = (x * lax.rsqrt(ms + EPS) * g_ref[...]).astype(BF16)
        o_ref[...] = x

    h = jnp.dot(xn_ref[...], wu_ref[...], preferred_element_type=F32)
    h = jnp.maximum(h, 0.0)
    h = (h * h).astype(BF16)
    o_ref[...] += jnp.dot(h, wd_ref[...], preferred_element_type=F32)


def _ffn(x, g, wu, wd, *, tm, tf):
    M, D = x.shape
    Fd = wu.shape[1]
    return pl.pallas_call(
        _ffn_body,
        grid=(M // tm, Fd // tf),
        in_specs=[pl.BlockSpec((tm, D), lambda i, f: (i, 0)),
                  pl.BlockSpec((1, D), lambda i, f: (0, 0)),
                  pl.BlockSpec((D, tf), lambda i, f: (0, f)),
                  pl.BlockSpec((tf, D), lambda i, f: (f, 0))],
        out_specs=pl.BlockSpec((tm, D), lambda i, f: (i, 0)),
        out_shape=jax.ShapeDtypeStruct((M, D), F32),
        scratch_shapes=[pltpu.VMEM((tm, D), BF16)],
        compiler_params=_cparams(("parallel", "arbitrary")),
        name="ffn")(x, g, wu, wd)


def _softmax_update(state, s, v):
    m, l, acc = state
    m_new = jnp.maximum(m, jnp.max(s, axis=-1, keepdims=True))
    alpha = jnp.exp(m - m_new)
    p = jnp.exp(s - m_new)
    l = alpha * l + jnp.sum(p, axis=-1, keepdims=True)
    acc = alpha * acc + jnp.dot(p.astype(BF16), v, preferred_element_type=F32)
    return m_new, l, acc


def _rms_rows(o, gain):
    ms = jnp.mean(o * o, axis=-1, keepdims=True)
    return o * lax.rsqrt(ms + EPS) * gain


def _attn_a_body(scal_ref, q_ref, k_ref, v_ref, bias_ref, gout_ref, o_ref, *, tq):
    qi = pl.program_id(2)
    q = q_ref[...]
    lane = lax.broadcasted_iota(I32, q.shape, 1)
    qz = (jnp.where(lane < HEAD_DIM, q, 0.0).astype(BF16),
          jnp.where(lane >= HEAD_DIM, q, 0.0).astype(BF16))
    vd = v_ref.shape[1]

    def load(j):
        r0 = pl.multiple_of(j * tq, tq)
        return k_ref[pl.ds(r0, tq), :].astype(BF16), v_ref[pl.ds(r0, tq), :].astype(BF16)

    row = lax.broadcasted_iota(I32, (tq, tq), 0)
    col = lax.broadcasted_iota(I32, (tq, tq), 1)
    init = (jnp.full((tq, 1), NEG_INF, F32), jnp.zeros((tq, 1), F32), jnp.zeros((tq, vd), F32))

    k, v = load(qi)
    st = []
    for m in range(2):
        s = _nt(qz[m], k) + bias_ref[0, m, 0]
        s = jnp.where(col <= row, s, NEG_INF)
        st.append(_softmax_update(init, s, v))
    has_sub = qi >= 1
    k, v = load(jnp.maximum(qi - 1, 0))
    for m in range(2):
        s = _nt(qz[m], k) + bias_ref[0, m, 1]
        s = jnp.where(has_sub, s, NEG_INF)
        st[m] = _softmax_update(st[m], s, v)

    def far(j, carry):
        k, v = load(j)
        return tuple(_softmax_update(carry[m], _nt(qz[m], k), v) for m in range(2))

    st = lax.fori_loop(0, jnp.maximum(qi - 1, 0), far, tuple(st))
    lam = scal_ref[0]
    o = st[0][2] / st[0][1] - lam * (st[1][2] / st[1][1])
    o_ref[...] = (_rms_rows(o, gout_ref[...]) * scal_ref[1]).astype(o_ref.dtype)


def _attn_a_prompt(P, scal, bias, gout, *, B, T, tq):
    H = bias.shape[0]
    nq = T // tq
    return pl.pallas_call(
        functools.partial(_attn_a_body, tq=tq),
        grid=(B, H, nq),
        in_specs=[pl.BlockSpec(memory_space=pltpu.SMEM),
                  pl.BlockSpec((tq, LANE), lambda b, h, i: (b * nq + i, C_AQ + h)),
                  pl.BlockSpec((T, LANE), lambda b, h, i: (b, C_AK + h)),
                  pl.BlockSpec((T, LANE), lambda b, h, i: (b, C_AV + h)),
                  pl.BlockSpec((1, 2, 2, tq, tq), lambda b, h, i: (h, 0, 0, 0, 0)),
                  pl.BlockSpec((1, LANE), lambda b, h, i: (0, 0))],
        out_specs=pl.BlockSpec((tq, LANE), lambda b, h, i: (b * nq + i, h)),
        out_shape=jax.ShapeDtypeStruct((B * T, H * LANE), BF16),
        compiler_params=_cparams(("parallel", "parallel", "arbitrary")),
        name="attn_a_prompt")(scal, P, P, P, bias, gout)


def _rglru_body(*refs, tt, t_real, n_t, has_state, pos0_is_zero):
    if has_state:
        (x_ref, g_ref, h0_ref, buf0_ref, cw_ref, cb_ref, wa_ref, ba_ref, wx_ref, bxb_ref, lam_ref,
         o_ref, hfin_ref, buf_ref, xpad_ref, a_ref, b_ref, hs_ref, hc_ref) = refs
    else:
        (x_ref, g_ref, cw_ref, cb_ref, wa_ref, ba_ref, wx_ref, bxb_ref, lam_ref,
         o_ref, hfin_ref, buf_ref, xpad_ref, a_ref, b_ref, hs_ref, hc_ref) = refs
    ti = pl.program_id(1)
    W = x_ref.shape[1]

    @pl.when(ti == 0)
    def _():
        if has_state:
            xpad_ref[0:SUBLANE, :] = buf0_ref[0]
            hc_ref[...] = jnp.broadcast_to(h0_ref[0], (SUBLANE, W))
        else:
            xpad_ref[0:SUBLANE, :] = jnp.zeros((SUBLANE, W), F32)
            hc_ref[...] = jnp.zeros((SUBLANE, W), F32)

    x = x_ref[...]
    xpad_ref[SUBLANE:SUBLANE + tt, :] = x
    xc = cb_ref[...] + x * cw_ref[CONV_W - 1:CONV_W, :]
    for j in range(CONV_W - 1):
        off = SUBLANE - (CONV_W - 1) + j
        xc = xc + xpad_ref[off:off + tt, :] * cw_ref[j:j + 1, :]
    xcb = xc.astype(BF16)
    r = jax.nn.sigmoid(jnp.dot(xcb, wa_ref[...], preferred_element_type=F32) + ba_ref[...])
    i = jax.nn.sigmoid(jnp.dot(xcb, wx_ref[...], preferred_element_type=F32) + bxb_ref[...])
    nl = -lam_ref[...]
    softplus = jnp.maximum(nl, 0.0) + jnp.log1p(jnp.exp(-jnp.abs(nl)))
    a = jnp.exp(-LRU_C * r * softplus)
    mult = jnp.sqrt(1.0 - a * a)
    if pos0_is_zero:
        rows = lax.broadcasted_iota(I32, (tt, W), 0)
        mult = jnp.where(jnp.logical_and(rows == 0, ti == 0), 1.0, mult)
    a_ref[...] = a
    b_ref[...] = mult * (i * xc)
    rowt = lax.broadcasted_iota(I32, (SUBLANE, W), 0)

    def tile(n, h_prev):
        r0 = pl.multiple_of(n * SUBLANE, SUBLANE)
        at = a_ref[pl.ds(r0, SUBLANE), :]
        bt = b_ref[pl.ds(r0, SUBLANE), :]
        for s in (1, 2, 4):
            keep = rowt >= s
            bt = jnp.where(keep, bt + at * pltpu.roll(bt, s, 0), bt)
            at = jnp.where(keep, at * pltpu.roll(at, s, 0), at)
        ht = bt + at * h_prev
        hs_ref[pl.ds(r0, SUBLANE), :] = ht
        return jnp.broadcast_to(ht[SUBLANE - 1:SUBLANE, :], (SUBLANE, W))

    hc_ref[...] = lax.fori_loop(0, tt // SUBLANE, tile, hc_ref[...])
    o_ref[...] = (hs_ref[...] * jax.nn.gelu(g_ref[...])).astype(o_ref.dtype)
    xpad_ref[0:SUBLANE, :] = xpad_ref[tt:tt + SUBLANE, :]

    @pl.when(ti == n_t - 1)
    def _():
        t_loc = t_real - (n_t - 1) * tt
        hfin_ref[0] = hs_ref[t_loc - 1:t_loc, :]
        buf_ref[0] = x_ref[t_loc - (CONV_W - 1):t_loc, :]


def _rglru(P, state, weights, *, B, T, tt, t_real, out_dtype):
    W = 4 * LANE
    n_t = T // tt
    has_state = state is not None
    xspec = pl.BlockSpec((tt, W), lambda b, i: (b * n_t + i, C_BX // 4))
    gspec = pl.BlockSpec((tt, W), lambda b, i: (b * n_t + i, C_BG // 4))
    full = lambda shape: pl.BlockSpec(shape, lambda b, i: (0,) * len(shape))
    wspecs = [full((CONV_W, W)), full((1, W)), full((W, W)), full((1, W)), full((W, W)),
              full((1, W)), full((1, W))]
    in_specs = [xspec, gspec]
    args = [P, P]
    if has_state:
        in_specs += [pl.BlockSpec((1, 1, W), lambda b, i: (b, 0, 0)),
                     pl.BlockSpec((1, SUBLANE, W), lambda b, i: (b, 0, 0))]
        args += list(state)
    return pl.pallas_call(
        functools.partial(_rglru_body, tt=tt, t_real=t_real, n_t=n_t, has_state=has_state,
                          pos0_is_zero=not has_state),
        grid=(B, n_t),
        in_specs=in_specs + wspecs,
        out_specs=[pl.BlockSpec((tt, W), lambda b, i: (b * n_t + i, 0)),
                   pl.BlockSpec((1, 1, W), lambda b, i: (b, 0, 0)),
                   pl.BlockSpec((1, CONV_W - 1, W), lambda b, i: (b, 0, 0))],
        out_shape=[jax.ShapeDtypeStruct((B * T, W), out_dtype),
                   jax.ShapeDtypeStruct((B, 1, W), F32),
                   jax.ShapeDtypeStruct((B, CONV_W - 1, W), F32)],
        scratch_shapes=[pltpu.VMEM((tt + SUBLANE, W), F32), pltpu.VMEM((tt, W), F32),
                        pltpu.VMEM((tt, W), F32), pltpu.VMEM((tt, W), F32),
                        pltpu.VMEM((SUBLANE, W), F32)],
        compiler_params=_cparams(("parallel", "arbitrary")),
        name="rglru")(*args, *weights)


def _hgrn_body(*refs, tt, ch, sub, t_real, n_t, has_state):
    if has_state:
        q_ref, f_ref, v_ref, g_ref, s0_ref, lb_ref, gain_ref, o_ref, sfin_ref, st_ref = refs
    else:
        q_ref, f_ref, v_ref, g_ref, lb_ref, gain_ref, o_ref, sfin_ref, st_ref = refs
    ti = pl.program_id(2)
    dk = q_ref.shape[1]

    @pl.when(ti == 0)
    def _():
        if has_state:
            st_ref[...] = s0_ref[0, 0].T
        else:
            st_ref[...] = jnp.zeros(st_ref.shape, F32)

    lb = lb_ref[0]
    log_lb = jnp.log(lb)
    log_1mlb = jnp.log1p(-lb)
    rr = lax.broadcasted_iota(I32, (ch, ch), 0)
    cc = lax.broadcasted_iota(I32, (ch, ch), 1)
    tril = (cc <= rr).astype(F32)
    rows = lax.broadcasted_iota(I32, (ch, dk), 0)
    srow = lax.broadcasted_iota(I32, (sub, 1), 0)
    n_sub = ch // sub

    def chunk(c, carry):
        r0 = pl.multiple_of(c * ch, ch)
        q = q_ref[pl.ds(r0, ch), :]
        q = q * jax.nn.sigmoid(q)
        fp = f_ref[pl.ds(r0, ch), :]
        v = v_ref[pl.ds(r0, ch), :]
        log_sig = jnp.minimum(fp, 0.0) - jnp.log1p(jnp.exp(-jnp.abs(fp)))
        b = log_1mlb + log_sig
        lf = jnp.maximum(log_lb, b) + jnp.log1p(jnp.exp(-jnp.abs(log_lb - b)))
        kk = (1.0 - lb) * jax.nn.sigmoid(-fp)
        if t_real < tt * n_t:
            live = (rows + (ti * tt + c * ch)) < t_real
            lf = jnp.where(live, lf, 0.0)
            kk = jnp.where(live, kk, 0.0)
        G = jnp.dot(tril, lf, preferred_element_type=F32, precision=lax.Precision.HIGHEST)
        st = st_ref[...]
        vb = v.astype(BF16)
        o = _nt((q * jnp.exp(G)).astype(BF16), st.astype(BF16))
        outs = []
        for i in range(n_sub):
            lo, hi = i * sub, (i + 1) * sub
            qi, Gi, ki, vi = q[lo:hi], G[lo:hi], kk[lo:hi], v[lo:hi]
            oi = o[lo:hi]
            if i > 0:
                R = G[lo - 1:lo]
                qp = (qi * jnp.exp(Gi - R)).astype(BF16)
                kp = (kk[:lo] * jnp.exp(R - G[:lo])).astype(BF16)
                att = _nt(qp, kp)
                oi = oi + jnp.dot(att.astype(BF16), vb[:lo], preferred_element_type=F32)
            for s in range(sub):
                w = jnp.exp(jnp.minimum(Gi - Gi[s:s + 1], 0.0))
                colv = jnp.sum(qi * w * ki[s:s + 1], axis=-1, keepdims=True)
                colv = jnp.where(srow >= s, colv, 0.0)
                oi = oi + colv * vi[s:s + 1]
            outs.append(oi)
        o = jnp.concatenate(outs, axis=0) if n_sub > 1 else outs[0]
        gl = G[ch - 1:ch]
        kpp = (kk * jnp.exp(gl - G)).astype(BF16)
        st_ref[...] = st * jnp.exp(gl) + _tn(vb, kpp)
        gate = g_ref[pl.ds(r0, ch), :]
        o = _rms_rows(o, gain_ref[...]) * (gate * jax.nn.sigmoid(gate))
        o_ref[pl.ds(r0, ch), :] = o.astype(o_ref.dtype)
        return carry

    lax.fori_loop(0, tt // ch, chunk, 0)

    @pl.when(ti == n_t - 1)
    def _():
        sfin_ref[0, 0] = st_ref[...].T


def _hgrn(P, s0, lb, gain, *, B, T, tt, ch, sub, t_real, out_dtype):
    H = lb.shape[0]
    n_t = T // tt
    has_state = s0 is not None

    def col(c0):
        return pl.BlockSpec((tt, LANE), lambda b, h, i: (b * n_t + i, c0 + h))

    in_specs = [col(C_CQ), col(C_CF), col(C_CI), col(C_CG)]
    args = [P, P, P, P]
    if has_state:
        in_specs.append(pl.BlockSpec((1, 1, LANE, LANE), lambda b, h, i: (b, h, 0, 0)))
        args.append(s0)
    in_specs += [pl.BlockSpec((1, 1, LANE), lambda b, h, i: (h, 0, 0)),
                 pl.BlockSpec((1, LANE), lambda b, h, i: (0, 0))]
    return pl.pallas_call(
        functools.partial(_hgrn_body, tt=tt, ch=ch, sub=sub, t_real=t_real, n_t=n_t,
                          has_state=has_state),
        grid=(B, H, n_t),
        in_specs=in_specs,
        out_specs=[pl.BlockSpec((tt, LANE), lambda b, h, i: (b * n_t + i, h)),
                   pl.BlockSpec((1, 1, LANE, LANE), lambda b, h, i: (b, h, 0, 0))],
        out_shape=[jax.ShapeDtypeStruct((B * T, H * LANE), out_dtype),
                   jax.ShapeDtypeStruct((B, H, LANE, LANE), F32)],
        scratch_shapes=[pltpu.VMEM((LANE, LANE), F32)],
        compiler_params=_cparams(("parallel", "parallel", "arbitrary")),
        name="hgrn2")(*args, lb, gain)


def _score_key(score):
    score = jnp.where(score == 0.0, 0.0, score)
    bits = pltpu.bitcast(score, I32)
    return bits ^ ((bits >> 31) & 0x7FFFFFFF)


def _kth_largest(count_ge, rows, k):
    zero = jnp.zeros((rows, 1), I32)
    v = jnp.where(count_ge(zero) >= k, zero, jnp.full((rows, 1), INT_MIN, I32))

    def bit_step(n, v):
        cand = v + (jnp.int32(1) << (30 - n))
        return jnp.where(count_ge(cand) >= k, cand, v)

    return lax.fori_loop(0, 31, bit_step, v)


def _tie_bound(count_tie_below, r, rows, n_bits):
    def bit_step(n, pos):
        cand = pos + (jnp.int32(1) << (n_bits - 1 - n))
        return jnp.where(count_tie_below(cand) < r, cand, pos)

    return lax.fori_loop(0, n_bits, bit_step, jnp.zeros((rows, 1), I32))


def _dsa_body(dq_ref, iq_ref, tq_ref, dk_ref, dv_ref, ik_ref, bias_ref, o_ref,
              key_ref, wb_ref, pos_ref, *, tq, n_top, n_bits):
    qb = pl.program_id(1)
    n_c = qb + 1
    hd = HEAD_DIM
    n_kv = dk_ref.shape[1] // hd
    n_q = dq_ref.shape[1] // hd
    grp = n_q // n_kv
    iq = iq_ref[...].astype(BF16)
    q_idx = jnp.concatenate([iq[:, h * hd:(h + 1) * hd] for h in range(IDX_HEADS)], axis=0)
    wscale = IDX_HEADS ** -0.5 * hd ** -0.5
    tail = tq_ref[...]
    for h in range(IDX_HEADS):
        wb_ref[h * tq:(h + 1) * tq, :] = jnp.broadcast_to(
            tail[:, IW_OFF + h:IW_OFF + h + 1] * wscale, (tq, LANE))
    row = lax.broadcasted_iota(I32, (tq, LANE), 0)
    col = lax.broadcasted_iota(I32, (tq, LANE), 1)

    def score_tile(c, carry):
        r0 = pl.multiple_of(c * LANE, LANE)
        ikc = ik_ref[pl.ds(r0, LANE), :][:, :hd].astype(BF16)
        s = jnp.maximum(_nt(q_idx, ikc), 0.0) * wb_ref[...]
        sc = s[0:tq]
        for h in range(1, IDX_HEADS):
            sc = sc + s[h * tq:(h + 1) * tq]
        sc = jnp.where(jnp.logical_or(c < qb, col <= row), sc, NEG_INF)
        key_ref[c] = _score_key(sc)
        return carry

    lax.fori_loop(0, n_c, score_tile, 0)

    def count(pred):
        def body(c, acc):
            return acc + pred(key_ref[c], c).astype(I32)
        acc = lax.fori_loop(0, n_c, body, jnp.zeros((tq, LANE), I32))
        return jnp.sum(acc, axis=-1, keepdims=True)

    def count_ge(cand):
        cb = jnp.broadcast_to(cand, (tq, LANE))
        return count(lambda key, c: key >= cb)

    thr = _kth_largest(count_ge, tq, n_top)
    thr_b = jnp.broadcast_to(thr, (tq, LANE))
    n_gt = count(lambda key, c: key > thr_b)
    n_ge = count(lambda key, c: key >= thr_b)
    need = jnp.logical_and(n_ge > n_top, thr > KEY_NEG_INF)
    pos_ref[...] = jnp.full((tq, LANE), 2 ** n_bits, I32)

    @pl.when(jnp.max(need.astype(I32)) > 0)
    def _():
        r = n_top - n_gt

        def count_tie_below(cand):
            cb = jnp.broadcast_to(cand, (tq, LANE))
            return count(lambda key, c: jnp.logical_and(key == thr_b, col + c * LANE < cb))

        pos = _tie_bound(count_tie_below, r, tq, n_bits)
        pos_ref[...] = jnp.broadcast_to(pos, (tq, LANE))

    pos_b = pos_ref[...]
    dq = dq_ref[...].astype(BF16)
    q_g = [jnp.concatenate([dq[:, (g * grp + j) * hd:(g * grp + j + 1) * hd] for j in range(grp)],
                           axis=0) for g in range(n_kv)]
    rows_g = grp * tq

    def sel_mask(c):
        key = key_ref[c]
        idx = col + c * LANE
        sel = jnp.logical_or(key > thr_b, jnp.logical_and(key == thr_b, idx <= pos_b))
        sel = jnp.logical_and(sel, key > KEY_NEG_INF)
        return jnp.concatenate([sel] * grp, axis=0)

    def step(c, st, bias_off, enabled):
        r0 = pl.multiple_of(c * LANE, LANE)
        kc = dk_ref[pl.ds(r0, LANE), :].astype(BF16)
        vc = dv_ref[pl.ds(r0, LANE), :].astype(BF16)
        sel = sel_mask(c)
        if enabled is not None:
            sel = jnp.logical_and(sel, enabled)
        new = []
        for g in range(n_kv):
            s = _nt(q_g[g], kc[:, g * hd:(g + 1) * hd])
            if bias_off is not None:
                s = s + bias_ref[g, bias_off]
            s = jnp.where(sel, s, NEG_INF)
            new.append(_softmax_update(st[g], s, vc[:, g * hd:(g + 1) * hd]))
        return tuple(new)

    init = tuple((jnp.full((rows_g, 1), -1e30, F32), jnp.zeros((rows_g, 1), F32),
                  jnp.zeros((rows_g, hd), F32)) for _ in range(n_kv))
    st = step(qb, init, 0, None)
    st = step(jnp.maximum(qb - 1, 0), st, 1, qb >= 1)
    st = lax.fori_loop(0, jnp.maximum(qb - 1, 0), lambda c, st: step(c, st, None, None), st)
    for g in range(n_kv):
        og = st[g][2] / st[g][1]
        for j in range(grp):
            h = g * grp + j
            o_ref[:, h * hd:(h + 1) * hd] = og[j * tq:(j + 1) * tq].astype(o_ref.dtype)


def _dsa_prompt(P, bias, *, B, T, tq, n_top):
    nq = T // tq
    n_bits = max(1, int(math.ceil(math.log2(T))))
    return pl.pallas_call(
        functools.partial(_dsa_body, tq=tq, n_top=n_top, n_bits=n_bits),
        grid=(B, nq),
        in_specs=[pl.BlockSpec((tq, 4 * LANE), lambda b, i: (b * nq + i, C_DQ // 4)),
                  pl.BlockSpec((tq, 4 * LANE), lambda b, i: (b * nq + i, C_IQ // 4)),
                  pl.BlockSpec((tq, LANE), lambda b, i: (b * nq + i, C_TAIL)),
                  pl.BlockSpec((T, LANE), lambda b, i: (b, C_DK)),
                  pl.BlockSpec((T, LANE), lambda b, i: (b, C_DV)),
                  pl.BlockSpec((T, LANE), lambda b, i: (b, C_TAIL)),
                  pl.BlockSpec(bias.shape, lambda b, i: (0, 0, 0, 0))],
        out_specs=pl.BlockSpec((tq, 4 * LANE), lambda b, i: (b * nq + i, 0)),
        out_shape=jax.ShapeDtypeStruct((B * T, 4 * LANE), BF16),
        scratch_shapes=[pltpu.VMEM((T // LANE, tq, LANE), I32),
                        pltpu.VMEM((IDX_HEADS * tq, LANE), F32),
                        pltpu.VMEM((tq, LANE), I32)],
        compiler_params=_cparams(("parallel", "arbitrary")),
        name="dsa_prompt")(P, P, P, P, P, P, bias)


def _attn_a_dec_body(pt_ref, scal_ref, q_ref, kn_ref, vn_ref, bl_ref, bn_ref, gout_ref, *rest,
                     pps, n_steps):
    k_refs = rest[:pps]
    v_refs = rest[pps:2 * pps]
    o_ref, m_ref, l_ref, acc_ref = rest[2 * pps:]
    s_i = pl.program_id(1)
    n_h = q_ref.shape[1] // LANE
    rows_h = 2 * T_PAD

    @pl.when(s_i == 0)
    def _():
        m_ref[...] = jnp.full(m_ref.shape, NEG_INF, F32)
        l_ref[...] = jnp.zeros(l_ref.shape, F32)
        acc_ref[...] = jnp.zeros(acc_ref.shape, F32)

    q = q_ref[...]
    lane = lax.broadcasted_iota(I32, (T_PAD, LANE), 1)

    def q_head(h):
        qh = q[:, h * LANE:(h + 1) * LANE]
        return jnp.concatenate([jnp.where(lane < HEAD_DIM, qh, 0.0),
                                jnp.where(lane >= HEAD_DIM, qh, 0.0)], axis=0).astype(BF16)

    qs = [q_head(h) for h in range(n_h)]
    is_last = s_i == n_steps - 1

    def attend(h, k_tiles, v_tiles, bias_tiles):
        rs = slice(h * rows_h, (h + 1) * rows_h)
        parts = []
        for kt, bt in zip(k_tiles, bias_tiles):
            s = _nt(qs[h], kt)
            if bt is not None:
                s = s + bt[rs]
            parts.append(s)
        s = jnp.concatenate(parts, axis=1) if len(parts) > 1 else parts[0]
        vh = jnp.concatenate(v_tiles, axis=0) if len(v_tiles) > 1 else v_tiles[0]
        m, l, acc = _softmax_update((m_ref[rs], l_ref[rs], acc_ref[rs]), s, vh)
        m_ref[rs] = m
        l_ref[rs] = l
        acc_ref[rs] = acc

    last_bias = jnp.where(is_last, bl_ref[...], 0.0)
    for h in range(n_h):
        head_rows = pl.ds(h, PAGE, stride=n_h)
        attend(h, [r[head_rows, :].astype(BF16) for r in k_refs],
               [r[head_rows, :].astype(BF16) for r in v_refs], [None] * (pps - 1) + [last_bias])

    @pl.when(is_last)
    def _():
        pad = jnp.zeros((LANE - T_PAD, LANE), BF16)
        for h in range(n_h):
            hs = slice(h * LANE, (h + 1) * LANE)
            kn = jnp.concatenate([kn_ref[:, hs].astype(BF16), pad], axis=0)
            vn = jnp.concatenate([vn_ref[:, hs].astype(BF16), pad], axis=0)
            attend(h, [kn], [vn], [bn_ref[...]])
        lam = scal_ref[0]
        for h in range(n_h):
            r1 = slice(h * rows_h, h * rows_h + T_PAD)
            r2 = slice(h * rows_h + T_PAD, (h + 1) * rows_h)
            o = acc_ref[r1] / l_ref[r1] - lam * (acc_ref[r2] / l_ref[r2])
            o_ref[:, h * LANE:(h + 1) * LANE] = _rms_rows(o, gout_ref[...]) * scal_ref[1]


def _attn_a_decode(P, cache_k, cache_v, page_table, layer, scal, bias_last, bias_new, gout, *, B, pps):
    n_pages = page_table.shape[1]
    n_steps = n_pages // pps
    W = 4 * LANE

    def page_spec(i):
        return pl.BlockSpec((None, None) + cache_k.shape[2:],
                            lambda b, s, pt: (pt[b, s * pps + i], layer, 0, 0))

    grid_spec = pltpu.PrefetchScalarGridSpec(
        num_scalar_prefetch=1,
        grid=(B, n_steps),
        in_specs=[pl.BlockSpec(memory_space=pltpu.SMEM),
                  pl.BlockSpec((T_PAD, W), lambda b, s, pt: (b, C_AQ // 4)),
                  pl.BlockSpec((T_PAD, W), lambda b, s, pt: (b, C_AK // 4)),
                  pl.BlockSpec((T_PAD, W), lambda b, s, pt: (b, C_AV // 4)),
                  pl.BlockSpec(bias_last.shape, lambda b, s, pt: (0, 0)),
                  pl.BlockSpec(bias_new.shape, lambda b, s, pt: (0, 0)),
                  pl.BlockSpec((1, LANE), lambda b, s, pt: (0, 0))]
        + [page_spec(i) for i in range(pps)] + [page_spec(i) for i in range(pps)],
        out_specs=pl.BlockSpec((T_PAD, W), lambda b, s, pt: (b, 0)),
        scratch_shapes=[pltpu.VMEM((bias_last.shape[0], 1), F32),
                        pltpu.VMEM((bias_last.shape[0], 1), F32),
                        pltpu.VMEM((bias_last.shape[0], LANE), F32)])
    return pl.pallas_call(
        functools.partial(_attn_a_dec_body, pps=pps, n_steps=n_steps),
        grid_spec=grid_spec,
        out_shape=jax.ShapeDtypeStruct((B * T_PAD, W), F32),
        compiler_params=_cparams(("parallel", "arbitrary")),
        name="attn_a_decode")(page_table, scal, P, P, P, bias_last, bias_new, gout,
                              *([cache_k] * pps), *([cache_v] * pps))


def _dsa_sel_body(pt_ref, iq_ref, tq_ref, ikn_ref, *rest, pps, n_steps, n_top, t_real, n_bits):
    ik_refs = rest[:pps]
    keys_ref, keyn_ref, thr_ref, pos_ref, all_ref, wb_ref = rest[pps:]
    s_i = pl.program_id(1)
    hd = HEAD_DIM
    iq = iq_ref[...].astype(BF16)
    q_idx = jnp.concatenate([iq[:, h * hd:(h + 1) * hd] for h in range(IDX_HEADS)], axis=0)
    wscale = IDX_HEADS ** -0.5 * hd ** -0.5
    tail = tq_ref[...]
    for h in range(IDX_HEADS):
        wb_ref[h * T_PAD:(h + 1) * T_PAD, :] = jnp.broadcast_to(
            tail[:, IW_OFF + h:IW_OFF + h + 1] * wscale, (T_PAD, LANE))

    def score(ik):
        s = jnp.maximum(_nt(q_idx, ik.astype(BF16)), 0.0) * wb_ref[...]
        sc = s[0:T_PAD]
        for h in range(1, IDX_HEADS):
            sc = sc + s[h * T_PAD:(h + 1) * T_PAD]
        return sc

    for i in range(pps):
        key = _score_key(score(ik_refs[i][...]))
        keys_ref[0, i] = key
        all_ref[s_i * pps + i] = key

    n_tiles = n_steps * pps + 1

    @pl.when(s_i == n_steps - 1)
    def _():
        row = lax.broadcasted_iota(I32, (T_PAD, LANE), 0)
        col = lax.broadcasted_iota(I32, (T_PAD, LANE), 1)
        pad = jnp.zeros((LANE - T_PAD, hd), F32)
        ikn = jnp.concatenate([ikn_ref[...][:, :hd], pad], axis=0)
        valid = jnp.logical_and(col <= row, col < t_real)
        keyn = _score_key(jnp.where(valid, score(ikn), NEG_INF))
        keyn_ref[0] = keyn
        all_ref[n_tiles - 1] = keyn

        def count(pred):
            def body(c, acc):
                return acc + pred(all_ref[c], c).astype(I32)
            acc = lax.fori_loop(0, n_tiles, body, jnp.zeros((T_PAD, LANE), I32))
            return jnp.sum(acc, axis=-1, keepdims=True)

        def count_ge(cand):
            cb = jnp.broadcast_to(cand, (T_PAD, LANE))
            return count(lambda key, c: key >= cb)

        thr = _kth_largest(count_ge, T_PAD, n_top)
        thr_b = jnp.broadcast_to(thr, (T_PAD, LANE))
        n_gt = count(lambda key, c: key > thr_b)
        r = n_top - n_gt

        def count_tie_below(cand):
            cb = jnp.broadcast_to(cand, (T_PAD, LANE))
            return count(lambda key, c: jnp.logical_and(key == thr_b, col + c * LANE < cb))

        pos = _tie_bound(count_tie_below, r, T_PAD, n_bits)
        thr_ref[0] = thr_b
        pos_ref[0] = jnp.broadcast_to(pos, (T_PAD, LANE))


def _dsa_select_decode(P, cache_idx, page_table, layer, *, B, pps, n_top, t_real):
    n_pages = page_table.shape[1]
    n_steps = n_pages // pps
    n_tiles = n_pages + 1
    n_bits = int(math.ceil(math.log2(n_tiles * LANE)))
    W = cache_idx.shape[-1]

    def page_spec(i):
        return pl.BlockSpec((None, None, PAGE, W), lambda b, s, pt: (pt[b, s * pps + i], layer, 0, 0))

    tile_spec = pl.BlockSpec((1, T_PAD, LANE), lambda b, s, pt: (b, 0, 0))
    grid_spec = pltpu.PrefetchScalarGridSpec(
        num_scalar_prefetch=1,
        grid=(B, n_steps),
        in_specs=[pl.BlockSpec((T_PAD, 4 * LANE), lambda b, s, pt: (b, C_IQ // 4)),
                  pl.BlockSpec((T_PAD, LANE), lambda b, s, pt: (b, C_TAIL)),
                  pl.BlockSpec((T_PAD, LANE), lambda b, s, pt: (b, C_TAIL))]
        + [page_spec(i) for i in range(pps)],
        out_specs=[pl.BlockSpec((1, pps, T_PAD, LANE), lambda b, s, pt: (b, s, 0, 0)),
                   tile_spec, tile_spec, tile_spec],
        scratch_shapes=[pltpu.VMEM((n_tiles, T_PAD, LANE), I32),
                        pltpu.VMEM((IDX_HEADS * T_PAD, LANE), F32)])
    tile_shape = jax.ShapeDtypeStruct((B, T_PAD, LANE), I32)
    return pl.pallas_call(
        functools.partial(_dsa_sel_body, pps=pps, n_steps=n_steps, n_top=n_top, t_real=t_real,
                          n_bits=n_bits),
        grid_spec=grid_spec,
        out_shape=[jax.ShapeDtypeStruct((B, n_pages, T_PAD, LANE), I32),
                   tile_shape, tile_shape, tile_shape],
        compiler_params=_cparams(("parallel", "arbitrary")),
        name="dsa_select_decode")(page_table, P, P, P, *([cache_idx] * pps))


def _dsa_att_body(pt_ref, dq_ref, kn_ref, vn_ref, keys_ref, keyn_ref, thr_ref, pos_ref,
                  bl_ref, bn_ref, *rest, pps, n_steps):
    k_refs = rest[:pps]
    v_refs = rest[pps:2 * pps]
    o_ref, m_ref, l_ref, acc_ref = rest[2 * pps:]
    s_i = pl.program_id(1)
    hd = HEAD_DIM
    n_kv = kn_ref.shape[1] // hd
    n_q = dq_ref.shape[1] // hd
    grp = n_q // n_kv
    rows_g = grp * T_PAD

    @pl.when(s_i == 0)
    def _():
        m_ref[...] = jnp.full(m_ref.shape, -1e30, F32)
        l_ref[...] = jnp.zeros(l_ref.shape, F32)
        acc_ref[...] = jnp.zeros(acc_ref.shape, F32)

    dq = dq_ref[...].astype(BF16)
    q_g = [jnp.concatenate([dq[:, (g * grp + j) * hd:(g * grp + j + 1) * hd] for j in range(grp)],
                           axis=0) for g in range(n_kv)]
    thr = thr_ref[0]
    pos = pos_ref[0]
    col = lax.broadcasted_iota(I32, (T_PAD, LANE), 1)
    is_last = s_i == n_steps - 1

    def sel_mask(key, tile):
        idx = col + tile * LANE
        sel = jnp.logical_or(key > thr, jnp.logical_and(key == thr, idx <= pos))
        sel = jnp.logical_and(sel, key > KEY_NEG_INF)
        return jnp.concatenate([sel] * grp, axis=0)

    def attend(g, k_tiles, v_tiles, masks, bias_tiles):
        rs = slice(g * rows_g, (g + 1) * rows_g)
        parts = []
        for kt, mk, bt in zip(k_tiles, masks, bias_tiles):
            s = _nt(q_g[g], kt)
            if bt is not None:
                s = s + bt[rs]
            parts.append(jnp.where(mk, s, NEG_INF))
        s = jnp.concatenate(parts, axis=1) if len(parts) > 1 else parts[0]
        vg = jnp.concatenate(v_tiles, axis=0) if len(v_tiles) > 1 else v_tiles[0]
        m, l, acc = _softmax_update((m_ref[rs], l_ref[rs], acc_ref[rs]), s, vg)
        m_ref[rs] = m
        l_ref[rs] = l
        acc_ref[rs] = acc

    masks = [sel_mask(keys_ref[0, i], s_i * pps + i) for i in range(pps)]
    last_bias = jnp.where(is_last, bl_ref[...], 0.0)
    for g in range(n_kv):
        head_rows = pl.ds(g, PAGE, stride=n_kv)
        attend(g, [r[head_rows, :].astype(BF16) for r in k_refs],
               [r[head_rows, :].astype(BF16) for r in v_refs], masks,
               [None] * (pps - 1) + [last_bias])

    @pl.when(is_last)
    def _():
        pad = jnp.zeros((LANE - T_PAD, hd), BF16)
        mask_new = sel_mask(keyn_ref[0], n_steps * pps)
        for g in range(n_kv):
            hs = slice(g * hd, (g + 1) * hd)
            kn = jnp.concatenate([kn_ref[:, hs].astype(BF16), pad], axis=0)
            vn = jnp.concatenate([vn_ref[:, hs].astype(BF16), pad], axis=0)
            attend(g, [kn], [vn], [mask_new], [bn_ref[...]])
        for g in range(n_kv):
            rs = slice(g * rows_g, (g + 1) * rows_g)
            og = acc_ref[rs] / l_ref[rs]
            for j in range(grp):
                h = g * grp + j
                o_ref[:, h * hd:(h + 1) * hd] = og[j * T_PAD:(j + 1) * T_PAD]


def _dsa_attend_decode(P, cache_k, cache_v, page_table, layer, keys, keyn, thr, pos,
                       bias_last, bias_new, *, B, pps):
    n_pages = page_table.shape[1]
    n_steps = n_pages // pps
    rows = bias_last.shape[0]

    def page_spec(i):
        return pl.BlockSpec((None, None) + cache_k.shape[2:],
                            lambda b, s, pt: (pt[b, s * pps + i], layer, 0, 0))

    tile_spec = pl.BlockSpec((1, T_PAD, LANE), lambda b, s, pt: (b, 0, 0))
    grid_spec = pltpu.PrefetchScalarGridSpec(
        num_scalar_prefetch=1,
        grid=(B, n_steps),
        in_specs=[pl.BlockSpec((T_PAD, 4 * LANE), lambda b, s, pt: (b, C_DQ // 4)),
                  pl.BlockSpec((T_PAD, LANE), lambda b, s, pt: (b, C_DK)),
                  pl.BlockSpec((T_PAD, LANE), lambda b, s, pt: (b, C_DV)),
                  pl.BlockSpec((1, pps, T_PAD, LANE), lambda b, s, pt: (b, s, 0, 0)),
                  tile_spec, tile_spec, tile_spec,
                  pl.BlockSpec(bias_last.shape, lambda b, s, pt: (0, 0)),
                  pl.BlockSpec(bias_new.shape, lambda b, s, pt: (0, 0))]
        + [page_spec(i) for i in range(pps)] + [page_spec(i) for i in range(pps)],
        out_specs=pl.BlockSpec((T_PAD, 4 * LANE), lambda b, s, pt: (b, 0)),
        scratch_shapes=[pltpu.VMEM((rows, 1), F32), pltpu.VMEM((rows, 1), F32),
                        pltpu.VMEM((rows, HEAD_DIM), F32)])
    return pl.pallas_call(
        functools.partial(_dsa_att_body, pps=pps, n_steps=n_steps),
        grid_spec=grid_spec,
        out_shape=jax.ShapeDtypeStruct((B * T_PAD, 4 * LANE), F32),
        compiler_params=_cparams(("parallel", "arbitrary")),
        name="dsa_attend_decode")(page_table, P, P, P, keys, keyn, thr, pos, bias_last, bias_new,
                                  *([cache_k] * pps), *([cache_v] * pps))


def _bias_minus_far(tab, dist):
    return tab[_rel_bucket(dist)] - tab[N_BUCKETS - 1]


def _toeplitz_tiles(tab, t):
    r = jnp.arange(t)[:, None]
    c = jnp.arange(t)[None, :]
    tiles = jnp.stack([_bias_minus_far(tab, r - c), _bias_minus_far(tab, t + r - c)], axis=0)
    return jnp.transpose(tiles, (3, 0, 1, 2))


def _decode_bias(tab, past, t_real):
    tq = jnp.arange(T_PAD)[:, None]
    kc = jnp.arange(LANE)[None, :]
    last = _bias_minus_far(tab, past + tq - (past - PAGE + kc))
    new = _bias_minus_far(tab, tq - kc)
    valid = jnp.logical_and(kc <= tq, kc < t_real)
    new = jnp.where(valid[..., None], new, NEG_INF)
    return jnp.transpose(last, (2, 0, 1)), jnp.transpose(new, (2, 0, 1))


def _block_diag(w):
    n, d, e = w.shape
    eye = jnp.eye(n, dtype=w.dtype)
    return (eye[:, None, :, None] * w[:, :, None, :]).reshape(n * d, n * e)


def kernel(x_prompt, x_sample, cache_a_k, cache_a_v, cache_d_k, cache_d_v, cache_d_idx, state_b_h, state_b_conv, state_c_s, page_table, rel_bias, ln1, w_in, a_q_norm, a_k_norm, a_lam_q1, a_lam_k1, a_lam_q2, a_lam_k2, a_out_norm, b_conv_w, b_conv_b, b_wa, b_ba, b_wx, b_bx, b_lambda, c_lb_logits, c_out_norm, d_q_norm, d_k_norm, w_out, ln2, w_up, w_down):
    B, T, D = x_prompt.shape
    Bs, Ts, _ = x_sample.shape
    L = w_in.shape[0]
    n_pages = page_table.shape[1]
    past = n_pages * PAGE
    gw = D // 4
    a_heads = gw // (2 * HEAD_DIM)
    c_heads = c_lb_logits.shape[1] // LANE
    d_heads = gw // HEAD_DIM
    d_kv = cache_d_k.shape[3]
    d_grp = d_heads // d_kv
    n_pool = cache_a_k.shape[0]
    assert Ts <= T_PAD - 0 and Ts >= CONV_W - 1 and past > 0

    tab = rel_bias.astype(F32)
    n_a_maps = 2 * a_heads
    tab_a, tab_d = tab[:, :n_a_maps], tab[:, n_a_maps:]
    tq_a = min(256, T)
    tq_d = min(128, T)
    bias_a = _toeplitz_tiles(tab_a, tq_a).reshape(a_heads, 2, 2, tq_a, tq_a)
    bias_d = _toeplitz_tiles(tab_d, tq_d).reshape(d_kv, d_grp, 2, tq_d, tq_d)
    bias_d = jnp.transpose(bias_d, (0, 2, 1, 3, 4)).reshape(d_kv, 2, d_grp * tq_d, tq_d)
    bl_a, bn_a = _decode_bias(tab_a, past, Ts)
    bl_a = bl_a.reshape(n_a_maps * T_PAD, LANE)
    bn_a = bn_a.reshape(n_a_maps * T_PAD, LANE)
    bl_d, bn_d = _decode_bias(tab_d, past, Ts)
    bl_d = bl_d.reshape(d_heads * T_PAD, LANE)
    bn_d = bn_d.reshape(d_heads * T_PAD, LANE)

    lb_cum = jnp.cumsum(jax.nn.softmax(c_lb_logits.astype(F32), axis=0), axis=0)
    lb_all = (lb_cum - lb_cum[0]).reshape(L, c_heads, 1, LANE)

    seg = jnp.kron(jnp.eye(LANE // HEAD_DIM, dtype=F32),
                   jnp.full((HEAD_DIM, HEAD_DIM), 1.0 / HEAD_DIM, F32)).astype(BF16)
    qscale = HEAD_DIM ** -0.5
    zeros = lambda n: jnp.zeros((n,), F32)
    ones = lambda n: jnp.ones((n,), F32)
    flag = jnp.concatenate([ones(2 * gw), zeros(7 * gw), ones(gw), zeros(gw), ones(LANE),
                            zeros(P_WIDTH - 11 * gw - LANE)]).reshape(1, P_WIDTH)
    norm_tiles = (0, 1, 9, 11)

    ca_k = cache_a_k.reshape(n_pool, L, PAGE * a_heads, 2 * HEAD_DIM)
    ca_v = cache_a_v.reshape(n_pool, L, PAGE * a_heads, 2 * HEAD_DIM)
    cd_k = cache_d_k.reshape(n_pool, L, PAGE * d_kv, HEAD_DIM)
    cd_v = cache_d_v.reshape(n_pool, L, PAGE * d_kv, HEAD_DIM)

    xp = x_prompt.reshape(B * T, D)
    xs = jnp.pad(x_sample, ((0, 0), (0, T_PAD - Ts), (0, 0))).reshape(Bs * T_PAD, D)
    s_conv = jnp.pad(state_b_conv, ((0, 0), (0, 0), (SUBLANE - (CONV_W - 1), 0), (0, 0)))

    n_top_p = min(TOPK_MAX, T // 4)
    n_top_s = min(TOPK_MAX, (past + Ts) // 4)
    tm_p = min(1024, B * T)
    tm_f = min(512, B * T)
    tt_b = min(256, T)
    tt_c = min(512, T)
    ch_c = math.gcd(T, 64)
    pps_a = math.gcd(n_pages, 8)
    pps_d = math.gcd(n_pages, 16)

    outs_p, outs_s = [], []
    for l in range(L):
        wl = w_in[l]
        n_head = 10 * gw
        w_perm = jnp.concatenate(
            [wl[:, :n_head], wl[:, n_head + 2 * LANE:n_head + 2 * LANE + gw],
             wl[:, n_head:n_head + 2 * LANE], wl[:, n_head + 2 * LANE + gw:],
             jnp.zeros((D, P_WIDTH - wl.shape[1]), wl.dtype)], axis=1).astype(BF16)
        gain = jnp.concatenate(
            [jnp.tile(a_q_norm[l], 2 * a_heads) * qscale, jnp.tile(a_k_norm[l], 2 * a_heads),
             ones(7 * gw), jnp.tile(d_q_norm[l], d_heads) * qscale, ones(gw),
             jnp.tile(d_k_norm[l], d_kv), ones(P_WIDTH - 11 * gw - LANE)]).reshape(1, P_WIDTH)
        g1 = ln1[l].reshape(1, D)
        g2 = ln2[l].reshape(1, D)
        w_o = w_out[l].astype(BF16)
        w_u = w_up[l].astype(BF16)
        w_d = w_down[l].astype(BF16)
        lam_init = 0.8 - 0.6 * math.exp(-0.3 * l)
        lam = (jnp.exp(jnp.sum(a_lam_q1[l].astype(F32) * a_lam_k1[l].astype(F32)))
               - jnp.exp(jnp.sum(a_lam_q2[l].astype(F32) * a_lam_k2[l].astype(F32))) + lam_init)
        scal = jnp.stack([lam, jnp.asarray(1.0 - lam_init, F32)]).astype(F32)
        g_a = a_out_norm[l].reshape(1, LANE)
        g_c = c_out_norm[l].reshape(1, LANE)
        b_weights = (b_conv_w[l], b_conv_b[l].reshape(1, gw), _block_diag(b_wa[l]).astype(BF16),
                     b_ba[l].reshape(1, gw), _block_diag(b_wx[l]).astype(BF16),
                     b_bx[l].reshape(1, gw), b_lambda[l].reshape(1, gw))

        def dense_tail(x, mixes, tm, tmf):
            x1 = _out_proj(mixes, w_o, x, tm=tm, tn=min(512, D))
            return _ffn(x1, g2, w_u, w_d, tm=tmf, tf=min(512, w_u.shape[1]))

        Pp = _in_proj(xp, g1, w_perm, gain, flag, seg, tm=tm_p, tn=4 * LANE, norm_tiles=norm_tiles)
        mix_a = _attn_a_prompt(Pp, scal, bias_a, g_a, B=B, T=T, tq=tq_a)
        mix_b, hfin, buf = _rglru(Pp, None, b_weights, B=B, T=T, tt=tt_b, t_real=T, out_dtype=BF16)
        mix_c, sfin = _hgrn(Pp, None, lb_all[l], g_c, B=B, T=T, tt=tt_c, ch=ch_c,
                            sub=min(16, ch_c), t_real=T, out_dtype=BF16)
        mix_d = _dsa_prompt(Pp, bias_d, B=B, T=T, tq=tq_d, n_top=n_top_p)
        xp = dense_tail(xp, (mix_a, mix_b, mix_c, mix_d), tm_p, tm_f)
        P3 = Pp.reshape(B, T, P_WIDTH)
        outs_p.append((P3[..., C_AK * LANE:C_AV * LANE].reshape(B, T, a_heads, 2 * HEAD_DIM),
                       P3[..., C_AV * LANE:C_BX * LANE].reshape(B, T, a_heads, 2 * HEAD_DIM),
                       P3[..., C_DK * LANE:C_DV * LANE].reshape(B, T, d_kv, HEAD_DIM),
                       P3[..., C_DV * LANE:C_TAIL * LANE].reshape(B, T, d_kv, HEAD_DIM),
                       P3[..., C_TAIL * LANE:C_TAIL * LANE + HEAD_DIM],
                       hfin.reshape(B, gw), buf, sfin))

        Ps = _in_proj(xs, g1, w_perm, gain, flag, seg, tm=Bs * T_PAD, tn=4 * LANE,
                      norm_tiles=norm_tiles)
        smix_a = _attn_a_decode(Ps, ca_k, ca_v, page_table, l, scal, bl_a, bn_a, g_a, B=Bs, pps=pps_a)
        smix_b, shfin, sbuf = _rglru(
            Ps, (state_b_h[:, l].reshape(Bs, 1, gw), s_conv[:, l]), b_weights,
            B=Bs, T=T_PAD, tt=T_PAD, t_real=Ts, out_dtype=F32)
        smix_c, ssfin = _hgrn(Ps, state_c_s[:, l], lb_all[l], g_c, B=Bs, T=T_PAD, tt=T_PAD,
                              ch=T_PAD, sub=T_PAD, t_real=Ts, out_dtype=F32)
        keys, keyn, thr, pos = _dsa_select_decode(Ps, cache_d_idx, page_table, l, B=Bs, pps=pps_d,
                                                  n_top=n_top_s, t_real=Ts)
        smix_d = _dsa_attend_decode(Ps, cd_k, cd_v, page_table, l, keys, keyn, thr, pos,
                                    bl_d, bn_d, B=Bs, pps=pps_d)
        xs = dense_tail(xs, (smix_a, smix_b, smix_c, smix_d), Bs * T_PAD, Bs * T_PAD)
        S3 = Ps.reshape(Bs, T_PAD, P_WIDTH)[:, :Ts]
        outs_s.append((S3[..., C_AK * LANE:C_AV * LANE].reshape(Bs, Ts, a_heads, 2 * HEAD_DIM),
                       S3[..., C_AV * LANE:C_BX * LANE].reshape(Bs, Ts, a_heads, 2 * HEAD_DIM),
                       S3[..., C_DK * LANE:C_DV * LANE].reshape(Bs, Ts, d_kv, HEAD_DIM),
                       S3[..., C_DV * LANE:C_TAIL * LANE].reshape(Bs, Ts, d_kv, HEAD_DIM),
                       S3[..., C_TAIL * LANE:C_TAIL * LANE + HEAD_DIM],
                       shfin.reshape(Bs, gw), sbuf, ssfin))

    y_prompt = xp.reshape(B, T, D)
    y_sample = xs.reshape(Bs, T_PAD, D)[:, :Ts]
    stack = lambda outs: [jnp.stack(s, axis=1) for s in zip(*outs)]
    return (y_prompt, y_sample, *stack(outs_p), *stack(outs_s))
```

```python
import functools
import math

import jax
import jax.numpy as jnp
import numpy as np
from jax import lax
from jax.experimental import pallas as pl
from jax.experimental.pallas import tpu as pltpu

F32 = jnp.float32
BF16 = jnp.bfloat16
I32 = jnp.int32

EPS = 1e-6
HEAD_DIM = 64
PAGE = 128
CONV_W = 4
LRU_C = 8.0
IDX_HEADS = 8
TOPK_MAX = 256
N_BUCKETS = 32
MAX_DIST = 128
LANE = 128
SUBLANE = 8
T_PAD = 8
INT_MIN = -(2 ** 31)
KEY_NEG_INF = -2139095041
VMEM_LIMIT = 56 * 1024 * 1024
NEG_INF = float("-inf")

C_AQ, C_AK, C_AV, C_BX, C_BG, C_CQ, C_CF, C_CI, C_CG, C_DQ, C_IQ = (4 * i for i in range(11))
C_DK, C_DV, C_TAIL = 44, 45, 46
P_WIDTH = 48 * LANE
IW_OFF = 64


def _cparams(sem):
    return pltpu.CompilerParams(dimension_semantics=sem, vmem_limit_bytes=VMEM_LIMIT)


def _nt(a, b):
    return lax.dot_general(a, b, (((1,), (1,)), ((), ())), preferred_element_type=F32)


def _tn(a, b):
    return lax.dot_general(a, b, (((0,), (0,)), ((), ())), preferred_element_type=F32)


def _rel_bucket(dist):
    n = jnp.maximum(dist, 0)
    exact = N_BUCKETS // 2
    large = exact + (jnp.log(jnp.maximum(n, 1).astype(F32) / exact)
                     / math.log(MAX_DIST / exact) * (N_BUCKETS - exact)).astype(I32)
    return jnp.where(n < exact, n, jnp.minimum(large, N_BUCKETS - 1))


def _in_proj_body(x_ref, g_ref, w_ref, gain_ref, flag_ref, seg_ref, o_ref, xn_ref, *, tn, norm_tiles):
    j = pl.program_id(1)

    @pl.when(j == 0)
    def _():
        x = x_ref[...]
        ms = jnp.mean(x * x, axis=-1, keepdims=True)
        xn_ref[...] = (x * lax.rsqrt(ms + EPS) * g_ref[...]).astype(BF16)

    y = jnp.dot(xn_ref[...], w_ref[...], preferred_element_type=F32)
    is_norm = functools.reduce(jnp.logical_or, [j == t for t in norm_tiles])

    @pl.when(is_norm)
    def _():
        seg = seg_ref[...]
        for c in range(tn // LANE):
            sl = slice(c * LANE, (c + 1) * LANE)
            yc = y[:, sl]
            y2 = yc * yc
            hi = y2.astype(BF16)
            lo = (y2 - hi.astype(F32)).astype(BF16)
            ms = (jnp.dot(hi, seg, preferred_element_type=F32)
                  + jnp.dot(lo, seg, preferred_element_type=F32))
            yn = yc * lax.rsqrt(ms + EPS) * gain_ref[:, sl]
            o_ref[:, sl] = jnp.where(flag_ref[:, sl] > 0, yn, yc)

    @pl.when(jnp.logical_not(is_norm))
    def _():
        o_ref[...] = y


def _in_proj(x, g, w, gain, flag, seg, *, tm, tn, norm_tiles):
    M, K = x.shape
    N = w.shape[1]
    return pl.pallas_call(
        functools.partial(_in_proj_body, tn=tn, norm_tiles=norm_tiles),
        grid=(M // tm, N // tn),
        in_specs=[pl.BlockSpec((tm, K), lambda i, j: (i, 0)),
                  pl.BlockSpec((1, K), lambda i, j: (0, 0)),
                  pl.BlockSpec((K, tn), lambda i, j: (0, j)),
                  pl.BlockSpec((1, tn), lambda i, j: (0, j)),
                  pl.BlockSpec((1, tn), lambda i, j: (0, j)),
                  pl.BlockSpec((LANE, LANE), lambda i, j: (0, 0))],
        out_specs=pl.BlockSpec((tm, tn), lambda i, j: (i, j)),
        out_shape=jax.ShapeDtypeStruct((M, N), F32),
        scratch_shapes=[pltpu.VMEM((tm, K), BF16)],
        compiler_params=_cparams(("parallel", "arbitrary")),
        name="in_proj")(x, g, w, gain, flag, seg)


def _out_proj_body(a_ref, b_ref, c_ref, d_ref, w_ref, x_ref, o_ref, *, gw):
    acc = x_ref[...]
    for g, m_ref in enumerate((a_ref, b_ref, c_ref, d_ref)):
        acc = acc + jnp.dot(m_ref[...].astype(BF16), w_ref[g * gw:(g + 1) * gw, :],
                            preferred_element_type=F32)
    o_ref[...] = acc


def _out_proj(mixes, w, x, *, tm, tn):
    M, D = x.shape
    gw = mixes[0].shape[1]
    mix_spec = pl.BlockSpec((tm, gw), lambda i, j: (i, 0))
    return pl.pallas_call(
        functools.partial(_out_proj_body, gw=gw),
        grid=(M // tm, D // tn),
        in_specs=[mix_spec, mix_spec, mix_spec, mix_spec,
                  pl.BlockSpec((w.shape[0], tn), lambda i, j: (0, j)),
                  pl.BlockSpec((tm, tn), lambda i, j: (i, j))],
        out_specs=pl.BlockSpec((tm, tn), lambda i, j: (i, j)),
        out_shape=jax.ShapeDtypeStruct((M, D), F32),
        compiler_params=_cparams(("parallel", "arbitrary")),
        name="out_proj")(*mixes, w, x)


def _ffn_body(x_ref, g_ref, wu_ref, wd_ref, o_ref, xn_ref):
    f = pl.program_id(1)

    @pl.when(f == 0)
    def _():
        x = x_ref[...]
        ms = jnp.mean(x * x, axis=-1, keepdims=True)
        xn_ref[...] = (x * lax.rsqrt(ms + EPS) * g_ref[...]).astype(BF16)
        o_ref[...] = x

    h = jnp.dot(xn_ref[...], wu_ref[...], preferred_element_type=F32)
    h = jnp.maximum(h, 0.0)
    h = (h * h).astype(BF16)
    o_ref[...] += jnp.dot(h, wd_ref[...], preferred_element_type=F32)


def _ffn(x, g, wu, wd, *, tm, tf):
    M, D = x.shape
    Fd = wu.shape[1]
    return pl.pallas_call(
        _ffn_body,
        grid=(M // tm, Fd // tf),
        in_specs=[pl.BlockSpec((tm, D), lambda i, f: (i, 0)),
                  pl.BlockSpec((1, D), lambda i, f: (0, 0)),
                  pl.BlockSpec((D, tf), lambda i, f: (0, f)),
                  pl.BlockSpec((tf, D), lambda i, f: (f, 0))],
        out_specs=pl.BlockSpec((tm, D), lambda i, f: (i, 0)),
        out_shape=jax.ShapeDtypeStruct((M, D), F32),
        scratch_shapes=[pltpu.VMEM((tm, D), BF16)],
        compiler_params=_cparams(("parallel", "arbitrary")),
        name="ffn")(x, g, wu, wd)


def _softmax_update(state, s, v):
    m, l, acc = state
    m_new = jnp.maximum(m, jnp.max(s, axis=-1, keepdims=True))
    alpha = jnp.exp(m - m_new)
    p = jnp.exp(s - m_new)
    l = alpha * l + jnp.sum(p, axis=-1, keepdims=True)
    acc = alpha * acc + jnp.dot(p.astype(BF16), v, preferred_element_type=F32)
    return m_new, l, acc


def _rms_rows(o, gain):
    ms = jnp.mean(o * o, axis=-1, keepdims=True)
    return o * lax.rsqrt(ms + EPS) * gain


def _attn_a_body(scal_ref, q_ref, k_ref, v_ref, bias_ref, gout_ref, o_ref, *, tq):
    qi = pl.program_id(2)
    q = q_ref[...]
    lane = lax.broadcasted_iota(I32, q.shape, 1)
    qz = (jnp.where(lane < HEAD_DIM, q, 0.0).astype(BF16),
          jnp.where(lane >= HEAD_DIM, q, 0.0).astype(BF16))
    vd = v_ref.shape[1]

    def load(j):
        r0 = pl.multiple_of(j * tq, tq)
        return k_ref[pl.ds(r0, tq), :].astype(BF16), v_ref[pl.ds(r0, tq), :].astype(BF16)

    row = lax.broadcasted_iota(I32, (tq, tq), 0)
    col = lax.broadcasted_iota(I32, (tq, tq), 1)
    init = (jnp.full((tq, 1), NEG_INF, F32), jnp.zeros((tq, 1), F32), jnp.zeros((tq, vd), F32))

    k, v = load(qi)
    st = []
    for m in range(2):
        s = _nt(qz[m], k) + bias_ref[0, m, 0]
        s = jnp.where(col <= row, s, NEG_INF)
        st.append(_softmax_update(init, s, v))
    has_sub = qi >= 1
    k, v = load(jnp.maximum(qi - 1, 0))
    for m in range(2):
        s = _nt(qz[m], k) + bias_ref[0, m, 1]
        s = jnp.where(has_sub, s, NEG_INF)
        st[m] = _softmax_update(st[m], s, v)

    def far(j, carry):
        k, v = load(j)
        return tuple(_softmax_update(carry[m], _nt(qz[m], k), v) for m in range(2))

    st = lax.fori_loop(0, jnp.maximum(qi - 1, 0), far, tuple(st))
    lam = scal_ref[0]
    o = st[0][2] / st[0][1] - lam * (st[1][2] / st[1][1])
    o_ref[...] = (_rms_rows(o, gout_ref[...]) * scal_ref[1]).astype(o_ref.dtype)


def _attn_a_prompt(P, scal, bias, gout, *, B, T, tq):
    H = bias.shape[0]
    nq = T // tq
    return pl.pallas_call(
        functools.partial(_attn_a_body, tq=tq),
        grid=(B, H, nq),
        in_specs=[pl.BlockSpec(memory_space=pltpu.SMEM),
                  pl.BlockSpec((tq, LANE), lambda b, h, i: (b * nq + i, C_AQ + h)),
                  pl.BlockSpec((T, LANE), lambda b, h, i: (b, C_AK + h)),
                  pl.BlockSpec((T, LANE), lambda b, h, i: (b, C_AV + h)),
                  pl.BlockSpec((1, 2, 2, tq, tq), lambda b, h, i: (h, 0, 0, 0, 0)),
                  pl.BlockSpec((1, LANE), lambda b, h, i: (0, 0))],
        out_specs=pl.BlockSpec((tq, LANE), lambda b, h, i: (b * nq + i, h)),
        out_shape=jax.ShapeDtypeStruct((B * T, H * LANE), BF16),
        compiler_params=_cparams(("parallel", "parallel", "arbitrary")),
        name="attn_a_prompt")(scal, P, P, P, bias, gout)


def _rglru_body(*refs, tt, t_real, n_t, has_state, pos0_is_zero):
    if has_state:
        (x_ref, g_ref, h0_ref, buf0_ref, cw_ref, cb_ref, wa_ref, ba_ref, wx_ref, bxb_ref, lam_ref,
         o_ref, hfin_ref, buf_ref, xpad_ref, a_ref, b_ref, hs_ref, hc_ref) = refs
    else:
        (x_ref, g_ref, cw_ref, cb_ref, wa_ref, ba_ref, wx_ref, bxb_ref, lam_ref,
         o_ref, hfin_ref, buf_ref, xpad_ref, a_ref, b_ref, hs_ref, hc_ref) = refs
    ti = pl.program_id(1)
    W = x_ref.shape[1]

    @pl.when(ti == 0)
    def _():
        if has_state:
            xpad_ref[0:SUBLANE, :] = buf0_ref[0]
            hc_ref[...] = jnp.broadcast_to(h0_ref[0], (SUBLANE, W))
        else:
            xpad_ref[0:SUBLANE, :] = jnp.zeros((SUBLANE, W), F32)
            hc_ref[...] = jnp.zeros((SUBLANE, W), F32)

    x = x_ref[...]
    xpad_ref[SUBLANE:SUBLANE + tt, :] = x
    xc = cb_ref[...] + x * cw_ref[CONV_W - 1:CONV_W, :]
    for j in range(CONV_W - 1):
        off = SUBLANE - (CONV_W - 1) + j
        xc = xc + xpad_ref[off:off + tt, :] * cw_ref[j:j + 1, :]
    xcb = xc.astype(BF16)
    r = jax.nn.sigmoid(jnp.dot(xcb, wa_ref[...], preferred_element_type=F32) + ba_ref[...])
    i = jax.nn.sigmoid(jnp.dot(xcb, wx_ref[...], preferred_element_type=F32) + bxb_ref[...])
    nl = -lam_ref[...]
    softplus = jnp.maximum(nl, 0.0) + jnp.log1p(jnp.exp(-jnp.abs(nl)))
    a = jnp.exp(-LRU_C * r * softplus)
    mult = jnp.sqrt(1.0 - a * a)
    if pos0_is_zero:
        rows = lax.broadcasted_iota(I32, (tt, W), 0)
        mult = jnp.where(jnp.logical_and(rows == 0, ti == 0), 1.0, mult)
    a_ref[...] = a
    b_ref[...] = mult * (i * xc)
    rowt = lax.broadcasted_iota(I32, (SUBLANE, W), 0)

    def tile(n, h_prev):
        r0 = pl.multiple_of(n * SUBLANE, SUBLANE)
        at = a_ref[pl.ds(r0, SUBLANE), :]
        bt = b_ref[pl.ds(r0, SUBLANE), :]
        for s in (1, 2, 4):
            keep = rowt >= s
            bt = jnp.where(keep, bt + at * pltpu.roll(bt, s, 0), bt)
            at = jnp.where(keep, at * pltpu.roll(at, s, 0), at)
        ht = bt + at * h_prev
        hs_ref[pl.ds(r0, SUBLANE), :] = ht
        return jnp.broadcast_to(ht[SUBLANE - 1:SUBLANE, :], (SUBLANE, W))

    hc_ref[...] = lax.fori_loop(0, tt // SUBLANE, tile, hc_ref[...])
    o_ref[...] = (hs_ref[...] * jax.nn.gelu(g_ref[...])).astype(o_ref.dtype)
    xpad_ref[0:SUBLANE, :] = xpad_ref[tt:tt + SUBLANE, :]

    @pl.when(ti == n_t - 1)
    def _():
        t_loc = t_real - (n_t - 1) * tt
        hfin_ref[0] = hs_ref[t_loc - 1:t_loc, :]
        buf_ref[0] = x_ref[t_loc - (CONV_W - 1):t_loc, :]


def _rglru(P, state, weights, *, B, T, tt, t_real, out_dtype):
    W = 4 * LANE
    n_t = T // tt
    has_state = state is not None
    xspec = pl.BlockSpec((tt, W), lambda b, i: (b * n_t + i, C_BX // 4))
    gspec = pl.BlockSpec((tt, W), lambda b, i: (b * n_t + i, C_BG // 4))
    full = lambda shape: pl.BlockSpec(shape, lambda b, i: (0,) * len(shape))
    wspecs = [full((CONV_W, W)), full((1, W)), full((W, W)), full((1, W)), full((W, W)),
              full((1, W)), full((1, W))]
    in_specs = [xspec, gspec]
    args = [P, P]
    if has_state:
        in_specs += [pl.BlockSpec((1, 1, W), lambda b, i: (b, 0, 0)),
                     pl.BlockSpec((1, SUBLANE, W), lambda b, i: (b, 0, 0))]
        args += list(state)
    return pl.pallas_call(
        functools.partial(_rglru_body, tt=tt, t_real=t_real, n_t=n_t, has_state=has_state,
                          pos0_is_zero=not has_state),
        grid=(B, n_t),
        in_specs=in_specs + wspecs,
        out_specs=[pl.BlockSpec((tt, W), lambda b, i: (b * n_t + i, 0)),
                   pl.BlockSpec((1, 1, W), lambda b, i: (b, 0, 0)),
                   pl.BlockSpec((1, CONV_W - 1, W), lambda b, i: (b, 0, 0))],
        out_shape=[jax.ShapeDtypeStruct((B * T, W), out_dtype),
                   jax.ShapeDtypeStruct((B, 1, W), F32),
                   jax.ShapeDtypeStruct((B, CONV_W - 1, W), F32)],
        scratch_shapes=[pltpu.VMEM((tt + SUBLANE, W), F32), pltpu.VMEM((tt, W), F32),
                        pltpu.VMEM((tt, W), F32), pltpu.VMEM((tt, W), F32),
                        pltpu.VMEM((SUBLANE, W), F32)],
        compiler_params=_cparams(("parallel", "arbitrary")),
        name="rglru")(*args, *weights)


def _hgrn_body(*refs, tt, ch, sub, t_real, n_t, has_state):
    if has_state:
        q_ref, f_ref, v_ref, g_ref, s0_ref, lb_ref, gain_ref, o_ref, sfin_ref, st_ref = refs
    else:
        q_ref, f_ref, v_ref, g_ref, lb_ref, gain_ref, o_ref, sfin_ref, st_ref = refs
    ti = pl.program_id(1)
    n_h = st_ref.shape[0]
    dk = LANE

    @pl.when(ti == 0)
    def _():
        for h in range(n_h):
            if has_state:
                st_ref[h] = s0_ref[0, h].T
            else:
                st_ref[h] = jnp.zeros((dk, dk), F32)

    rr = lax.broadcasted_iota(I32, (ch, ch), 0)
    cc = lax.broadcasted_iota(I32, (ch, ch), 1)
    tril = (cc <= rr).astype(F32)
    rows = lax.broadcasted_iota(I32, (ch, dk), 0)
    srow = lax.broadcasted_iota(I32, (sub, 1), 0)
    n_sub = ch // sub

    def head_chunk(c, h):
        r0 = pl.multiple_of(c * ch, ch)
        hs = slice(h * dk, (h + 1) * dk)
        lb = lb_ref[h]
        log_lb = jnp.log(lb)
        log_1mlb = jnp.log1p(-lb)
        q = q_ref[pl.ds(r0, ch), hs]
        q = q * jax.nn.sigmoid(q)
        fp = f_ref[pl.ds(r0, ch), hs]
        v = v_ref[pl.ds(r0, ch), hs]
        log_sig = jnp.minimum(fp, 0.0) - jnp.log1p(jnp.exp(-jnp.abs(fp)))
        b = log_1mlb + log_sig
        lf = jnp.maximum(log_lb, b) + jnp.log1p(jnp.exp(-jnp.abs(log_lb - b)))
        kk = (1.0 - lb) * jax.nn.sigmoid(-fp)
        if t_real < tt * n_t:
            live = (rows + (ti * tt + c * ch)) < t_real
            lf = jnp.where(live, lf, 0.0)
            kk = jnp.where(live, kk, 0.0)
        G = jnp.dot(tril, lf, preferred_element_type=F32, precision=lax.Precision.HIGHEST)
        st = st_ref[h]
        vb = v.astype(BF16)
        o = _nt((q * jnp.exp(G)).astype(BF16), st.astype(BF16))
        outs = []
        for i in range(n_sub):
            lo, hi = i * sub, (i + 1) * sub
            qi, Gi, ki, vi = q[lo:hi], G[lo:hi], kk[lo:hi], v[lo:hi]
            oi = o[lo:hi]
            if i > 0:
                R = G[lo - 1:lo]
                qp = (qi * jnp.exp(Gi - R)).astype(BF16)
                kp = (kk[:lo] * jnp.exp(R - G[:lo])).astype(BF16)
                att = _nt(qp, kp)
                oi = oi + jnp.dot(att.astype(BF16), vb[:lo], preferred_element_type=F32)
            for s in range(sub):
                w = jnp.exp(jnp.minimum(Gi - Gi[s:s + 1], 0.0))
                colv = jnp.sum(qi * w * ki[s:s + 1], axis=-1, keepdims=True)
                colv = jnp.where(srow >= s, colv, 0.0)
                oi = oi + colv * vi[s:s + 1]
            outs.append(oi)
        o = jnp.concatenate(outs, axis=0) if n_sub > 1 else outs[0]
        gl = G[ch - 1:ch]
        kpp = (kk * jnp.exp(gl - G)).astype(BF16)
        st_ref[h] = st * jnp.exp(gl) + _tn(vb, kpp)
        gate = g_ref[pl.ds(r0, ch), hs]
        o = _rms_rows(o, gain_ref[...]) * (gate * jax.nn.sigmoid(gate))
        o_ref[pl.ds(r0, ch), hs] = o.astype(o_ref.dtype)

    def chunk(c, carry):
        for h in range(n_h):
            head_chunk(c, h)
        return carry

    lax.fori_loop(0, tt // ch, chunk, 0)

    @pl.when(ti == n_t - 1)
    def _():
        for h in range(n_h):
            sfin_ref[0, h] = st_ref[h].T


def _hgrn(P, s0, lb, gain, *, B, T, tt, ch, sub, t_real, out_dtype):
    H = lb.shape[0]
    n_t = T // tt
    has_state = s0 is not None

    W = H * LANE

    def col(c0):
        return pl.BlockSpec((tt, W), lambda b, i: (b * n_t + i, c0 // H))

    in_specs = [col(C_CQ), col(C_CF), col(C_CI), col(C_CG)]
    args = [P, P, P, P]
    if has_state:
        in_specs.append(pl.BlockSpec((1, H, LANE, LANE), lambda b, i: (b, 0, 0, 0)))
        args.append(s0)
    in_specs += [pl.BlockSpec((H, 1, LANE), lambda b, i: (0, 0, 0)),
                 pl.BlockSpec((1, LANE), lambda b, i: (0, 0))]
    return pl.pallas_call(
        functools.partial(_hgrn_body, tt=tt, ch=ch, sub=sub, t_real=t_real, n_t=n_t,
                          has_state=has_state),
        grid=(B, n_t),
        in_specs=in_specs,
        out_specs=[pl.BlockSpec((tt, W), lambda b, i: (b * n_t + i, 0)),
                   pl.BlockSpec((1, H, LANE, LANE), lambda b, i: (b, 0, 0, 0))],
        out_shape=[jax.ShapeDtypeStruct((B * T, W), out_dtype),
                   jax.ShapeDtypeStruct((B, H, LANE, LANE), F32)],
        scratch_shapes=[pltpu.VMEM((H, LANE, LANE), F32)],
        compiler_params=_cparams(("parallel", "arbitrary")),
        name="hgrn2")(*args, lb, gain)


def _score_key(score):
    score = jnp.where(score == 0.0, 0.0, score)
    bits = pltpu.bitcast(score, I32)
    return bits ^ ((bits >> 31) & 0x7FFFFFFF)


def _kth_largest(count_ge, shape, k, bits=32):
    zero = jnp.zeros(shape, I32)
    v = jnp.where(count_ge(zero) >= k, zero, jnp.full(shape, -(2 ** (bits - 1)), I32))

    def bit_step(n, v):
        cand = v + (jnp.int32(1) << (bits - 2 - n))
        return jnp.where(count_ge(cand) >= k, cand, v)

    return lax.fori_loop(0, bits - 1, bit_step, v)


def _tie_bound(count_tie_below, r, shape, n_bits):
    def bit_step(n, pos):
        cand = pos + (jnp.int32(1) << (n_bits - 1 - n))
        return jnp.where(count_tie_below(cand) < r, cand, pos)

    return lax.fori_loop(0, n_bits, bit_step, jnp.zeros(shape, I32))


def _dsa_body(dq_ref, iq_ref, tq_ref, dk_ref, dv_ref, ik_ref, bias_ref, o_ref,
              key_ref, hi_ref, lo_ref, mask_ref, pos_ref, kd_ref, vb_ref, ikb_ref,
              s_ref, m_ref, l_ref, acc_ref, *, tq, n_top, n_bits, tile_grp):
    qb = pl.program_id(1)
    n_c = qb + 1
    hd = HEAD_DIM
    n_kv = dk_ref.shape[1] // hd
    n_q = dq_ref.shape[1] // hd
    grp = n_q // n_kv
    n_pair = n_q // 2
    row = lax.broadcasted_iota(I32, (tq, LANE), 0)
    col = lax.broadcasted_iota(I32, (tq, LANE), 1)
    low_half = col < hd

    @pl.when(qb == 0)
    def _():
        kd_ref[...] = jnp.zeros(kd_ref.shape, BF16)
        vb_ref[...] = jnp.zeros(vb_ref.shape, BF16)
        ikb_ref[...] = jnp.zeros(ikb_ref.shape, BF16)

    def pair_blocks(at_low, at_high):
        return jnp.concatenate([jnp.where(low_half, at_low, 0.0),
                                jnp.where(low_half, 0.0, at_high)], axis=0).astype(BF16)

    r_q = pl.multiple_of(qb * LANE, LANE)
    kx = dk_ref[pl.ds(r_q, LANE), :]
    kx_sw = pltpu.roll(kx, hd, 1)
    kd_ref[0, qb] = pair_blocks(kx, kx_sw)
    kd_ref[1, qb] = pair_blocks(kx_sw, kx)
    ikb_ref[pl.ds(r_q, LANE), :] = ik_ref[pl.ds(r_q, LANE), :].astype(BF16)
    vb_ref[pl.ds(r_q, LANE), :] = dv_ref[pl.ds(r_q, LANE), :].astype(BF16)

    iq = iq_ref[...]
    iqz = []
    for p in range(IDX_HEADS // 2):
        pair = iq[:, p * LANE:(p + 1) * LANE]
        iqz.append(jnp.concatenate([jnp.where(low_half, pair, 0.0),
                                    jnp.where(low_half, pltpu.roll(pair, hd, 1), 0.0)],
                                   axis=0).astype(BF16))
    wscale = IDX_HEADS ** -0.5 * hd ** -0.5
    tail_t = tq_ref[...].T
    w_rows = [tail_t[IW_OFF + h:IW_OFF + h + 1, :] * wscale for h in range(IDX_HEADS)]

    n_grp = (n_c + tile_grp - 1) // tile_grp

    def score_tile(c):
        r0 = pl.multiple_of(c * LANE, LANE)
        ikc = ikb_ref[pl.ds(r0, LANE), :]
        sc = None
        for p in range(IDX_HEADS // 2):
            s = jnp.maximum(_nt(ikc, iqz[p]), 0.0)
            t = s[:, :LANE] * w_rows[2 * p] + s[:, LANE:] * w_rows[2 * p + 1]
            sc = t if sc is None else sc + t
        causal = jnp.logical_or(c < qb, jnp.logical_and(c == qb, row <= col))
        key = _score_key(jnp.where(causal, sc, NEG_INF))
        key_ref[c] = key
        hi_ref[c] = (key >> 16).astype(jnp.int16)
        lo_ref[c] = (((key ^ 0x8000) << 16) >> 16).astype(jnp.int16)

    def score_group(i, carry):
        for j in range(tile_grp):
            score_tile(i * tile_grp + j)
        return carry

    lax.fori_loop(0, n_grp, score_group, 0)
    vec = (1, tq)

    def count(pred, ref=key_ref, dtype=I32):
        def body(i, acc):
            for j in range(tile_grp):
                c = i * tile_grp + j
                acc = acc + pred(ref[c], c).astype(dtype)
            return acc
        acc = lax.fori_loop(0, n_grp, body, jnp.zeros((LANE, tq), dtype))
        return jnp.sum(acc.astype(I32), axis=0, keepdims=True)

    def count16_ge(ref):
        def f(cand):
            cb = jnp.broadcast_to(cand, (LANE, tq)).astype(jnp.int16)
            return count(lambda key, c: key >= cb, ref, jnp.int16)
        return f

    min16 = -(2 ** 15)
    t_hi = _kth_largest(count16_ge(hi_ref), vec, n_top, bits=16)
    t_hi_b = jnp.broadcast_to(t_hi, (LANE, tq)).astype(jnp.int16)
    k_lo = n_top - count(lambda key, c: key > t_hi_b, hi_ref, jnp.int16)

    def band_group(i, carry):
        for j in range(tile_grp):
            c = i * tile_grp + j
            lo_ref[c] = jnp.where(hi_ref[c] == t_hi_b, lo_ref[c], jnp.int16(min16))
        return carry

    lax.fori_loop(0, n_grp, band_group, 0)
    t_lo = _kth_largest(count16_ge(lo_ref), vec, k_lo, bits=16)
    thr = (t_hi << 16) | ((t_lo ^ min16) & 0xFFFF)
    thr_b = jnp.broadcast_to(thr, (LANE, tq))
    n_gt = count(lambda key, c: key > thr_b)
    n_ge = count(lambda key, c: key >= thr_b)
    need = jnp.logical_and(n_ge > n_top, thr > KEY_NEG_INF)
    pos_ref[...] = jnp.full((LANE, tq), 2 ** n_bits, I32)

    @pl.when(jnp.max(need.astype(I32)) > 0)
    def _():
        r = n_top - n_gt

        def count_tie_below(cand):
            cb = jnp.broadcast_to(cand, (LANE, tq))
            return count(lambda key, c: jnp.logical_and(key == thr_b, row + c * LANE < cb))

        pos = _tie_bound(count_tie_below, r, vec, n_bits)
        pos_ref[...] = jnp.broadcast_to(pos, (LANE, tq))

    pos_b = pos_ref[...]

    def mask_group(i, carry):
        for j in range(tile_grp):
            c = i * tile_grp + j
            key = key_ref[c]
            idx = row + c * LANE
            sel = jnp.logical_or(key > thr_b, jnp.logical_and(key == thr_b, idx <= pos_b))
            sel = jnp.logical_and(sel, key > KEY_NEG_INF)
            mask_ref[c] = jnp.where(sel, 0.0, NEG_INF).T
        return carry

    lax.fori_loop(0, n_grp, mask_group, 0)

    dq = dq_ref[...].astype(BF16)

    def max_pass(i, carry):
        for p in range(n_pair):
            g = (2 * p) // grp
            t = []
            for j in range(tile_grp):
                c = i * tile_grp + j
                mk = mask_ref[c]
                s = (_nt(dq[:, p * LANE:(p + 1) * LANE], kd_ref[g, c])
                     + jnp.concatenate([mk, mk], axis=1) + bias_ref[p, jnp.clip(qb - c, 0, 2)])
                s_ref[p, c] = s
                t.append(s)
            while len(t) > 1:
                t = [jnp.maximum(a, b) for a, b in zip(t[::2], t[1::2])]
            m_ref[p] = jnp.maximum(m_ref[p], t[0])
        return carry

    m_ref[...] = jnp.full(m_ref.shape, NEG_INF, F32)
    lax.fori_loop(0, n_grp, max_pass, 0)
    for p in range(n_pair):
        m = m_ref[p]
        m_ref[p] = jnp.concatenate(
            [jnp.broadcast_to(jnp.max(m[:, :LANE], axis=-1, keepdims=True), (tq, LANE)),
             jnp.broadcast_to(jnp.max(m[:, LANE:], axis=-1, keepdims=True), (tq, LANE))], axis=1)
    l_ref[...] = jnp.zeros(l_ref.shape, F32)
    acc_ref[...] = jnp.zeros(acc_ref.shape, F32)

    def sum_pass(i, carry):
        r0 = pl.multiple_of(i * tile_grp * LANE, tile_grp * LANE)
        vt = vb_ref[pl.ds(r0, tile_grp * LANE), :]
        for p in range(n_pair):
            e = [jnp.exp(s_ref[p, i * tile_grp + j] - m_ref[p]) for j in range(tile_grp)]
            l_ref[p] += functools.reduce(lambda a, b: a + b, e)
            pb = jnp.concatenate(
                [jnp.concatenate([x[:, :LANE], x[:, LANE:]], axis=0).astype(BF16) for x in e], axis=1)
            acc_ref[p] += jnp.dot(pb, vt, preferred_element_type=F32)
        return carry

    lax.fori_loop(0, n_grp, sum_pass, 0)
    for p in range(n_pair):
        g = (2 * p) // grp
        l = l_ref[p]
        acc = acc_ref[p]
        oa = acc[:tq] / jnp.sum(l[:, :LANE], axis=-1, keepdims=True)
        ob = acc[tq:] / jnp.sum(l[:, LANE:], axis=-1, keepdims=True)
        if g == 0:
            ob = pltpu.roll(ob, hd, 1)
        else:
            oa = pltpu.roll(oa, hd, 1)
        o_ref[:, p * LANE:(p + 1) * LANE] = jnp.where(low_half, oa, ob).astype(o_ref.dtype)


def _dsa_prompt(P, bias, *, B, T, tq, n_top):
    assert tq == LANE
    nq = T // tq
    n_bits = max(1, int(math.ceil(math.log2(T))))
    tile_grp = math.gcd(nq, 4)
    n_pair = bias.shape[0]
    n_tiles = T // LANE
    return pl.pallas_call(
        functools.partial(_dsa_body, tq=tq, n_top=n_top, n_bits=n_bits, tile_grp=tile_grp),
        grid=(B, nq),
        in_specs=[pl.BlockSpec((tq, 4 * LANE), lambda b, i: (b * nq + i, C_DQ // 4)),
                  pl.BlockSpec((tq, 4 * LANE), lambda b, i: (b * nq + i, C_IQ // 4)),
                  pl.BlockSpec((tq, LANE), lambda b, i: (b * nq + i, C_TAIL)),
                  pl.BlockSpec((T, LANE), lambda b, i: (b, C_DK)),
                  pl.BlockSpec((T, LANE), lambda b, i: (b, C_DV)),
                  pl.BlockSpec((T, LANE), lambda b, i: (b, C_TAIL)),
                  pl.BlockSpec(bias.shape, lambda b, i: (0, 0, 0, 0))],
        out_specs=pl.BlockSpec((tq, 4 * LANE), lambda b, i: (b * nq + i, 0)),
        out_shape=jax.ShapeDtypeStruct((B * T, 4 * LANE), BF16),
        scratch_shapes=[pltpu.VMEM((n_tiles, tq, LANE), I32),
                        pltpu.VMEM((n_tiles, tq, LANE), jnp.int16),
                        pltpu.VMEM((n_tiles, tq, LANE), jnp.int16),
                        pltpu.VMEM((n_tiles, tq, LANE), F32),
                        pltpu.VMEM((tq, LANE), I32),
                        pltpu.VMEM((2, n_tiles, 2 * LANE, LANE), BF16),
                        pltpu.VMEM((T, LANE), BF16),
                        pltpu.VMEM((T, LANE), BF16),
                        pltpu.VMEM((n_pair, n_tiles, tq, 2 * LANE), F32),
                        pltpu.VMEM((n_pair, tq, 2 * LANE), F32),
                        pltpu.VMEM((n_pair, tq, 2 * LANE), F32),
                        pltpu.VMEM((n_pair, 2 * tq, LANE), F32)],
        compiler_params=_cparams(("parallel", "arbitrary")),
        name="dsa_prompt")(P, P, P, P, P, P, bias)


def _attn_a_dec_body(pt_ref, scal_ref, q_ref, kn_ref, vn_ref, bl_ref, bn_ref, gout_ref, *rest,
                     pps, n_steps):
    k_refs = rest[:pps]
    v_refs = rest[pps:2 * pps]
    o_ref, m_ref, l_ref, acc_ref = rest[2 * pps:]
    s_i = pl.program_id(1)
    n_h = q_ref.shape[1] // LANE
    rows_h = 2 * T_PAD

    @pl.when(s_i == 0)
    def _():
        m_ref[...] = jnp.full(m_ref.shape, NEG_INF, F32)
        l_ref[...] = jnp.zeros(l_ref.shape, F32)
        acc_ref[...] = jnp.zeros(acc_ref.shape, F32)

    q = q_ref[...]
    lane = lax.broadcasted_iota(I32, (T_PAD, LANE), 1)

    def q_head(h):
        qh = q[:, h * LANE:(h + 1) * LANE]
        return jnp.concatenate([jnp.where(lane < HEAD_DIM, qh, 0.0),
                                jnp.where(lane >= HEAD_DIM, qh, 0.0)], axis=0).astype(BF16)

    qs = [q_head(h) for h in range(n_h)]
    is_last = s_i == n_steps - 1

    def attend(h, k_tiles, v_tiles, bias_tiles):
        rs = slice(h * rows_h, (h + 1) * rows_h)
        parts = []
        for kt, bt in zip(k_tiles, bias_tiles):
            s = _nt(qs[h], kt)
            if bt is not None:
                s = s + bt[rs]
            parts.append(s)
        s = jnp.concatenate(parts, axis=1) if len(parts) > 1 else parts[0]
        vh = jnp.concatenate(v_tiles, axis=0) if len(v_tiles) > 1 else v_tiles[0]
        m, l, acc = _softmax_update((m_ref[rs], l_ref[rs], acc_ref[rs]), s, vh)
        m_ref[rs] = m
        l_ref[rs] = l
        acc_ref[rs] = acc

    last_bias = jnp.where(is_last, bl_ref[...], 0.0)
    for h in range(n_h):
        head_rows = pl.ds(h, PAGE, stride=n_h)
        attend(h, [r[head_rows, :].astype(BF16) for r in k_refs],
               [r[head_rows, :].astype(BF16) for r in v_refs], [None] * (pps - 1) + [last_bias])

    @pl.when(is_last)
    def _():
        pad = jnp.zeros((LANE - T_PAD, LANE), BF16)
        for h in range(n_h):
            hs = slice(h * LANE, (h + 1) * LANE)
            kn = jnp.concatenate([kn_ref[:, hs].astype(BF16), pad], axis=0)
            vn = jnp.concatenate([vn_ref[:, hs].astype(BF16), pad], axis=0)
            attend(h, [kn], [vn], [bn_ref[...]])
        lam = scal_ref[0]
        for h in range(n_h):
            r1 = slice(h * rows_h, h * rows_h + T_PAD)
            r2 = slice(h * rows_h + T_PAD, (h + 1) * rows_h)
            o = acc_ref[r1] / l_ref[r1] - lam * (acc_ref[r2] / l_ref[r2])
            o_ref[:, h * LANE:(h + 1) * LANE] = _rms_rows(o, gout_ref[...]) * scal_ref[1]


def _attn_a_decode(P, cache_k, cache_v, page_table, layer, scal, bias_last, bias_new, gout, *, B, pps):
    n_pages = page_table.shape[1]
    n_steps = n_pages // pps
    W = 4 * LANE

    def page_spec(i):
        return pl.BlockSpec((None, None) + cache_k.shape[2:],
                            lambda b, s, pt: (pt[b, s * pps + i], layer, 0, 0))

    grid_spec = pltpu.PrefetchScalarGridSpec(
        num_scalar_prefetch=1,
        grid=(B, n_steps),
        in_specs=[pl.BlockSpec(memory_space=pltpu.SMEM),
                  pl.BlockSpec((T_PAD, W), lambda b, s, pt: (b, C_AQ // 4)),
                  pl.BlockSpec((T_PAD, W), lambda b, s, pt: (b, C_AK // 4)),
                  pl.BlockSpec((T_PAD, W), lambda b, s, pt: (b, C_AV // 4)),
                  pl.BlockSpec(bias_last.shape, lambda b, s, pt: (0, 0)),
                  pl.BlockSpec(bias_new.shape, lambda b, s, pt: (0, 0)),
                  pl.BlockSpec((1, LANE), lambda b, s, pt: (0, 0))]
        + [page_spec(i) for i in range(pps)] + [page_spec(i) for i in range(pps)],
        out_specs=pl.BlockSpec((T_PAD, W), lambda b, s, pt: (b, 0)),
        scratch_shapes=[pltpu.VMEM((bias_last.shape[0], 1), F32),
                        pltpu.VMEM((bias_last.shape[0], 1), F32),
                        pltpu.VMEM((bias_last.shape[0], LANE), F32)])
    return pl.pallas_call(
        functools.partial(_attn_a_dec_body, pps=pps, n_steps=n_steps),
        grid_spec=grid_spec,
        out_shape=jax.ShapeDtypeStruct((B * T_PAD, W), F32),
        compiler_params=_cparams(("parallel", "arbitrary")),
        name="attn_a_decode")(page_table, scal, P, P, P, bias_last, bias_new, gout,
                              *([cache_k] * pps), *([cache_v] * pps))


def _dsa_sel_body(pt_ref, iq_ref, tq_ref, ikn_ref, *rest, pps, n_steps, n_top, t_real, n_bits):
    ik_refs = rest[:pps]
    keys_ref, keyn_ref, thr_ref, pos_ref, all_ref, wb_ref = rest[pps:]
    s_i = pl.program_id(1)
    hd = HEAD_DIM
    iq = iq_ref[...].astype(BF16)
    q_idx = jnp.concatenate([iq[:, h * hd:(h + 1) * hd] for h in range(IDX_HEADS)], axis=0)
    wscale = IDX_HEADS ** -0.5 * hd ** -0.5
    tail = tq_ref[...]
    for h in range(IDX_HEADS):
        wb_ref[h * T_PAD:(h + 1) * T_PAD, :] = jnp.broadcast_to(
            tail[:, IW_OFF + h:IW_OFF + h + 1] * wscale, (T_PAD, LANE))

    def score(raw):
        s = jnp.maximum(raw, 0.0) * wb_ref[...]
        sc = s[0:T_PAD]
        for h in range(1, IDX_HEADS):
            sc = sc + s[h * T_PAD:(h + 1) * T_PAD]
        return sc

    for i in range(pps):
        raw = jnp.dot(q_idx, ik_refs[i][...].astype(BF16), preferred_element_type=F32)
        key = _score_key(score(raw))
        keys_ref[0, i] = key
        all_ref[s_i * pps + i] = key

    n_tiles = n_steps * pps + 1

    @pl.when(s_i == n_steps - 1)
    def _():
        row = lax.broadcasted_iota(I32, (T_PAD, LANE), 0)
        col = lax.broadcasted_iota(I32, (T_PAD, LANE), 1)
        pad = jnp.zeros((LANE - T_PAD, hd), F32)
        ikn = jnp.concatenate([ikn_ref[...][:, :hd], pad], axis=0)
        valid = jnp.logical_and(col <= row, col < t_real)
        keyn = _score_key(jnp.where(valid, score(_nt(q_idx, ikn.astype(BF16))), NEG_INF))
        keyn_ref[0] = keyn
        all_ref[n_tiles - 1] = keyn
        idx = (lax.broadcasted_iota(I32, all_ref.shape, 0) * LANE
               + lax.broadcasted_iota(I32, all_ref.shape, 2))

        def count(pred):
            acc = jnp.sum(pred(all_ref[...]).astype(I32), axis=0)
            return jnp.sum(acc, axis=-1, keepdims=True)

        def count_ge(cand):
            cb = jnp.broadcast_to(cand, (T_PAD, LANE))
            return count(lambda key: key >= cb)

        thr = _kth_largest(count_ge, (T_PAD, 1), n_top)
        thr_b = jnp.broadcast_to(thr, (T_PAD, LANE))
        n_gt = count(lambda key: key > thr_b)
        r = n_top - n_gt

        def count_tie_below(cand):
            cb = jnp.broadcast_to(cand, (T_PAD, LANE))
            return count(lambda key: jnp.logical_and(key == thr_b, idx < cb))

        pos = _tie_bound(count_tie_below, r, (T_PAD, 1), n_bits)
        thr_ref[0] = thr_b
        pos_ref[0] = jnp.broadcast_to(pos, (T_PAD, LANE))


def _dsa_select_decode(P, cache_idx, page_table, layer, *, B, pps, n_top, t_real):
    n_pages = page_table.shape[1]
    n_steps = n_pages // pps
    n_tiles = n_pages + 1
    n_bits = int(math.ceil(math.log2(n_tiles * LANE)))

    def page_spec(i):
        return pl.BlockSpec((None, None) + cache_idx.shape[2:],
                            lambda b, s, pt: (pt[b, s * pps + i], layer, 0, 0))

    tile_spec = pl.BlockSpec((1, T_PAD, LANE), lambda b, s, pt: (b, 0, 0))
    grid_spec = pltpu.PrefetchScalarGridSpec(
        num_scalar_prefetch=1,
        grid=(B, n_steps),
        in_specs=[pl.BlockSpec((T_PAD, 4 * LANE), lambda b, s, pt: (b, C_IQ // 4)),
                  pl.BlockSpec((T_PAD, LANE), lambda b, s, pt: (b, C_TAIL)),
                  pl.BlockSpec((T_PAD, LANE), lambda b, s, pt: (b, C_TAIL))]
        + [page_spec(i) for i in range(pps)],
        out_specs=[pl.BlockSpec((1, pps, T_PAD, LANE), lambda b, s, pt: (b, s, 0, 0)),
                   tile_spec, tile_spec, tile_spec],
        scratch_shapes=[pltpu.VMEM((n_tiles, T_PAD, LANE), I32),
                        pltpu.VMEM((IDX_HEADS * T_PAD, LANE), F32)])
    tile_shape = jax.ShapeDtypeStruct((B, T_PAD, LANE), I32)
    return pl.pallas_call(
        functools.partial(_dsa_sel_body, pps=pps, n_steps=n_steps, n_top=n_top, t_real=t_real,
                          n_bits=n_bits),
        grid_spec=grid_spec,
        out_shape=[jax.ShapeDtypeStruct((B, n_pages, T_PAD, LANE), I32),
                   tile_shape, tile_shape, tile_shape],
        compiler_params=_cparams(("parallel", "arbitrary")),
        name="dsa_select_decode")(page_table, P, P, P, *([cache_idx] * pps))


def _dsa_att_body(pt_ref, dq_ref, kn_ref, vn_ref, keys_ref, keyn_ref, thr_ref, pos_ref,
                  bl_ref, bn_ref, *rest, pps, n_steps):
    k_refs = rest[:pps]
    v_refs = rest[pps:2 * pps]
    o_ref, m_ref, l_ref, acc_ref = rest[2 * pps:]
    s_i = pl.program_id(1)
    hd = HEAD_DIM
    n_kv = kn_ref.shape[1] // hd
    n_q = dq_ref.shape[1] // hd
    grp = n_q // n_kv
    rows_g = grp * T_PAD

    @pl.when(s_i == 0)
    def _():
        m_ref[...] = jnp.full(m_ref.shape, -1e30, F32)
        l_ref[...] = jnp.zeros(l_ref.shape, F32)
        acc_ref[...] = jnp.zeros(acc_ref.shape, F32)

    dq = dq_ref[...].astype(BF16)
    q_g = [jnp.concatenate([dq[:, (g * grp + j) * hd:(g * grp + j + 1) * hd] for j in range(grp)],
                           axis=0) for g in range(n_kv)]
    thr = thr_ref[0]
    pos = pos_ref[0]
    col = lax.broadcasted_iota(I32, (T_PAD, LANE), 1)
    is_last = s_i == n_steps - 1

    def sel_mask(key, tile):
        idx = col + tile * LANE
        sel = jnp.logical_or(key > thr, jnp.logical_and(key == thr, idx <= pos))
        sel = jnp.logical_and(sel, key > KEY_NEG_INF)
        return jnp.concatenate([sel] * grp, axis=0)

    def attend(g, logit_tiles, masks, bias_tiles, pv):
        rs = slice(g * rows_g, (g + 1) * rows_g)
        parts = []
        for s, mk, bt in zip(logit_tiles, masks, bias_tiles):
            if bt is not None:
                s = s + bt[rs]
            parts.append(jnp.where(mk, s, NEG_INF))
        s = jnp.concatenate(parts, axis=1) if len(parts) > 1 else parts[0]
        m, l, acc = m_ref[rs], l_ref[rs], acc_ref[rs]
        m_new = jnp.maximum(m, jnp.max(s, axis=-1, keepdims=True))
        alpha = jnp.exp(m - m_new)
        p = jnp.exp(s - m_new)
        m_ref[rs] = m_new
        l_ref[rs] = alpha * l + jnp.sum(p, axis=-1, keepdims=True)
        acc_ref[rs] = alpha * acc + pv(p.astype(BF16))

    masks = [sel_mask(keys_ref[0, i], s_i * pps + i) for i in range(pps)]
    last_bias = jnp.where(is_last, bl_ref[...], 0.0)
    for g in range(n_kv):
        vt = jnp.concatenate([r[g].astype(BF16) for r in v_refs], axis=1)
        attend(g, [jnp.dot(q_g[g], r[g].astype(BF16), preferred_element_type=F32) for r in k_refs],
               masks, [None] * (pps - 1) + [last_bias], lambda p, vt=vt: _nt(p, vt))

    @pl.when(is_last)
    def _():
        pad = jnp.zeros((LANE - T_PAD, hd), BF16)
        mask_new = sel_mask(keyn_ref[0], n_steps * pps)
        for g in range(n_kv):
            hs = slice(g * hd, (g + 1) * hd)
            kn = jnp.concatenate([kn_ref[:, hs].astype(BF16), pad], axis=0)
            vn = jnp.concatenate([vn_ref[:, hs].astype(BF16), pad], axis=0)
            attend(g, [_nt(q_g[g], kn)], [mask_new], [bn_ref[...]],
                   lambda p, vn=vn: jnp.dot(p, vn, preferred_element_type=F32))
        for g in range(n_kv):
            rs = slice(g * rows_g, (g + 1) * rows_g)
            og = acc_ref[rs] / l_ref[rs]
            for j in range(grp):
                h = g * grp + j
                o_ref[:, h * hd:(h + 1) * hd] = og[j * T_PAD:(j + 1) * T_PAD]


def _dsa_attend_decode(P, cache_k, cache_v, page_table, layer, keys, keyn, thr, pos,
                       bias_last, bias_new, *, B, pps):
    n_pages = page_table.shape[1]
    n_steps = n_pages // pps
    rows = bias_last.shape[0]

    def page_spec(i):
        return pl.BlockSpec((None, None) + cache_k.shape[2:],
                            lambda b, s, pt: (pt[b, s * pps + i], layer, 0, 0, 0))

    tile_spec = pl.BlockSpec((1, T_PAD, LANE), lambda b, s, pt: (b, 0, 0))
    grid_spec = pltpu.PrefetchScalarGridSpec(
        num_scalar_prefetch=1,
        grid=(B, n_steps),
        in_specs=[pl.BlockSpec((T_PAD, 4 * LANE), lambda b, s, pt: (b, C_DQ // 4)),
                  pl.BlockSpec((T_PAD, LANE), lambda b, s, pt: (b, C_DK)),
                  pl.BlockSpec((T_PAD, LANE), lambda b, s, pt: (b, C_DV)),
                  pl.BlockSpec((1, pps, T_PAD, LANE), lambda b, s, pt: (b, s, 0, 0)),
                  tile_spec, tile_spec, tile_spec,
                  pl.BlockSpec(bias_last.shape, lambda b, s, pt: (0, 0)),
                  pl.BlockSpec(bias_new.shape, lambda b, s, pt: (0, 0))]
        + [page_spec(i) for i in range(pps)] + [page_spec(i) for i in range(pps)],
        out_specs=pl.BlockSpec((T_PAD, 4 * LANE), lambda b, s, pt: (b, 0)),
        scratch_shapes=[pltpu.VMEM((rows, 1), F32), pltpu.VMEM((rows, 1), F32),
                        pltpu.VMEM((rows, HEAD_DIM), F32)])
    return pl.pallas_call(
        functools.partial(_dsa_att_body, pps=pps, n_steps=n_steps),
        grid_spec=grid_spec,
        out_shape=jax.ShapeDtypeStruct((B * T_PAD, 4 * LANE), F32),
        compiler_params=_cparams(("parallel", "arbitrary")),
        name="dsa_attend_decode")(page_table, P, P, P, keys, keyn, thr, pos, bias_last, bias_new,
                                  *([cache_k] * pps), *([cache_v] * pps))


def _bias_minus_far(tab, dist):
    onehot = jax.nn.one_hot(_rel_bucket(dist), N_BUCKETS, dtype=F32)
    bias = jnp.einsum("...k,km->...m", onehot, tab, precision=lax.Precision.HIGHEST)
    return bias - tab[N_BUCKETS - 1]


def _toeplitz_tiles(tab, t):
    r = jnp.arange(t)[:, None]
    c = jnp.arange(t)[None, :]
    tiles = jnp.stack([_bias_minus_far(tab, r - c), _bias_minus_far(tab, t + r - c)], axis=0)
    return jnp.transpose(tiles, (3, 0, 1, 2))


def _decode_bias(tab, past, t_real):
    tq = jnp.arange(T_PAD)[:, None]
    kc = jnp.arange(LANE)[None, :]
    last = _bias_minus_far(tab, past + tq - (past - PAGE + kc))
    new = _bias_minus_far(tab, tq - kc)
    valid = jnp.logical_and(kc <= tq, kc < t_real)
    new = jnp.where(valid[..., None], new, NEG_INF)
    return jnp.transpose(last, (2, 0, 1)), jnp.transpose(new, (2, 0, 1))


def _block_diag(w):
    n, d, e = w.shape
    eye = jnp.eye(n, dtype=w.dtype)
    return (eye[:, None, :, None] * w[:, :, None, :]).reshape(n * d, n * e)


def kernel(x_prompt, x_sample, cache_a_k, cache_a_v, cache_d_k, cache_d_v, cache_d_idx, state_b_h, state_b_conv, state_c_s, page_table, rel_bias, ln1, w_in, a_q_norm, a_k_norm, a_lam_q1, a_lam_k1, a_lam_q2, a_lam_k2, a_out_norm, b_conv_w, b_conv_b, b_wa, b_ba, b_wx, b_bx, b_lambda, c_lb_logits, c_out_norm, d_q_norm, d_k_norm, w_out, ln2, w_up, w_down):
    B, T, D = x_prompt.shape
    Bs, Ts, _ = x_sample.shape
    L = w_in.shape[0]
    n_pages = page_table.shape[1]
    past = n_pages * PAGE
    gw = D // 4
    a_heads = gw // (2 * HEAD_DIM)
    c_heads = c_lb_logits.shape[1] // LANE
    d_heads = gw // HEAD_DIM
    d_kv = cache_d_k.shape[3]
    d_grp = d_heads // d_kv
    n_pool = cache_a_k.shape[0]
    assert Ts <= T_PAD - 0 and Ts >= CONV_W - 1 and past > 0

    tab = rel_bias.astype(F32)
    n_a_maps = 2 * a_heads
    tab_a, tab_d = tab[:, :n_a_maps], tab[:, n_a_maps:]
    tq_a = min(256, T)
    tq_d = min(128, T)
    bias_a = _toeplitz_tiles(tab_a, tq_a).reshape(a_heads, 2, 2, tq_a, tq_a)
    bias_d = jnp.concatenate([_toeplitz_tiles(tab_d, tq_d),
                              jnp.zeros((d_heads, 1, tq_d, tq_d), F32)], axis=1)
    bias_d = jnp.transpose(bias_d.reshape(d_heads // 2, 2, 3, tq_d, tq_d),
                           (0, 2, 3, 1, 4)).reshape(d_heads // 2, 3, tq_d, 2 * tq_d)
    bl_a, bn_a = _decode_bias(tab_a, past, Ts)
    bl_a = bl_a.reshape(n_a_maps * T_PAD, LANE)
    bn_a = bn_a.reshape(n_a_maps * T_PAD, LANE)
    bl_d, bn_d = _decode_bias(tab_d, past, Ts)
    bl_d = bl_d.reshape(d_heads * T_PAD, LANE)
    bn_d = bn_d.reshape(d_heads * T_PAD, LANE)

    lb_cum = jnp.cumsum(jax.nn.softmax(c_lb_logits.astype(F32), axis=0), axis=0)
    lb_all = (lb_cum - lb_cum[0]).reshape(L, c_heads, 1, LANE)

    seg = jnp.kron(jnp.eye(LANE // HEAD_DIM, dtype=F32),
                   jnp.full((HEAD_DIM, HEAD_DIM), 1.0 / HEAD_DIM, F32)).astype(BF16)
    qscale = HEAD_DIM ** -0.5
    zeros = lambda n: jnp.zeros((n,), F32)
    ones = lambda n: jnp.ones((n,), F32)
    flag = jnp.concatenate([ones(2 * gw), zeros(7 * gw), ones(gw), zeros(gw), ones(LANE),
                            zeros(P_WIDTH - 11 * gw - LANE)]).reshape(1, P_WIDTH)
    norm_tiles = (0, 1, 9, 11)

    ca_k = cache_a_k.reshape(n_pool, L, PAGE * a_heads, 2 * HEAD_DIM)
    ca_v = cache_a_v.reshape(n_pool, L, PAGE * a_heads, 2 * HEAD_DIM)
    cd_k = jnp.transpose(cache_d_k, (0, 1, 3, 4, 2))
    cd_v = jnp.transpose(cache_d_v, (0, 1, 3, 4, 2))
    cd_idx = jnp.transpose(cache_d_idx, (0, 1, 3, 2))

    xp = x_prompt.reshape(B * T, D)
    xs = jnp.pad(x_sample, ((0, 0), (0, T_PAD - Ts), (0, 0))).reshape(Bs * T_PAD, D)
    s_conv = jnp.pad(state_b_conv, ((0, 0), (0, 0), (SUBLANE - (CONV_W - 1), 0), (0, 0)))

    n_top_p = min(TOPK_MAX, T // 4)
    n_top_s = min(TOPK_MAX, (past + Ts) // 4)
    tm_p = min(1024, B * T)
    tm_f = min(1024, B * T)
    tt_b = min(256, T)
    tt_c = min(512, T)
    ch_c = math.gcd(T, 64)
    pps_a = math.gcd(n_pages, 8)
    pps_d = math.gcd(n_pages, 16)

    outs_p, outs_s = [], []
    for l in range(L):
        wl = w_in[l]
        n_head = 10 * gw
        w_perm = jnp.concatenate(
            [wl[:, :n_head], wl[:, n_head + 2 * LANE:n_head + 2 * LANE + gw],
             wl[:, n_head:n_head + 2 * LANE], wl[:, n_head + 2 * LANE + gw:],
             jnp.zeros((D, P_WIDTH - wl.shape[1]), wl.dtype)], axis=1).astype(BF16)
        gain = jnp.concatenate(
            [jnp.tile(a_q_norm[l], 2 * a_heads) * qscale, jnp.tile(a_k_norm[l], 2 * a_heads),
             ones(7 * gw), jnp.tile(d_q_norm[l], d_heads) * qscale, ones(gw),
             jnp.tile(d_k_norm[l], d_kv), ones(P_WIDTH - 11 * gw - LANE)]).reshape(1, P_WIDTH)
        g1 = ln1[l].reshape(1, D)
        g2 = ln2[l].reshape(1, D)
        w_o = w_out[l].astype(BF16)
        w_u = w_up[l].astype(BF16)
        w_d = w_down[l].astype(BF16)
        lam_init = 0.8 - 0.6 * math.exp(-0.3 * l)
        lam = (jnp.exp(jnp.sum(a_lam_q1[l].astype(F32) * a_lam_k1[l].astype(F32)))
               - jnp.exp(jnp.sum(a_lam_q2[l].astype(F32) * a_lam_k2[l].astype(F32))) + lam_init)
        scal = jnp.stack([lam, jnp.asarray(1.0 - lam_init, F32)]).astype(F32)
        g_a = a_out_norm[l].reshape(1, LANE)
        g_c = c_out_norm[l].reshape(1, LANE)
        b_weights = (b_conv_w[l], b_conv_b[l].reshape(1, gw), _block_diag(b_wa[l]).astype(BF16),
                     b_ba[l].reshape(1, gw), _block_diag(b_wx[l]).astype(BF16),
                     b_bx[l].reshape(1, gw), b_lambda[l].reshape(1, gw))

        def dense_tail(x, mixes, tm, tmf):
            x1 = _out_proj(mixes, w_o, x, tm=tm, tn=min(512, D))
            return _ffn(x1, g2, w_u, w_d, tm=tmf, tf=min(512, w_u.shape[1]))

        Pp = _in_proj(xp, g1, w_perm, gain, flag, seg, tm=tm_p, tn=4 * LANE, norm_tiles=norm_tiles)
        mix_a = _attn_a_prompt(Pp, scal, bias_a, g_a, B=B, T=T, tq=tq_a)
        mix_b, hfin, buf = _rglru(Pp, None, b_weights, B=B, T=T, tt=tt_b, t_real=T, out_dtype=BF16)
        mix_c, sfin = _hgrn(Pp, None, lb_all[l], g_c, B=B, T=T, tt=tt_c, ch=ch_c,
                            sub=min(16, ch_c), t_real=T, out_dtype=BF16)
        mix_d = _dsa_prompt(Pp, bias_d, B=B, T=T, tq=tq_d, n_top=n_top_p)
        xp = dense_tail(xp, (mix_a, mix_b, mix_c, mix_d), tm_p, tm_f)
        P3 = Pp.reshape(B, T, P_WIDTH)
        outs_p.append((P3[..., C_AK * LANE:C_AV * LANE].reshape(B, T, a_heads, 2 * HEAD_DIM),
                       P3[..., C_AV * LANE:C_BX * LANE].reshape(B, T, a_heads, 2 * HEAD_DIM),
                       P3[..., C_DK * LANE:C_DV * LANE].reshape(B, T, d_kv, HEAD_DIM),
                       P3[..., C_DV * LANE:C_TAIL * LANE].reshape(B, T, d_kv, HEAD_DIM),
                       P3[..., C_TAIL * LANE:C_TAIL * LANE + HEAD_DIM],
                       hfin.reshape(B, gw), buf, sfin))

        Ps = _in_proj(xs, g1, w_perm, gain, flag, seg, tm=Bs * T_PAD, tn=4 * LANE,
                      norm_tiles=norm_tiles)
        smix_a = _attn_a_decode(Ps, ca_k, ca_v, page_table, l, scal, bl_a, bn_a, g_a, B=Bs, pps=pps_a)
        smix_b, shfin, sbuf = _rglru(
            Ps, (state_b_h[:, l].reshape(Bs, 1, gw), s_conv[:, l]), b_weights,
            B=Bs, T=T_PAD, tt=T_PAD, t_real=Ts, out_dtype=F32)
        smix_c, ssfin = _hgrn(Ps, state_c_s[:, l], lb_all[l], g_c, B=Bs, T=T_PAD, tt=T_PAD,
                              ch=T_PAD, sub=T_PAD, t_real=Ts, out_dtype=F32)
        keys, keyn, thr, pos = _dsa_select_decode(Ps, cd_idx, page_table, l, B=Bs, pps=pps_d,
                                                  n_top=n_top_s, t_real=Ts)
        smix_d = _dsa_attend_decode(Ps, cd_k, cd_v, page_table, l, keys, keyn, thr, pos,
                                    bl_d, bn_d, B=Bs, pps=pps_d)
        xs = dense_tail(xs, (smix_a, smix_b, smix_c, smix_d), Bs * T_PAD, Bs * T_PAD)
        S3 = Ps.reshape(Bs, T_PAD, P_WIDTH)[:, :Ts]
        outs_s.append((S3[..., C_AK * LANE:C_AV * LANE].reshape(Bs, Ts, a_heads, 2 * HEAD_DIM),
                       S3[..., C_AV * LANE:C_BX * LANE].reshape(Bs, Ts, a_heads, 2 * HEAD_DIM),
                       S3[..., C_DK * LANE:C_DV * LANE].reshape(Bs, Ts, d_kv, HEAD_DIM),
                       S3[..., C_DV * LANE:C_TAIL * LANE].reshape(Bs, Ts, d_kv, HEAD_DIM),
                       S3[..., C_TAIL * LANE:C_TAIL * LANE + HEAD_DIM],
                       shfin.reshape(Bs, gw), sbuf, ssfin))

    y_prompt = xp.reshape(B, T, D)
    y_sample = xs.reshape(Bs, T_PAD, D)[:, :Ts]
    stack = lambda outs: [jnp.stack(s, axis=1) for s in zip(*outs)]
    return (y_prompt, y_sample, *stack(outs_p), *stack(outs_s))
```

```python
import functools
import math

import jax
import jax.numpy as jnp
import numpy as np
from jax import lax
from jax.experimental import pallas as pl
from jax.experimental.pallas import tpu as pltpu

F32 = jnp.float32
BF16 = jnp.bfloat16
I32 = jnp.int32

EPS = 1e-6
HEAD_DIM = 64
PAGE = 128
CONV_W = 4
LRU_C = 8.0
IDX_HEADS = 8
TOPK_MAX = 256
N_BUCKETS = 32
MAX_DIST = 128
LANE = 128
SUBLANE = 8
T_PAD = 8
INT_MIN = -(2 ** 31)
KEY_NEG_INF = -2139095041
VMEM_LIMIT = 56 * 1024 * 1024
NEG_INF = float("-inf")

C_AQ, C_AK, C_AV, C_BX, C_BG, C_CQ, C_CF, C_CI, C_CG, C_DQ, C_IQ = (4 * i for i in range(11))
C_DK, C_DV, C_TAIL = 44, 45, 46
P_WIDTH = 48 * LANE
IW_OFF = 64


def _cparams(sem):
    return pltpu.CompilerParams(dimension_semantics=sem, vmem_limit_bytes=VMEM_LIMIT)


def _nt(a, b):
    return lax.dot_general(a, b, (((1,), (1,)), ((), ())), preferred_element_type=F32)


def _tn(a, b):
    return lax.dot_general(a, b, (((0,), (0,)), ((), ())), preferred_element_type=F32)


def _rel_bucket(dist):
    n = jnp.maximum(dist, 0)
    exact = N_BUCKETS // 2
    large = exact + (jnp.log(jnp.maximum(n, 1).astype(F32) / exact)
                     / math.log(MAX_DIST / exact) * (N_BUCKETS - exact)).astype(I32)
    return jnp.where(n < exact, n, jnp.minimum(large, N_BUCKETS - 1))


def _in_proj_body(x_ref, g_ref, w_ref, gain_ref, flag_ref, seg_ref, o_ref, xn_ref, *, tn, norm_tiles):
    j = pl.program_id(1)

    @pl.when(j == 0)
    def _():
        x = x_ref[...]
        ms = jnp.mean(x * x, axis=-1, keepdims=True)
        xn_ref[...] = (x * lax.rsqrt(ms + EPS) * g_ref[...]).astype(BF16)

    y = jnp.dot(xn_ref[...], w_ref[...], preferred_element_type=F32)
    is_norm = functools.reduce(jnp.logical_or, [j == t for t in norm_tiles])

    @pl.when(is_norm)
    def _():
        seg = seg_ref[...]
        for c in range(tn // LANE):
            sl = slice(c * LANE, (c + 1) * LANE)
            yc = y[:, sl]
            y2 = yc * yc
            hi = y2.astype(BF16)
            lo = (y2 - hi.astype(F32)).astype(BF16)
            ms = (jnp.dot(hi, seg, preferred_element_type=F32)
                  + jnp.dot(lo, seg, preferred_element_type=F32))
            yn = yc * lax.rsqrt(ms + EPS) * gain_ref[:, sl]
            o_ref[:, sl] = jnp.where(flag_ref[:, sl] > 0, yn, yc)

    @pl.when(jnp.logical_not(is_norm))
    def _():
        o_ref[...] = y


def _in_proj(x, g, w, gain, flag, seg, *, tm, tn, norm_tiles):
    M, K = x.shape
    N = w.shape[1]
    return pl.pallas_call(
        functools.partial(_in_proj_body, tn=tn, norm_tiles=norm_tiles),
        grid=(M // tm, N // tn),
        in_specs=[pl.BlockSpec((tm, K), lambda i, j: (i, 0)),
                  pl.BlockSpec((1, K), lambda i, j: (0, 0)),
                  pl.BlockSpec((K, tn), lambda i, j: (0, j)),
                  pl.BlockSpec((1, tn), lambda i, j: (0, j)),
                  pl.BlockSpec((1, tn), lambda i, j: (0, j)),
                  pl.BlockSpec((LANE, LANE), lambda i, j: (0, 0))],
        out_specs=pl.BlockSpec((tm, tn), lambda i, j: (i, j)),
        out_shape=jax.ShapeDtypeStruct((M, N), F32),
        scratch_shapes=[pltpu.VMEM((tm, K), BF16)],
        compiler_params=_cparams(("parallel", "arbitrary")),
        name="in_proj")(x, g, w, gain, flag, seg)


def _out_proj_body(a_ref, b_ref, c_ref, d_ref, w_ref, x_ref, o_ref, *, gw):
    acc = x_ref[...]
    for g, m_ref in enumerate((a_ref, b_ref, c_ref, d_ref)):
        acc = acc + jnp.dot(m_ref[...].astype(BF16), w_ref[g * gw:(g + 1) * gw, :],
                            preferred_element_type=F32)
    o_ref[...] = acc


def _out_proj(mixes, w, x, *, tm, tn):
    M, D = x.shape
    gw = mixes[0].shape[1]
    mix_spec = pl.BlockSpec((tm, gw), lambda i, j: (i, 0))
    return pl.pallas_call(
        functools.partial(_out_proj_body, gw=gw),
        grid=(M // tm, D // tn),
        in_specs=[mix_spec, mix_spec, mix_spec, mix_spec,
                  pl.BlockSpec((w.shape[0], tn), lambda i, j: (0, j)),
                  pl.BlockSpec((tm, tn), lambda i, j: (i, j))],
        out_specs=pl.BlockSpec((tm, tn), lambda i, j: (i, j)),
        out_shape=jax.ShapeDtypeStruct((M, D), F32),
        compiler_params=_cparams(("parallel", "arbitrary")),
        name="out_proj")(*mixes, w, x)


def _ffn_body(x_ref, g_ref, wu_ref, wd_ref, o_ref, xn_ref):
    f = pl.program_id(1)

    @pl.when(f == 0)
    def _():
        x = x_ref[...]
        ms = jnp.mean(x * x, axis=-1, keepdims=True)
        xn_ref[...] = (x * lax.rsqrt(ms + EPS) * g_ref[...]).astype(BF16)
        o_ref[...] = x

    h = jnp.dot(xn_ref[...], wu_ref[...], preferred_element_type=F32)
    h = jnp.maximum(h, 0.0)
    h = (h * h).astype(BF16)
    o_ref[...] += jnp.dot(h, wd_ref[...], preferred_element_type=F32)


def _ffn(x, g, wu, wd, *, tm, tf):
    M, D = x.shape
    Fd = wu.shape[1]
    return pl.pallas_call(
        _ffn_body,
        grid=(M // tm, Fd // tf),
        in_specs=[pl.BlockSpec((tm, D), lambda i, f: (i, 0)),
                  pl.BlockSpec((1, D), lambda i, f: (0, 0)),
                  pl.BlockSpec((D, tf), lambda i, f: (0, f)),
                  pl.BlockSpec((tf, D), lambda i, f: (f, 0))],
        out_specs=pl.BlockSpec((tm, D), lambda i, f: (i, 0)),
        out_shape=jax.ShapeDtypeStruct((M, D), F32),
        scratch_shapes=[pltpu.VMEM((tm, D), BF16)],
        compiler_params=_cparams(("parallel", "arbitrary")),
        name="ffn")(x, g, wu, wd)


def _softmax_update(state, s, v):
    m, l, acc = state
    m_new = jnp.maximum(m, jnp.max(s, axis=-1, keepdims=True))
    alpha = jnp.exp(m - m_new)
    p = jnp.exp(s - m_new)
    l = alpha * l + jnp.sum(p, axis=-1, keepdims=True)
    acc = alpha * acc + jnp.dot(p.astype(BF16), v, preferred_element_type=F32)
    return m_new, l, acc


def _rms_rows(o, gain):
    ms = jnp.mean(o * o, axis=-1, keepdims=True)
    return o * lax.rsqrt(ms + EPS) * gain


def _pair_attention(n_grp, tile_grp, n_pair, tq, logits, values, s_ref, m_ref, l_ref, acc_ref):
    def max_pass(i, carry):
        for p in range(n_pair):
            t = []
            for j in range(tile_grp):
                c = i * tile_grp + j
                s = logits(p, c)
                s_ref[p, c] = s
                t.append(s)
            while len(t) > 1:
                t = [jnp.maximum(a, b) for a, b in zip(t[::2], t[1::2])]
            m_ref[p] = jnp.maximum(m_ref[p], t[0])
        return carry

    m_ref[...] = jnp.full(m_ref.shape, NEG_INF, F32)
    lax.fori_loop(0, n_grp, max_pass, 0)
    for p in range(n_pair):
        m = m_ref[p]
        m_ref[p] = jnp.concatenate(
            [jnp.broadcast_to(jnp.max(m[:, :LANE], axis=-1, keepdims=True), (tq, LANE)),
             jnp.broadcast_to(jnp.max(m[:, LANE:], axis=-1, keepdims=True), (tq, LANE))], axis=1)
    l_ref[...] = jnp.zeros(l_ref.shape, F32)
    acc_ref[...] = jnp.zeros(acc_ref.shape, F32)

    def sum_pass(i, carry):
        for p in range(n_pair):
            e = [jnp.exp(s_ref[p, i * tile_grp + j] - m_ref[p]) for j in range(tile_grp)]
            l_ref[p] += functools.reduce(lambda a, b: a + b, e)
            pb = jnp.concatenate(
                [jnp.concatenate([x[:, :LANE], x[:, LANE:]], axis=0).astype(BF16) for x in e], axis=1)
            acc_ref[p] += jnp.dot(pb, values(p, i), preferred_element_type=F32)
        return carry

    lax.fori_loop(0, n_grp, sum_pass, 0)
    outs = []
    for p in range(n_pair):
        l = l_ref[p]
        acc = acc_ref[p]
        outs.append((acc[:tq] / jnp.sum(l[:, :LANE], axis=-1, keepdims=True),
                     acc[tq:] / jnp.sum(l[:, LANE:], axis=-1, keepdims=True)))
    return outs


def _attn_a_body(scal_ref, q_ref, k_ref, v_ref, bias_ref, gout_ref, o_ref,
                 kd_ref, vb_ref, s_ref, m_ref, l_ref, acc_ref, *, tq, tile_grp):
    qb = pl.program_id(1)
    n_h = q_ref.shape[1] // LANE
    n_grp = (qb + tile_grp) // tile_grp
    low_half = lax.broadcasted_iota(I32, (tq, LANE), 1) < HEAD_DIM

    @pl.when(qb == 0)
    def _():
        kd_ref[...] = jnp.zeros(kd_ref.shape, BF16)
        vb_ref[...] = jnp.zeros(vb_ref.shape, BF16)

    r_q = pl.multiple_of(qb * LANE, LANE)
    for h in range(n_h):
        kx = k_ref[pl.ds(r_q, LANE), h * LANE:(h + 1) * LANE]
        kd_ref[h, qb] = jnp.concatenate([jnp.where(low_half, kx, 0.0),
                                         jnp.where(low_half, 0.0, kx)], axis=0).astype(BF16)
    vb_ref[pl.ds(r_q, LANE), :] = v_ref[pl.ds(r_q, LANE), :].astype(BF16)
    q = q_ref[...].astype(BF16)

    def logits(h, c):
        return (_nt(q[:, h * LANE:(h + 1) * LANE], kd_ref[h, c])
                + bias_ref[h, jnp.clip(qb - c, -1, 2) + 1])

    def values(h, i):
        r0 = pl.multiple_of(i * tile_grp * LANE, tile_grp * LANE)
        return vb_ref[pl.ds(r0, tile_grp * LANE), h * LANE:(h + 1) * LANE]

    outs = _pair_attention(n_grp, tile_grp, n_h, tq, logits, values, s_ref, m_ref, l_ref, acc_ref)
    lam = scal_ref[0]
    for h, (o1, o2) in enumerate(outs):
        o = _rms_rows(o1 - lam * o2, gout_ref[...]) * scal_ref[1]
        o_ref[:, h * LANE:(h + 1) * LANE] = o.astype(o_ref.dtype)


def _attn_a_prompt(P, scal, bias, gout, *, B, T, tq):
    assert tq == LANE
    H = bias.shape[0]
    W = H * LANE
    nq = T // tq
    tile_grp = math.gcd(nq, 4)
    return pl.pallas_call(
        functools.partial(_attn_a_body, tq=tq, tile_grp=tile_grp),
        grid=(B, nq),
        in_specs=[pl.BlockSpec(memory_space=pltpu.SMEM),
                  pl.BlockSpec((tq, W), lambda b, i: (b * nq + i, C_AQ // H)),
                  pl.BlockSpec((T, W), lambda b, i: (b, C_AK // H)),
                  pl.BlockSpec((T, W), lambda b, i: (b, C_AV // H)),
                  pl.BlockSpec(bias.shape, lambda b, i: (0, 0, 0, 0)),
                  pl.BlockSpec((1, LANE), lambda b, i: (0, 0))],
        out_specs=pl.BlockSpec((tq, W), lambda b, i: (b * nq + i, 0)),
        out_shape=jax.ShapeDtypeStruct((B * T, W), BF16),
        scratch_shapes=[pltpu.VMEM((H, nq, 2 * LANE, LANE), BF16),
                        pltpu.VMEM((T, W), BF16),
                        pltpu.VMEM((H, nq, tq, 2 * LANE), F32),
                        pltpu.VMEM((H, tq, 2 * LANE), F32),
                        pltpu.VMEM((H, tq, 2 * LANE), F32),
                        pltpu.VMEM((H, 2 * tq, LANE), F32)],
        compiler_params=_cparams(("parallel", "arbitrary")),
        name="attn_a_prompt")(scal, P, P, P, bias, gout)


def _rglru_body(*refs, tt, t_real, n_t, has_state, pos0_is_zero):
    if has_state:
        (x_ref, g_ref, h0_ref, buf0_ref, cw_ref, cb_ref, wa_ref, ba_ref, wx_ref, bxb_ref, lam_ref,
         o_ref, hfin_ref, buf_ref, xpad_ref, a_ref, b_ref, hs_ref, hc_ref) = refs
    else:
        (x_ref, g_ref, cw_ref, cb_ref, wa_ref, ba_ref, wx_ref, bxb_ref, lam_ref,
         o_ref, hfin_ref, buf_ref, xpad_ref, a_ref, b_ref, hs_ref, hc_ref) = refs
    ti = pl.program_id(1)
    W = x_ref.shape[1]

    @pl.when(ti == 0)
    def _():
        if has_state:
            xpad_ref[0:SUBLANE, :] = buf0_ref[0]
            hc_ref[...] = jnp.broadcast_to(h0_ref[0], (SUBLANE, W))
        else:
            xpad_ref[0:SUBLANE, :] = jnp.zeros((SUBLANE, W), F32)
            hc_ref[...] = jnp.zeros((SUBLANE, W), F32)

    x = x_ref[...]
    xpad_ref[SUBLANE:SUBLANE + tt, :] = x
    xc = cb_ref[...] + x * cw_ref[CONV_W - 1:CONV_W, :]
    for j in range(CONV_W - 1):
        off = SUBLANE - (CONV_W - 1) + j
        xc = xc + xpad_ref[off:off + tt, :] * cw_ref[j:j + 1, :]
    xcb = xc.astype(BF16)
    r = jax.nn.sigmoid(jnp.dot(xcb, wa_ref[...], preferred_element_type=F32) + ba_ref[...])
    i = jax.nn.sigmoid(jnp.dot(xcb, wx_ref[...], preferred_element_type=F32) + bxb_ref[...])
    nl = -lam_ref[...]
    softplus = jnp.maximum(nl, 0.0) + jnp.log1p(jnp.exp(-jnp.abs(nl)))
    a = jnp.exp(-LRU_C * r * softplus)
    mult = jnp.sqrt(1.0 - a * a)
    if pos0_is_zero:
        rows = lax.broadcasted_iota(I32, (tt, W), 0)
        mult = jnp.where(jnp.logical_and(rows == 0, ti == 0), 1.0, mult)
    a_ref[...] = a
    b_ref[...] = mult * (i * xc)
    rowt = lax.broadcasted_iota(I32, (SUBLANE, W), 0)

    def tile(n, h_prev):
        r0 = pl.multiple_of(n * SUBLANE, SUBLANE)
        at = a_ref[pl.ds(r0, SUBLANE), :]
        bt = b_ref[pl.ds(r0, SUBLANE), :]
        for s in (1, 2, 4):
            keep = rowt >= s
            bt = jnp.where(keep, bt + at * pltpu.roll(bt, s, 0), bt)
            at = jnp.where(keep, at * pltpu.roll(at, s, 0), at)
        ht = bt + at * h_prev
        hs_ref[pl.ds(r0, SUBLANE), :] = ht
        return jnp.broadcast_to(ht[SUBLANE - 1:SUBLANE, :], (SUBLANE, W))

    hc_ref[...] = lax.fori_loop(0, tt // SUBLANE, tile, hc_ref[...])
    o_ref[...] = (hs_ref[...] * jax.nn.gelu(g_ref[...])).astype(o_ref.dtype)
    xpad_ref[0:SUBLANE, :] = xpad_ref[tt:tt + SUBLANE, :]

    @pl.when(ti == n_t - 1)
    def _():
        t_loc = t_real - (n_t - 1) * tt
        hfin_ref[0] = hs_ref[t_loc - 1:t_loc, :]
        buf_ref[0] = x_ref[t_loc - (CONV_W - 1):t_loc, :]


def _rglru(P, state, weights, *, B, T, tt, t_real, out_dtype):
    W = 4 * LANE
    n_t = T // tt
    has_state = state is not None
    xspec = pl.BlockSpec((tt, W), lambda b, i: (b * n_t + i, C_BX // 4))
    gspec = pl.BlockSpec((tt, W), lambda b, i: (b * n_t + i, C_BG // 4))
    full = lambda shape: pl.BlockSpec(shape, lambda b, i: (0,) * len(shape))
    wspecs = [full((CONV_W, W)), full((1, W)), full((W, W)), full((1, W)), full((W, W)),
              full((1, W)), full((1, W))]
    in_specs = [xspec, gspec]
    args = [P, P]
    if has_state:
        in_specs += [pl.BlockSpec((1, 1, W), lambda b, i: (b, 0, 0)),
                     pl.BlockSpec((1, SUBLANE, W), lambda b, i: (b, 0, 0))]
        args += list(state)
    return pl.pallas_call(
        functools.partial(_rglru_body, tt=tt, t_real=t_real, n_t=n_t, has_state=has_state,
                          pos0_is_zero=not has_state),
        grid=(B, n_t),
        in_specs=in_specs + wspecs,
        out_specs=[pl.BlockSpec((tt, W), lambda b, i: (b * n_t + i, 0)),
                   pl.BlockSpec((1, 1, W), lambda b, i: (b, 0, 0)),
                   pl.BlockSpec((1, CONV_W - 1, W), lambda b, i: (b, 0, 0))],
        out_shape=[jax.ShapeDtypeStruct((B * T, W), out_dtype),
                   jax.ShapeDtypeStruct((B, 1, W), F32),
                   jax.ShapeDtypeStruct((B, CONV_W - 1, W), F32)],
        scratch_shapes=[pltpu.VMEM((tt + SUBLANE, W), F32), pltpu.VMEM((tt, W), F32),
                        pltpu.VMEM((tt, W), F32), pltpu.VMEM((tt, W), F32),
                        pltpu.VMEM((SUBLANE, W), F32)],
        compiler_params=_cparams(("parallel", "arbitrary")),
        name="rglru")(*args, *weights)


def _hgrn_body(*refs, tt, ch, sub, t_real, n_t, has_state):
    if has_state:
        q_ref, f_ref, v_ref, g_ref, s0_ref, lb_ref, gain_ref, o_ref, sfin_ref, st_ref = refs
    else:
        q_ref, f_ref, v_ref, g_ref, lb_ref, gain_ref, o_ref, sfin_ref, st_ref = refs
    ti = pl.program_id(1)
    n_h = st_ref.shape[0]
    dk = LANE

    @pl.when(ti == 0)
    def _():
        for h in range(n_h):
            if has_state:
                st_ref[h] = s0_ref[0, h].T
            else:
                st_ref[h] = jnp.zeros((dk, dk), F32)

    rr = lax.broadcasted_iota(I32, (ch, ch), 0)
    cc = lax.broadcasted_iota(I32, (ch, ch), 1)
    tril = (cc <= rr).astype(F32)
    rows = lax.broadcasted_iota(I32, (ch, dk), 0)
    srow = lax.broadcasted_iota(I32, (sub, 1), 0)
    n_sub = ch // sub

    def head_chunk(c, h):
        r0 = pl.multiple_of(c * ch, ch)
        hs = slice(h * dk, (h + 1) * dk)
        lb = lb_ref[h]
        log_lb = jnp.log(lb)
        log_1mlb = jnp.log1p(-lb)
        q = q_ref[pl.ds(r0, ch), hs]
        q = q * jax.nn.sigmoid(q)
        fp = f_ref[pl.ds(r0, ch), hs]
        v = v_ref[pl.ds(r0, ch), hs]
        log_sig = jnp.minimum(fp, 0.0) - jnp.log1p(jnp.exp(-jnp.abs(fp)))
        b = log_1mlb + log_sig
        lf = jnp.maximum(log_lb, b) + jnp.log1p(jnp.exp(-jnp.abs(log_lb - b)))
        kk = (1.0 - lb) * jax.nn.sigmoid(-fp)
        if t_real < tt * n_t:
            live = (rows + (ti * tt + c * ch)) < t_real
            lf = jnp.where(live, lf, 0.0)
            kk = jnp.where(live, kk, 0.0)
        G = jnp.dot(tril, lf, preferred_element_type=F32, precision=lax.Precision.HIGHEST)
        st = st_ref[h]
        vb = v.astype(BF16)
        o = _nt((q * jnp.exp(G)).astype(BF16), st.astype(BF16))
        outs = []
        for i in range(n_sub):
            lo, hi = i * sub, (i + 1) * sub
            qi, Gi, ki, vi = q[lo:hi], G[lo:hi], kk[lo:hi], v[lo:hi]
            oi = o[lo:hi]
            if i > 0:
                R = G[lo - 1:lo]
                qp = (qi * jnp.exp(Gi - R)).astype(BF16)
                kp = (kk[:lo] * jnp.exp(R - G[:lo])).astype(BF16)
                att = _nt(qp, kp)
                oi = oi + jnp.dot(att.astype(BF16), vb[:lo], preferred_element_type=F32)
            for s in range(sub):
                w = jnp.exp(jnp.minimum(Gi - Gi[s:s + 1], 0.0))
                colv = jnp.sum(qi * w * ki[s:s + 1], axis=-1, keepdims=True)
                colv = jnp.where(srow >= s, colv, 0.0)
                oi = oi + colv * vi[s:s + 1]
            outs.append(oi)
        o = jnp.concatenate(outs, axis=0) if n_sub > 1 else outs[0]
        gl = G[ch - 1:ch]
        kpp = (kk * jnp.exp(gl - G)).astype(BF16)
        st_ref[h] = st * jnp.exp(gl) + _tn(vb, kpp)
        gate = g_ref[pl.ds(r0, ch), hs]
        o = _rms_rows(o, gain_ref[...]) * (gate * jax.nn.sigmoid(gate))
        o_ref[pl.ds(r0, ch), hs] = o.astype(o_ref.dtype)

    def chunk(c, carry):
        for h in range(n_h):
            head_chunk(c, h)
        return carry

    lax.fori_loop(0, tt // ch, chunk, 0)

    @pl.when(ti == n_t - 1)
    def _():
        for h in range(n_h):
            sfin_ref[0, h] = st_ref[h].T


def _hgrn(P, s0, lb, gain, *, B, T, tt, ch, sub, t_real, out_dtype):
    H = lb.shape[0]
    n_t = T // tt
    has_state = s0 is not None

    W = H * LANE

    def col(c0):
        return pl.BlockSpec((tt, W), lambda b, i: (b * n_t + i, c0 // H))

    in_specs = [col(C_CQ), col(C_CF), col(C_CI), col(C_CG)]
    args = [P, P, P, P]
    if has_state:
        in_specs.append(pl.BlockSpec((1, H, LANE, LANE), lambda b, i: (b, 0, 0, 0)))
        args.append(s0)
    in_specs += [pl.BlockSpec((H, 1, LANE), lambda b, i: (0, 0, 0)),
                 pl.BlockSpec((1, LANE), lambda b, i: (0, 0))]
    return pl.pallas_call(
        functools.partial(_hgrn_body, tt=tt, ch=ch, sub=sub, t_real=t_real, n_t=n_t,
                          has_state=has_state),
        grid=(B, n_t),
        in_specs=in_specs,
        out_specs=[pl.BlockSpec((tt, W), lambda b, i: (b * n_t + i, 0)),
                   pl.BlockSpec((1, H, LANE, LANE), lambda b, i: (b, 0, 0, 0))],
        out_shape=[jax.ShapeDtypeStruct((B * T, W), out_dtype),
                   jax.ShapeDtypeStruct((B, H, LANE, LANE), F32)],
        scratch_shapes=[pltpu.VMEM((H, LANE, LANE), F32)],
        compiler_params=_cparams(("parallel", "arbitrary")),
        name="hgrn2")(*args, lb, gain)


def _score_key(score):
    score = jnp.where(score == 0.0, 0.0, score)
    bits = pltpu.bitcast(score, I32)
    return bits ^ ((bits >> 31) & 0x7FFFFFFF)


def _kth_largest(count_ge, shape, k, bits=32):
    zero = jnp.zeros(shape, I32)
    v = jnp.where(count_ge(zero) >= k, zero, jnp.full(shape, -(2 ** (bits - 1)), I32))

    def bit_step(n, v):
        cand = v + (jnp.int32(1) << (bits - 2 - n))
        return jnp.where(count_ge(cand) >= k, cand, v)

    return lax.fori_loop(0, bits - 1, bit_step, v)


def _tie_bound(count_tie_below, r, shape, n_bits):
    def bit_step(n, pos):
        cand = pos + (jnp.int32(1) << (n_bits - 1 - n))
        return jnp.where(count_tie_below(cand) < r, cand, pos)

    return lax.fori_loop(0, n_bits, bit_step, jnp.zeros(shape, I32))


def _dsa_body(dq_ref, iq_ref, tq_ref, dk_ref, dv_ref, ik_ref, bias_ref, o_ref,
              key_ref, hi_ref, lo_ref, mask_ref, pos_ref, kd_ref, vb_ref, ikb_ref,
              s_ref, m_ref, l_ref, acc_ref, *, tq, n_top, n_bits, tile_grp):
    qb = pl.program_id(1)
    n_c = qb + 1
    hd = HEAD_DIM
    n_kv = dk_ref.shape[1] // hd
    n_q = dq_ref.shape[1] // hd
    grp = n_q // n_kv
    n_pair = n_q // 2
    row = lax.broadcasted_iota(I32, (tq, LANE), 0)
    col = lax.broadcasted_iota(I32, (tq, LANE), 1)
    low_half = col < hd

    @pl.when(qb == 0)
    def _():
        kd_ref[...] = jnp.zeros(kd_ref.shape, BF16)
        vb_ref[...] = jnp.zeros(vb_ref.shape, BF16)
        ikb_ref[...] = jnp.zeros(ikb_ref.shape, BF16)

    def pair_blocks(at_low, at_high):
        return jnp.concatenate([jnp.where(low_half, at_low, 0.0),
                                jnp.where(low_half, 0.0, at_high)], axis=0).astype(BF16)

    r_q = pl.multiple_of(qb * LANE, LANE)
    kx = dk_ref[pl.ds(r_q, LANE), :]
    kx_sw = pltpu.roll(kx, hd, 1)
    kd_ref[0, qb] = pair_blocks(kx, kx_sw)
    kd_ref[1, qb] = pair_blocks(kx_sw, kx)
    ikb_ref[pl.ds(r_q, LANE), :] = ik_ref[pl.ds(r_q, LANE), :].astype(BF16)
    vb_ref[pl.ds(r_q, LANE), :] = dv_ref[pl.ds(r_q, LANE), :].astype(BF16)

    iq = iq_ref[...]
    iqz = []
    for p in range(IDX_HEADS // 2):
        pair = iq[:, p * LANE:(p + 1) * LANE]
        iqz.append(jnp.concatenate([jnp.where(low_half, pair, 0.0),
                                    jnp.where(low_half, pltpu.roll(pair, hd, 1), 0.0)],
                                   axis=0).astype(BF16))
    wscale = IDX_HEADS ** -0.5 * hd ** -0.5
    tail_t = tq_ref[...].T
    w_rows = [tail_t[IW_OFF + h:IW_OFF + h + 1, :] * wscale for h in range(IDX_HEADS)]

    n_grp = (n_c + tile_grp - 1) // tile_grp

    def score_tile(c):
        r0 = pl.multiple_of(c * LANE, LANE)
        ikc = ikb_ref[pl.ds(r0, LANE), :]
        sc = None
        for p in range(IDX_HEADS // 2):
            s = jnp.maximum(_nt(ikc, iqz[p]), 0.0)
            t = s[:, :LANE] * w_rows[2 * p] + s[:, LANE:] * w_rows[2 * p + 1]
            sc = t if sc is None else sc + t
        causal = jnp.logical_or(c < qb, jnp.logical_and(c == qb, row <= col))
        key = _score_key(jnp.where(causal, sc, NEG_INF))
        key_ref[c] = key
        hi_ref[c] = (key >> 16).astype(jnp.int16)
        lo_ref[c] = (((key ^ 0x8000) << 16) >> 16).astype(jnp.int16)

    def score_group(i, carry):
        for j in range(tile_grp):
            score_tile(i * tile_grp + j)
        return carry

    lax.fori_loop(0, n_grp, score_group, 0)
    vec = (1, tq)

    def count(pred, ref=key_ref, dtype=I32):
        def body(i, acc):
            for j in range(tile_grp):
                c = i * tile_grp + j
                acc = acc + pred(ref[c], c).astype(dtype)
            return acc
        acc = lax.fori_loop(0, n_grp, body, jnp.zeros((LANE, tq), dtype))
        return jnp.sum(acc.astype(I32), axis=0, keepdims=True)

    def count16_ge(ref):
        def f(cand):
            cb = jnp.broadcast_to(cand, (LANE, tq)).astype(jnp.int16)
            return count(lambda key, c: key >= cb, ref, jnp.int16)
        return f

    min16 = -(2 ** 15)
    t_hi = _kth_largest(count16_ge(hi_ref), vec, n_top, bits=16)
    t_hi_b = jnp.broadcast_to(t_hi, (LANE, tq)).astype(jnp.int16)
    k_lo = n_top - count(lambda key, c: key > t_hi_b, hi_ref, jnp.int16)

    def band_group(i, carry):
        for j in range(tile_grp):
            c = i * tile_grp + j
            lo_ref[c] = jnp.where(hi_ref[c] == t_hi_b, lo_ref[c], jnp.int16(min16))
        return carry

    lax.fori_loop(0, n_grp, band_group, 0)
    t_lo = _kth_largest(count16_ge(lo_ref), vec, k_lo, bits=16)
    thr = (t_hi << 16) | ((t_lo ^ min16) & 0xFFFF)
    thr_b = jnp.broadcast_to(thr, (LANE, tq))
    n_gt = count(lambda key, c: key > thr_b)
    n_ge = count(lambda key, c: key >= thr_b)
    need = jnp.logical_and(n_ge > n_top, thr > KEY_NEG_INF)
    pos_ref[...] = jnp.full((LANE, tq), 2 ** n_bits, I32)

    @pl.when(jnp.max(need.astype(I32)) > 0)
    def _():
        r = n_top - n_gt

        def count_tie_below(cand):
            cb = jnp.broadcast_to(cand, (LANE, tq))
            return count(lambda key, c: jnp.logical_and(key == thr_b, row + c * LANE < cb))

        pos = _tie_bound(count_tie_below, r, vec, n_bits)
        pos_ref[...] = jnp.broadcast_to(pos, (LANE, tq))

    pos_b = pos_ref[...]

    def mask_group(i, carry):
        for j in range(tile_grp):
            c = i * tile_grp + j
            key = key_ref[c]
            idx = row + c * LANE
            sel = jnp.logical_or(key > thr_b, jnp.logical_and(key == thr_b, idx <= pos_b))
            sel = jnp.logical_and(sel, key > KEY_NEG_INF)
            mask_ref[c] = jnp.where(sel, 0.0, NEG_INF).T
        return carry

    lax.fori_loop(0, n_grp, mask_group, 0)

    dq = dq_ref[...].astype(BF16)

    def logits(p, c):
        mk = mask_ref[c]
        return (_nt(dq[:, p * LANE:(p + 1) * LANE], kd_ref[(2 * p) // grp, c])
                + jnp.concatenate([mk, mk], axis=1) + bias_ref[p, jnp.clip(qb - c, 0, 2)])

    def values(p, i):
        r0 = pl.multiple_of(i * tile_grp * LANE, tile_grp * LANE)
        return vb_ref[pl.ds(r0, tile_grp * LANE), :]

    outs = _pair_attention(n_grp, tile_grp, n_pair, tq, logits, values, s_ref, m_ref, l_ref, acc_ref)
    for p, (oa, ob) in enumerate(outs):
        if (2 * p) // grp == 0:
            ob = pltpu.roll(ob, hd, 1)
        else:
            oa = pltpu.roll(oa, hd, 1)
        o_ref[:, p * LANE:(p + 1) * LANE] = jnp.where(low_half, oa, ob).astype(o_ref.dtype)


def _dsa_prompt(P, bias, *, B, T, tq, n_top):
    assert tq == LANE
    nq = T // tq
    n_bits = max(1, int(math.ceil(math.log2(T))))
    tile_grp = math.gcd(nq, 4)
    n_pair = bias.shape[0]
    n_tiles = T // LANE
    return pl.pallas_call(
        functools.partial(_dsa_body, tq=tq, n_top=n_top, n_bits=n_bits, tile_grp=tile_grp),
        grid=(B, nq),
        in_specs=[pl.BlockSpec((tq, 4 * LANE), lambda b, i: (b * nq + i, C_DQ // 4)),
                  pl.BlockSpec((tq, 4 * LANE), lambda b, i: (b * nq + i, C_IQ // 4)),
                  pl.BlockSpec((tq, LANE), lambda b, i: (b * nq + i, C_TAIL)),
                  pl.BlockSpec((T, LANE), lambda b, i: (b, C_DK)),
                  pl.BlockSpec((T, LANE), lambda b, i: (b, C_DV)),
                  pl.BlockSpec((T, LANE), lambda b, i: (b, C_TAIL)),
                  pl.BlockSpec(bias.shape, lambda b, i: (0, 0, 0, 0))],
        out_specs=pl.BlockSpec((tq, 4 * LANE), lambda b, i: (b * nq + i, 0)),
        out_shape=jax.ShapeDtypeStruct((B * T, 4 * LANE), BF16),
        scratch_shapes=[pltpu.VMEM((n_tiles, tq, LANE), I32),
                        pltpu.VMEM((n_tiles, tq, LANE), jnp.int16),
                        pltpu.VMEM((n_tiles, tq, LANE), jnp.int16),
                        pltpu.VMEM((n_tiles, tq, LANE), F32),
                        pltpu.VMEM((tq, LANE), I32),
                        pltpu.VMEM((2, n_tiles, 2 * LANE, LANE), BF16),
                        pltpu.VMEM((T, LANE), BF16),
                        pltpu.VMEM((T, LANE), BF16),
                        pltpu.VMEM((n_pair, n_tiles, tq, 2 * LANE), F32),
                        pltpu.VMEM((n_pair, tq, 2 * LANE), F32),
                        pltpu.VMEM((n_pair, tq, 2 * LANE), F32),
                        pltpu.VMEM((n_pair, 2 * tq, LANE), F32)],
        compiler_params=_cparams(("parallel", "arbitrary")),
        name="dsa_prompt")(P, P, P, P, P, P, bias)


def _attn_a_dec_body(pt_ref, scal_ref, q_ref, kn_ref, vn_ref, bl_ref, bn_ref, gout_ref, *rest,
                     pps, n_steps):
    k_refs = rest[:pps]
    v_refs = rest[pps:2 * pps]
    o_ref, m_ref, l_ref, acc_ref = rest[2 * pps:]
    s_i = pl.program_id(1)
    n_h = q_ref.shape[1] // LANE
    rows_h = 2 * T_PAD

    @pl.when(s_i == 0)
    def _():
        m_ref[...] = jnp.full(m_ref.shape, NEG_INF, F32)
        l_ref[...] = jnp.zeros(l_ref.shape, F32)
        acc_ref[...] = jnp.zeros(acc_ref.shape, F32)

    q = q_ref[...]
    lane = lax.broadcasted_iota(I32, (T_PAD, LANE), 1)

    def q_head(h):
        qh = q[:, h * LANE:(h + 1) * LANE]
        return jnp.concatenate([jnp.where(lane < HEAD_DIM, qh, 0.0),
                                jnp.where(lane >= HEAD_DIM, qh, 0.0)], axis=0).astype(BF16)

    qs = [q_head(h) for h in range(n_h)]
    q_all = jnp.concatenate(qs, axis=0)
    is_last = s_i == n_steps - 1

    n_rows = n_h * rows_h
    page_w = PAGE * n_h
    row_head = lax.broadcasted_iota(I32, (n_rows, page_w), 0) // rows_h
    col_head = lax.broadcasted_iota(I32, (n_rows, page_w), 1) % n_h
    head_mask = jnp.where(row_head == col_head, 0.0, NEG_INF)
    last_bias = jnp.where(is_last, bl_ref[...], 0.0) + head_mask
    parts = [_nt(q_all, r[...].astype(BF16)) + (last_bias if i == pps - 1 else head_mask)
             for i, r in enumerate(k_refs)]
    v_all = jnp.concatenate([r[...].astype(BF16) for r in v_refs], axis=0)
    m, l, acc = _softmax_update((m_ref[...], l_ref[...], acc_ref[...]),
                                jnp.concatenate(parts, axis=1), v_all)
    m_ref[...] = m
    l_ref[...] = l
    acc_ref[...] = acc

    @pl.when(is_last)
    def _():
        pad = jnp.zeros((LANE - T_PAD, LANE), BF16)
        m_all, l_all, acc_all = m_ref[...], l_ref[...], acc_ref[...]
        new = []
        for h in range(n_h):
            hs = slice(h * LANE, (h + 1) * LANE)
            rs = slice(h * rows_h, (h + 1) * rows_h)
            kn = jnp.concatenate([kn_ref[:, hs].astype(BF16), pad], axis=0)
            vn = jnp.concatenate([vn_ref[:, hs].astype(BF16), pad], axis=0)
            s = _nt(qs[h], kn) + bn_ref[rs]
            new.append(_softmax_update((m_all[rs], l_all[rs], acc_all[rs]), s, vn))
        m_ref[...] = jnp.concatenate([x[0] for x in new], axis=0)
        l_ref[...] = jnp.concatenate([x[1] for x in new], axis=0)
        acc_ref[...] = jnp.concatenate([x[2] for x in new], axis=0)
        lam = scal_ref[0]
        for h in range(n_h):
            r1 = slice(h * rows_h, h * rows_h + T_PAD)
            r2 = slice(h * rows_h + T_PAD, (h + 1) * rows_h)
            o = acc_ref[r1] / l_ref[r1] - lam * (acc_ref[r2] / l_ref[r2])
            o_ref[:, h * LANE:(h + 1) * LANE] = _rms_rows(o, gout_ref[...]) * scal_ref[1]


def _attn_a_decode(P, cache_k, cache_v, page_table, layer, scal, bias_last, bias_new, gout, *, B, pps):
    n_pages = page_table.shape[1]
    n_steps = n_pages // pps
    W = 4 * LANE

    def page_spec(i):
        return pl.BlockSpec((None, None) + cache_k.shape[2:],
                            lambda b, s, pt: (pt[b, s * pps + i], layer, 0, 0))

    grid_spec = pltpu.PrefetchScalarGridSpec(
        num_scalar_prefetch=1,
        grid=(B, n_steps),
        in_specs=[pl.BlockSpec(memory_space=pltpu.SMEM),
                  pl.BlockSpec((T_PAD, W), lambda b, s, pt: (b, C_AQ // 4)),
                  pl.BlockSpec((T_PAD, W), lambda b, s, pt: (b, C_AK // 4)),
                  pl.BlockSpec((T_PAD, W), lambda b, s, pt: (b, C_AV // 4)),
                  pl.BlockSpec(bias_last.shape, lambda b, s, pt: (0, 0)),
                  pl.BlockSpec(bias_new.shape, lambda b, s, pt: (0, 0)),
                  pl.BlockSpec((1, LANE), lambda b, s, pt: (0, 0))]
        + [page_spec(i) for i in range(pps)] + [page_spec(i) for i in range(pps)],
        out_specs=pl.BlockSpec((T_PAD, W), lambda b, s, pt: (b, 0)),
        scratch_shapes=[pltpu.VMEM((bias_last.shape[0], 1), F32),
                        pltpu.VMEM((bias_last.shape[0], 1), F32),
                        pltpu.VMEM((bias_last.shape[0], LANE), F32)])
    return pl.pallas_call(
        functools.partial(_attn_a_dec_body, pps=pps, n_steps=n_steps),
        grid_spec=grid_spec,
        out_shape=jax.ShapeDtypeStruct((B * T_PAD, W), F32),
        compiler_params=_cparams(("parallel", "arbitrary")),
        name="attn_a_decode")(page_table, scal, P, P, P, bias_last, bias_new, gout,
                              *([cache_k] * pps), *([cache_v] * pps))


def _dsa_sel_body(pt_ref, iq_ref, tq_ref, ikn_ref, *rest, pps, n_steps, n_top, t_real, n_bits):
    ik_refs = rest[:pps]
    keys_ref, keyn_ref, thr_ref, pos_ref, all_ref, wb_ref = rest[pps:]
    s_i = pl.program_id(1)
    hd = HEAD_DIM
    iq = iq_ref[...].astype(BF16)
    q_idx = jnp.concatenate([iq[:, h * hd:(h + 1) * hd] for h in range(IDX_HEADS)], axis=0)
    wscale = IDX_HEADS ** -0.5 * hd ** -0.5
    tail = tq_ref[...]
    for h in range(IDX_HEADS):
        wb_ref[h * T_PAD:(h + 1) * T_PAD, :] = jnp.broadcast_to(
            tail[:, IW_OFF + h:IW_OFF + h + 1] * wscale, (T_PAD, LANE))

    def score(raw):
        s = jnp.maximum(raw, 0.0) * wb_ref[...]
        sc = s[0:T_PAD]
        for h in range(1, IDX_HEADS):
            sc = sc + s[h * T_PAD:(h + 1) * T_PAD]
        return sc

    for i in range(pps):
        raw = jnp.dot(q_idx, ik_refs[i][...].astype(BF16), preferred_element_type=F32)
        key = _score_key(score(raw))
        keys_ref[0, i] = key
        all_ref[s_i * pps + i] = key

    n_tiles = n_steps * pps + 1

    @pl.when(s_i == n_steps - 1)
    def _():
        row = lax.broadcasted_iota(I32, (T_PAD, LANE), 0)
        col = lax.broadcasted_iota(I32, (T_PAD, LANE), 1)
        pad = jnp.zeros((LANE - T_PAD, hd), F32)
        ikn = jnp.concatenate([ikn_ref[...][:, :hd], pad], axis=0)
        valid = jnp.logical_and(col <= row, col < t_real)
        keyn = _score_key(jnp.where(valid, score(_nt(q_idx, ikn.astype(BF16))), NEG_INF))
        keyn_ref[0] = keyn
        all_ref[n_tiles - 1] = keyn
        idx = (lax.broadcasted_iota(I32, all_ref.shape, 0) * LANE
               + lax.broadcasted_iota(I32, all_ref.shape, 2))

        def count(pred):
            acc = jnp.sum(pred(all_ref[...]).astype(I32), axis=0)
            return jnp.sum(acc, axis=-1, keepdims=True)

        def count_ge(cand):
            cb = jnp.broadcast_to(cand, (T_PAD, LANE))
            return count(lambda key: key >= cb)

        thr = _kth_largest(count_ge, (T_PAD, 1), n_top)
        thr_b = jnp.broadcast_to(thr, (T_PAD, LANE))
        n_gt = count(lambda key: key > thr_b)
        r = n_top - n_gt

        def count_tie_below(cand):
            cb = jnp.broadcast_to(cand, (T_PAD, LANE))
            return count(lambda key: jnp.logical_and(key == thr_b, idx < cb))

        pos = _tie_bound(count_tie_below, r, (T_PAD, 1), n_bits)
        thr_ref[0] = thr_b
        pos_ref[0] = jnp.broadcast_to(pos, (T_PAD, LANE))


def _dsa_select_decode(P, cache_idx, page_table, layer, *, B, pps, n_top, t_real):
    n_pages = page_table.shape[1]
    n_steps = n_pages // pps
    n_tiles = n_pages + 1
    n_bits = int(math.ceil(math.log2(n_tiles * LANE)))

    def page_spec(i):
        return pl.BlockSpec((None, None) + cache_idx.shape[2:],
                            lambda b, s, pt: (pt[b, s * pps + i], layer, 0, 0))

    tile_spec = pl.BlockSpec((1, T_PAD, LANE), lambda b, s, pt: (b, 0, 0))
    grid_spec = pltpu.PrefetchScalarGridSpec(
        num_scalar_prefetch=1,
        grid=(B, n_steps),
        in_specs=[pl.BlockSpec((T_PAD, 4 * LANE), lambda b, s, pt: (b, C_IQ // 4)),
                  pl.BlockSpec((T_PAD, LANE), lambda b, s, pt: (b, C_TAIL)),
                  pl.BlockSpec((T_PAD, LANE), lambda b, s, pt: (b, C_TAIL))]
        + [page_spec(i) for i in range(pps)],
        out_specs=[pl.BlockSpec((1, pps, T_PAD, LANE), lambda b, s, pt: (b, s, 0, 0)),
                   tile_spec, tile_spec, tile_spec],
        scratch_shapes=[pltpu.VMEM((n_tiles, T_PAD, LANE), I32),
                        pltpu.VMEM((IDX_HEADS * T_PAD, LANE), F32)])
    tile_shape = jax.ShapeDtypeStruct((B, T_PAD, LANE), I32)
    return pl.pallas_call(
        functools.partial(_dsa_sel_body, pps=pps, n_steps=n_steps, n_top=n_top, t_real=t_real,
                          n_bits=n_bits),
        grid_spec=grid_spec,
        out_shape=[jax.ShapeDtypeStruct((B, n_pages, T_PAD, LANE), I32),
                   tile_shape, tile_shape, tile_shape],
        compiler_params=_cparams(("parallel", "arbitrary")),
        name="dsa_select_decode")(page_table, P, P, P, *([cache_idx] * pps))


def _dsa_att_body(pt_ref, dq_ref, kn_ref, vn_ref, keys_ref, keyn_ref, thr_ref, pos_ref,
                  bl_ref, bn_ref, *rest, pps, n_steps):
    k_refs = rest[:pps]
    v_refs = rest[pps:2 * pps]
    o_ref, m_ref, l_ref, acc_ref = rest[2 * pps:]
    s_i = pl.program_id(1)
    hd = HEAD_DIM
    n_kv = kn_ref.shape[1] // hd
    n_q = dq_ref.shape[1] // hd
    grp = n_q // n_kv
    rows_g = grp * T_PAD

    @pl.when(s_i == 0)
    def _():
        m_ref[...] = jnp.full(m_ref.shape, -1e30, F32)
        l_ref[...] = jnp.zeros(l_ref.shape, F32)
        acc_ref[...] = jnp.zeros(acc_ref.shape, F32)

    dq = dq_ref[...].astype(BF16)
    q_g = [jnp.concatenate([dq[:, (g * grp + j) * hd:(g * grp + j + 1) * hd] for j in range(grp)],
                           axis=0) for g in range(n_kv)]
    thr = thr_ref[0]
    pos = pos_ref[0]
    col = lax.broadcasted_iota(I32, (T_PAD, LANE), 1)
    is_last = s_i == n_steps - 1

    def sel_mask(key, tile):
        idx = col + tile * LANE
        sel = jnp.logical_or(key > thr, jnp.logical_and(key == thr, idx <= pos))
        sel = jnp.logical_and(sel, key > KEY_NEG_INF)
        return jnp.concatenate([sel] * grp, axis=0)

    def attend(tiles, masks, bias_tiles):
        m_all, l_all, acc_all = m_ref[...], l_ref[...], acc_ref[...]
        new = []
        for g in range(n_kv):
            rs = slice(g * rows_g, (g + 1) * rows_g)
            logit_tiles, pv = tiles(g)
            parts = []
            for s, mk, bt in zip(logit_tiles, masks, bias_tiles):
                if bt is not None:
                    s = s + bt[rs]
                parts.append(jnp.where(mk, s, NEG_INF))
            s = jnp.concatenate(parts, axis=1) if len(parts) > 1 else parts[0]
            m_new = jnp.maximum(m_all[rs], jnp.max(s, axis=-1, keepdims=True))
            alpha = jnp.exp(m_all[rs] - m_new)
            p = jnp.exp(s - m_new)
            new.append((m_new, alpha * l_all[rs] + jnp.sum(p, axis=-1, keepdims=True),
                        alpha * acc_all[rs] + pv(p.astype(BF16))))
        m_ref[...] = jnp.concatenate([x[0] for x in new], axis=0)
        l_ref[...] = jnp.concatenate([x[1] for x in new], axis=0)
        acc_ref[...] = jnp.concatenate([x[2] for x in new], axis=0)

    masks = [sel_mask(keys_ref[0, i], s_i * pps + i) for i in range(pps)]
    last_bias = jnp.where(is_last, bl_ref[...], 0.0)

    def page_tiles(g):
        vt = jnp.concatenate([r[g].astype(BF16) for r in v_refs], axis=1)
        return ([jnp.dot(q_g[g], r[g].astype(BF16), preferred_element_type=F32) for r in k_refs],
                lambda p: _nt(p, vt))

    attend(page_tiles, masks, [None] * (pps - 1) + [last_bias])

    @pl.when(is_last)
    def _():
        pad = jnp.zeros((LANE - T_PAD, hd), BF16)

        def new_tiles(g):
            hs = slice(g * hd, (g + 1) * hd)
            kn = jnp.concatenate([kn_ref[:, hs].astype(BF16), pad], axis=0)
            vn = jnp.concatenate([vn_ref[:, hs].astype(BF16), pad], axis=0)
            return [_nt(q_g[g], kn)], lambda p: jnp.dot(p, vn, preferred_element_type=F32)

        attend(new_tiles, [sel_mask(keyn_ref[0], n_steps * pps)], [bn_ref[...]])
        for g in range(n_kv):
            rs = slice(g * rows_g, (g + 1) * rows_g)
            og = acc_ref[rs] / l_ref[rs]
            for j in range(grp):
                h = g * grp + j
                o_ref[:, h * hd:(h + 1) * hd] = og[j * T_PAD:(j + 1) * T_PAD]


def _dsa_attend_decode(P, cache_k, cache_v, page_table, layer, keys, keyn, thr, pos,
                       bias_last, bias_new, *, B, pps):
    n_pages = page_table.shape[1]
    n_steps = n_pages // pps
    rows = bias_last.shape[0]

    def page_spec(i):
        return pl.BlockSpec((None, None) + cache_k.shape[2:],
                            lambda b, s, pt: (pt[b, s * pps + i], layer, 0, 0, 0))

    tile_spec = pl.BlockSpec((1, T_PAD, LANE), lambda b, s, pt: (b, 0, 0))
    grid_spec = pltpu.PrefetchScalarGridSpec(
        num_scalar_prefetch=1,
        grid=(B, n_steps),
        in_specs=[pl.BlockSpec((T_PAD, 4 * LANE), lambda b, s, pt: (b, C_DQ // 4)),
                  pl.BlockSpec((T_PAD, LANE), lambda b, s, pt: (b, C_DK)),
                  pl.BlockSpec((T_PAD, LANE), lambda b, s, pt: (b, C_DV)),
                  pl.BlockSpec((1, pps, T_PAD, LANE), lambda b, s, pt: (b, s, 0, 0)),
                  tile_spec, tile_spec, tile_spec,
                  pl.BlockSpec(bias_last.shape, lambda b, s, pt: (0, 0)),
                  pl.BlockSpec(bias_new.shape, lambda b, s, pt: (0, 0))]
        + [page_spec(i) for i in range(pps)] + [page_spec(i) for i in range(pps)],
        out_specs=pl.BlockSpec((T_PAD, 4 * LANE), lambda b, s, pt: (b, 0)),
        scratch_shapes=[pltpu.VMEM((rows, 1), F32), pltpu.VMEM((rows, 1), F32),
                        pltpu.VMEM((rows, HEAD_DIM), F32)])
    return pl.pallas_call(
        functools.partial(_dsa_att_body, pps=pps, n_steps=n_steps),
        grid_spec=grid_spec,
        out_shape=jax.ShapeDtypeStruct((B * T_PAD, 4 * LANE), F32),
        compiler_params=_cparams(("parallel", "arbitrary")),
        name="dsa_attend_decode")(page_table, P, P, P, keys, keyn, thr, pos, bias_last, bias_new,
                                  *([cache_k] * pps), *([cache_v] * pps))


def _bias_minus_far(tab, dist):
    onehot = jax.nn.one_hot(_rel_bucket(dist), N_BUCKETS, dtype=F32)
    bias = jnp.einsum("...k,km->...m", onehot, tab, precision=lax.Precision.HIGHEST)
    return bias - tab[N_BUCKETS - 1]


def _toeplitz_tiles(tab, t):
    r = jnp.arange(t)[:, None]
    c = jnp.arange(t)[None, :]
    tiles = jnp.stack([_bias_minus_far(tab, r - c), _bias_minus_far(tab, t + r - c)], axis=0)
    return jnp.transpose(tiles, (3, 0, 1, 2))


def _decode_bias(tab, past, t_real):
    tq = jnp.arange(T_PAD)[:, None]
    kc = jnp.arange(LANE)[None, :]
    last = _bias_minus_far(tab, past + tq - (past - PAGE + kc))
    new = _bias_minus_far(tab, tq - kc)
    valid = jnp.logical_and(kc <= tq, kc < t_real)
    new = jnp.where(valid[..., None], new, NEG_INF)
    return jnp.transpose(last, (2, 0, 1)), jnp.transpose(new, (2, 0, 1))


def _block_diag(w):
    n, d, e = w.shape
    eye = jnp.eye(n, dtype=w.dtype)
    return (eye[:, None, :, None] * w[:, :, None, :]).reshape(n * d, n * e)


def kernel(x_prompt, x_sample, cache_a_k, cache_a_v, cache_d_k, cache_d_v, cache_d_idx, state_b_h, state_b_conv, state_c_s, page_table, rel_bias, ln1, w_in, a_q_norm, a_k_norm, a_lam_q1, a_lam_k1, a_lam_q2, a_lam_k2, a_out_norm, b_conv_w, b_conv_b, b_wa, b_ba, b_wx, b_bx, b_lambda, c_lb_logits, c_out_norm, d_q_norm, d_k_norm, w_out, ln2, w_up, w_down):
    B, T, D = x_prompt.shape
    Bs, Ts, _ = x_sample.shape
    L = w_in.shape[0]
    n_pages = page_table.shape[1]
    past = n_pages * PAGE
    gw = D // 4
    a_heads = gw // (2 * HEAD_DIM)
    c_heads = c_lb_logits.shape[1] // LANE
    d_heads = gw // HEAD_DIM
    d_kv = cache_d_k.shape[3]
    d_grp = d_heads // d_kv
    n_pool = cache_a_k.shape[0]
    assert Ts <= T_PAD - 0 and Ts >= CONV_W - 1 and past > 0

    tab = rel_bias.astype(F32)
    n_a_maps = 2 * a_heads
    tab_a, tab_d = tab[:, :n_a_maps], tab[:, n_a_maps:]
    tq_a = min(128, T)
    tq_d = min(128, T)
    bias_a = jnp.transpose(_toeplitz_tiles(tab_a, tq_a).reshape(a_heads, 2, 2, tq_a, tq_a),
                           (0, 2, 3, 1, 4)).reshape(a_heads, 2, tq_a, 2 * tq_a)
    causal = jnp.where(jnp.arange(tq_a)[None, :] <= jnp.arange(tq_a)[:, None], 0.0, NEG_INF)
    causal = jnp.tile(causal, (1, 2)).astype(F32)
    bias_a = jnp.stack([jnp.full((a_heads, tq_a, 2 * tq_a), NEG_INF, F32), bias_a[:, 0] + causal,
                        bias_a[:, 1], jnp.zeros((a_heads, tq_a, 2 * tq_a), F32)], axis=1)
    bias_d = jnp.concatenate([_toeplitz_tiles(tab_d, tq_d),
                              jnp.zeros((d_heads, 1, tq_d, tq_d), F32)], axis=1)
    bias_d = jnp.transpose(bias_d.reshape(d_heads // 2, 2, 3, tq_d, tq_d),
                           (0, 2, 3, 1, 4)).reshape(d_heads // 2, 3, tq_d, 2 * tq_d)
    bl_a, bn_a = _decode_bias(tab_a, past, Ts)
    bl_a = jnp.repeat(bl_a.reshape(n_a_maps * T_PAD, LANE), a_heads, axis=1)
    bn_a = bn_a.reshape(n_a_maps * T_PAD, LANE)
    bl_d, bn_d = _decode_bias(tab_d, past, Ts)
    bl_d = bl_d.reshape(d_heads * T_PAD, LANE)
    bn_d = bn_d.reshape(d_heads * T_PAD, LANE)

    lb_cum = jnp.cumsum(jax.nn.softmax(c_lb_logits.astype(F32), axis=0), axis=0)
    lb_all = (lb_cum - lb_cum[0]).reshape(L, c_heads, 1, LANE)

    seg = jnp.kron(jnp.eye(LANE // HEAD_DIM, dtype=F32),
                   jnp.full((HEAD_DIM, HEAD_DIM), 1.0 / HEAD_DIM, F32)).astype(BF16)
    qscale = HEAD_DIM ** -0.5
    zeros = lambda n: jnp.zeros((n,), F32)
    ones = lambda n: jnp.ones((n,), F32)
    flag = jnp.concatenate([ones(2 * gw), zeros(7 * gw), ones(gw), zeros(gw), ones(LANE),
                            zeros(P_WIDTH - 11 * gw - LANE)]).reshape(1, P_WIDTH)
    norm_tiles = (0, 1, 9, 11)

    ca_k = cache_a_k.reshape(n_pool, L, PAGE * a_heads, 2 * HEAD_DIM)
    ca_v = cache_a_v.reshape(n_pool, L, PAGE * a_heads, 2 * HEAD_DIM)
    cd_k = jnp.transpose(cache_d_k, (0, 1, 3, 4, 2))
    cd_v = jnp.transpose(cache_d_v, (0, 1, 3, 4, 2))
    cd_idx = jnp.transpose(cache_d_idx, (0, 1, 3, 2))

    xp = x_prompt.reshape(B * T, D)
    xs = jnp.pad(x_sample, ((0, 0), (0, T_PAD - Ts), (0, 0))).reshape(Bs * T_PAD, D)
    s_conv = jnp.pad(state_b_conv, ((0, 0), (0, 0), (SUBLANE - (CONV_W - 1), 0), (0, 0)))

    n_top_p = min(TOPK_MAX, T // 4)
    n_top_s = min(TOPK_MAX, (past + Ts) // 4)
    tm_p = min(1024, B * T)
    tm_f = min(1024, B * T)
    tt_b = min(256, T)
    tt_c = min(512, T)
    ch_c = math.gcd(T, 64)
    pps_a = math.gcd(n_pages, 16)
    pps_d = math.gcd(n_pages, 16)

    outs_p, outs_s = [], []
    for l in range(L):
        wl = w_in[l]
        n_head = 10 * gw
        w_perm = jnp.concatenate(
            [wl[:, :n_head], wl[:, n_head + 2 * LANE:n_head + 2 * LANE + gw],
             wl[:, n_head:n_head + 2 * LANE], wl[:, n_head + 2 * LANE + gw:],
             jnp.zeros((D, P_WIDTH - wl.shape[1]), wl.dtype)], axis=1).astype(BF16)
        gain = jnp.concatenate(
            [jnp.tile(a_q_norm[l], 2 * a_heads) * qscale, jnp.tile(a_k_norm[l], 2 * a_heads),
             ones(7 * gw), jnp.tile(d_q_norm[l], d_heads) * qscale, ones(gw),
             jnp.tile(d_k_norm[l], d_kv), ones(P_WIDTH - 11 * gw - LANE)]).reshape(1, P_WIDTH)
        g1 = ln1[l].reshape(1, D)
        g2 = ln2[l].reshape(1, D)
        w_o = w_out[l].astype(BF16)
        w_u = w_up[l].astype(BF16)
        w_d = w_down[l].astype(BF16)
        lam_init = 0.8 - 0.6 * math.exp(-0.3 * l)
        lam = (jnp.exp(jnp.sum(a_lam_q1[l].astype(F32) * a_lam_k1[l].astype(F32)))
               - jnp.exp(jnp.sum(a_lam_q2[l].astype(F32) * a_lam_k2[l].astype(F32))) + lam_init)
        scal = jnp.stack([lam, jnp.asarray(1.0 - lam_init, F32)]).astype(F32)
        g_a = a_out_norm[l].reshape(1, LANE)
        g_c = c_out_norm[l].reshape(1, LANE)
        b_weights = (b_conv_w[l], b_conv_b[l].reshape(1, gw), _block_diag(b_wa[l]).astype(BF16),
                     b_ba[l].reshape(1, gw), _block_diag(b_wx[l]).astype(BF16),
                     b_bx[l].reshape(1, gw), b_lambda[l].reshape(1, gw))

        def dense_tail(x, mixes, tm, tmf):
            x1 = _out_proj(mixes, w_o, x, tm=tm, tn=min(512, D))
            return _ffn(x1, g2, w_u, w_d, tm=tmf, tf=min(512, w_u.shape[1]))

        Pp = _in_proj(xp, g1, w_perm, gain, flag, seg, tm=tm_p, tn=4 * LANE, norm_tiles=norm_tiles)
        mix_a = _attn_a_prompt(Pp, scal, bias_a, g_a, B=B, T=T, tq=tq_a)
        mix_b, hfin, buf = _rglru(Pp, None, b_weights, B=B, T=T, tt=tt_b, t_real=T, out_dtype=BF16)
        mix_c, sfin = _hgrn(Pp, None, lb_all[l], g_c, B=B, T=T, tt=tt_c, ch=ch_c,
                            sub=min(16, ch_c), t_real=T, out_dtype=BF16)
        mix_d = _dsa_prompt(Pp, bias_d, B=B, T=T, tq=tq_d, n_top=n_top_p)
        xp = dense_tail(xp, (mix_a, mix_b, mix_c, mix_d), tm_p, tm_f)
        P3 = Pp.reshape(B, T, P_WIDTH)
        outs_p.append((P3[..., C_AK * LANE:C_AV * LANE].reshape(B, T, a_heads, 2 * HEAD_DIM),
                       P3[..., C_AV * LANE:C_BX * LANE].reshape(B, T, a_heads, 2 * HEAD_DIM),
                       P3[..., C_DK * LANE:C_DV * LANE].reshape(B, T, d_kv, HEAD_DIM),
                       P3[..., C_DV * LANE:C_TAIL * LANE].reshape(B, T, d_kv, HEAD_DIM),
                       P3[..., C_TAIL * LANE:C_TAIL * LANE + HEAD_DIM],
                       hfin.reshape(B, gw), buf, sfin))

        Ps = _in_proj(xs, g1, w_perm, gain, flag, seg, tm=Bs * T_PAD, tn=4 * LANE,
                      norm_tiles=norm_tiles)
        smix_a = _attn_a_decode(Ps, ca_k, ca_v, page_table, l, scal, bl_a, bn_a, g_a, B=Bs, pps=pps_a)
        smix_b, shfin, sbuf = _rglru(
            Ps, (state_b_h[:, l].reshape(Bs, 1, gw), s_conv[:, l]), b_weights,
            B=Bs, T=T_PAD, tt=T_PAD, t_real=Ts, out_dtype=F32)
        smix_c, ssfin = _hgrn(Ps, state_c_s[:, l], lb_all[l], g_c, B=Bs, T=T_PAD, tt=T_PAD,
                              ch=T_PAD, sub=T_PAD, t_real=Ts, out_dtype=F32)
        keys, keyn, thr, pos = _dsa_select_decode(Ps, cd_idx, page_table, l, B=Bs, pps=pps_d,
                                                  n_top=n_top_s, t_real=Ts)
        smix_d = _dsa_attend_decode(Ps, cd_k, cd_v, page_table, l, keys, keyn, thr, pos,
                                    bl_d, bn_d, B=Bs, pps=pps_d)
        xs = dense_tail(xs, (smix_a, smix_b, smix_c, smix_d), Bs * T_PAD, Bs * T_PAD)
        S3 = Ps.reshape(Bs, T_PAD, P_WIDTH)[:, :Ts]
        outs_s.append((S3[..., C_AK * LANE:C_AV * LANE].reshape(Bs, Ts, a_heads, 2 * HEAD_DIM),
                       S3[..., C_AV * LANE:C_BX * LANE].reshape(Bs, Ts, a_heads, 2 * HEAD_DIM),
                       S3[..., C_DK * LANE:C_DV * LANE].reshape(Bs, Ts, d_kv, HEAD_DIM),
                       S3[..., C_DV * LANE:C_TAIL * LANE].reshape(Bs, Ts, d_kv, HEAD_DIM),
                       S3[..., C_TAIL * LANE:C_TAIL * LANE + HEAD_DIM],
                       shfin.reshape(Bs, gw), sbuf, ssfin))

    y_prompt = xp.reshape(B, T, D)
    y_sample = xs.reshape(Bs, T_PAD, D)[:, :Ts]
    stack = lambda outs: [jnp.stack(s, axis=1) for s in zip(*outs)]
    return (y_prompt, y_sample, *stack(outs_p), *stack(outs_s))
```

```python
import functools
import math

import jax
import jax.numpy as jnp
import numpy as np
from jax import lax
from jax.experimental import pallas as pl
from jax.experimental.pallas import tpu as pltpu

F32 = jnp.float32
BF16 = jnp.bfloat16
I32 = jnp.int32

EPS = 1e-6
HEAD_DIM = 64
PAGE = 128
CONV_W = 4
LRU_C = 8.0
IDX_HEADS = 8
TOPK_MAX = 256
N_BUCKETS = 32
MAX_DIST = 128
LANE = 128
SUBLANE = 8
T_PAD = 8
INT_MIN = -(2 ** 31)
KEY_NEG_INF = -2139095041
VMEM_LIMIT = 56 * 1024 * 1024
NEG_INF = float("-inf")

C_AQ, C_AK, C_AV, C_BX, C_BG, C_CQ, C_CF, C_CI, C_CG, C_DQ = (4 * i for i in range(10))
C_DK, C_DV, C_IQ, C_TAIL = 40, 41, 42, 46
P_WIDTH = 48 * LANE
IW_OFF = 64
D_BLOCK = 8 * LANE
D_BLOCK_IQ = (C_IQ - C_DK) * LANE
D_BLOCK_TAIL = (C_TAIL - C_DK) * LANE


def _cparams(sem):
    return pltpu.CompilerParams(dimension_semantics=sem, vmem_limit_bytes=VMEM_LIMIT)


def _nt(a, b):
    return lax.dot_general(a, b, (((1,), (1,)), ((), ())), preferred_element_type=F32)


def _tn(a, b):
    return lax.dot_general(a, b, (((0,), (0,)), ((), ())), preferred_element_type=F32)


def _rel_bucket(dist):
    n = jnp.maximum(dist, 0)
    exact = N_BUCKETS // 2
    large = exact + (jnp.log(jnp.maximum(n, 1).astype(F32) / exact)
                     / math.log(MAX_DIST / exact) * (N_BUCKETS - exact)).astype(I32)
    return jnp.where(n < exact, n, jnp.minimum(large, N_BUCKETS - 1))


def _in_proj_body(x_ref, g_ref, w_ref, wtail_ref, gain_ref, flag_ref, seg_ref, o_ref, xn_ref, *,
                  tn, norm_tiles):
    j = pl.program_id(1)

    @pl.when(j == 0)
    def _():
        x = x_ref[...]
        ms = jnp.mean(x * x, axis=-1, keepdims=True)
        xn_ref[...] = (x * lax.rsqrt(ms + EPS) * g_ref[...]).astype(BF16)

    w = jnp.where(j == pl.num_programs(1) - 1, wtail_ref[...], w_ref[...])
    y = _nt(xn_ref[...], w.astype(BF16))
    is_norm = functools.reduce(jnp.logical_or, [j == t for t in norm_tiles])

    @pl.when(is_norm)
    def _():
        seg = seg_ref[...]
        for c in range(tn // LANE):
            sl = slice(c * LANE, (c + 1) * LANE)
            yc = y[:, sl]
            y2 = yc * yc
            hi = y2.astype(BF16)
            lo = (y2 - hi.astype(F32)).astype(BF16)
            ms = (jnp.dot(hi, seg, preferred_element_type=F32)
                  + jnp.dot(lo, seg, preferred_element_type=F32))
            yn = yc * lax.rsqrt(ms + EPS) * gain_ref[:, sl]
            o_ref[:, sl] = jnp.where(flag_ref[:, sl] > 0, yn, yc)

    @pl.when(jnp.logical_not(is_norm))
    def _():
        o_ref[...] = y


IN_PROJ_TN = 4 * LANE


def _in_proj(x, g, wt, wtail, layer, gain, flag, seg, *, tm):
    M, K = x.shape
    tn = IN_PROJ_TN
    n_tiles = P_WIDTH // tn
    norm_ranges = ((C_AQ, 4), (C_AK, 4), (C_DQ, 4), (C_DK, 1))
    norm_tiles = tuple(sorted({t for c, n in norm_ranges
                               for t in range(c * LANE // tn, ((c + n) * LANE - 1) // tn + 1)}))

    return pl.pallas_call(
        functools.partial(_in_proj_body, tn=tn, norm_tiles=norm_tiles),
        grid=(M // tm, n_tiles),
        in_specs=[pl.BlockSpec((tm, K), lambda i, j: (i, 0)),
                  pl.BlockSpec((1, K), lambda i, j: (0, 0)),
                  pl.BlockSpec((None, tn, K), lambda i, j: (layer, jnp.minimum(j, n_tiles - 2), 0)),
                  pl.BlockSpec((None, tn, K), lambda i, j: (layer, 0, 0)),
                  pl.BlockSpec((1, tn), lambda i, j: (0, j)),
                  pl.BlockSpec((1, tn), lambda i, j: (0, j)),
                  pl.BlockSpec((LANE, LANE), lambda i, j: (0, 0))],
        out_specs=pl.BlockSpec((tm, tn), lambda i, j: (i, j)),
        out_shape=jax.ShapeDtypeStruct((M, P_WIDTH), F32),
        scratch_shapes=[pltpu.VMEM((tm, K), BF16)],
        compiler_params=_cparams(("parallel", "arbitrary")),
        name="in_proj")(x, g, wt, wtail, gain, flag, seg)


def _out_proj_body(a_ref, b_ref, c_ref, d_ref, w_ref, x_ref, o_ref, *, gw):
    acc = x_ref[...]
    for g, m_ref in enumerate((a_ref, b_ref, c_ref, d_ref)):
        acc = acc + jnp.dot(m_ref[...].astype(BF16), w_ref[g * gw:(g + 1) * gw, :].astype(BF16),
                            preferred_element_type=F32)
    o_ref[...] = acc


def _out_proj(mixes, w, layer, x, *, tm, tn):
    M, D = x.shape
    gw = mixes[0].shape[1]
    mix_spec = pl.BlockSpec((tm, gw), lambda i, j: (i, 0))
    return pl.pallas_call(
        functools.partial(_out_proj_body, gw=gw),
        grid=(M // tm, D // tn),
        in_specs=[mix_spec, mix_spec, mix_spec, mix_spec,
                  pl.BlockSpec((None, w.shape[1], tn), lambda i, j: (layer, 0, j)),
                  pl.BlockSpec((tm, tn), lambda i, j: (i, j))],
        out_specs=pl.BlockSpec((tm, tn), lambda i, j: (i, j)),
        out_shape=jax.ShapeDtypeStruct((M, D), F32),
        compiler_params=_cparams(("parallel", "arbitrary")),
        name="out_proj")(*mixes, w, x)


def _ffn_body(x_ref, g_ref, wu_ref, wd_ref, o_ref, xn_ref):
    f = pl.program_id(1)

    @pl.when(f == 0)
    def _():
        x = x_ref[...]
        ms = jnp.mean(x * x, axis=-1, keepdims=True)
        xn_ref[...] = (x * lax.rsqrt(ms + EPS) * g_ref[...]).astype(BF16)
        o_ref[...] = x

    h = jnp.dot(xn_ref[...], wu_ref[...].astype(BF16), preferred_element_type=F32)
    h = jnp.maximum(h, 0.0)
    h = (h * h).astype(BF16)
    o_ref[...] += jnp.dot(h, wd_ref[...].astype(BF16), preferred_element_type=F32)


def _ffn(x, g, wu, wd, layer, *, tm, tf):
    M, D = x.shape
    Fd = wu.shape[2]
    return pl.pallas_call(
        _ffn_body,
        grid=(M // tm, Fd // tf),
        in_specs=[pl.BlockSpec((tm, D), lambda i, f: (i, 0), pipeline_mode=pl.Buffered(1)),
                  pl.BlockSpec((1, D), lambda i, f: (0, 0)),
                  pl.BlockSpec((None, D, tf), lambda i, f: (layer, 0, f)),
                  pl.BlockSpec((None, tf, D), lambda i, f: (layer, f, 0))],
        out_specs=pl.BlockSpec((tm, D), lambda i, f: (i, 0)),
        out_shape=jax.ShapeDtypeStruct((M, D), F32),
        scratch_shapes=[pltpu.VMEM((tm, D), BF16)],
        compiler_params=_cparams(("parallel", "arbitrary")),
        name="ffn")(x, g, wu, wd)


def _softmax_update(state, s, v):
    m, l, acc = state
    m_new = jnp.maximum(m, jnp.max(s, axis=-1, keepdims=True))
    alpha = jnp.exp(m - m_new)
    p = jnp.exp(s - m_new)
    l = alpha * l + jnp.sum(p, axis=-1, keepdims=True)
    acc = alpha * acc + jnp.dot(p.astype(BF16), v, preferred_element_type=F32)
    return m_new, l, acc


def _rms_rows(o, gain):
    ms = jnp.mean(o * o, axis=-1, keepdims=True)
    return o * lax.rsqrt(ms + EPS) * gain


def _pair_attention(n_grp, tile_grp, n_pair, tq, logits, values, s_ref, m_ref, l_ref, acc_ref):
    def max_pass(i, carry):
        for p in range(n_pair):
            t = []
            for j in range(tile_grp):
                c = i * tile_grp + j
                s = logits(p, c)
                s_ref[p, c] = s
                t.append(s)
            while len(t) > 1:
                t = [jnp.maximum(a, b) for a, b in zip(t[::2], t[1::2])]
            m_ref[p] = jnp.maximum(m_ref[p], t[0])
        return carry

    m_ref[...] = jnp.full(m_ref.shape, NEG_INF, F32)
    lax.fori_loop(0, n_grp, max_pass, 0)
    for p in range(n_pair):
        m = m_ref[p]
        m_ref[p] = jnp.concatenate(
            [jnp.broadcast_to(jnp.max(m[:, :LANE], axis=-1, keepdims=True), (tq, LANE)),
             jnp.broadcast_to(jnp.max(m[:, LANE:], axis=-1, keepdims=True), (tq, LANE))], axis=1)
    l_ref[...] = jnp.zeros(l_ref.shape, F32)
    acc_ref[...] = jnp.zeros(acc_ref.shape, F32)

    def sum_pass(i, carry):
        for p in range(n_pair):
            e = [jnp.exp(s_ref[p, i * tile_grp + j] - m_ref[p]) for j in range(tile_grp)]
            l_ref[p] += functools.reduce(lambda a, b: a + b, e)
            pb = jnp.concatenate(
                [jnp.concatenate([x[:, :LANE], x[:, LANE:]], axis=0).astype(BF16) for x in e], axis=1)
            acc_ref[p] += jnp.dot(pb, values(p, i), preferred_element_type=F32)
        return carry

    lax.fori_loop(0, n_grp, sum_pass, 0)
    outs = []
    for p in range(n_pair):
        l = l_ref[p]
        acc = acc_ref[p]
        outs.append((acc[:tq] / jnp.sum(l[:, :LANE], axis=-1, keepdims=True),
                     acc[tq:] / jnp.sum(l[:, LANE:], axis=-1, keepdims=True)))
    return outs


def _attn_a_body(scal_ref, q_ref, k_ref, v_ref, bias_ref, gout_ref, o_ref,
                 kd_ref, vb_ref, s_ref, m_ref, l_ref, acc_ref, *, tq, tile_grp):
    qb = pl.program_id(1)
    n_h = q_ref.shape[1] // LANE
    n_grp = (qb + tile_grp) // tile_grp
    low_half = lax.broadcasted_iota(I32, (tq, LANE), 1) < HEAD_DIM

    @pl.when(qb == 0)
    def _():
        kd_ref[...] = jnp.zeros(kd_ref.shape, BF16)
        vb_ref[...] = jnp.zeros(vb_ref.shape, BF16)

    r_q = pl.multiple_of(qb * LANE, LANE)
    for h in range(n_h):
        kx = k_ref[pl.ds(r_q, LANE), h * LANE:(h + 1) * LANE]
        kd_ref[h, qb] = jnp.concatenate([jnp.where(low_half, kx, 0.0),
                                         jnp.where(low_half, 0.0, kx)], axis=0).astype(BF16)
    vb_ref[pl.ds(r_q, LANE), :] = v_ref[pl.ds(r_q, LANE), :].astype(BF16)
    q = q_ref[...].astype(BF16)

    def logits(h, c):
        return (_nt(q[:, h * LANE:(h + 1) * LANE], kd_ref[h, c])
                + bias_ref[h, jnp.clip(qb - c, -1, 2) + 1])

    def values(h, i):
        r0 = pl.multiple_of(i * tile_grp * LANE, tile_grp * LANE)
        return vb_ref[pl.ds(r0, tile_grp * LANE), h * LANE:(h + 1) * LANE]

    outs = _pair_attention(n_grp, tile_grp, n_h, tq, logits, values, s_ref, m_ref, l_ref, acc_ref)
    lam = scal_ref[0]
    for h, (o1, o2) in enumerate(outs):
        o = _rms_rows(o1 - lam * o2, gout_ref[...]) * scal_ref[1]
        o_ref[:, h * LANE:(h + 1) * LANE] = o.astype(o_ref.dtype)


def _attn_a_prompt(P, scal, bias, gout, *, B, T, tq):
    assert tq == LANE
    H = bias.shape[0]
    W = H * LANE
    nq = T // tq
    tile_grp = math.gcd(nq, 4)
    return pl.pallas_call(
        functools.partial(_attn_a_body, tq=tq, tile_grp=tile_grp),
        grid=(B, nq),
        in_specs=[pl.BlockSpec(memory_space=pltpu.SMEM),
                  pl.BlockSpec((tq, W), lambda b, i: (b * nq + i, C_AQ // H)),
                  pl.BlockSpec((T, W), lambda b, i: (b, C_AK // H)),
                  pl.BlockSpec((T, W), lambda b, i: (b, C_AV // H)),
                  pl.BlockSpec(bias.shape, lambda b, i: (0, 0, 0, 0)),
                  pl.BlockSpec((1, LANE), lambda b, i: (0, 0))],
        out_specs=pl.BlockSpec((tq, W), lambda b, i: (b * nq + i, 0)),
        out_shape=jax.ShapeDtypeStruct((B * T, W), BF16),
        scratch_shapes=[pltpu.VMEM((H, nq, 2 * LANE, LANE), BF16),
                        pltpu.VMEM((T, W), BF16),
                        pltpu.VMEM((H, nq, tq, 2 * LANE), F32),
                        pltpu.VMEM((H, tq, 2 * LANE), F32),
                        pltpu.VMEM((H, tq, 2 * LANE), F32),
                        pltpu.VMEM((H, 2 * tq, LANE), F32)],
        compiler_params=_cparams(("parallel", "arbitrary")),
        name="attn_a_prompt")(scal, P, P, P, bias, gout)


def _rglru_body(*refs, tt, t_real, n_t, has_state, pos0_is_zero):
    if has_state:
        (x_ref, g_ref, h0_ref, buf0_ref, cw_ref, cb_ref, wa_ref, ba_ref, wx_ref, bxb_ref, lam_ref,
         o_ref, hfin_ref, buf_ref, xpad_ref, a_ref, b_ref, hs_ref, hc_ref) = refs
    else:
        (x_ref, g_ref, cw_ref, cb_ref, wa_ref, ba_ref, wx_ref, bxb_ref, lam_ref,
         o_ref, hfin_ref, buf_ref, xpad_ref, a_ref, b_ref, hs_ref, hc_ref) = refs
    ti = pl.program_id(1)
    W = x_ref.shape[1]

    @pl.when(ti == 0)
    def _():
        if has_state:
            xpad_ref[0:SUBLANE, :] = buf0_ref[0]
            hc_ref[...] = jnp.broadcast_to(h0_ref[0], (SUBLANE, W))
        else:
            xpad_ref[0:SUBLANE, :] = jnp.zeros((SUBLANE, W), F32)
            hc_ref[...] = jnp.zeros((SUBLANE, W), F32)

    x = x_ref[...]
    xpad_ref[SUBLANE:SUBLANE + tt, :] = x
    xc = cb_ref[...] + x * cw_ref[CONV_W - 1:CONV_W, :]
    for j in range(CONV_W - 1):
        off = SUBLANE - (CONV_W - 1) + j
        xc = xc + xpad_ref[off:off + tt, :] * cw_ref[j:j + 1, :]
    xcb = xc.astype(BF16)
    r = jax.nn.sigmoid(jnp.dot(xcb, wa_ref[...], preferred_element_type=F32) + ba_ref[...])
    i = jax.nn.sigmoid(jnp.dot(xcb, wx_ref[...], preferred_element_type=F32) + bxb_ref[...])
    nl = -lam_ref[...]
    softplus = jnp.maximum(nl, 0.0) + jnp.log1p(jnp.exp(-jnp.abs(nl)))
    a = jnp.exp(-LRU_C * r * softplus)
    mult = jnp.sqrt(1.0 - a * a)
    if pos0_is_zero:
        rows = lax.broadcasted_iota(I32, (tt, W), 0)
        mult = jnp.where(jnp.logical_and(rows == 0, ti == 0), 1.0, mult)
    a_ref[...] = a
    b_ref[...] = mult * (i * xc)
    rowt = lax.broadcasted_iota(I32, (SUBLANE, W), 0)

    def tile(n, h_prev):
        r0 = pl.multiple_of(n * SUBLANE, SUBLANE)
        at = a_ref[pl.ds(r0, SUBLANE), :]
        bt = b_ref[pl.ds(r0, SUBLANE), :]
        for s in (1, 2, 4):
            keep = rowt >= s
            bt = jnp.where(keep, bt + at * pltpu.roll(bt, s, 0), bt)
            at = jnp.where(keep, at * pltpu.roll(at, s, 0), at)
        ht = bt + at * h_prev
        hs_ref[pl.ds(r0, SUBLANE), :] = ht
        return jnp.broadcast_to(ht[SUBLANE - 1:SUBLANE, :], (SUBLANE, W))

    hc_ref[...] = lax.fori_loop(0, tt // SUBLANE, tile, hc_ref[...])
    o_ref[...] = (hs_ref[...] * jax.nn.gelu(g_ref[...])).astype(o_ref.dtype)
    xpad_ref[0:SUBLANE, :] = xpad_ref[tt:tt + SUBLANE, :]

    @pl.when(ti == n_t - 1)
    def _():
        t_loc = t_real - (n_t - 1) * tt
        hfin_ref[0] = hs_ref[t_loc - 1:t_loc, :]
        buf_ref[0] = x_ref[t_loc - (CONV_W - 1):t_loc, :]


def _rglru(P, state, weights, *, B, T, tt, t_real, out_dtype):
    W = 4 * LANE
    n_t = T // tt
    has_state = state is not None
    xspec = pl.BlockSpec((tt, W), lambda b, i: (b * n_t + i, C_BX // 4))
    gspec = pl.BlockSpec((tt, W), lambda b, i: (b * n_t + i, C_BG // 4))
    full = lambda shape: pl.BlockSpec(shape, lambda b, i: (0,) * len(shape))
    wspecs = [full((CONV_W, W)), full((1, W)), full((W, W)), full((1, W)), full((W, W)),
              full((1, W)), full((1, W))]
    in_specs = [xspec, gspec]
    args = [P, P]
    if has_state:
        in_specs += [pl.BlockSpec((1, 1, W), lambda b, i: (b, 0, 0)),
                     pl.BlockSpec((1, SUBLANE, W), lambda b, i: (b, 0, 0))]
        args += list(state)
    return pl.pallas_call(
        functools.partial(_rglru_body, tt=tt, t_real=t_real, n_t=n_t, has_state=has_state,
                          pos0_is_zero=not has_state),
        grid=(B, n_t),
        in_specs=in_specs + wspecs,
        out_specs=[pl.BlockSpec((tt, W), lambda b, i: (b * n_t + i, 0)),
                   pl.BlockSpec((1, 1, W), lambda b, i: (b, 0, 0)),
                   pl.BlockSpec((1, CONV_W - 1, W), lambda b, i: (b, 0, 0))],
        out_shape=[jax.ShapeDtypeStruct((B * T, W), out_dtype),
                   jax.ShapeDtypeStruct((B, 1, W), F32),
                   jax.ShapeDtypeStruct((B, CONV_W - 1, W), F32)],
        scratch_shapes=[pltpu.VMEM((tt + SUBLANE, W), F32), pltpu.VMEM((tt, W), F32),
                        pltpu.VMEM((tt, W), F32), pltpu.VMEM((tt, W), F32),
                        pltpu.VMEM((SUBLANE, W), F32)],
        compiler_params=_cparams(("parallel", "arbitrary")),
        name="rglru")(*args, *weights)


def _hgrn_body(*refs, tt, ch, sub, t_real, n_t, has_state):
    if has_state:
        q_ref, f_ref, v_ref, g_ref, s0_ref, lb_ref, gain_ref, o_ref, sfin_ref, st_ref = refs
    else:
        q_ref, f_ref, v_ref, g_ref, lb_ref, gain_ref, o_ref, sfin_ref, st_ref = refs
    ti = pl.program_id(1)
    n_h = st_ref.shape[0]
    dk = LANE

    @pl.when(ti == 0)
    def _():
        for h in range(n_h):
            if has_state:
                st_ref[h] = s0_ref[0, h].T
            else:
                st_ref[h] = jnp.zeros((dk, dk), F32)

    rr = lax.broadcasted_iota(I32, (ch, ch), 0)
    cc = lax.broadcasted_iota(I32, (ch, ch), 1)
    tril = (cc <= rr).astype(F32)
    rows = lax.broadcasted_iota(I32, (ch, dk), 0)
    srow = lax.broadcasted_iota(I32, (sub, 1), 0)
    n_sub = ch // sub

    def head_chunk(c, h):
        r0 = pl.multiple_of(c * ch, ch)
        hs = slice(h * dk, (h + 1) * dk)
        lb = lb_ref[h]
        log_lb = jnp.log(lb)
        log_1mlb = jnp.log1p(-lb)
        q = q_ref[pl.ds(r0, ch), hs]
        q = q * jax.nn.sigmoid(q)
        fp = f_ref[pl.ds(r0, ch), hs]
        v = v_ref[pl.ds(r0, ch), hs]
        log_sig = jnp.minimum(fp, 0.0) - jnp.log1p(jnp.exp(-jnp.abs(fp)))
        b = log_1mlb + log_sig
        lf = jnp.maximum(log_lb, b) + jnp.log1p(jnp.exp(-jnp.abs(log_lb - b)))
        kk = (1.0 - lb) * jax.nn.sigmoid(-fp)
        if t_real < tt * n_t:
            live = (rows + (ti * tt + c * ch)) < t_real
            lf = jnp.where(live, lf, 0.0)
            kk = jnp.where(live, kk, 0.0)
        G = jnp.dot(tril, lf, preferred_element_type=F32, precision=lax.Precision.HIGHEST)
        st = st_ref[h]
        vb = v.astype(BF16)
        o = _nt((q * jnp.exp(G)).astype(BF16), st.astype(BF16))
        outs = []
        for i in range(n_sub):
            lo, hi = i * sub, (i + 1) * sub
            qi, Gi, ki, vi = q[lo:hi], G[lo:hi], kk[lo:hi], v[lo:hi]
            oi = o[lo:hi]
            if i > 0:
                R = G[lo - 1:lo]
                qp = (qi * jnp.exp(Gi - R)).astype(BF16)
                kp = (kk[:lo] * jnp.exp(R - G[:lo])).astype(BF16)
                att = _nt(qp, kp)
                oi = oi + jnp.dot(att.astype(BF16), vb[:lo], preferred_element_type=F32)
            for s in range(sub):
                w = jnp.exp(jnp.minimum(Gi - Gi[s:s + 1], 0.0))
                colv = jnp.sum(qi * w * ki[s:s + 1], axis=-1, keepdims=True)
                colv = jnp.where(srow >= s, colv, 0.0)
                oi = oi + colv * vi[s:s + 1]
            outs.append(oi)
        o = jnp.concatenate(outs, axis=0) if n_sub > 1 else outs[0]
        gl = G[ch - 1:ch]
        kpp = (kk * jnp.exp(gl - G)).astype(BF16)
        st_ref[h] = st * jnp.exp(gl) + _tn(vb, kpp)
        gate = g_ref[pl.ds(r0, ch), hs]
        o = _rms_rows(o, gain_ref[...]) * (gate * jax.nn.sigmoid(gate))
        o_ref[pl.ds(r0, ch), hs] = o.astype(o_ref.dtype)

    def chunk(c, carry):
        for h in range(n_h):
            head_chunk(c, h)
        return carry

    lax.fori_loop(0, tt // ch, chunk, 0)

    @pl.when(ti == n_t - 1)
    def _():
        for h in range(n_h):
            sfin_ref[0, h] = st_ref[h].T


def _hgrn(P, s0, lb, gain, *, B, T, tt, ch, sub, t_real, out_dtype):
    H = lb.shape[0]
    n_t = T // tt
    has_state = s0 is not None

    W = H * LANE

    def col(c0):
        return pl.BlockSpec((tt, W), lambda b, i: (b * n_t + i, c0 // H))

    in_specs = [col(C_CQ), col(C_CF), col(C_CI), col(C_CG)]
    args = [P, P, P, P]
    if has_state:
        in_specs.append(pl.BlockSpec((1, H, LANE, LANE), lambda b, i: (b, 0, 0, 0)))
        args.append(s0)
    in_specs += [pl.BlockSpec((H, 1, LANE), lambda b, i: (0, 0, 0)),
                 pl.BlockSpec((1, LANE), lambda b, i: (0, 0))]
    return pl.pallas_call(
        functools.partial(_hgrn_body, tt=tt, ch=ch, sub=sub, t_real=t_real, n_t=n_t,
                          has_state=has_state),
        grid=(B, n_t),
        in_specs=in_specs,
        out_specs=[pl.BlockSpec((tt, W), lambda b, i: (b * n_t + i, 0)),
                   pl.BlockSpec((1, H, LANE, LANE), lambda b, i: (b, 0, 0, 0))],
        out_shape=[jax.ShapeDtypeStruct((B * T, W), out_dtype),
                   jax.ShapeDtypeStruct((B, H, LANE, LANE), F32)],
        scratch_shapes=[pltpu.VMEM((H, LANE, LANE), F32)],
        compiler_params=_cparams(("parallel", "arbitrary")),
        name="hgrn2")(*args, lb, gain)


def _score_key(score):
    score = jnp.where(score == 0.0, 0.0, score)
    bits = pltpu.bitcast(score, I32)
    return bits ^ ((bits >> 31) & 0x7FFFFFFF)


def _kth_largest(count_ge, shape, k, bits=32):
    zero = jnp.zeros(shape, I32)
    v = jnp.where(count_ge(zero) >= k, zero, jnp.full(shape, -(2 ** (bits - 1)), I32))

    def bit_step(n, v):
        cand = v + (jnp.int32(1) << (bits - 2 - n))
        return jnp.where(count_ge(cand) >= k, cand, v)

    return lax.fori_loop(0, bits - 1, bit_step, v)


def _tie_bound(count_tie_below, r, shape, n_bits):
    def bit_step(n, pos):
        cand = pos + (jnp.int32(1) << (n_bits - 1 - n))
        return jnp.where(count_tie_below(cand) < r, cand, pos)

    return lax.fori_loop(0, n_bits, bit_step, jnp.zeros(shape, I32))


def _dsa_body(dq_ref, iq_ref, tq_ref, dk_ref, dv_ref, ik_ref, bias_ref, o_ref,
              key_ref, hi_ref, lo_ref, mask_ref, pos_ref, kd_ref, vb_ref, ikb_ref,
              s_ref, m_ref, l_ref, acc_ref, *, tq, n_top, n_bits, tile_grp):
    qb = pl.program_id(1)
    n_c = qb + 1
    hd = HEAD_DIM
    n_kv = dk_ref.shape[1] // hd
    n_q = dq_ref.shape[1] // hd
    grp = n_q // n_kv
    n_pair = n_q // 2
    row = lax.broadcasted_iota(I32, (tq, LANE), 0)
    col = lax.broadcasted_iota(I32, (tq, LANE), 1)
    low_half = col < hd

    @pl.when(qb == 0)
    def _():
        kd_ref[...] = jnp.zeros(kd_ref.shape, BF16)
        vb_ref[...] = jnp.zeros(vb_ref.shape, BF16)
        ikb_ref[...] = jnp.zeros(ikb_ref.shape, BF16)

    def pair_blocks(at_low, at_high):
        return jnp.concatenate([jnp.where(low_half, at_low, 0.0),
                                jnp.where(low_half, 0.0, at_high)], axis=0).astype(BF16)

    r_q = pl.multiple_of(qb * LANE, LANE)
    kx = dk_ref[pl.ds(r_q, LANE), :]
    kx_sw = pltpu.roll(kx, hd, 1)
    kd_ref[0, qb] = pair_blocks(kx, kx_sw)
    kd_ref[1, qb] = pair_blocks(kx_sw, kx)
    ikb_ref[pl.ds(r_q, LANE), :] = ik_ref[pl.ds(r_q, LANE), :].astype(BF16)
    vb_ref[pl.ds(r_q, LANE), :] = dv_ref[pl.ds(r_q, LANE), :].astype(BF16)

    iq = iq_ref[:, D_BLOCK_IQ:D_BLOCK_IQ + IDX_HEADS * hd]
    iqz = []
    for p in range(IDX_HEADS // 2):
        pair = iq[:, p * LANE:(p + 1) * LANE]
        iqz.append(jnp.concatenate([jnp.where(low_half, pair, 0.0),
                                    jnp.where(low_half, pltpu.roll(pair, hd, 1), 0.0)],
                                   axis=0).astype(BF16))
    wscale = IDX_HEADS ** -0.5 * hd ** -0.5
    tail_t = tq_ref[...].T
    w_rows = [tail_t[IW_OFF + h:IW_OFF + h + 1, :] * wscale for h in range(IDX_HEADS)]

    n_grp = (n_c + tile_grp - 1) // tile_grp

    def score_tile(c):
        r0 = pl.multiple_of(c * LANE, LANE)
        ikc = ikb_ref[pl.ds(r0, LANE), :]
        sc = None
        for p in range(IDX_HEADS // 2):
            s = jnp.maximum(_nt(ikc, iqz[p]), 0.0)
            t = s[:, :LANE] * w_rows[2 * p] + s[:, LANE:] * w_rows[2 * p + 1]
            sc = t if sc is None else sc + t
        causal = jnp.logical_or(c < qb, jnp.logical_and(c == qb, row <= col))
        key = _score_key(jnp.where(causal, sc, NEG_INF))
        key_ref[c] = key
        hi_ref[c] = (key >> 16).astype(jnp.int16)
        lo_ref[c] = (((key ^ 0x8000) << 16) >> 16).astype(jnp.int16)

    def score_group(i, carry):
        for j in range(tile_grp):
            score_tile(i * tile_grp + j)
        return carry

    lax.fori_loop(0, n_grp, score_group, 0)
    vec = (1, tq)

    def count(pred, ref=key_ref, dtype=I32):
        def body(i, acc):
            for j in range(tile_grp):
                c = i * tile_grp + j
                acc = acc + pred(ref[c], c).astype(dtype)
            return acc
        acc = lax.fori_loop(0, n_grp, body, jnp.zeros((LANE, tq), dtype))
        return jnp.sum(acc.astype(I32), axis=0, keepdims=True)

    def count16_ge(ref):
        def f(cand):
            cb = jnp.broadcast_to(cand, (LANE, tq)).astype(jnp.int16)
            return count(lambda key, c: key >= cb, ref, jnp.int16)
        return f

    min16 = -(2 ** 15)
    t_hi = _kth_largest(count16_ge(hi_ref), vec, n_top, bits=16)
    t_hi_b = jnp.broadcast_to(t_hi, (LANE, tq)).astype(jnp.int16)
    k_lo = n_top - count(lambda key, c: key > t_hi_b, hi_ref, jnp.int16)

    def band_group(i, carry):
        for j in range(tile_grp):
            c = i * tile_grp + j
            lo_ref[c] = jnp.where(hi_ref[c] == t_hi_b, lo_ref[c], jnp.int16(min16))
        return carry

    lax.fori_loop(0, n_grp, band_group, 0)
    t_lo = _kth_largest(count16_ge(lo_ref), vec, k_lo, bits=16)
    thr = (t_hi << 16) | ((t_lo ^ min16) & 0xFFFF)
    thr_b = jnp.broadcast_to(thr, (LANE, tq))
    n_gt = count(lambda key, c: key > thr_b)
    n_ge = count(lambda key, c: key >= thr_b)
    need = jnp.logical_and(n_ge > n_top, thr > KEY_NEG_INF)
    pos_ref[...] = jnp.full((LANE, tq), 2 ** n_bits, I32)

    @pl.when(jnp.max(need.astype(I32)) > 0)
    def _():
        r = n_top - n_gt

        def count_tie_below(cand):
            cb = jnp.broadcast_to(cand, (LANE, tq))
            return count(lambda key, c: jnp.logical_and(key == thr_b, row + c * LANE < cb))

        pos = _tie_bound(count_tie_below, r, vec, n_bits)
        pos_ref[...] = jnp.broadcast_to(pos, (LANE, tq))

    pos_b = pos_ref[...]

    def mask_group(i, carry):
        for j in range(tile_grp):
            c = i * tile_grp + j
            key = key_ref[c]
            idx = row + c * LANE
            sel = jnp.logical_or(key > thr_b, jnp.logical_and(key == thr_b, idx <= pos_b))
            sel = jnp.logical_and(sel, key > KEY_NEG_INF)
            mask_ref[c] = jnp.where(sel, 0.0, NEG_INF).T
        return carry

    lax.fori_loop(0, n_grp, mask_group, 0)

    dq = dq_ref[...].astype(BF16)

    def logits(p, c):
        mk = mask_ref[c]
        return (_nt(dq[:, p * LANE:(p + 1) * LANE], kd_ref[(2 * p) // grp, c])
                + jnp.concatenate([mk, mk], axis=1) + bias_ref[p, jnp.clip(qb - c, 0, 2)])

    def values(p, i):
        r0 = pl.multiple_of(i * tile_grp * LANE, tile_grp * LANE)
        return vb_ref[pl.ds(r0, tile_grp * LANE), :]

    outs = _pair_attention(n_grp, tile_grp, n_pair, tq, logits, values, s_ref, m_ref, l_ref, acc_ref)
    for p, (oa, ob) in enumerate(outs):
        if (2 * p) // grp == 0:
            ob = pltpu.roll(ob, hd, 1)
        else:
            oa = pltpu.roll(oa, hd, 1)
        o_ref[:, p * LANE:(p + 1) * LANE] = jnp.where(low_half, oa, ob).astype(o_ref.dtype)


def _dsa_prompt(P, bias, *, B, T, tq, n_top):
    assert tq == LANE
    nq = T // tq
    n_bits = max(1, int(math.ceil(math.log2(T))))
    tile_grp = math.gcd(nq, 4)
    n_pair = bias.shape[0]
    n_tiles = T // LANE
    return pl.pallas_call(
        functools.partial(_dsa_body, tq=tq, n_top=n_top, n_bits=n_bits, tile_grp=tile_grp),
        grid=(B, nq),
        in_specs=[pl.BlockSpec((tq, 4 * LANE), lambda b, i: (b * nq + i, C_DQ // 4)),
                  pl.BlockSpec((tq, D_BLOCK), lambda b, i: (b * nq + i, C_DK * LANE // D_BLOCK)),
                  pl.BlockSpec((tq, LANE), lambda b, i: (b * nq + i, C_TAIL)),
                  pl.BlockSpec((T, LANE), lambda b, i: (b, C_DK)),
                  pl.BlockSpec((T, LANE), lambda b, i: (b, C_DV)),
                  pl.BlockSpec((T, LANE), lambda b, i: (b, C_TAIL)),
                  pl.BlockSpec(bias.shape, lambda b, i: (0, 0, 0, 0))],
        out_specs=pl.BlockSpec((tq, 4 * LANE), lambda b, i: (b * nq + i, 0)),
        out_shape=jax.ShapeDtypeStruct((B * T, 4 * LANE), BF16),
        scratch_shapes=[pltpu.VMEM((n_tiles, tq, LANE), I32),
                        pltpu.VMEM((n_tiles, tq, LANE), jnp.int16),
                        pltpu.VMEM((n_tiles, tq, LANE), jnp.int16),
                        pltpu.VMEM((n_tiles, tq, LANE), F32),
                        pltpu.VMEM((tq, LANE), I32),
                        pltpu.VMEM((2, n_tiles, 2 * LANE, LANE), BF16),
                        pltpu.VMEM((T, LANE), BF16),
                        pltpu.VMEM((T, LANE), BF16),
                        pltpu.VMEM((n_pair, n_tiles, tq, 2 * LANE), F32),
                        pltpu.VMEM((n_pair, tq, 2 * LANE), F32),
                        pltpu.VMEM((n_pair, tq, 2 * LANE), F32),
                        pltpu.VMEM((n_pair, 2 * tq, LANE), F32)],
        compiler_params=_cparams(("parallel", "arbitrary")),
        name="dsa_prompt")(P, P, P, P, P, P, bias)


def _attn_a_dec_body(pt_ref, scal_ref, q_ref, kn_ref, vn_ref, bl_ref, bn_ref, gout_ref, *rest,
                     pps, n_steps):
    k_refs = rest[:pps]
    v_refs = rest[pps:2 * pps]
    o_ref, m_ref, l_ref, acc_ref = rest[2 * pps:]
    s_i = pl.program_id(1)
    n_h = q_ref.shape[1] // LANE
    rows_h = 2 * T_PAD

    @pl.when(s_i == 0)
    def _():
        m_ref[...] = jnp.full(m_ref.shape, NEG_INF, F32)
        l_ref[...] = jnp.zeros(l_ref.shape, F32)
        acc_ref[...] = jnp.zeros(acc_ref.shape, F32)

    q = q_ref[...]
    lane = lax.broadcasted_iota(I32, (T_PAD, LANE), 1)

    def q_head(h):
        qh = q[:, h * LANE:(h + 1) * LANE]
        return jnp.concatenate([jnp.where(lane < HEAD_DIM, qh, 0.0),
                                jnp.where(lane >= HEAD_DIM, qh, 0.0)], axis=0).astype(BF16)

    qs = [q_head(h) for h in range(n_h)]
    q_all = jnp.concatenate(qs, axis=0)
    is_last = s_i == n_steps - 1

    n_rows = n_h * rows_h
    page_w = PAGE * n_h
    row_head = lax.broadcasted_iota(I32, (n_rows, page_w), 0) // rows_h
    col_head = lax.broadcasted_iota(I32, (n_rows, page_w), 1) % n_h
    head_mask = jnp.where(row_head == col_head, 0.0, NEG_INF)
    last_bias = jnp.where(is_last, bl_ref[...], 0.0) + head_mask
    parts = [_nt(q_all, r[...].astype(BF16)) + (last_bias if i == pps - 1 else head_mask)
             for i, r in enumerate(k_refs)]
    v_all = jnp.concatenate([r[...].astype(BF16) for r in v_refs], axis=0)
    m, l, acc = _softmax_update((m_ref[...], l_ref[...], acc_ref[...]),
                                jnp.concatenate(parts, axis=1), v_all)
    m_ref[...] = m
    l_ref[...] = l
    acc_ref[...] = acc

    @pl.when(is_last)
    def _():
        pad = jnp.zeros((LANE - T_PAD, LANE), BF16)
        m_all, l_all, acc_all = m_ref[...], l_ref[...], acc_ref[...]
        new = []
        for h in range(n_h):
            hs = slice(h * LANE, (h + 1) * LANE)
            rs = slice(h * rows_h, (h + 1) * rows_h)
            kn = jnp.concatenate([kn_ref[:, hs].astype(BF16), pad], axis=0)
            vn = jnp.concatenate([vn_ref[:, hs].astype(BF16), pad], axis=0)
            s = _nt(qs[h], kn) + bn_ref[rs]
            new.append(_softmax_update((m_all[rs], l_all[rs], acc_all[rs]), s, vn))
        m_ref[...] = jnp.concatenate([x[0] for x in new], axis=0)
        l_ref[...] = jnp.concatenate([x[1] for x in new], axis=0)
        acc_ref[...] = jnp.concatenate([x[2] for x in new], axis=0)
        lam = scal_ref[0]
        for h in range(n_h):
            r1 = slice(h * rows_h, h * rows_h + T_PAD)
            r2 = slice(h * rows_h + T_PAD, (h + 1) * rows_h)
            o = acc_ref[r1] / l_ref[r1] - lam * (acc_ref[r2] / l_ref[r2])
            o_ref[:, h * LANE:(h + 1) * LANE] = _rms_rows(o, gout_ref[...]) * scal_ref[1]


def _attn_a_decode(P, cache_k, cache_v, page_table, layer, scal, bias_last, bias_new, gout, *, B, pps):
    n_pages = page_table.shape[1]
    n_steps = n_pages // pps
    W = 4 * LANE

    def page_spec(i):
        return pl.BlockSpec((None, None) + cache_k.shape[2:],
                            lambda b, s, pt: (pt[b, s * pps + i], layer, 0, 0))

    grid_spec = pltpu.PrefetchScalarGridSpec(
        num_scalar_prefetch=1,
        grid=(B, n_steps),
        in_specs=[pl.BlockSpec(memory_space=pltpu.SMEM),
                  pl.BlockSpec((T_PAD, W), lambda b, s, pt: (b, C_AQ // 4)),
                  pl.BlockSpec((T_PAD, W), lambda b, s, pt: (b, C_AK // 4)),
                  pl.BlockSpec((T_PAD, W), lambda b, s, pt: (b, C_AV // 4)),
                  pl.BlockSpec(bias_last.shape, lambda b, s, pt: (0, 0)),
                  pl.BlockSpec(bias_new.shape, lambda b, s, pt: (0, 0)),
                  pl.BlockSpec((1, LANE), lambda b, s, pt: (0, 0))]
        + [page_spec(i) for i in range(pps)] + [page_spec(i) for i in range(pps)],
        out_specs=pl.BlockSpec((T_PAD, W), lambda b, s, pt: (b, 0)),
        scratch_shapes=[pltpu.VMEM((bias_last.shape[0], 1), F32),
                        pltpu.VMEM((bias_last.shape[0], 1), F32),
                        pltpu.VMEM((bias_last.shape[0], LANE), F32)])
    return pl.pallas_call(
        functools.partial(_attn_a_dec_body, pps=pps, n_steps=n_steps),
        grid_spec=grid_spec,
        out_shape=jax.ShapeDtypeStruct((B * T_PAD, W), F32),
        compiler_params=_cparams(("parallel", "arbitrary")),
        name="attn_a_decode")(page_table, scal, P, P, P, bias_last, bias_new, gout,
                              *([cache_k] * pps), *([cache_v] * pps))


def _dsa_sel_body(pt_ref, iq_ref, tq_ref, ikn_ref, *rest, pps, n_steps, n_top, t_real, n_bits):
    ik_refs = rest[:pps]
    keys_ref, keyn_ref, thr_ref, pos_ref, all_ref, wb_ref = rest[pps:]
    s_i = pl.program_id(1)
    hd = HEAD_DIM
    iq = iq_ref[:, D_BLOCK_IQ:D_BLOCK_IQ + IDX_HEADS * hd].astype(BF16)
    q_idx = jnp.concatenate([iq[:, h * hd:(h + 1) * hd] for h in range(IDX_HEADS)], axis=0)
    wscale = IDX_HEADS ** -0.5 * hd ** -0.5
    tail = tq_ref[...]
    for h in range(IDX_HEADS):
        wb_ref[h * T_PAD:(h + 1) * T_PAD, :] = jnp.broadcast_to(
            tail[:, IW_OFF + h:IW_OFF + h + 1] * wscale, (T_PAD, LANE))

    def score(raw):
        s = jnp.maximum(raw, 0.0) * wb_ref[...]
        sc = s[0:T_PAD]
        for h in range(1, IDX_HEADS):
            sc = sc + s[h * T_PAD:(h + 1) * T_PAD]
        return sc

    for i in range(pps):
        raw = jnp.dot(q_idx, ik_refs[i][...].astype(BF16), preferred_element_type=F32)
        key = _score_key(score(raw))
        keys_ref[0, i] = key
        all_ref[s_i * pps + i] = key

    n_tiles = n_steps * pps + 1

    @pl.when(s_i == n_steps - 1)
    def _():
        row = lax.broadcasted_iota(I32, (T_PAD, LANE), 0)
        col = lax.broadcasted_iota(I32, (T_PAD, LANE), 1)
        pad = jnp.zeros((LANE - T_PAD, hd), F32)
        ikn = jnp.concatenate([ikn_ref[...][:, :hd], pad], axis=0)
        valid = jnp.logical_and(col <= row, col < t_real)
        keyn = _score_key(jnp.where(valid, score(_nt(q_idx, ikn.astype(BF16))), NEG_INF))
        keyn_ref[0] = keyn
        all_ref[n_tiles - 1] = keyn
        idx = (lax.broadcasted_iota(I32, all_ref.shape, 0) * LANE
               + lax.broadcasted_iota(I32, all_ref.shape, 2))

        def count(pred):
            acc = jnp.sum(pred(all_ref[...]).astype(I32), axis=0)
            return jnp.sum(acc, axis=-1, keepdims=True)

        def count_ge(cand):
            cb = jnp.broadcast_to(cand, (T_PAD, LANE))
            return count(lambda key: key >= cb)

        thr = _kth_largest(count_ge, (T_PAD, 1), n_top)
        thr_b = jnp.broadcast_to(thr, (T_PAD, LANE))
        n_gt = count(lambda key: key > thr_b)
        r = n_top - n_gt

        def count_tie_below(cand):
            cb = jnp.broadcast_to(cand, (T_PAD, LANE))
            return count(lambda key: jnp.logical_and(key == thr_b, idx < cb))

        pos = _tie_bound(count_tie_below, r, (T_PAD, 1), n_bits)
        thr_ref[0] = thr_b
        pos_ref[0] = jnp.broadcast_to(pos, (T_PAD, LANE))


def _dsa_select_decode(P, cache_idx, page_table, layer, *, B, pps, n_top, t_real):
    n_pages = page_table.shape[1]
    n_steps = n_pages // pps
    n_tiles = n_pages + 1
    n_bits = int(math.ceil(math.log2(n_tiles * LANE)))

    def page_spec(i):
        return pl.BlockSpec((None, None) + cache_idx.shape[2:],
                            lambda b, s, pt: (pt[b, s * pps + i], layer, 0, 0))

    tile_spec = pl.BlockSpec((1, T_PAD, LANE), lambda b, s, pt: (b, 0, 0))
    grid_spec = pltpu.PrefetchScalarGridSpec(
        num_scalar_prefetch=1,
        grid=(B, n_steps),
        in_specs=[pl.BlockSpec((T_PAD, D_BLOCK), lambda b, s, pt: (b, C_DK * LANE // D_BLOCK)),
                  pl.BlockSpec((T_PAD, LANE), lambda b, s, pt: (b, C_TAIL)),
                  pl.BlockSpec((T_PAD, LANE), lambda b, s, pt: (b, C_TAIL))]
        + [page_spec(i) for i in range(pps)],
        out_specs=[pl.BlockSpec((1, pps, T_PAD, LANE), lambda b, s, pt: (b, s, 0, 0)),
                   tile_spec, tile_spec, tile_spec],
        scratch_shapes=[pltpu.VMEM((n_tiles, T_PAD, LANE), I32),
                        pltpu.VMEM((IDX_HEADS * T_PAD, LANE), F32)])
    tile_shape = jax.ShapeDtypeStruct((B, T_PAD, LANE), I32)
    return pl.pallas_call(
        functools.partial(_dsa_sel_body, pps=pps, n_steps=n_steps, n_top=n_top, t_real=t_real,
                          n_bits=n_bits),
        grid_spec=grid_spec,
        out_shape=[jax.ShapeDtypeStruct((B, n_pages, T_PAD, LANE), I32),
                   tile_shape, tile_shape, tile_shape],
        compiler_params=_cparams(("parallel", "arbitrary")),
        name="dsa_select_decode")(page_table, P, P, P, *([cache_idx] * pps))


def _dsa_att_body(pt_ref, dq_ref, kn_ref, vn_ref, keys_ref, keyn_ref, thr_ref, pos_ref,
                  bl_ref, bn_ref, *rest, pps, n_steps):
    k_refs = rest[:pps]
    v_refs = rest[pps:2 * pps]
    o_ref, m_ref, l_ref, acc_ref = rest[2 * pps:]
    s_i = pl.program_id(1)
    hd = HEAD_DIM
    n_kv = kn_ref.shape[1] // hd
    n_q = dq_ref.shape[1] // hd
    grp = n_q // n_kv
    rows_g = grp * T_PAD

    @pl.when(s_i == 0)
    def _():
        m_ref[...] = jnp.full(m_ref.shape, -1e30, F32)
        l_ref[...] = jnp.zeros(l_ref.shape, F32)
        acc_ref[...] = jnp.zeros(acc_ref.shape, F32)

    dq = dq_ref[...].astype(BF16)
    q_g = [jnp.concatenate([dq[:, (g * grp + j) * hd:(g * grp + j + 1) * hd] for j in range(grp)],
                           axis=0) for g in range(n_kv)]
    thr = thr_ref[0]
    pos = pos_ref[0]
    col = lax.broadcasted_iota(I32, (T_PAD, LANE), 1)
    is_last = s_i == n_steps - 1

    def sel_mask(key, tile):
        idx = col + tile * LANE
        sel = jnp.logical_or(key > thr, jnp.logical_and(key == thr, idx <= pos))
        sel = jnp.logical_and(sel, key > KEY_NEG_INF)
        return jnp.concatenate([sel] * grp, axis=0)

    def attend(tiles, masks, bias_tiles):
        m_all, l_all, acc_all = m_ref[...], l_ref[...], acc_ref[...]
        new = []
        for g in range(n_kv):
            rs = slice(g * rows_g, (g + 1) * rows_g)
            logit_tiles, pv = tiles(g)
            parts = []
            for s, mk, bt in zip(logit_tiles, masks, bias_tiles):
                if bt is not None:
                    s = s + bt[rs]
                parts.append(jnp.where(mk, s, NEG_INF))
            s = jnp.concatenate(parts, axis=1) if len(parts) > 1 else parts[0]
            m_new = jnp.maximum(m_all[rs], jnp.max(s, axis=-1, keepdims=True))
            alpha = jnp.exp(m_all[rs] - m_new)
            p = jnp.exp(s - m_new)
            new.append((m_new, alpha * l_all[rs] + jnp.sum(p, axis=-1, keepdims=True),
                        alpha * acc_all[rs] + pv(p.astype(BF16))))
        m_ref[...] = jnp.concatenate([x[0] for x in new], axis=0)
        l_ref[...] = jnp.concatenate([x[1] for x in new], axis=0)
        acc_ref[...] = jnp.concatenate([x[2] for x in new], axis=0)

    masks = [sel_mask(keys_ref[0, i], s_i * pps + i) for i in range(pps)]
    last_bias = jnp.where(is_last, bl_ref[...], 0.0)

    def page_tiles(g):
        vt = jnp.concatenate([r[g].astype(BF16) for r in v_refs], axis=1)
        return ([jnp.dot(q_g[g], r[g].astype(BF16), preferred_element_type=F32) for r in k_refs],
                lambda p: _nt(p, vt))

    attend(page_tiles, masks, [None] * (pps - 1) + [last_bias])

    @pl.when(is_last)
    def _():
        pad = jnp.zeros((LANE - T_PAD, hd), BF16)

        def new_tiles(g):
            hs = slice(g * hd, (g + 1) * hd)
            kn = jnp.concatenate([kn_ref[:, hs].astype(BF16), pad], axis=0)
            vn = jnp.concatenate([vn_ref[:, hs].astype(BF16), pad], axis=0)
            return [_nt(q_g[g], kn)], lambda p: jnp.dot(p, vn, preferred_element_type=F32)

        attend(new_tiles, [sel_mask(keyn_ref[0], n_steps * pps)], [bn_ref[...]])
        for g in range(n_kv):
            rs = slice(g * rows_g, (g + 1) * rows_g)
            og = acc_ref[rs] / l_ref[rs]
            for j in range(grp):
                h = g * grp + j
                o_ref[:, h * hd:(h + 1) * hd] = og[j * T_PAD:(j + 1) * T_PAD]


def _dsa_attend_decode(P, cache_k, cache_v, page_table, layer, keys, keyn, thr, pos,
                       bias_last, bias_new, *, B, pps):
    n_pages = page_table.shape[1]
    n_steps = n_pages // pps
    rows = bias_last.shape[0]

    def page_spec(i):
        return pl.BlockSpec((None, None) + cache_k.shape[2:],
                            lambda b, s, pt: (pt[b, s * pps + i], layer, 0, 0, 0))

    tile_spec = pl.BlockSpec((1, T_PAD, LANE), lambda b, s, pt: (b, 0, 0))
    grid_spec = pltpu.PrefetchScalarGridSpec(
        num_scalar_prefetch=1,
        grid=(B, n_steps),
        in_specs=[pl.BlockSpec((T_PAD, 4 * LANE), lambda b, s, pt: (b, C_DQ // 4)),
                  pl.BlockSpec((T_PAD, LANE), lambda b, s, pt: (b, C_DK)),
                  pl.BlockSpec((T_PAD, LANE), lambda b, s, pt: (b, C_DV)),
                  pl.BlockSpec((1, pps, T_PAD, LANE), lambda b, s, pt: (b, s, 0, 0)),
                  tile_spec, tile_spec, tile_spec,
                  pl.BlockSpec(bias_last.shape, lambda b, s, pt: (0, 0)),
                  pl.BlockSpec(bias_new.shape, lambda b, s, pt: (0, 0))]
        + [page_spec(i) for i in range(pps)] + [page_spec(i) for i in range(pps)],
        out_specs=pl.BlockSpec((T_PAD, 4 * LANE), lambda b, s, pt: (b, 0)),
        scratch_shapes=[pltpu.VMEM((rows, 1), F32), pltpu.VMEM((rows, 1), F32),
                        pltpu.VMEM((rows, HEAD_DIM), F32)])
    return pl.pallas_call(
        functools.partial(_dsa_att_body, pps=pps, n_steps=n_steps),
        grid_spec=grid_spec,
        out_shape=jax.ShapeDtypeStruct((B * T_PAD, 4 * LANE), F32),
        compiler_params=_cparams(("parallel", "arbitrary")),
        name="dsa_attend_decode")(page_table, P, P, P, keys, keyn, thr, pos, bias_last, bias_new,
                                  *([cache_k] * pps), *([cache_v] * pps))


def _bias_minus_far(tab, dist):
    onehot = jax.nn.one_hot(_rel_bucket(dist), N_BUCKETS, dtype=F32)
    bias = jnp.einsum("...k,km->...m", onehot, tab, precision=lax.Precision.HIGHEST)
    return bias - tab[N_BUCKETS - 1]


def _toeplitz_tiles(tab, t):
    r = jnp.arange(t)[:, None]
    c = jnp.arange(t)[None, :]
    tiles = jnp.stack([_bias_minus_far(tab, r - c), _bias_minus_far(tab, t + r - c)], axis=0)
    return jnp.transpose(tiles, (3, 0, 1, 2))


def _decode_bias(tab, past, t_real):
    tq = jnp.arange(T_PAD)[:, None]
    kc = jnp.arange(LANE)[None, :]
    last = _bias_minus_far(tab, past + tq - (past - PAGE + kc))
    new = _bias_minus_far(tab, tq - kc)
    valid = jnp.logical_and(kc <= tq, kc < t_real)
    new = jnp.where(valid[..., None], new, NEG_INF)
    return jnp.transpose(last, (2, 0, 1)), jnp.transpose(new, (2, 0, 1))


def _block_diag(w):
    n, d, e = w.shape
    eye = jnp.eye(n, dtype=w.dtype)
    return (eye[:, None, :, None] * w[:, :, None, :]).reshape(n * d, n * e)


def kernel(x_prompt, x_sample, cache_a_k, cache_a_v, cache_d_k, cache_d_v, cache_d_idx, state_b_h, state_b_conv, state_c_s, page_table, rel_bias, ln1, w_in, a_q_norm, a_k_norm, a_lam_q1, a_lam_k1, a_lam_q2, a_lam_k2, a_out_norm, b_conv_w, b_conv_b, b_wa, b_ba, b_wx, b_bx, b_lambda, c_lb_logits, c_out_norm, d_q_norm, d_k_norm, w_out, ln2, w_up, w_down):
    B, T, D = x_prompt.shape
    Bs, Ts, _ = x_sample.shape
    L = w_in.shape[0]
    n_pages = page_table.shape[1]
    past = n_pages * PAGE
    gw = D // 4
    a_heads = gw // (2 * HEAD_DIM)
    c_heads = c_lb_logits.shape[1] // LANE
    d_heads = gw // HEAD_DIM
    d_kv = cache_d_k.shape[3]
    d_grp = d_heads // d_kv
    n_pool = cache_a_k.shape[0]
    assert Ts <= T_PAD - 0 and Ts >= CONV_W - 1 and past > 0

    tab = rel_bias.astype(F32)
    n_a_maps = 2 * a_heads
    tab_a, tab_d = tab[:, :n_a_maps], tab[:, n_a_maps:]
    tq_a = min(128, T)
    tq_d = min(128, T)
    bias_a = jnp.transpose(_toeplitz_tiles(tab_a, tq_a).reshape(a_heads, 2, 2, tq_a, tq_a),
                           (0, 2, 3, 1, 4)).reshape(a_heads, 2, tq_a, 2 * tq_a)
    causal = jnp.where(jnp.arange(tq_a)[None, :] <= jnp.arange(tq_a)[:, None], 0.0, NEG_INF)
    causal = jnp.tile(causal, (1, 2)).astype(F32)
    bias_a = jnp.stack([jnp.full((a_heads, tq_a, 2 * tq_a), NEG_INF, F32), bias_a[:, 0] + causal,
                        bias_a[:, 1], jnp.zeros((a_heads, tq_a, 2 * tq_a), F32)], axis=1)
    bias_d = jnp.concatenate([_toeplitz_tiles(tab_d, tq_d),
                              jnp.zeros((d_heads, 1, tq_d, tq_d), F32)], axis=1)
    bias_d = jnp.transpose(bias_d.reshape(d_heads // 2, 2, 3, tq_d, tq_d),
                           (0, 2, 3, 1, 4)).reshape(d_heads // 2, 3, tq_d, 2 * tq_d)
    bl_a, bn_a = _decode_bias(tab_a, past, Ts)
    bl_a = jnp.repeat(bl_a.reshape(n_a_maps * T_PAD, LANE), a_heads, axis=1)
    bn_a = bn_a.reshape(n_a_maps * T_PAD, LANE)
    bl_d, bn_d = _decode_bias(tab_d, past, Ts)
    bl_d = bl_d.reshape(d_heads * T_PAD, LANE)
    bn_d = bn_d.reshape(d_heads * T_PAD, LANE)

    lb_cum = jnp.cumsum(jax.nn.softmax(c_lb_logits.astype(F32), axis=0), axis=0)
    lb_all = (lb_cum - lb_cum[0]).reshape(L, c_heads, 1, LANE)

    seg = jnp.kron(jnp.eye(LANE // HEAD_DIM, dtype=F32),
                   jnp.full((HEAD_DIM, HEAD_DIM), 1.0 / HEAD_DIM, F32)).astype(BF16)
    qscale = HEAD_DIM ** -0.5
    zeros = lambda n: jnp.zeros((n,), F32)
    ones = lambda n: jnp.ones((n,), F32)
    flag = jnp.concatenate([ones(2 * gw), zeros(7 * gw), ones(gw), ones(LANE),
                            zeros(P_WIDTH - 10 * gw - LANE)]).reshape(1, P_WIDTH)
    w_in_t = jnp.transpose(w_in, (0, 2, 1))
    n_in = w_in_t.shape[1]
    tail0 = (P_WIDTH // IN_PROJ_TN - 1) * IN_PROJ_TN
    assert tail0 < n_in <= P_WIDTH
    w_in_tail = jnp.pad(w_in_t[:, tail0:], ((0, 0), (0, tail0 + IN_PROJ_TN - n_in), (0, 0)))

    ca_k = cache_a_k.reshape(n_pool, L, PAGE * a_heads, 2 * HEAD_DIM)
    ca_v = cache_a_v.reshape(n_pool, L, PAGE * a_heads, 2 * HEAD_DIM)
    cd_k = jnp.transpose(cache_d_k, (0, 1, 3, 4, 2))
    cd_v = jnp.transpose(cache_d_v, (0, 1, 3, 4, 2))
    cd_idx = jnp.transpose(cache_d_idx, (0, 1, 3, 2))

    xp = x_prompt.reshape(B * T, D)
    xs = jnp.pad(x_sample, ((0, 0), (0, T_PAD - Ts), (0, 0))).reshape(Bs * T_PAD, D)
    s_conv = jnp.pad(state_b_conv, ((0, 0), (0, 0), (SUBLANE - (CONV_W - 1), 0), (0, 0)))

    n_top_p = min(TOPK_MAX, T // 4)
    n_top_s = min(TOPK_MAX, (past + Ts) // 4)
    tm_p = min(1024, B * T)
    tm_f = min(1024, B * T)
    tt_b = min(256, T)
    tt_c = min(512, T)
    ch_c = math.gcd(T, 64)
    pps_a = math.gcd(n_pages, 16)
    pps_d = math.gcd(n_pages, 16)

    outs_p, outs_s = [], []
    for l in range(L):
        gain = jnp.concatenate(
            [jnp.tile(a_q_norm[l], 2 * a_heads) * qscale, jnp.tile(a_k_norm[l], 2 * a_heads),
             ones(7 * gw), jnp.tile(d_q_norm[l], d_heads) * qscale,
             jnp.tile(d_k_norm[l], d_kv), ones(P_WIDTH - 10 * gw - LANE)]).reshape(1, P_WIDTH)
        g1 = ln1[l].reshape(1, D)
        g2 = ln2[l].reshape(1, D)
        lam_init = 0.8 - 0.6 * math.exp(-0.3 * l)
        lam = (jnp.exp(jnp.sum(a_lam_q1[l].astype(F32) * a_lam_k1[l].astype(F32)))
               - jnp.exp(jnp.sum(a_lam_q2[l].astype(F32) * a_lam_k2[l].astype(F32))) + lam_init)
        scal = jnp.stack([lam, jnp.asarray(1.0 - lam_init, F32)]).astype(F32)
        g_a = a_out_norm[l].reshape(1, LANE)
        g_c = c_out_norm[l].reshape(1, LANE)
        b_weights = (b_conv_w[l], b_conv_b[l].reshape(1, gw), _block_diag(b_wa[l]).astype(BF16),
                     b_ba[l].reshape(1, gw), _block_diag(b_wx[l]).astype(BF16),
                     b_bx[l].reshape(1, gw), b_lambda[l].reshape(1, gw))

        def dense_tail(x, mixes, tm, tmf):
            x1 = _out_proj(mixes, w_out, l, x, tm=tm, tn=min(512, D))
            return _ffn(x1, g2, w_up, w_down, l, tm=tmf, tf=min(512, w_up.shape[2]))

        Pp = _in_proj(xp, g1, w_in_t, w_in_tail, l, gain, flag, seg, tm=tm_p)
        mix_a = _attn_a_prompt(Pp, scal, bias_a, g_a, B=B, T=T, tq=tq_a)
        mix_b, hfin, buf = _rglru(Pp, None, b_weights, B=B, T=T, tt=tt_b, t_real=T, out_dtype=BF16)
        mix_c, sfin = _hgrn(Pp, None, lb_all[l], g_c, B=B, T=T, tt=tt_c, ch=ch_c,
                            sub=min(16, ch_c), t_real=T, out_dtype=BF16)
        mix_d = _dsa_prompt(Pp, bias_d, B=B, T=T, tq=tq_d, n_top=n_top_p)
        xp = dense_tail(xp, (mix_a, mix_b, mix_c, mix_d), tm_p, tm_f)
        P3 = Pp.reshape(B, T, P_WIDTH)
        outs_p.append((P3[..., C_AK * LANE:C_AV * LANE].reshape(B, T, a_heads, 2 * HEAD_DIM),
                       P3[..., C_AV * LANE:C_BX * LANE].reshape(B, T, a_heads, 2 * HEAD_DIM),
                       P3[..., C_DK * LANE:C_DV * LANE].reshape(B, T, d_kv, HEAD_DIM),
                       P3[..., C_DV * LANE:(C_DV + 1) * LANE].reshape(B, T, d_kv, HEAD_DIM),
                       P3[..., C_TAIL * LANE:C_TAIL * LANE + HEAD_DIM],
                       hfin.reshape(B, gw), buf, sfin))

        Ps = _in_proj(xs, g1, w_in_t, w_in_tail, l, gain, flag, seg, tm=Bs * T_PAD)
        smix_a = _attn_a_decode(Ps, ca_k, ca_v, page_table, l, scal, bl_a, bn_a, g_a, B=Bs, pps=pps_a)
        smix_b, shfin, sbuf = _rglru(
            Ps, (state_b_h[:, l].reshape(Bs, 1, gw), s_conv[:, l]), b_weights,
            B=Bs, T=T_PAD, tt=T_PAD, t_real=Ts, out_dtype=F32)
        smix_c, ssfin = _hgrn(Ps, state_c_s[:, l], lb_all[l], g_c, B=Bs, T=T_PAD, tt=T_PAD,
                              ch=T_PAD, sub=T_PAD, t_real=Ts, out_dtype=F32)
        keys, keyn, thr, pos = _dsa_select_decode(Ps, cd_idx, page_table, l, B=Bs, pps=pps_d,
                                                  n_top=n_top_s, t_real=Ts)
        smix_d = _dsa_attend_decode(Ps, cd_k, cd_v, page_table, l, keys, keyn, thr, pos,
                                    bl_d, bn_d, B=Bs, pps=pps_d)
        xs = dense_tail(xs, (smix_a, smix_b, smix_c, smix_d), Bs * T_PAD, Bs * T_PAD)
        S3 = Ps.reshape(Bs, T_PAD, P_WIDTH)[:, :Ts]
        outs_s.append((S3[..., C_AK * LANE:C_AV * LANE].reshape(Bs, Ts, a_heads, 2 * HEAD_DIM),
                       S3[..., C_AV * LANE:C_BX * LANE].reshape(Bs, Ts, a_heads, 2 * HEAD_DIM),
                       S3[..., C_DK * LANE:C_DV * LANE].reshape(Bs, Ts, d_kv, HEAD_DIM),
                       S3[..., C_DV * LANE:(C_DV + 1) * LANE].reshape(Bs, Ts, d_kv, HEAD_DIM),
                       S3[..., C_TAIL * LANE:C_TAIL * LANE + HEAD_DIM],
                       shfin.reshape(Bs, gw), sbuf, ssfin))

    y_prompt = xp.reshape(B, T, D)
    y_sample = xs.reshape(Bs, T_PAD, D)[:, :Ts]
    stack = lambda outs: [jnp.stack(s, axis=1) for s in zip(*outs)]
    return (y_prompt, y_sample, *stack(outs_p), *stack(outs_s))
```

```python
import functools
import math

import jax
import jax.numpy as jnp
import numpy as np
from jax import lax
from jax.experimental import pallas as pl
from jax.experimental.pallas import tpu as pltpu

F32 = jnp.float32
BF16 = jnp.bfloat16
I32 = jnp.int32

EPS = 1e-6
HEAD_DIM = 64
PAGE = 128
CONV_W = 4
LRU_C = 8.0
IDX_HEADS = 8
TOPK_MAX = 256
N_BUCKETS = 32
MAX_DIST = 128
LANE = 128
SUBLANE = 8
T_PAD = 8
INT_MIN = -(2 ** 31)
KEY_NEG_INF = -2139095041
VMEM_LIMIT = 56 * 1024 * 1024
NEG_INF = float("-inf")

C_AQ, C_AK, C_AV, C_BX, C_BG, C_CQ, C_CF, C_CI, C_CG, C_DQ = (4 * i for i in range(10))
C_DK, C_DV, C_IQ, C_TAIL = 40, 41, 42, 46
P_WIDTH = 48 * LANE
IW_OFF = 64
D_BLOCK = 8 * LANE
D_BLOCK_IQ = (C_IQ - C_DK) * LANE
D_BLOCK_TAIL = (C_TAIL - C_DK) * LANE


def _cparams(sem):
    return pltpu.CompilerParams(dimension_semantics=sem, vmem_limit_bytes=VMEM_LIMIT)


def _nt(a, b):
    return lax.dot_general(a, b, (((1,), (1,)), ((), ())), preferred_element_type=F32)


def _tn(a, b):
    return lax.dot_general(a, b, (((0,), (0,)), ((), ())), preferred_element_type=F32)


def _rel_bucket(dist):
    n = jnp.maximum(dist, 0)
    exact = N_BUCKETS // 2
    large = exact + (jnp.log(jnp.maximum(n, 1).astype(F32) / exact)
                     / math.log(MAX_DIST / exact) * (N_BUCKETS - exact)).astype(I32)
    return jnp.where(n < exact, n, jnp.minimum(large, N_BUCKETS - 1))


def _in_proj_body(x_ref, g_ref, w_ref, wtail_ref, gain_ref, flag_ref, seg_ref, o_ref, ak_ref, av_ref,
                  xn_ref, *, tn, norm_tiles, head_tiles):
    j = pl.program_id(1)
    n_sub = tn // LANE
    tm = o_ref.shape[0]

    def head_rows(dst_ref):
        for c in range(n_sub):
            dst_ref[pl.ds(c, tm, stride=n_sub), :] = o_ref[:, c * LANE:(c + 1) * LANE]

    @pl.when(j == 0)
    def _():
        x = x_ref[...]
        ms = jnp.mean(x * x, axis=-1, keepdims=True)
        xn_ref[...] = (x * lax.rsqrt(ms + EPS) * g_ref[...]).astype(BF16)

    w = jnp.where(j == pl.num_programs(1) - 1, wtail_ref[...], w_ref[...])
    y = _nt(xn_ref[...], w.astype(BF16))
    is_norm = functools.reduce(jnp.logical_or, [j == t for t in norm_tiles])

    @pl.when(is_norm)
    def _():
        seg = seg_ref[...]
        for c in range(tn // LANE):
            sl = slice(c * LANE, (c + 1) * LANE)
            yc = y[:, sl]
            y2 = yc * yc
            hi = y2.astype(BF16)
            lo = (y2 - hi.astype(F32)).astype(BF16)
            ms = (jnp.dot(hi, seg, preferred_element_type=F32)
                  + jnp.dot(lo, seg, preferred_element_type=F32))
            yn = yc * lax.rsqrt(ms + EPS) * gain_ref[:, sl]
            o_ref[:, sl] = jnp.where(flag_ref[:, sl] > 0, yn, yc)

    @pl.when(jnp.logical_not(is_norm))
    def _():
        o_ref[...] = y

    @pl.when(j == head_tiles[0])
    def _():
        head_rows(ak_ref)

    @pl.when(j == head_tiles[1])
    def _():
        head_rows(av_ref)


IN_PROJ_TN = 4 * LANE


def _in_proj(x, g, wt, wtail, layer, gain, flag, seg, *, tm):
    M, K = x.shape
    tn = IN_PROJ_TN
    n_tiles = P_WIDTH // tn
    norm_ranges = ((C_AQ, 4), (C_AK, 4), (C_DQ, 4), (C_DK, 1))
    norm_tiles = tuple(sorted({t for c, n in norm_ranges
                               for t in range(c * LANE // tn, ((c + n) * LANE - 1) // tn + 1)}))

    n_sub = tn // LANE
    assert C_AK % n_sub == 0 and C_AV % n_sub == 0
    head_spec = pl.BlockSpec((tm * n_sub, LANE), lambda i, j: (i, 0))
    head_shape = jax.ShapeDtypeStruct((M * n_sub, LANE), F32)
    return pl.pallas_call(
        functools.partial(_in_proj_body, tn=tn, norm_tiles=norm_tiles,
                          head_tiles=(C_AK // n_sub, C_AV // n_sub)),
        grid=(M // tm, n_tiles),
        in_specs=[pl.BlockSpec((tm, K), lambda i, j: (i, 0)),
                  pl.BlockSpec((1, K), lambda i, j: (0, 0)),
                  pl.BlockSpec((None, tn, K), lambda i, j: (layer, jnp.minimum(j, n_tiles - 2), 0)),
                  pl.BlockSpec((None, tn, K), lambda i, j: (layer, 0, 0)),
                  pl.BlockSpec((1, tn), lambda i, j: (0, j)),
                  pl.BlockSpec((1, tn), lambda i, j: (0, j)),
                  pl.BlockSpec((LANE, LANE), lambda i, j: (0, 0))],
        out_specs=[pl.BlockSpec((tm, tn), lambda i, j: (i, j)), head_spec, head_spec],
        out_shape=[jax.ShapeDtypeStruct((M, P_WIDTH), F32), head_shape, head_shape],
        scratch_shapes=[pltpu.VMEM((tm, K), BF16)],
        compiler_params=_cparams(("parallel", "arbitrary")),
        name="in_proj")(x, g, wt, wtail, gain, flag, seg)


def _out_proj_body(a_ref, b_ref, c_ref, d_ref, w_ref, x_ref, o_ref, *, gw):
    acc = x_ref[...]
    for g, m_ref in enumerate((a_ref, b_ref, c_ref, d_ref)):
        acc = acc + jnp.dot(m_ref[...].astype(BF16), w_ref[g * gw:(g + 1) * gw, :].astype(BF16),
                            preferred_element_type=F32)
    o_ref[...] = acc


def _out_proj(mixes, w, layer, x, *, tm, tn):
    M, D = x.shape
    gw = mixes[0].shape[1]
    mix_spec = pl.BlockSpec((tm, gw), lambda i, j: (i, 0))
    return pl.pallas_call(
        functools.partial(_out_proj_body, gw=gw),
        grid=(M // tm, D // tn),
        in_specs=[mix_spec, mix_spec, mix_spec, mix_spec,
                  pl.BlockSpec((None, w.shape[1], tn), lambda i, j: (layer, 0, j)),
                  pl.BlockSpec((tm, tn), lambda i, j: (i, j))],
        out_specs=pl.BlockSpec((tm, tn), lambda i, j: (i, j)),
        out_shape=jax.ShapeDtypeStruct((M, D), F32),
        compiler_params=_cparams(("parallel", "arbitrary")),
        name="out_proj")(*mixes, w, x)


def _ffn_body(x_ref, g_ref, wu_ref, wd_ref, o_ref, xn_ref):
    f = pl.program_id(1)

    @pl.when(f == 0)
    def _():
        x = x_ref[...]
        ms = jnp.mean(x * x, axis=-1, keepdims=True)
        xn_ref[...] = (x * lax.rsqrt(ms + EPS) * g_ref[...]).astype(BF16)
        o_ref[...] = x

    h = jnp.dot(xn_ref[...], wu_ref[...].astype(BF16), preferred_element_type=F32)
    h = jnp.maximum(h, 0.0)
    h = (h * h).astype(BF16)
    o_ref[...] += jnp.dot(h, wd_ref[...].astype(BF16), preferred_element_type=F32)


def _ffn(x, g, wu, wd, layer, *, tm, tf):
    M, D = x.shape
    Fd = wu.shape[2]
    return pl.pallas_call(
        _ffn_body,
        grid=(M // tm, Fd // tf),
        in_specs=[pl.BlockSpec((tm, D), lambda i, f: (i, 0), pipeline_mode=pl.Buffered(1)),
                  pl.BlockSpec((1, D), lambda i, f: (0, 0)),
                  pl.BlockSpec((None, D, tf), lambda i, f: (layer, 0, f)),
                  pl.BlockSpec((None, tf, D), lambda i, f: (layer, f, 0))],
        out_specs=pl.BlockSpec((tm, D), lambda i, f: (i, 0)),
        out_shape=jax.ShapeDtypeStruct((M, D), F32),
        scratch_shapes=[pltpu.VMEM((tm, D), BF16)],
        compiler_params=_cparams(("parallel", "arbitrary")),
        name="ffn")(x, g, wu, wd)


def _softmax_update(state, s, v):
    m, l, acc = state
    m_new = jnp.maximum(m, jnp.max(s, axis=-1, keepdims=True))
    alpha = jnp.exp(m - m_new)
    p = jnp.exp(s - m_new)
    l = alpha * l + jnp.sum(p, axis=-1, keepdims=True)
    acc = alpha * acc + jnp.dot(p.astype(BF16), v, preferred_element_type=F32)
    return m_new, l, acc


def _rms_rows(o, gain):
    ms = jnp.mean(o * o, axis=-1, keepdims=True)
    return o * lax.rsqrt(ms + EPS) * gain


def _pair_attention(n_grp, tile_grp, n_pair, tq, logits, values, s_ref, m_ref, l_ref, acc_ref,
                    tile_aux=None):
    def max_pass(i, carry):
        tiles = [[] for _ in range(n_pair)]
        for j in range(tile_grp):
            c = i * tile_grp + j
            aux = tile_aux(c) if tile_aux is not None else None
            for p in range(n_pair):
                s = logits(p, c) if tile_aux is None else logits(p, c, aux)
                s_ref[p, c] = s
                tiles[p].append(s)
        for p in range(n_pair):
            t = tiles[p]
            while len(t) > 1:
                t = [jnp.maximum(a, b) for a, b in zip(t[::2], t[1::2])]
            m_ref[p] = jnp.maximum(m_ref[p], t[0])
        return carry

    m_ref[...] = jnp.full(m_ref.shape, NEG_INF, F32)
    lax.fori_loop(0, n_grp, max_pass, 0)
    for p in range(n_pair):
        m = m_ref[p]
        m_ref[p] = jnp.concatenate(
            [jnp.broadcast_to(jnp.max(m[:, :LANE], axis=-1, keepdims=True), (tq, LANE)),
             jnp.broadcast_to(jnp.max(m[:, LANE:], axis=-1, keepdims=True), (tq, LANE))], axis=1)
    l_ref[...] = jnp.zeros(l_ref.shape, F32)
    acc_ref[...] = jnp.zeros(acc_ref.shape, F32)

    def sum_pass(i, carry):
        for p in range(n_pair):
            e = [jnp.exp(s_ref[p, i * tile_grp + j] - m_ref[p]) for j in range(tile_grp)]
            l_ref[p] += functools.reduce(lambda a, b: a + b, e)
            pb = jnp.concatenate(
                [jnp.concatenate([x[:, :LANE], x[:, LANE:]], axis=0).astype(BF16) for x in e], axis=1)
            acc_ref[p] += jnp.dot(pb, values(p, i), preferred_element_type=F32)
        return carry

    lax.fori_loop(0, n_grp, sum_pass, 0)
    outs = []
    for p in range(n_pair):
        l = l_ref[p]
        acc = acc_ref[p]
        outs.append((acc[:tq] / jnp.sum(l[:, :LANE], axis=-1, keepdims=True),
                     acc[tq:] / jnp.sum(l[:, LANE:], axis=-1, keepdims=True)))
    return outs


def _attn_a_body(scal_ref, q_ref, k_ref, v_ref, bias_ref, gout_ref, o_ref,
                 kd_ref, vb_ref, s_ref, m_ref, l_ref, acc_ref, *, tq, tile_grp):
    qb = pl.program_id(1)
    n_h = q_ref.shape[1] // LANE
    n_grp = (qb + tile_grp) // tile_grp
    low_half = lax.broadcasted_iota(I32, (tq, LANE), 1) < HEAD_DIM

    @pl.when(qb == 0)
    def _():
        kd_ref[...] = jnp.zeros(kd_ref.shape, BF16)
        vb_ref[...] = jnp.zeros(vb_ref.shape, BF16)

    r_q = pl.multiple_of(qb * LANE, LANE)
    for h in range(n_h):
        kx = k_ref[pl.ds(r_q, LANE), h * LANE:(h + 1) * LANE]
        kd_ref[h, qb] = jnp.concatenate([jnp.where(low_half, kx, 0.0),
                                         jnp.where(low_half, 0.0, kx)], axis=0).astype(BF16)
    vb_ref[pl.ds(r_q, LANE), :] = v_ref[pl.ds(r_q, LANE), :].astype(BF16)
    q = q_ref[...].astype(BF16)

    def logits(h, c):
        return (_nt(q[:, h * LANE:(h + 1) * LANE], kd_ref[h, c])
                + bias_ref[h, jnp.clip(qb - c, -1, 2) + 1])

    def values(h, i):
        r0 = pl.multiple_of(i * tile_grp * LANE, tile_grp * LANE)
        return vb_ref[pl.ds(r0, tile_grp * LANE), h * LANE:(h + 1) * LANE]

    outs = _pair_attention(n_grp, tile_grp, n_h, tq, logits, values, s_ref, m_ref, l_ref, acc_ref)
    lam = scal_ref[0]
    for h, (o1, o2) in enumerate(outs):
        o = _rms_rows(o1 - lam * o2, gout_ref[...]) * scal_ref[1]
        o_ref[:, h * LANE:(h + 1) * LANE] = o.astype(o_ref.dtype)


def _attn_a_prompt(P, scal, bias, gout, *, B, T, tq):
    assert tq == LANE
    H = bias.shape[0]
    W = H * LANE
    nq = T // tq
    tile_grp = math.gcd(nq, 4)
    return pl.pallas_call(
        functools.partial(_attn_a_body, tq=tq, tile_grp=tile_grp),
        grid=(B, nq),
        in_specs=[pl.BlockSpec(memory_space=pltpu.SMEM),
                  pl.BlockSpec((tq, W), lambda b, i: (b * nq + i, C_AQ // H)),
                  pl.BlockSpec((T, W), lambda b, i: (b, C_AK // H)),
                  pl.BlockSpec((T, W), lambda b, i: (b, C_AV // H)),
                  pl.BlockSpec(bias.shape, lambda b, i: (0, 0, 0, 0)),
                  pl.BlockSpec((1, LANE), lambda b, i: (0, 0))],
        out_specs=pl.BlockSpec((tq, W), lambda b, i: (b * nq + i, 0)),
        out_shape=jax.ShapeDtypeStruct((B * T, W), BF16),
        scratch_shapes=[pltpu.VMEM((H, nq, 2 * LANE, LANE), BF16),
                        pltpu.VMEM((T, W), BF16),
                        pltpu.VMEM((H, nq, tq, 2 * LANE), F32),
                        pltpu.VMEM((H, tq, 2 * LANE), F32),
                        pltpu.VMEM((H, tq, 2 * LANE), F32),
                        pltpu.VMEM((H, 2 * tq, LANE), F32)],
        compiler_params=_cparams(("parallel", "arbitrary")),
        name="attn_a_prompt")(scal, P, P, P, bias, gout)


def _rglru_body(*refs, tt, t_real, n_t, has_state, pos0_is_zero):
    if has_state:
        (x_ref, g_ref, h0_ref, buf0_ref, cw_ref, cb_ref, wa_ref, ba_ref, wx_ref, bxb_ref, lam_ref,
         o_ref, hfin_ref, buf_ref, xpad_ref, a_ref, b_ref, hs_ref, hc_ref) = refs
    else:
        (x_ref, g_ref, cw_ref, cb_ref, wa_ref, ba_ref, wx_ref, bxb_ref, lam_ref,
         o_ref, hfin_ref, buf_ref, xpad_ref, a_ref, b_ref, hs_ref, hc_ref) = refs
    ti = pl.program_id(1)
    W = x_ref.shape[1]

    @pl.when(ti == 0)
    def _():
        if has_state:
            xpad_ref[0:SUBLANE, :] = buf0_ref[0]
            hc_ref[...] = jnp.broadcast_to(h0_ref[0], (SUBLANE, W))
        else:
            xpad_ref[0:SUBLANE, :] = jnp.zeros((SUBLANE, W), F32)
            hc_ref[...] = jnp.zeros((SUBLANE, W), F32)

    x = x_ref[...]
    xpad_ref[SUBLANE:SUBLANE + tt, :] = x
    xc = cb_ref[...] + x * cw_ref[CONV_W - 1:CONV_W, :]
    for j in range(CONV_W - 1):
        off = SUBLANE - (CONV_W - 1) + j
        xc = xc + xpad_ref[off:off + tt, :] * cw_ref[j:j + 1, :]
    xcb = xc.astype(BF16)
    r = jax.nn.sigmoid(jnp.dot(xcb, wa_ref[...], preferred_element_type=F32) + ba_ref[...])
    i = jax.nn.sigmoid(jnp.dot(xcb, wx_ref[...], preferred_element_type=F32) + bxb_ref[...])
    nl = -lam_ref[...]
    softplus = jnp.maximum(nl, 0.0) + jnp.log1p(jnp.exp(-jnp.abs(nl)))
    a = jnp.exp(-LRU_C * r * softplus)
    mult = jnp.sqrt(1.0 - a * a)
    if pos0_is_zero:
        rows = lax.broadcasted_iota(I32, (tt, W), 0)
        mult = jnp.where(jnp.logical_and(rows == 0, ti == 0), 1.0, mult)
    a_ref[...] = a
    b_ref[...] = mult * (i * xc)
    rowt = lax.broadcasted_iota(I32, (SUBLANE, W), 0)

    def tile(n, h_prev):
        r0 = pl.multiple_of(n * SUBLANE, SUBLANE)
        at = a_ref[pl.ds(r0, SUBLANE), :]
        bt = b_ref[pl.ds(r0, SUBLANE), :]
        for s in (1, 2, 4):
            keep = rowt >= s
            bt = jnp.where(keep, bt + at * pltpu.roll(bt, s, 0), bt)
            at = jnp.where(keep, at * pltpu.roll(at, s, 0), at)
        ht = bt + at * h_prev
        hs_ref[pl.ds(r0, SUBLANE), :] = ht
        return jnp.broadcast_to(ht[SUBLANE - 1:SUBLANE, :], (SUBLANE, W))

    hc_ref[...] = lax.fori_loop(0, tt // SUBLANE, tile, hc_ref[...])
    o_ref[...] = (hs_ref[...] * jax.nn.gelu(g_ref[...])).astype(o_ref.dtype)
    xpad_ref[0:SUBLANE, :] = xpad_ref[tt:tt + SUBLANE, :]

    @pl.when(ti == n_t - 1)
    def _():
        t_loc = t_real - (n_t - 1) * tt
        hfin_ref[0] = hs_ref[t_loc - 1:t_loc, :]
        buf_ref[0] = x_ref[t_loc - (CONV_W - 1):t_loc, :]


def _rglru(P, state, weights, *, B, T, tt, t_real, out_dtype):
    W = 4 * LANE
    n_t = T // tt
    has_state = state is not None
    xspec = pl.BlockSpec((tt, W), lambda b, i: (b * n_t + i, C_BX // 4))
    gspec = pl.BlockSpec((tt, W), lambda b, i: (b * n_t + i, C_BG // 4))
    full = lambda shape: pl.BlockSpec(shape, lambda b, i: (0,) * len(shape))
    wspecs = [full((CONV_W, W)), full((1, W)), full((W, W)), full((1, W)), full((W, W)),
              full((1, W)), full((1, W))]
    in_specs = [xspec, gspec]
    args = [P, P]
    if has_state:
        in_specs += [pl.BlockSpec((1, 1, W), lambda b, i: (b, 0, 0)),
                     pl.BlockSpec((1, SUBLANE, W), lambda b, i: (b, 0, 0))]
        args += list(state)
    return pl.pallas_call(
        functools.partial(_rglru_body, tt=tt, t_real=t_real, n_t=n_t, has_state=has_state,
                          pos0_is_zero=not has_state),
        grid=(B, n_t),
        in_specs=in_specs + wspecs,
        out_specs=[pl.BlockSpec((tt, W), lambda b, i: (b * n_t + i, 0)),
                   pl.BlockSpec((1, 1, W), lambda b, i: (b, 0, 0)),
                   pl.BlockSpec((1, CONV_W - 1, W), lambda b, i: (b, 0, 0))],
        out_shape=[jax.ShapeDtypeStruct((B * T, W), out_dtype),
                   jax.ShapeDtypeStruct((B, 1, W), F32),
                   jax.ShapeDtypeStruct((B, CONV_W - 1, W), F32)],
        scratch_shapes=[pltpu.VMEM((tt + SUBLANE, W), F32), pltpu.VMEM((tt, W), F32),
                        pltpu.VMEM((tt, W), F32), pltpu.VMEM((tt, W), F32),
                        pltpu.VMEM((SUBLANE, W), F32)],
        compiler_params=_cparams(("parallel", "arbitrary")),
        name="rglru")(*args, *weights)


def _hgrn_body(*refs, tt, ch, sub, t_real, n_t, has_state):
    if has_state:
        q_ref, f_ref, v_ref, g_ref, s0_ref, lb_ref, gain_ref, o_ref, sfin_ref, st_ref = refs
    else:
        q_ref, f_ref, v_ref, g_ref, lb_ref, gain_ref, o_ref, sfin_ref, st_ref = refs
    ti = pl.program_id(1)
    n_h = st_ref.shape[0]
    dk = LANE

    @pl.when(ti == 0)
    def _():
        for h in range(n_h):
            if has_state:
                st_ref[h] = s0_ref[0, h].T
            else:
                st_ref[h] = jnp.zeros((dk, dk), F32)

    rr = lax.broadcasted_iota(I32, (ch, ch), 0)
    cc = lax.broadcasted_iota(I32, (ch, ch), 1)
    tril = (cc <= rr).astype(F32)
    rows = lax.broadcasted_iota(I32, (ch, dk), 0)
    srow = lax.broadcasted_iota(I32, (sub, 1), 0)
    n_sub = ch // sub

    def head_chunk(c, h):
        r0 = pl.multiple_of(c * ch, ch)
        hs = slice(h * dk, (h + 1) * dk)
        lb = lb_ref[h]
        log_lb = jnp.log(lb)
        log_1mlb = jnp.log1p(-lb)
        q = q_ref[pl.ds(r0, ch), hs]
        q = q * jax.nn.sigmoid(q)
        fp = f_ref[pl.ds(r0, ch), hs]
        v = v_ref[pl.ds(r0, ch), hs]
        log_sig = jnp.minimum(fp, 0.0) - jnp.log1p(jnp.exp(-jnp.abs(fp)))
        b = log_1mlb + log_sig
        lf = jnp.maximum(log_lb, b) + jnp.log1p(jnp.exp(-jnp.abs(log_lb - b)))
        kk = (1.0 - lb) * jax.nn.sigmoid(-fp)
        if t_real < tt * n_t:
            live = (rows + (ti * tt + c * ch)) < t_real
            lf = jnp.where(live, lf, 0.0)
            kk = jnp.where(live, kk, 0.0)
        G = jnp.dot(tril, lf, preferred_element_type=F32, precision=lax.Precision.HIGHEST)
        st = st_ref[h]
        vb = v.astype(BF16)
        o = _nt((q * jnp.exp(G)).astype(BF16), st.astype(BF16))
        outs = []
        for i in range(n_sub):
            lo, hi = i * sub, (i + 1) * sub
            qi, Gi, ki, vi = q[lo:hi], G[lo:hi], kk[lo:hi], v[lo:hi]
            oi = o[lo:hi]
            if i > 0:
                R = G[lo - 1:lo]
                qp = (qi * jnp.exp(Gi - R)).astype(BF16)
                kp = (kk[:lo] * jnp.exp(R - G[:lo])).astype(BF16)
                att = _nt(qp, kp)
                oi = oi + jnp.dot(att.astype(BF16), vb[:lo], preferred_element_type=F32)
            for s in range(sub):
                w = jnp.exp(jnp.minimum(Gi - Gi[s:s + 1], 0.0))
                colv = jnp.sum(qi * w * ki[s:s + 1], axis=-1, keepdims=True)
                colv = jnp.where(srow >= s, colv, 0.0)
                oi = oi + colv * vi[s:s + 1]
            outs.append(oi)
        o = jnp.concatenate(outs, axis=0) if n_sub > 1 else outs[0]
        gl = G[ch - 1:ch]
        kpp = (kk * jnp.exp(gl - G)).astype(BF16)
        st_ref[h] = st * jnp.exp(gl) + _tn(vb, kpp)
        gate = g_ref[pl.ds(r0, ch), hs]
        o = _rms_rows(o, gain_ref[...]) * (gate * jax.nn.sigmoid(gate))
        o_ref[pl.ds(r0, ch), hs] = o.astype(o_ref.dtype)

    n_chunks = tt // ch
    per_iter = 2 if n_chunks % 2 == 0 else 1

    def chunk(c, carry):
        for u in range(per_iter):
            for h in range(n_h):
                head_chunk(c * per_iter + u, h)
        return carry

    lax.fori_loop(0, n_chunks // per_iter, chunk, 0)

    @pl.when(ti == n_t - 1)
    def _():
        for h in range(n_h):
            sfin_ref[0, h] = st_ref[h].T


def _hgrn(P, s0, lb, gain, *, B, T, tt, ch, sub, t_real, out_dtype):
    H = lb.shape[0]
    n_t = T // tt
    has_state = s0 is not None

    W = H * LANE

    def col(c0):
        return pl.BlockSpec((tt, W), lambda b, i: (b * n_t + i, c0 // H))

    in_specs = [col(C_CQ), col(C_CF), col(C_CI), col(C_CG)]
    args = [P, P, P, P]
    if has_state:
        in_specs.append(pl.BlockSpec((1, H, LANE, LANE), lambda b, i: (b, 0, 0, 0)))
        args.append(s0)
    in_specs += [pl.BlockSpec((H, 1, LANE), lambda b, i: (0, 0, 0)),
                 pl.BlockSpec((1, LANE), lambda b, i: (0, 0))]
    return pl.pallas_call(
        functools.partial(_hgrn_body, tt=tt, ch=ch, sub=sub, t_real=t_real, n_t=n_t,
                          has_state=has_state),
        grid=(B, n_t),
        in_specs=in_specs,
        out_specs=[pl.BlockSpec((tt, W), lambda b, i: (b * n_t + i, 0)),
                   pl.BlockSpec((1, H, LANE, LANE), lambda b, i: (b, 0, 0, 0))],
        out_shape=[jax.ShapeDtypeStruct((B * T, W), out_dtype),
                   jax.ShapeDtypeStruct((B, H, LANE, LANE), F32)],
        scratch_shapes=[pltpu.VMEM((H, LANE, LANE), F32)],
        compiler_params=_cparams(("parallel", "arbitrary")),
        name="hgrn2")(*args, lb, gain)


def _score_key(score):
    score = jnp.where(score == 0.0, 0.0, score)
    bits = pltpu.bitcast(score, I32)
    return bits ^ ((bits >> 31) & 0x7FFFFFFF)


def _kth_largest(count_ge, shape, k, bits=32):
    zero = jnp.zeros(shape, I32)
    v = jnp.where(count_ge(zero) >= k, zero, jnp.full(shape, -(2 ** (bits - 1)), I32))

    def bit_step(n, v):
        cand = v + (jnp.int32(1) << (bits - 2 - n))
        return jnp.where(count_ge(cand) >= k, cand, v)

    return lax.fori_loop(0, bits - 1, bit_step, v)


def _tie_bound(count_tie_below, r, shape, n_bits):
    def bit_step(n, pos):
        cand = pos + (jnp.int32(1) << (n_bits - 1 - n))
        return jnp.where(count_tie_below(cand) < r, cand, pos)

    return lax.fori_loop(0, n_bits, bit_step, jnp.zeros(shape, I32))


def _dsa_body(dq_ref, iq_ref, tq_ref, dk_ref, dv_ref, ik_ref, bias_ref, o_ref,
              key_ref, hi_ref, lo_ref, pos_ref, kd_ref, vb_ref, ikb_ref,
              s_ref, m_ref, l_ref, acc_ref, *, tq, n_top, n_bits, tile_grp):
    qb = pl.program_id(1)
    n_c = qb + 1
    hd = HEAD_DIM
    n_kv = dk_ref.shape[1] // hd
    n_q = dq_ref.shape[1] // hd
    grp = n_q // n_kv
    n_pair = n_q // 2
    row = lax.broadcasted_iota(I32, (tq, LANE), 0)
    col = lax.broadcasted_iota(I32, (tq, LANE), 1)
    low_half = col < hd

    @pl.when(qb == 0)
    def _():
        kd_ref[...] = jnp.zeros(kd_ref.shape, BF16)
        vb_ref[...] = jnp.zeros(vb_ref.shape, BF16)
        ikb_ref[...] = jnp.zeros(ikb_ref.shape, BF16)

    def pair_blocks(at_low, at_high):
        return jnp.concatenate([jnp.where(low_half, at_low, 0.0),
                                jnp.where(low_half, 0.0, at_high)], axis=0).astype(BF16)

    r_q = pl.multiple_of(qb * LANE, LANE)
    kx = dk_ref[pl.ds(r_q, LANE), :]
    kx_sw = pltpu.roll(kx, hd, 1)
    kd_ref[0, qb] = pair_blocks(kx, kx_sw)
    kd_ref[1, qb] = pair_blocks(kx_sw, kx)
    ikb_ref[pl.ds(r_q, LANE), :] = ik_ref[pl.ds(r_q, LANE), :].astype(BF16)
    vb_ref[pl.ds(r_q, LANE), :] = dv_ref[pl.ds(r_q, LANE), :].astype(BF16)

    iq = iq_ref[:, D_BLOCK_IQ:D_BLOCK_IQ + IDX_HEADS * hd]
    iqz = []
    for p in range(IDX_HEADS // 2):
        pair = iq[:, p * LANE:(p + 1) * LANE]
        iqz.append(jnp.concatenate([jnp.where(low_half, pair, 0.0),
                                    jnp.where(low_half, pltpu.roll(pair, hd, 1), 0.0)],
                                   axis=0).astype(BF16))
    wscale = IDX_HEADS ** -0.5 * hd ** -0.5
    tail_t = tq_ref[...].T
    w_rows = [tail_t[IW_OFF + h:IW_OFF + h + 1, :] * wscale for h in range(IDX_HEADS)]

    n_grp = (n_c + tile_grp - 1) // tile_grp

    def score_tile(c):
        r0 = pl.multiple_of(c * LANE, LANE)
        ikc = ikb_ref[pl.ds(r0, LANE), :]
        sc = None
        for p in range(IDX_HEADS // 2):
            s = jnp.maximum(_nt(ikc, iqz[p]), 0.0)
            t = s[:, :LANE] * w_rows[2 * p] + s[:, LANE:] * w_rows[2 * p + 1]
            sc = t if sc is None else sc + t
        causal = jnp.logical_or(c < qb, jnp.logical_and(c == qb, row <= col))
        key = _score_key(jnp.where(causal, sc, NEG_INF))
        key_ref[c] = key
        hi_ref[c] = (key >> 16).astype(jnp.int16)
        lo_ref[c] = (((key ^ 0x8000) << 16) >> 16).astype(jnp.int16)

    def score_group(i, carry):
        for j in range(tile_grp):
            score_tile(i * tile_grp + j)
        return carry

    lax.fori_loop(0, n_grp, score_group, 0)
    vec = (1, tq)

    def count(pred, ref=key_ref, dtype=I32):
        def body(i, acc):
            for j in range(tile_grp):
                c = i * tile_grp + j
                acc = acc + pred(ref[c], c).astype(dtype)
            return acc
        acc = lax.fori_loop(0, n_grp, body, jnp.zeros((LANE, tq), dtype))
        return jnp.sum(acc.astype(I32), axis=0, keepdims=True)

    def count16_ge(ref):
        def f(cand):
            cb = jnp.broadcast_to(cand, (LANE, tq)).astype(jnp.int16)
            return count(lambda key, c: key >= cb, ref, jnp.int16)
        return f

    min16 = -(2 ** 15)
    t_hi = _kth_largest(count16_ge(hi_ref), vec, n_top, bits=16)
    t_hi_b = jnp.broadcast_to(t_hi, (LANE, tq)).astype(jnp.int16)
    k_lo = n_top - count(lambda key, c: key > t_hi_b, hi_ref, jnp.int16)

    def band_group(i, carry):
        for j in range(tile_grp):
            c = i * tile_grp + j
            lo_ref[c] = jnp.where(hi_ref[c] == t_hi_b, lo_ref[c], jnp.int16(min16))
        return carry

    lax.fori_loop(0, n_grp, band_group, 0)
    t_lo = _kth_largest(count16_ge(lo_ref), vec, k_lo, bits=16)
    thr = (t_hi << 16) | ((t_lo ^ min16) & 0xFFFF)
    thr_b = jnp.broadcast_to(thr, (LANE, tq))
    n_gt = count(lambda key, c: key > thr_b)
    n_ge = count(lambda key, c: key >= thr_b)
    need = jnp.logical_and(n_ge > n_top, thr > KEY_NEG_INF)
    pos_ref[...] = jnp.full((LANE, tq), 2 ** n_bits, I32)

    @pl.when(jnp.max(need.astype(I32)) > 0)
    def _():
        r = n_top - n_gt

        def count_tie_below(cand):
            cb = jnp.broadcast_to(cand, (LANE, tq))
            return count(lambda key, c: jnp.logical_and(key == thr_b, row + c * LANE < cb))

        pos = _tie_bound(count_tie_below, r, vec, n_bits)
        pos_ref[...] = jnp.broadcast_to(pos, (LANE, tq))

    pos_b = pos_ref[...]

    def tile_mask(c):
        key = key_ref[c]
        idx = row + c * LANE
        sel = jnp.logical_or(key > thr_b, jnp.logical_and(key == thr_b, idx <= pos_b))
        sel = jnp.logical_and(sel, key > KEY_NEG_INF)
        mk = jnp.where(sel, 0.0, NEG_INF).T
        return jnp.concatenate([mk, mk], axis=1)

    dq = dq_ref[...].astype(BF16)

    def logits(p, c, mask):
        return (_nt(dq[:, p * LANE:(p + 1) * LANE], kd_ref[(2 * p) // grp, c])
                + mask + bias_ref[p, jnp.clip(qb - c, 0, 2)])

    def values(p, i):
        r0 = pl.multiple_of(i * tile_grp * LANE, tile_grp * LANE)
        return vb_ref[pl.ds(r0, tile_grp * LANE), :]

    outs = _pair_attention(n_grp, tile_grp, n_pair, tq, logits, values, s_ref, m_ref, l_ref, acc_ref,
                           tile_aux=tile_mask)
    for p, (oa, ob) in enumerate(outs):
        if (2 * p) // grp == 0:
            ob = pltpu.roll(ob, hd, 1)
        else:
            oa = pltpu.roll(oa, hd, 1)
        o_ref[:, p * LANE:(p + 1) * LANE] = jnp.where(low_half, oa, ob).astype(o_ref.dtype)


def _dsa_prompt(P, bias, *, B, T, tq, n_top):
    assert tq == LANE
    nq = T // tq
    n_bits = max(1, int(math.ceil(math.log2(T))))
    tile_grp = math.gcd(nq, 4)
    n_pair = bias.shape[0]
    n_tiles = T // LANE
    return pl.pallas_call(
        functools.partial(_dsa_body, tq=tq, n_top=n_top, n_bits=n_bits, tile_grp=tile_grp),
        grid=(B, nq),
        in_specs=[pl.BlockSpec((tq, 4 * LANE), lambda b, i: (b * nq + i, C_DQ // 4)),
                  pl.BlockSpec((tq, D_BLOCK), lambda b, i: (b * nq + i, C_DK * LANE // D_BLOCK)),
                  pl.BlockSpec((tq, LANE), lambda b, i: (b * nq + i, C_TAIL)),
                  pl.BlockSpec((T, LANE), lambda b, i: (b, C_DK)),
                  pl.BlockSpec((T, LANE), lambda b, i: (b, C_DV)),
                  pl.BlockSpec((T, LANE), lambda b, i: (b, C_TAIL)),
                  pl.BlockSpec(bias.shape, lambda b, i: (0, 0, 0, 0))],
        out_specs=pl.BlockSpec((tq, 4 * LANE), lambda b, i: (b * nq + i, 0)),
        out_shape=jax.ShapeDtypeStruct((B * T, 4 * LANE), BF16),
        scratch_shapes=[pltpu.VMEM((n_tiles, tq, LANE), I32),
                        pltpu.VMEM((n_tiles, tq, LANE), jnp.int16),
                        pltpu.VMEM((n_tiles, tq, LANE), jnp.int16),
                        pltpu.VMEM((tq, LANE), I32),
                        pltpu.VMEM((2, n_tiles, 2 * LANE, LANE), BF16),
                        pltpu.VMEM((T, LANE), BF16),
                        pltpu.VMEM((T, LANE), BF16),
                        pltpu.VMEM((n_pair, n_tiles, tq, 2 * LANE), F32),
                        pltpu.VMEM((n_pair, tq, 2 * LANE), F32),
                        pltpu.VMEM((n_pair, tq, 2 * LANE), F32),
                        pltpu.VMEM((n_pair, 2 * tq, LANE), F32)],
        compiler_params=_cparams(("parallel", "arbitrary")),
        name="dsa_prompt")(P, P, P, P, P, P, bias)


def _attn_a_dec_body(pt_ref, scal_ref, q_ref, kn_ref, vn_ref, bl_ref, bn_ref, gout_ref, *rest,
                     pps, n_steps):
    k_refs = rest[:pps]
    v_refs = rest[pps:2 * pps]
    o_ref, m_ref, l_ref, acc_ref = rest[2 * pps:]
    s_i = pl.program_id(1)
    n_h = q_ref.shape[1] // LANE
    rows_h = 2 * T_PAD

    @pl.when(s_i == 0)
    def _():
        m_ref[...] = jnp.full(m_ref.shape, NEG_INF, F32)
        l_ref[...] = jnp.zeros(l_ref.shape, F32)
        acc_ref[...] = jnp.zeros(acc_ref.shape, F32)

    q = q_ref[...]
    lane = lax.broadcasted_iota(I32, (T_PAD, LANE), 1)

    def q_head(h):
        qh = q[:, h * LANE:(h + 1) * LANE]
        return jnp.concatenate([jnp.where(lane < HEAD_DIM, qh, 0.0),
                                jnp.where(lane >= HEAD_DIM, qh, 0.0)], axis=0).astype(BF16)

    qs = [q_head(h) for h in range(n_h)]
    q_all = jnp.concatenate(qs, axis=0)
    is_last = s_i == n_steps - 1

    n_rows = n_h * rows_h
    page_w = PAGE * n_h
    row_head = lax.broadcasted_iota(I32, (n_rows, page_w), 0) // rows_h
    col_head = lax.broadcasted_iota(I32, (n_rows, page_w), 1) % n_h
    head_mask = jnp.where(row_head == col_head, 0.0, NEG_INF)
    last_bias = jnp.where(is_last, bl_ref[...], 0.0) + head_mask
    parts = [_nt(q_all, r[...].astype(BF16)) + (last_bias if i == pps - 1 else head_mask)
             for i, r in enumerate(k_refs)]
    v_all = jnp.concatenate([r[...].astype(BF16) for r in v_refs], axis=0)
    m, l, acc = _softmax_update((m_ref[...], l_ref[...], acc_ref[...]),
                                jnp.concatenate(parts, axis=1), v_all)
    m_ref[...] = m
    l_ref[...] = l
    acc_ref[...] = acc

    @pl.when(is_last)
    def _():
        pad = jnp.zeros((LANE - T_PAD, LANE), BF16)
        m_all, l_all, acc_all = m_ref[...], l_ref[...], acc_ref[...]
        new = []
        for h in range(n_h):
            hs = slice(h * LANE, (h + 1) * LANE)
            rs = slice(h * rows_h, (h + 1) * rows_h)
            kn = jnp.concatenate([kn_ref[:, hs].astype(BF16), pad], axis=0)
            vn = jnp.concatenate([vn_ref[:, hs].astype(BF16), pad], axis=0)
            s = _nt(qs[h], kn) + bn_ref[rs]
            new.append(_softmax_update((m_all[rs], l_all[rs], acc_all[rs]), s, vn))
        m_ref[...] = jnp.concatenate([x[0] for x in new], axis=0)
        l_ref[...] = jnp.concatenate([x[1] for x in new], axis=0)
        acc_ref[...] = jnp.concatenate([x[2] for x in new], axis=0)
        lam = scal_ref[0]
        for h in range(n_h):
            r1 = slice(h * rows_h, h * rows_h + T_PAD)
            r2 = slice(h * rows_h + T_PAD, (h + 1) * rows_h)
            o = acc_ref[r1] / l_ref[r1] - lam * (acc_ref[r2] / l_ref[r2])
            o_ref[:, h * LANE:(h + 1) * LANE] = _rms_rows(o, gout_ref[...]) * scal_ref[1]


def _attn_a_decode(P, cache_k, cache_v, page_table, layer, scal, bias_last, bias_new, gout, *, B, pps):
    n_pages = page_table.shape[1]
    n_steps = n_pages // pps
    W = 4 * LANE

    def page_spec(i):
        return pl.BlockSpec((None, None) + cache_k.shape[2:],
                            lambda b, s, pt: (pt[b, s * pps + i], layer, 0, 0))

    grid_spec = pltpu.PrefetchScalarGridSpec(
        num_scalar_prefetch=1,
        grid=(B, n_steps),
        in_specs=[pl.BlockSpec(memory_space=pltpu.SMEM),
                  pl.BlockSpec((T_PAD, W), lambda b, s, pt: (b, C_AQ // 4)),
                  pl.BlockSpec((T_PAD, W), lambda b, s, pt: (b, C_AK // 4)),
                  pl.BlockSpec((T_PAD, W), lambda b, s, pt: (b, C_AV // 4)),
                  pl.BlockSpec(bias_last.shape, lambda b, s, pt: (0, 0)),
                  pl.BlockSpec(bias_new.shape, lambda b, s, pt: (0, 0)),
                  pl.BlockSpec((1, LANE), lambda b, s, pt: (0, 0))]
        + [page_spec(i) for i in range(pps)] + [page_spec(i) for i in range(pps)],
        out_specs=pl.BlockSpec((T_PAD, W), lambda b, s, pt: (b, 0)),
        scratch_shapes=[pltpu.VMEM((bias_last.shape[0], 1), F32),
                        pltpu.VMEM((bias_last.shape[0], 1), F32),
                        pltpu.VMEM((bias_last.shape[0], LANE), F32)])
    return pl.pallas_call(
        functools.partial(_attn_a_dec_body, pps=pps, n_steps=n_steps),
        grid_spec=grid_spec,
        out_shape=jax.ShapeDtypeStruct((B * T_PAD, W), F32),
        compiler_params=_cparams(("parallel", "arbitrary")),
        name="attn_a_decode")(page_table, scal, P, P, P, bias_last, bias_new, gout,
                              *([cache_k] * pps), *([cache_v] * pps))


def _dsa_sel_body(pt_ref, iq_ref, tq_ref, ikn_ref, *rest, pps, n_steps, n_top, t_real, n_bits):
    ik_refs = rest[:pps]
    keys_ref, keyn_ref, thr_ref, pos_ref, all_ref, wb_ref = rest[pps:]
    s_i = pl.program_id(1)
    hd = HEAD_DIM
    iq = iq_ref[:, D_BLOCK_IQ:D_BLOCK_IQ + IDX_HEADS * hd].astype(BF16)
    q_idx = jnp.concatenate([iq[:, h * hd:(h + 1) * hd] for h in range(IDX_HEADS)], axis=0)
    wscale = IDX_HEADS ** -0.5 * hd ** -0.5
    tail = tq_ref[...]
    for h in range(IDX_HEADS):
        wb_ref[h * T_PAD:(h + 1) * T_PAD, :] = jnp.broadcast_to(
            tail[:, IW_OFF + h:IW_OFF + h + 1] * wscale, (T_PAD, LANE))

    def score(raw):
        s = jnp.maximum(raw, 0.0) * wb_ref[...]
        sc = s[0:T_PAD]
        for h in range(1, IDX_HEADS):
            sc = sc + s[h * T_PAD:(h + 1) * T_PAD]
        return sc

    for i in range(pps):
        raw = jnp.dot(q_idx, ik_refs[i][...].astype(BF16), preferred_element_type=F32)
        key = _score_key(score(raw))
        keys_ref[0, i] = key
        all_ref[s_i * pps + i] = key

    n_tiles = n_steps * pps + 1

    @pl.when(s_i == n_steps - 1)
    def _():
        row = lax.broadcasted_iota(I32, (T_PAD, LANE), 0)
        col = lax.broadcasted_iota(I32, (T_PAD, LANE), 1)
        pad = jnp.zeros((LANE - T_PAD, hd), F32)
        ikn = jnp.concatenate([ikn_ref[...][:, :hd], pad], axis=0)
        valid = jnp.logical_and(col <= row, col < t_real)
        keyn = _score_key(jnp.where(valid, score(_nt(q_idx, ikn.astype(BF16))), NEG_INF))
        keyn_ref[0] = keyn
        all_ref[n_tiles - 1] = keyn
        idx = (lax.broadcasted_iota(I32, all_ref.shape, 0) * LANE
               + lax.broadcasted_iota(I32, all_ref.shape, 2))

        def count(pred):
            acc = jnp.sum(pred(all_ref[...]).astype(I32), axis=0)
            return jnp.sum(acc, axis=-1, keepdims=True)

        def count_ge(cand):
            cb = jnp.broadcast_to(cand, (T_PAD, LANE))
            return count(lambda key: key >= cb)

        thr = _kth_largest(count_ge, (T_PAD, 1), n_top)
        thr_b = jnp.broadcast_to(thr, (T_PAD, LANE))
        n_gt = count(lambda key: key > thr_b)
        r = n_top - n_gt

        def count_tie_below(cand):
            cb = jnp.broadcast_to(cand, (T_PAD, LANE))
            return count(lambda key: jnp.logical_and(key == thr_b, idx < cb))

        pos = _tie_bound(count_tie_below, r, (T_PAD, 1), n_bits)
        thr_ref[0] = thr_b
        pos_ref[0] = jnp.broadcast_to(pos, (T_PAD, LANE))


def _dsa_select_decode(P, cache_idx, page_table, layer, *, B, pps, n_top, t_real):
    n_pages = page_table.shape[1]
    n_steps = n_pages // pps
    n_tiles = n_pages + 1
    n_bits = int(math.ceil(math.log2(n_tiles * LANE)))

    def page_spec(i):
        return pl.BlockSpec((None, None) + cache_idx.shape[2:],
                            lambda b, s, pt: (pt[b, s * pps + i], layer, 0, 0))

    tile_spec = pl.BlockSpec((1, T_PAD, LANE), lambda b, s, pt: (b, 0, 0))
    grid_spec = pltpu.PrefetchScalarGridSpec(
        num_scalar_prefetch=1,
        grid=(B, n_steps),
        in_specs=[pl.BlockSpec((T_PAD, D_BLOCK), lambda b, s, pt: (b, C_DK * LANE // D_BLOCK)),
                  pl.BlockSpec((T_PAD, LANE), lambda b, s, pt: (b, C_TAIL)),
                  pl.BlockSpec((T_PAD, LANE), lambda b, s, pt: (b, C_TAIL))]
        + [page_spec(i) for i in range(pps)],
        out_specs=[pl.BlockSpec((1, pps, T_PAD, LANE), lambda b, s, pt: (b, s, 0, 0)),
                   tile_spec, tile_spec, tile_spec],
        scratch_shapes=[pltpu.VMEM((n_tiles, T_PAD, LANE), I32),
                        pltpu.VMEM((IDX_HEADS * T_PAD, LANE), F32)])
    tile_shape = jax.ShapeDtypeStruct((B, T_PAD, LANE), I32)
    return pl.pallas_call(
        functools.partial(_dsa_sel_body, pps=pps, n_steps=n_steps, n_top=n_top, t_real=t_real,
                          n_bits=n_bits),
        grid_spec=grid_spec,
        out_shape=[jax.ShapeDtypeStruct((B, n_pages, T_PAD, LANE), I32),
                   tile_shape, tile_shape, tile_shape],
        compiler_params=_cparams(("parallel", "arbitrary")),
        name="dsa_select_decode")(page_table, P, P, P, *([cache_idx] * pps))


def _dsa_att_body(pt_ref, dq_ref, kn_ref, vn_ref, keys_ref, keyn_ref, thr_ref, pos_ref,
                  bl_ref, bn_ref, *rest, pps, n_steps):
    k_refs = rest[:pps]
    v_refs = rest[pps:2 * pps]
    o_ref, m_ref, l_ref, acc_ref = rest[2 * pps:]
    s_i = pl.program_id(1)
    hd = HEAD_DIM
    n_kv = kn_ref.shape[1] // hd
    n_q = dq_ref.shape[1] // hd
    grp = n_q // n_kv
    rows_g = grp * T_PAD

    @pl.when(s_i == 0)
    def _():
        m_ref[...] = jnp.full(m_ref.shape, -1e30, F32)
        l_ref[...] = jnp.zeros(l_ref.shape, F32)
        acc_ref[...] = jnp.zeros(acc_ref.shape, F32)

    dq = dq_ref[...].astype(BF16)
    q_g = [jnp.concatenate([dq[:, (g * grp + j) * hd:(g * grp + j + 1) * hd] for j in range(grp)],
                           axis=0) for g in range(n_kv)]
    thr = thr_ref[0]
    pos = pos_ref[0]
    col = lax.broadcasted_iota(I32, (T_PAD, LANE), 1)
    is_last = s_i == n_steps - 1

    def sel_mask(key, tile):
        idx = col + tile * LANE
        sel = jnp.logical_or(key > thr, jnp.logical_and(key == thr, idx <= pos))
        sel = jnp.logical_and(sel, key > KEY_NEG_INF)
        return jnp.concatenate([sel] * grp, axis=0)

    def attend(tiles, masks, bias_tiles):
        m_all, l_all, acc_all = m_ref[...], l_ref[...], acc_ref[...]
        new = []
        for g in range(n_kv):
            rs = slice(g * rows_g, (g + 1) * rows_g)
            logit_tiles, pv = tiles(g)
            parts = []
            for s, mk, bt in zip(logit_tiles, masks, bias_tiles):
                if bt is not None:
                    s = s + bt[rs]
                parts.append(jnp.where(mk, s, NEG_INF))
            s = jnp.concatenate(parts, axis=1) if len(parts) > 1 else parts[0]
            m_new = jnp.maximum(m_all[rs], jnp.max(s, axis=-1, keepdims=True))
            alpha = jnp.exp(m_all[rs] - m_new)
            p = jnp.exp(s - m_new)
            new.append((m_new, alpha * l_all[rs] + jnp.sum(p, axis=-1, keepdims=True),
                        alpha * acc_all[rs] + pv(p.astype(BF16))))
        m_ref[...] = jnp.concatenate([x[0] for x in new], axis=0)
        l_ref[...] = jnp.concatenate([x[1] for x in new], axis=0)
        acc_ref[...] = jnp.concatenate([x[2] for x in new], axis=0)

    masks = [sel_mask(keys_ref[0, i], s_i * pps + i) for i in range(pps)]
    last_bias = jnp.where(is_last, bl_ref[...], 0.0)

    def page_tiles(g):
        vt = jnp.concatenate([r[g].astype(BF16) for r in v_refs], axis=1)
        return ([jnp.dot(q_g[g], r[g].astype(BF16), preferred_element_type=F32) for r in k_refs],
                lambda p: _nt(p, vt))

    attend(page_tiles, masks, [None] * (pps - 1) + [last_bias])

    @pl.when(is_last)
    def _():
        pad = jnp.zeros((LANE - T_PAD, hd), BF16)

        def new_tiles(g):
            hs = slice(g * hd, (g + 1) * hd)
            kn = jnp.concatenate([kn_ref[:, hs].astype(BF16), pad], axis=0)
            vn = jnp.concatenate([vn_ref[:, hs].astype(BF16), pad], axis=0)
            return [_nt(q_g[g], kn)], lambda p: jnp.dot(p, vn, preferred_element_type=F32)

        attend(new_tiles, [sel_mask(keyn_ref[0], n_steps * pps)], [bn_ref[...]])
        for g in range(n_kv):
            rs = slice(g * rows_g, (g + 1) * rows_g)
            og = acc_ref[rs] / l_ref[rs]
            for j in range(grp):
                h = g * grp + j
                o_ref[:, h * hd:(h + 1) * hd] = og[j * T_PAD:(j + 1) * T_PAD]


def _dsa_attend_decode(P, cache_k, cache_v, page_table, layer, keys, keyn, thr, pos,
                       bias_last, bias_new, *, B, pps):
    n_pages = page_table.shape[1]
    n_steps = n_pages // pps
    rows = bias_last.shape[0]

    def page_spec(i):
        return pl.BlockSpec((None, None) + cache_k.shape[2:],
                            lambda b, s, pt: (pt[b, s * pps + i], layer, 0, 0, 0))

    tile_spec = pl.BlockSpec((1, T_PAD, LANE), lambda b, s, pt: (b, 0, 0))
    grid_spec = pltpu.PrefetchScalarGridSpec(
        num_scalar_prefetch=1,
        grid=(B, n_steps),
        in_specs=[pl.BlockSpec((T_PAD, 4 * LANE), lambda b, s, pt: (b, C_DQ // 4)),
                  pl.BlockSpec((T_PAD, LANE), lambda b, s, pt: (b, C_DK)),
                  pl.BlockSpec((T_PAD, LANE), lambda b, s, pt: (b, C_DV)),
                  pl.BlockSpec((1, pps, T_PAD, LANE), lambda b, s, pt: (b, s, 0, 0)),
                  tile_spec, tile_spec, tile_spec,
                  pl.BlockSpec(bias_last.shape, lambda b, s, pt: (0, 0)),
                  pl.BlockSpec(bias_new.shape, lambda b, s, pt: (0, 0))]
        + [page_spec(i) for i in range(pps)] + [page_spec(i) for i in range(pps)],
        out_specs=pl.BlockSpec((T_PAD, 4 * LANE), lambda b, s, pt: (b, 0)),
        scratch_shapes=[pltpu.VMEM((rows, 1), F32), pltpu.VMEM((rows, 1), F32),
                        pltpu.VMEM((rows, HEAD_DIM), F32)])
    return pl.pallas_call(
        functools.partial(_dsa_att_body, pps=pps, n_steps=n_steps),
        grid_spec=grid_spec,
        out_shape=jax.ShapeDtypeStruct((B * T_PAD, 4 * LANE), F32),
        compiler_params=_cparams(("parallel", "arbitrary")),
        name="dsa_attend_decode")(page_table, P, P, P, keys, keyn, thr, pos, bias_last, bias_new,
                                  *([cache_k] * pps), *([cache_v] * pps))


def _bias_minus_far(tab, dist):
    onehot = jax.nn.one_hot(_rel_bucket(dist), N_BUCKETS, dtype=F32)
    bias = jnp.einsum("...k,km->...m", onehot, tab, precision=lax.Precision.HIGHEST)
    return bias - tab[N_BUCKETS - 1]


def _toeplitz_tiles(tab, t):
    r = jnp.arange(t)[:, None]
    c = jnp.arange(t)[None, :]
    tiles = jnp.stack([_bias_minus_far(tab, r - c), _bias_minus_far(tab, t + r - c)], axis=0)
    return jnp.transpose(tiles, (3, 0, 1, 2))


def _decode_bias(tab, past, t_real):
    tq = jnp.arange(T_PAD)[:, None]
    kc = jnp.arange(LANE)[None, :]
    last = _bias_minus_far(tab, past + tq - (past - PAGE + kc))
    new = _bias_minus_far(tab, tq - kc)
    valid = jnp.logical_and(kc <= tq, kc < t_real)
    new = jnp.where(valid[..., None], new, NEG_INF)
    return jnp.transpose(last, (2, 0, 1)), jnp.transpose(new, (2, 0, 1))


def _block_diag(w):
    n, d, e = w.shape
    eye = jnp.eye(n, dtype=w.dtype)
    return (eye[:, None, :, None] * w[:, :, None, :]).reshape(n * d, n * e)


def kernel(x_prompt, x_sample, cache_a_k, cache_a_v, cache_d_k, cache_d_v, cache_d_idx, state_b_h, state_b_conv, state_c_s, page_table, rel_bias, ln1, w_in, a_q_norm, a_k_norm, a_lam_q1, a_lam_k1, a_lam_q2, a_lam_k2, a_out_norm, b_conv_w, b_conv_b, b_wa, b_ba, b_wx, b_bx, b_lambda, c_lb_logits, c_out_norm, d_q_norm, d_k_norm, w_out, ln2, w_up, w_down):
    B, T, D = x_prompt.shape
    Bs, Ts, _ = x_sample.shape
    L = w_in.shape[0]
    n_pages = page_table.shape[1]
    past = n_pages * PAGE
    gw = D // 4
    a_heads = gw // (2 * HEAD_DIM)
    c_heads = c_lb_logits.shape[1] // LANE
    d_heads = gw // HEAD_DIM
    d_kv = cache_d_k.shape[3]
    d_grp = d_heads // d_kv
    n_pool = cache_a_k.shape[0]
    assert Ts <= T_PAD - 0 and Ts >= CONV_W - 1 and past > 0

    tab = rel_bias.astype(F32)
    n_a_maps = 2 * a_heads
    tab_a, tab_d = tab[:, :n_a_maps], tab[:, n_a_maps:]
    tq_a = min(128, T)
    tq_d = min(128, T)
    bias_a = jnp.transpose(_toeplitz_tiles(tab_a, tq_a).reshape(a_heads, 2, 2, tq_a, tq_a),
                           (0, 2, 3, 1, 4)).reshape(a_heads, 2, tq_a, 2 * tq_a)
    causal = jnp.where(jnp.arange(tq_a)[None, :] <= jnp.arange(tq_a)[:, None], 0.0, NEG_INF)
    causal = jnp.tile(causal, (1, 2)).astype(F32)
    bias_a = jnp.stack([jnp.full((a_heads, tq_a, 2 * tq_a), NEG_INF, F32), bias_a[:, 0] + causal,
                        bias_a[:, 1], jnp.zeros((a_heads, tq_a, 2 * tq_a), F32)], axis=1)
    bias_d = jnp.concatenate([_toeplitz_tiles(tab_d, tq_d),
                              jnp.zeros((d_heads, 1, tq_d, tq_d), F32)], axis=1)
    bias_d = jnp.transpose(bias_d.reshape(d_heads // 2, 2, 3, tq_d, tq_d),
                           (0, 2, 3, 1, 4)).reshape(d_heads // 2, 3, tq_d, 2 * tq_d)
    bl_a, bn_a = _decode_bias(tab_a, past, Ts)
    bl_a = jnp.repeat(bl_a.reshape(n_a_maps * T_PAD, LANE), a_heads, axis=1)
    bn_a = bn_a.reshape(n_a_maps * T_PAD, LANE)
    bl_d, bn_d = _decode_bias(tab_d, past, Ts)
    bl_d = bl_d.reshape(d_heads * T_PAD, LANE)
    bn_d = bn_d.reshape(d_heads * T_PAD, LANE)

    lb_cum = jnp.cumsum(jax.nn.softmax(c_lb_logits.astype(F32), axis=0), axis=0)
    lb_all = (lb_cum - lb_cum[0]).reshape(L, c_heads, 1, LANE)

    seg = jnp.kron(jnp.eye(LANE // HEAD_DIM, dtype=F32),
                   jnp.full((HEAD_DIM, HEAD_DIM), 1.0 / HEAD_DIM, F32)).astype(BF16)
    qscale = HEAD_DIM ** -0.5
    zeros = lambda n: jnp.zeros((n,), F32)
    ones = lambda n: jnp.ones((n,), F32)
    flag = jnp.concatenate([ones(2 * gw), zeros(7 * gw), ones(gw), ones(LANE),
                            zeros(P_WIDTH - 10 * gw - LANE)]).reshape(1, P_WIDTH)
    w_in_t = jnp.transpose(w_in, (0, 2, 1))
    n_in = w_in_t.shape[1]
    tail0 = (P_WIDTH // IN_PROJ_TN - 1) * IN_PROJ_TN
    assert tail0 < n_in <= P_WIDTH
    w_in_tail = jnp.pad(w_in_t[:, tail0:], ((0, 0), (0, tail0 + IN_PROJ_TN - n_in), (0, 0)))

    ca_k = cache_a_k.reshape(n_pool, L, PAGE * a_heads, 2 * HEAD_DIM)
    ca_v = cache_a_v.reshape(n_pool, L, PAGE * a_heads, 2 * HEAD_DIM)
    cd_k = jnp.transpose(cache_d_k, (0, 1, 3, 4, 2))
    cd_v = jnp.transpose(cache_d_v, (0, 1, 3, 4, 2))
    cd_idx = jnp.transpose(cache_d_idx, (0, 1, 3, 2))

    xp = x_prompt.reshape(B * T, D)
    xs = jnp.pad(x_sample, ((0, 0), (0, T_PAD - Ts), (0, 0))).reshape(Bs * T_PAD, D)
    s_conv = jnp.pad(state_b_conv, ((0, 0), (0, 0), (SUBLANE - (CONV_W - 1), 0), (0, 0)))

    n_top_p = min(TOPK_MAX, T // 4)
    n_top_s = min(TOPK_MAX, (past + Ts) // 4)
    tm_p = min(1024, B * T)
    tm_f = min(1024, B * T)
    tt_b = min(256, T)
    tt_c = min(512, T)
    ch_c = math.gcd(T, 64)
    pps_a = math.gcd(n_pages, 16)
    pps_d = math.gcd(n_pages, 16)
    pps_att = math.gcd(n_pages, 32)

    outs_p, outs_s = [], []
    for l in range(L):
        gain = jnp.concatenate(
            [jnp.tile(a_q_norm[l], 2 * a_heads) * qscale, jnp.tile(a_k_norm[l], 2 * a_heads),
             ones(7 * gw), jnp.tile(d_q_norm[l], d_heads) * qscale,
             jnp.tile(d_k_norm[l], d_kv), ones(P_WIDTH - 10 * gw - LANE)]).reshape(1, P_WIDTH)
        g1 = ln1[l].reshape(1, D)
        g2 = ln2[l].reshape(1, D)
        lam_init = 0.8 - 0.6 * math.exp(-0.3 * l)
        lam = (jnp.exp(jnp.sum(a_lam_q1[l].astype(F32) * a_lam_k1[l].astype(F32)))
               - jnp.exp(jnp.sum(a_lam_q2[l].astype(F32) * a_lam_k2[l].astype(F32))) + lam_init)
        scal = jnp.stack([lam, jnp.asarray(1.0 - lam_init, F32)]).astype(F32)
        g_a = a_out_norm[l].reshape(1, LANE)
        g_c = c_out_norm[l].reshape(1, LANE)
        b_weights = (b_conv_w[l], b_conv_b[l].reshape(1, gw), _block_diag(b_wa[l]).astype(BF16),
                     b_ba[l].reshape(1, gw), _block_diag(b_wx[l]).astype(BF16),
                     b_bx[l].reshape(1, gw), b_lambda[l].reshape(1, gw))

        def dense_tail(x, mixes, tm, tmf):
            x1 = _out_proj(mixes, w_out, l, x, tm=tm, tn=min(512, D))
            return _ffn(x1, g2, w_up, w_down, l, tm=tmf, tf=min(512, w_up.shape[2]))

        Pp, ak_p, av_p = _in_proj(xp, g1, w_in_t, w_in_tail, l, gain, flag, seg, tm=tm_p)
        mix_a = _attn_a_prompt(Pp, scal, bias_a, g_a, B=B, T=T, tq=tq_a)
        mix_b, hfin, buf = _rglru(Pp, None, b_weights, B=B, T=T, tt=tt_b, t_real=T, out_dtype=BF16)
        mix_c, sfin = _hgrn(Pp, None, lb_all[l], g_c, B=B, T=T, tt=tt_c, ch=ch_c,
                            sub=min(16, ch_c), t_real=T, out_dtype=BF16)
        mix_d = _dsa_prompt(Pp, bias_d, B=B, T=T, tq=tq_d, n_top=n_top_p)
        xp = dense_tail(xp, (mix_a, mix_b, mix_c, mix_d), tm_p, tm_f)
        P3 = Pp.reshape(B, T, P_WIDTH)
        outs_p.append((ak_p.reshape(B, T, a_heads, 2 * HEAD_DIM),
                       av_p.reshape(B, T, a_heads, 2 * HEAD_DIM),
                       P3[..., C_DK * LANE:C_DV * LANE].reshape(B, T, d_kv, HEAD_DIM),
                       P3[..., C_DV * LANE:(C_DV + 1) * LANE].reshape(B, T, d_kv, HEAD_DIM),
                       P3[..., C_TAIL * LANE:C_TAIL * LANE + HEAD_DIM],
                       hfin.reshape(B, gw), buf, sfin))

        Ps, ak_s, av_s = _in_proj(xs, g1, w_in_t, w_in_tail, l, gain, flag, seg, tm=Bs * T_PAD)
        smix_a = _attn_a_decode(Ps, ca_k, ca_v, page_table, l, scal, bl_a, bn_a, g_a, B=Bs, pps=pps_a)
        smix_b, shfin, sbuf = _rglru(
            Ps, (state_b_h[:, l].reshape(Bs, 1, gw), s_conv[:, l]), b_weights,
            B=Bs, T=T_PAD, tt=T_PAD, t_real=Ts, out_dtype=F32)
        smix_c, ssfin = _hgrn(Ps, state_c_s[:, l], lb_all[l], g_c, B=Bs, T=T_PAD, tt=T_PAD,
                              ch=T_PAD, sub=T_PAD, t_real=Ts, out_dtype=F32)
        keys, keyn, thr, pos = _dsa_select_decode(Ps, cd_idx, page_table, l, B=Bs, pps=pps_d,
                                                  n_top=n_top_s, t_real=Ts)
        smix_d = _dsa_attend_decode(Ps, cd_k, cd_v, page_table, l, keys, keyn, thr, pos,
                                    bl_d, bn_d, B=Bs, pps=pps_att)
        xs = dense_tail(xs, (smix_a, smix_b, smix_c, smix_d), Bs * T_PAD, Bs * T_PAD)
        S3 = Ps.reshape(Bs, T_PAD, P_WIDTH)[:, :Ts]
        outs_s.append((ak_s.reshape(Bs, T_PAD, a_heads, 2 * HEAD_DIM)[:, :Ts],
                       av_s.reshape(Bs, T_PAD, a_heads, 2 * HEAD_DIM)[:, :Ts],
                       S3[..., C_DK * LANE:C_DV * LANE].reshape(Bs, Ts, d_kv, HEAD_DIM),
                       S3[..., C_DV * LANE:(C_DV + 1) * LANE].reshape(Bs, Ts, d_kv, HEAD_DIM),
                       S3[..., C_TAIL * LANE:C_TAIL * LANE + HEAD_DIM],
                       shfin.reshape(Bs, gw), sbuf, ssfin))

    y_prompt = xp.reshape(B, T, D)
    y_sample = xs.reshape(Bs, T_PAD, D)[:, :Ts]
    stack = lambda outs: [jnp.stack(s, axis=1) for s in zip(*outs)]
    return (y_prompt, y_sample, *stack(outs_p), *stack(outs_s))
```

```python
import functools
import math

import jax
import jax.numpy as jnp
import numpy as np
from jax import lax
from jax.experimental import pallas as pl
from jax.experimental.pallas import tpu as pltpu

F32 = jnp.float32
BF16 = jnp.bfloat16
I32 = jnp.int32

EPS = 1e-6
HEAD_DIM = 64
PAGE = 128
CONV_W = 4
LRU_C = 8.0
IDX_HEADS = 8
TOPK_MAX = 256
N_BUCKETS = 32
MAX_DIST = 128
LANE = 128
SUBLANE = 8
T_PAD = 8
INT_MIN = -(2 ** 31)
KEY_NEG_INF = -2139095041
VMEM_LIMIT = 56 * 1024 * 1024
NEG_INF = float("-inf")

C_AQ, C_AK, C_AV, C_BX, C_BG, C_CQ, C_CF, C_CI, C_CG, C_DQ = (4 * i for i in range(10))
C_DK, C_DV, C_IQ, C_TAIL = 40, 41, 42, 46
P_WIDTH = 48 * LANE
IW_OFF = 64
D_BLOCK = 8 * LANE
D_BLOCK_IQ = (C_IQ - C_DK) * LANE
D_BLOCK_TAIL = (C_TAIL - C_DK) * LANE


def _cparams(sem):
    return pltpu.CompilerParams(dimension_semantics=sem, vmem_limit_bytes=VMEM_LIMIT)


def _nt(a, b):
    return lax.dot_general(a, b, (((1,), (1,)), ((), ())), preferred_element_type=F32)


def _tn(a, b):
    return lax.dot_general(a, b, (((0,), (0,)), ((), ())), preferred_element_type=F32)


def _rel_bucket(dist):
    n = jnp.maximum(dist, 0)
    exact = N_BUCKETS // 2
    large = exact + (jnp.log(jnp.maximum(n, 1).astype(F32) / exact)
                     / math.log(MAX_DIST / exact) * (N_BUCKETS - exact)).astype(I32)
    return jnp.where(n < exact, n, jnp.minimum(large, N_BUCKETS - 1))


def _in_proj_body(x_ref, g_ref, w_ref, wtail_ref, gain_ref, flag_ref, seg_ref, o_ref, ak_ref, av_ref,
                  xn_ref, *, tn, norm_tiles, head_tiles):
    j = pl.program_id(1)
    n_sub = tn // LANE
    tm = o_ref.shape[0]

    def head_rows(dst_ref):
        for c in range(n_sub):
            dst_ref[pl.ds(c, tm, stride=n_sub), :] = o_ref[:, c * LANE:(c + 1) * LANE]

    @pl.when(j == 0)
    def _():
        x = x_ref[...]
        ms = jnp.mean(x * x, axis=-1, keepdims=True)
        xn_ref[...] = (x * lax.rsqrt(ms + EPS) * g_ref[...]).astype(BF16)

    w = jnp.where(j == pl.num_programs(1) - 1, wtail_ref[...], w_ref[...])
    y = _nt(xn_ref[...], w.astype(BF16))
    is_norm = functools.reduce(jnp.logical_or, [j == t for t in norm_tiles])

    @pl.when(is_norm)
    def _():
        seg = seg_ref[...]
        for c in range(tn // LANE):
            sl = slice(c * LANE, (c + 1) * LANE)
            yc = y[:, sl]
            y2 = yc * yc
            hi = y2.astype(BF16)
            lo = (y2 - hi.astype(F32)).astype(BF16)
            ms = (jnp.dot(hi, seg, preferred_element_type=F32)
                  + jnp.dot(lo, seg, preferred_element_type=F32))
            yn = yc * lax.rsqrt(ms + EPS) * gain_ref[:, sl]
            o_ref[:, sl] = jnp.where(flag_ref[:, sl] > 0, yn, yc)

    @pl.when(jnp.logical_not(is_norm))
    def _():
        o_ref[...] = y

    @pl.when(j == head_tiles[0])
    def _():
        head_rows(ak_ref)

    @pl.when(j == head_tiles[1])
    def _():
        head_rows(av_ref)


IN_PROJ_TN = 4 * LANE


def _in_proj(x, g, wt, wtail, layer, gain, flag, seg, *, tm):
    M, K = x.shape
    tn = IN_PROJ_TN
    n_tiles = P_WIDTH // tn
    norm_ranges = ((C_AQ, 4), (C_AK, 4), (C_DQ, 4), (C_DK, 1))
    norm_tiles = tuple(sorted({t for c, n in norm_ranges
                               for t in range(c * LANE // tn, ((c + n) * LANE - 1) // tn + 1)}))

    n_sub = tn // LANE
    assert C_AK % n_sub == 0 and C_AV % n_sub == 0
    head_spec = pl.BlockSpec((tm * n_sub, LANE), lambda i, j: (i, 0))
    head_shape = jax.ShapeDtypeStruct((M * n_sub, LANE), F32)
    return pl.pallas_call(
        functools.partial(_in_proj_body, tn=tn, norm_tiles=norm_tiles,
                          head_tiles=(C_AK // n_sub, C_AV // n_sub)),
        grid=(M // tm, n_tiles),
        in_specs=[pl.BlockSpec((tm, K), lambda i, j: (i, 0)),
                  pl.BlockSpec((1, K), lambda i, j: (0, 0)),
                  pl.BlockSpec((None, tn, K), lambda i, j: (layer, jnp.minimum(j, n_tiles - 2), 0)),
                  pl.BlockSpec((None, tn, K), lambda i, j: (layer, 0, 0)),
                  pl.BlockSpec((1, tn), lambda i, j: (0, j)),
                  pl.BlockSpec((1, tn), lambda i, j: (0, j)),
                  pl.BlockSpec((LANE, LANE), lambda i, j: (0, 0))],
        out_specs=[pl.BlockSpec((tm, tn), lambda i, j: (i, j)), head_spec, head_spec],
        out_shape=[jax.ShapeDtypeStruct((M, P_WIDTH), F32), head_shape, head_shape],
        scratch_shapes=[pltpu.VMEM((tm, K), BF16)],
        compiler_params=_cparams(("parallel", "arbitrary")),
        name="in_proj")(x, g, wt, wtail, gain, flag, seg)


def _out_proj_body(a_ref, b_ref, c_ref, d_ref, w_ref, x_ref, o_ref, wb_ref, *, gw):
    @pl.when(pl.program_id(1) == 0)
    def _():
        wb_ref[...] = w_ref[...].astype(BF16)

    acc = x_ref[...]
    for g, m_ref in enumerate((a_ref, b_ref, c_ref, d_ref)):
        acc = acc + jnp.dot(m_ref[...].astype(BF16), wb_ref[g * gw:(g + 1) * gw, :],
                            preferred_element_type=F32)
    o_ref[...] = acc


def _out_proj(mixes, w, layer, x, *, tm, tn):
    M, D = x.shape
    gw = mixes[0].shape[1]
    mix_spec = pl.BlockSpec((tm, gw), lambda j, i: (i, 0))
    return pl.pallas_call(
        functools.partial(_out_proj_body, gw=gw),
        grid=(D // tn, M // tm),
        in_specs=[mix_spec, mix_spec, mix_spec, mix_spec,
                  pl.BlockSpec((None, w.shape[1], tn), lambda j, i: (layer, 0, j)),
                  pl.BlockSpec((tm, tn), lambda j, i: (i, j))],
        out_specs=pl.BlockSpec((tm, tn), lambda j, i: (i, j)),
        out_shape=jax.ShapeDtypeStruct((M, D), F32),
        scratch_shapes=[pltpu.VMEM((w.shape[1], tn), BF16)],
        compiler_params=_cparams(("parallel", "arbitrary")),
        name="out_proj")(*mixes, w, x)


def _ffn_body(x_ref, g_ref, wu_ref, wd_ref, o_ref, xn_ref):
    f = pl.program_id(1)

    @pl.when(f == 0)
    def _():
        x = x_ref[...]
        ms = jnp.mean(x * x, axis=-1, keepdims=True)
        xn_ref[...] = (x * lax.rsqrt(ms + EPS) * g_ref[...]).astype(BF16)
        o_ref[...] = x

    h = jnp.dot(xn_ref[...], wu_ref[...].astype(BF16), preferred_element_type=F32)
    h = jnp.maximum(h, 0.0)
    h = (h * h).astype(BF16)
    o_ref[...] += jnp.dot(h, wd_ref[...].astype(BF16), preferred_element_type=F32)


def _ffn(x, g, wu, wd, layer, *, tm, tf):
    M, D = x.shape
    Fd = wu.shape[2]
    return pl.pallas_call(
        _ffn_body,
        grid=(M // tm, Fd // tf),
        in_specs=[pl.BlockSpec((tm, D), lambda i, f: (i, 0), pipeline_mode=pl.Buffered(1)),
                  pl.BlockSpec((1, D), lambda i, f: (0, 0)),
                  pl.BlockSpec((None, D, tf), lambda i, f: (layer, 0, f)),
                  pl.BlockSpec((None, tf, D), lambda i, f: (layer, f, 0))],
        out_specs=pl.BlockSpec((tm, D), lambda i, f: (i, 0)),
        out_shape=jax.ShapeDtypeStruct((M, D), F32),
        scratch_shapes=[pltpu.VMEM((tm, D), BF16)],
        compiler_params=_cparams(("parallel", "arbitrary")),
        name="ffn")(x, g, wu, wd)


def _softmax_update(state, s, v):
    m, l, acc = state
    m_new = jnp.maximum(m, jnp.max(s, axis=-1, keepdims=True))
    alpha = jnp.exp(m - m_new)
    p = jnp.exp(s - m_new)
    l = alpha * l + jnp.sum(p, axis=-1, keepdims=True)
    acc = alpha * acc + jnp.dot(p.astype(BF16), v, preferred_element_type=F32)
    return m_new, l, acc


def _rms_rows(o, gain):
    ms = jnp.mean(o * o, axis=-1, keepdims=True)
    return o * lax.rsqrt(ms + EPS) * gain


def _pair_attention(n_grp, tile_grp, n_pair, tq, logits, values, s_ref, m_ref, l_ref, acc_ref,
                    tile_aux=None):
    def max_pass(i, carry):
        cs = [i * tile_grp + j for j in range(tile_grp)]
        aux = [tile_aux(c) for c in cs] if tile_aux is not None else None
        for p in range(n_pair):
            t = []
            for j, c in enumerate(cs):
                s = logits(p, c) if aux is None else logits(p, c, aux[j])
                s_ref[p, c] = s
                t.append(s)
            while len(t) > 1:
                t = [jnp.maximum(a, b) for a, b in zip(t[::2], t[1::2])]
            m_ref[p] = jnp.maximum(m_ref[p], t[0])
        return carry

    m_ref[...] = jnp.full(m_ref.shape, NEG_INF, F32)
    lax.fori_loop(0, n_grp, max_pass, 0)
    for p in range(n_pair):
        m = m_ref[p]
        m_ref[p] = jnp.concatenate(
            [jnp.broadcast_to(jnp.max(m[:, :LANE], axis=-1, keepdims=True), (tq, LANE)),
             jnp.broadcast_to(jnp.max(m[:, LANE:], axis=-1, keepdims=True), (tq, LANE))], axis=1)
    l_ref[...] = jnp.zeros(l_ref.shape, F32)
    acc_ref[...] = jnp.zeros(acc_ref.shape, F32)

    def sum_pass(i, carry):
        for p in range(n_pair):
            e = [jnp.exp(s_ref[p, i * tile_grp + j] - m_ref[p]) for j in range(tile_grp)]
            l_ref[p] += functools.reduce(lambda a, b: a + b, e)
            pb = jnp.concatenate(
                [jnp.concatenate([x[:, :LANE], x[:, LANE:]], axis=0).astype(BF16) for x in e], axis=1)
            acc_ref[p] += jnp.dot(pb, values(p, i), preferred_element_type=F32)
        return carry

    lax.fori_loop(0, n_grp, sum_pass, 0)
    outs = []
    for p in range(n_pair):
        l = l_ref[p]
        acc = acc_ref[p]
        outs.append((acc[:tq] / jnp.sum(l[:, :LANE], axis=-1, keepdims=True),
                     acc[tq:] / jnp.sum(l[:, LANE:], axis=-1, keepdims=True)))
    return outs


def _attn_a_body(scal_ref, q_ref, k_ref, v_ref, bias_ref, gout_ref, o_ref,
                 kd_ref, vb_ref, s_ref, m_ref, l_ref, acc_ref, *, tq, tile_grp):
    qb = pl.program_id(1)
    n_h = q_ref.shape[1] // LANE
    n_grp = (qb + tile_grp) // tile_grp
    low_half = lax.broadcasted_iota(I32, (tq, LANE), 1) < HEAD_DIM

    @pl.when(qb == 0)
    def _():
        kd_ref[...] = jnp.zeros(kd_ref.shape, BF16)
        vb_ref[...] = jnp.zeros(vb_ref.shape, BF16)

    r_q = pl.multiple_of(qb * LANE, LANE)
    for h in range(n_h):
        kx = k_ref[pl.ds(r_q, LANE), h * LANE:(h + 1) * LANE]
        kd_ref[h, qb] = jnp.concatenate([jnp.where(low_half, kx, 0.0),
                                         jnp.where(low_half, 0.0, kx)], axis=0).astype(BF16)
    vb_ref[pl.ds(r_q, LANE), :] = v_ref[pl.ds(r_q, LANE), :].astype(BF16)
    q = q_ref[...].astype(BF16)

    def logits(h, c):
        return (_nt(q[:, h * LANE:(h + 1) * LANE], kd_ref[h, c])
                + bias_ref[h, jnp.clip(qb - c, -1, 2) + 1])

    def values(h, i):
        r0 = pl.multiple_of(i * tile_grp * LANE, tile_grp * LANE)
        return vb_ref[pl.ds(r0, tile_grp * LANE), h * LANE:(h + 1) * LANE]

    outs = _pair_attention(n_grp, tile_grp, n_h, tq, logits, values, s_ref, m_ref, l_ref, acc_ref)
    lam = scal_ref[0]
    for h, (o1, o2) in enumerate(outs):
        o = _rms_rows(o1 - lam * o2, gout_ref[...]) * scal_ref[1]
        o_ref[:, h * LANE:(h + 1) * LANE] = o.astype(o_ref.dtype)


def _attn_a_prompt(P, scal, bias, gout, *, B, T, tq):
    assert tq == LANE
    H = bias.shape[0]
    W = H * LANE
    nq = T // tq
    tile_grp = math.gcd(nq, 4)
    return pl.pallas_call(
        functools.partial(_attn_a_body, tq=tq, tile_grp=tile_grp),
        grid=(B, nq),
        in_specs=[pl.BlockSpec(memory_space=pltpu.SMEM),
                  pl.BlockSpec((tq, W), lambda b, i: (b * nq + i, C_AQ // H)),
                  pl.BlockSpec((T, W), lambda b, i: (b, C_AK // H)),
                  pl.BlockSpec((T, W), lambda b, i: (b, C_AV // H)),
                  pl.BlockSpec(bias.shape, lambda b, i: (0, 0, 0, 0)),
                  pl.BlockSpec((1, LANE), lambda b, i: (0, 0))],
        out_specs=pl.BlockSpec((tq, W), lambda b, i: (b * nq + i, 0)),
        out_shape=jax.ShapeDtypeStruct((B * T, W), BF16),
        scratch_shapes=[pltpu.VMEM((H, nq, 2 * LANE, LANE), BF16),
                        pltpu.VMEM((T, W), BF16),
                        pltpu.VMEM((H, nq, tq, 2 * LANE), F32),
                        pltpu.VMEM((H, tq, 2 * LANE), F32),
                        pltpu.VMEM((H, tq, 2 * LANE), F32),
                        pltpu.VMEM((H, 2 * tq, LANE), F32)],
        compiler_params=_cparams(("parallel", "arbitrary")),
        name="attn_a_prompt")(scal, P, P, P, bias, gout)


def _rglru_body(*refs, tt, t_real, n_t, has_state, pos0_is_zero):
    if has_state:
        (x_ref, g_ref, h0_ref, buf0_ref, cw_ref, cb_ref, wa_ref, ba_ref, wx_ref, bxb_ref, lam_ref,
         o_ref, hfin_ref, buf_ref, xpad_ref, a_ref, b_ref, hs_ref, hc_ref) = refs
    else:
        (x_ref, g_ref, cw_ref, cb_ref, wa_ref, ba_ref, wx_ref, bxb_ref, lam_ref,
         o_ref, hfin_ref, buf_ref, xpad_ref, a_ref, b_ref, hs_ref, hc_ref) = refs
    ti = pl.program_id(1)
    W = x_ref.shape[1]

    @pl.when(ti == 0)
    def _():
        if has_state:
            xpad_ref[0:SUBLANE, :] = buf0_ref[0]
            hc_ref[...] = jnp.broadcast_to(h0_ref[0], (SUBLANE, W))
        else:
            xpad_ref[0:SUBLANE, :] = jnp.zeros((SUBLANE, W), F32)
            hc_ref[...] = jnp.zeros((SUBLANE, W), F32)

    x = x_ref[...]
    xpad_ref[SUBLANE:SUBLANE + tt, :] = x
    xc = cb_ref[...] + x * cw_ref[CONV_W - 1:CONV_W, :]
    for j in range(CONV_W - 1):
        off = SUBLANE - (CONV_W - 1) + j
        xc = xc + xpad_ref[off:off + tt, :] * cw_ref[j:j + 1, :]
    xcb = xc.astype(BF16)
    r = jax.nn.sigmoid(jnp.dot(xcb, wa_ref[...], preferred_element_type=F32) + ba_ref[...])
    i = jax.nn.sigmoid(jnp.dot(xcb, wx_ref[...], preferred_element_type=F32) + bxb_ref[...])
    nl = -lam_ref[...]
    softplus = jnp.maximum(nl, 0.0) + jnp.log1p(jnp.exp(-jnp.abs(nl)))
    a = jnp.exp(-LRU_C * r * softplus)
    mult = jnp.sqrt(1.0 - a * a)
    if pos0_is_zero:
        rows = lax.broadcasted_iota(I32, (tt, W), 0)
        mult = jnp.where(jnp.logical_and(rows == 0, ti == 0), 1.0, mult)
    a_ref[...] = a
    b_ref[...] = mult * (i * xc)
    rowt = lax.broadcasted_iota(I32, (SUBLANE, W), 0)

    def tile(n, h_prev):
        r0 = pl.multiple_of(n * SUBLANE, SUBLANE)
        at = a_ref[pl.ds(r0, SUBLANE), :]
        bt = b_ref[pl.ds(r0, SUBLANE), :]
        for s in (1, 2, 4):
            keep = rowt >= s
            bt = jnp.where(keep, bt + at * pltpu.roll(bt, s, 0), bt)
            at = jnp.where(keep, at * pltpu.roll(at, s, 0), at)
        ht = bt + at * h_prev
        hs_ref[pl.ds(r0, SUBLANE), :] = ht
        return jnp.broadcast_to(ht[SUBLANE - 1:SUBLANE, :], (SUBLANE, W))

    hc_ref[...] = lax.fori_loop(0, tt // SUBLANE, tile, hc_ref[...])
    o_ref[...] = (hs_ref[...] * jax.nn.gelu(g_ref[...])).astype(o_ref.dtype)
    xpad_ref[0:SUBLANE, :] = xpad_ref[tt:tt + SUBLANE, :]

    @pl.when(ti == n_t - 1)
    def _():
        t_loc = t_real - (n_t - 1) * tt
        hfin_ref[0] = hs_ref[t_loc - 1:t_loc, :]
        buf_ref[0] = x_ref[t_loc - (CONV_W - 1):t_loc, :]


def _rglru(P, state, weights, *, B, T, tt, t_real, out_dtype):
    W = 4 * LANE
    n_t = T // tt
    has_state = state is not None
    xspec = pl.BlockSpec((tt, W), lambda b, i: (b * n_t + i, C_BX // 4))
    gspec = pl.BlockSpec((tt, W), lambda b, i: (b * n_t + i, C_BG // 4))
    full = lambda shape: pl.BlockSpec(shape, lambda b, i: (0,) * len(shape))
    wspecs = [full((CONV_W, W)), full((1, W)), full((W, W)), full((1, W)), full((W, W)),
              full((1, W)), full((1, W))]
    in_specs = [xspec, gspec]
    args = [P, P]
    if has_state:
        in_specs += [pl.BlockSpec((1, 1, W), lambda b, i: (b, 0, 0)),
                     pl.BlockSpec((1, SUBLANE, W), lambda b, i: (b, 0, 0))]
        args += list(state)
    return pl.pallas_call(
        functools.partial(_rglru_body, tt=tt, t_real=t_real, n_t=n_t, has_state=has_state,
                          pos0_is_zero=not has_state),
        grid=(B, n_t),
        in_specs=in_specs + wspecs,
        out_specs=[pl.BlockSpec((tt, W), lambda b, i: (b * n_t + i, 0)),
                   pl.BlockSpec((1, 1, W), lambda b, i: (b, 0, 0)),
                   pl.BlockSpec((1, CONV_W - 1, W), lambda b, i: (b, 0, 0))],
        out_shape=[jax.ShapeDtypeStruct((B * T, W), out_dtype),
                   jax.ShapeDtypeStruct((B, 1, W), F32),
                   jax.ShapeDtypeStruct((B, CONV_W - 1, W), F32)],
        scratch_shapes=[pltpu.VMEM((tt + SUBLANE, W), F32), pltpu.VMEM((tt, W), F32),
                        pltpu.VMEM((tt, W), F32), pltpu.VMEM((tt, W), F32),
                        pltpu.VMEM((SUBLANE, W), F32)],
        compiler_params=_cparams(("parallel", "arbitrary")),
        name="rglru")(*args, *weights)


def _hgrn_body(*refs, tt, ch, sub, t_real, n_t, has_state):
    if has_state:
        q_ref, f_ref, v_ref, g_ref, s0_ref, lb_ref, gain_ref, o_ref, sfin_ref, st_ref = refs
    else:
        q_ref, f_ref, v_ref, g_ref, lb_ref, gain_ref, o_ref, sfin_ref, st_ref = refs
    ti = pl.program_id(1)
    n_h = st_ref.shape[0]
    dk = LANE

    @pl.when(ti == 0)
    def _():
        for h in range(n_h):
            if has_state:
                st_ref[h] = s0_ref[0, h].T
            else:
                st_ref[h] = jnp.zeros((dk, dk), F32)

    rr = lax.broadcasted_iota(I32, (ch, ch), 0)
    cc = lax.broadcasted_iota(I32, (ch, ch), 1)
    tril = (cc <= rr).astype(F32)
    rows = lax.broadcasted_iota(I32, (ch, dk), 0)
    srow = lax.broadcasted_iota(I32, (sub, 1), 0)
    n_sub = ch // sub

    def head_chunk(c, h):
        r0 = pl.multiple_of(c * ch, ch)
        hs = slice(h * dk, (h + 1) * dk)
        lb = lb_ref[h]
        log_lb = jnp.log(lb)
        log_1mlb = jnp.log1p(-lb)
        q = q_ref[pl.ds(r0, ch), hs]
        q = q * jax.nn.sigmoid(q)
        fp = f_ref[pl.ds(r0, ch), hs]
        v = v_ref[pl.ds(r0, ch), hs]
        log_sig = jnp.minimum(fp, 0.0) - jnp.log1p(jnp.exp(-jnp.abs(fp)))
        b = log_1mlb + log_sig
        lf = jnp.maximum(log_lb, b) + jnp.log1p(jnp.exp(-jnp.abs(log_lb - b)))
        kk = (1.0 - lb) * jax.nn.sigmoid(-fp)
        if t_real < tt * n_t:
            live = (rows + (ti * tt + c * ch)) < t_real
            lf = jnp.where(live, lf, 0.0)
            kk = jnp.where(live, kk, 0.0)
        G = jnp.dot(tril, lf, preferred_element_type=F32, precision=lax.Precision.HIGHEST)
        st = st_ref[h]
        vb = v.astype(BF16)
        o = _nt((q * jnp.exp(G)).astype(BF16), st.astype(BF16))
        outs = []
        for i in range(n_sub):
            lo, hi = i * sub, (i + 1) * sub
            qi, Gi, ki, vi = q[lo:hi], G[lo:hi], kk[lo:hi], v[lo:hi]
            oi = o[lo:hi]
            if i > 0:
                R = G[lo - 1:lo]
                qp = (qi * jnp.exp(Gi - R)).astype(BF16)
                kp = (kk[:lo] * jnp.exp(R - G[:lo])).astype(BF16)
                att = _nt(qp, kp)
                oi = oi + jnp.dot(att.astype(BF16), vb[:lo], preferred_element_type=F32)
            for s in range(sub):
                w = jnp.exp(jnp.minimum(Gi - Gi[s:s + 1], 0.0))
                colv = jnp.sum(qi * w * ki[s:s + 1], axis=-1, keepdims=True)
                colv = jnp.where(srow >= s, colv, 0.0)
                oi = oi + colv * vi[s:s + 1]
            outs.append(oi)
        o = jnp.concatenate(outs, axis=0) if n_sub > 1 else outs[0]
        gl = G[ch - 1:ch]
        kpp = (kk * jnp.exp(gl - G)).astype(BF16)
        st_ref[h] = st * jnp.exp(gl) + _tn(vb, kpp)
        gate = g_ref[pl.ds(r0, ch), hs]
        o = _rms_rows(o, gain_ref[...]) * (gate * jax.nn.sigmoid(gate))
        o_ref[pl.ds(r0, ch), hs] = o.astype(o_ref.dtype)

    n_chunks = tt // ch
    per_iter = 2 if n_chunks % 2 == 0 else 1

    def chunk(c, carry):
        for u in range(per_iter):
            for h in range(n_h):
                head_chunk(c * per_iter + u, h)
        return carry

    lax.fori_loop(0, n_chunks // per_iter, chunk, 0)

    @pl.when(ti == n_t - 1)
    def _():
        for h in range(n_h):
            sfin_ref[0, h] = st_ref[h].T


def _hgrn(P, s0, lb, gain, *, B, T, tt, ch, sub, t_real, out_dtype):
    H = lb.shape[0]
    n_t = T // tt
    has_state = s0 is not None

    W = H * LANE

    def col(c0):
        return pl.BlockSpec((tt, W), lambda b, i: (b * n_t + i, c0 // H))

    in_specs = [col(C_CQ), col(C_CF), col(C_CI), col(C_CG)]
    args = [P, P, P, P]
    if has_state:
        in_specs.append(pl.BlockSpec((1, H, LANE, LANE), lambda b, i: (b, 0, 0, 0)))
        args.append(s0)
    in_specs += [pl.BlockSpec((H, 1, LANE), lambda b, i: (0, 0, 0)),
                 pl.BlockSpec((1, LANE), lambda b, i: (0, 0))]
    return pl.pallas_call(
        functools.partial(_hgrn_body, tt=tt, ch=ch, sub=sub, t_real=t_real, n_t=n_t,
                          has_state=has_state),
        grid=(B, n_t),
        in_specs=in_specs,
        out_specs=[pl.BlockSpec((tt, W), lambda b, i: (b * n_t + i, 0)),
                   pl.BlockSpec((1, H, LANE, LANE), lambda b, i: (b, 0, 0, 0))],
        out_shape=[jax.ShapeDtypeStruct((B * T, W), out_dtype),
                   jax.ShapeDtypeStruct((B, H, LANE, LANE), F32)],
        scratch_shapes=[pltpu.VMEM((H, LANE, LANE), F32)],
        compiler_params=_cparams(("parallel", "arbitrary")),
        name="hgrn2")(*args, lb, gain)


def _score_key(score):
    score = jnp.where(score == 0.0, 0.0, score)
    bits = pltpu.bitcast(score, I32)
    return bits ^ ((bits >> 31) & 0x7FFFFFFF)


def _kth_largest(count_ge, shape, k, bits=32):
    zero = jnp.zeros(shape, I32)
    v = jnp.where(count_ge(zero) >= k, zero, jnp.full(shape, -(2 ** (bits - 1)), I32))

    def bit_step(n, v):
        cand = v + (jnp.int32(1) << (bits - 2 - n))
        return jnp.where(count_ge(cand) >= k, cand, v)

    return lax.fori_loop(0, bits - 1, bit_step, v)


def _tie_bound(count_tie_below, r, shape, n_bits):
    def bit_step(n, pos):
        cand = pos + (jnp.int32(1) << (n_bits - 1 - n))
        return jnp.where(count_tie_below(cand) < r, cand, pos)

    return lax.fori_loop(0, n_bits, bit_step, jnp.zeros(shape, I32))


def _dsa_body(dq_ref, iq_ref, tq_ref, dk_ref, dv_ref, ik_ref, bias_ref, o_ref,
              key_ref, hi_ref, lo_ref, pos_ref, kd_ref, vb_ref, ikb_ref,
              s_ref, m_ref, l_ref, acc_ref, *, tq, n_top, n_bits, tile_grp):
    qb = pl.program_id(1)
    n_c = qb + 1
    hd = HEAD_DIM
    n_kv = dk_ref.shape[1] // hd
    n_q = dq_ref.shape[1] // hd
    grp = n_q // n_kv
    n_pair = n_q // 2
    row = lax.broadcasted_iota(I32, (tq, LANE), 0)
    col = lax.broadcasted_iota(I32, (tq, LANE), 1)
    low_half = col < hd

    @pl.when(qb == 0)
    def _():
        kd_ref[...] = jnp.zeros(kd_ref.shape, BF16)
        vb_ref[...] = jnp.zeros(vb_ref.shape, BF16)
        ikb_ref[...] = jnp.zeros(ikb_ref.shape, BF16)

    def pair_blocks(at_low, at_high):
        return jnp.concatenate([jnp.where(low_half, at_low, 0.0),
                                jnp.where(low_half, 0.0, at_high)], axis=0).astype(BF16)

    r_q = pl.multiple_of(qb * LANE, LANE)
    kx = dk_ref[pl.ds(r_q, LANE), :]
    kx_sw = pltpu.roll(kx, hd, 1)
    kd_ref[0, qb] = pair_blocks(kx, kx_sw)
    kd_ref[1, qb] = pair_blocks(kx_sw, kx)
    ikb_ref[pl.ds(r_q, LANE), :] = ik_ref[pl.ds(r_q, LANE), :].astype(BF16)
    vb_ref[pl.ds(r_q, LANE), :] = dv_ref[pl.ds(r_q, LANE), :].astype(BF16)

    iq = iq_ref[:, D_BLOCK_IQ:D_BLOCK_IQ + IDX_HEADS * hd]
    iqz = []
    for p in range(IDX_HEADS // 2):
        pair = iq[:, p * LANE:(p + 1) * LANE]
        iqz.append(jnp.concatenate([jnp.where(low_half, pair, 0.0),
                                    jnp.where(low_half, pltpu.roll(pair, hd, 1), 0.0)],
                                   axis=0).astype(BF16))
    wscale = IDX_HEADS ** -0.5 * hd ** -0.5
    tail_t = tq_ref[...].T
    w_rows = [tail_t[IW_OFF + h:IW_OFF + h + 1, :] * wscale for h in range(IDX_HEADS)]

    n_grp = (n_c + tile_grp - 1) // tile_grp

    def score_tile(c):
        r0 = pl.multiple_of(c * LANE, LANE)
        ikc = ikb_ref[pl.ds(r0, LANE), :]
        sc = None
        for p in range(IDX_HEADS // 2):
            s = jnp.maximum(_nt(ikc, iqz[p]), 0.0)
            t = s[:, :LANE] * w_rows[2 * p] + s[:, LANE:] * w_rows[2 * p + 1]
            sc = t if sc is None else sc + t
        causal = jnp.logical_or(c < qb, jnp.logical_and(c == qb, row <= col))
        key = _score_key(jnp.where(causal, sc, NEG_INF))
        key_ref[c] = key
        hi_ref[c] = (key >> 16).astype(jnp.int16)
        lo_ref[c] = (((key ^ 0x8000) << 16) >> 16).astype(jnp.int16)

    def score_group(i, carry):
        for j in range(tile_grp):
            score_tile(i * tile_grp + j)
        return carry

    lax.fori_loop(0, n_grp, score_group, 0)
    vec = (1, tq)

    def count(pred, ref=key_ref, dtype=I32):
        def body(i, acc):
            for j in range(tile_grp):
                c = i * tile_grp + j
                acc = acc + pred(ref[c], c).astype(dtype)
            return acc
        acc = lax.fori_loop(0, n_grp, body, jnp.zeros((LANE, tq), dtype))
        return jnp.sum(acc.astype(I32), axis=0, keepdims=True)

    def count16_ge(ref):
        def f(cand):
            cb = jnp.broadcast_to(cand, (LANE, tq)).astype(jnp.int16)
            return count(lambda key, c: key >= cb, ref, jnp.int16)
        return f

    min16 = -(2 ** 15)
    t_hi = _kth_largest(count16_ge(hi_ref), vec, n_top, bits=16)
    t_hi_b = jnp.broadcast_to(t_hi, (LANE, tq)).astype(jnp.int16)
    k_lo = n_top - count(lambda key, c: key > t_hi_b, hi_ref, jnp.int16)

    def band_group(i, carry):
        for j in range(tile_grp):
            c = i * tile_grp + j
            lo_ref[c] = jnp.where(hi_ref[c] == t_hi_b, lo_ref[c], jnp.int16(min16))
        return carry

    lax.fori_loop(0, n_grp, band_group, 0)
    t_lo = _kth_largest(count16_ge(lo_ref), vec, k_lo, bits=16)
    thr = (t_hi << 16) | ((t_lo ^ min16) & 0xFFFF)
    thr_b = jnp.broadcast_to(thr, (LANE, tq))
    n_gt = count(lambda key, c: key > thr_b)
    n_ge = count(lambda key, c: key >= thr_b)
    need = jnp.logical_and(n_ge > n_top, thr > KEY_NEG_INF)
    pos_ref[...] = jnp.full((LANE, tq), 2 ** n_bits, I32)

    @pl.when(jnp.max(need.astype(I32)) > 0)
    def _():
        r = n_top - n_gt

        def count_tie_below(cand):
            cb = jnp.broadcast_to(cand, (LANE, tq))
            return count(lambda key, c: jnp.logical_and(key == thr_b, row + c * LANE < cb))

        pos = _tie_bound(count_tie_below, r, vec, n_bits)
        pos_ref[...] = jnp.broadcast_to(pos, (LANE, tq))

    pos_b = pos_ref[...]

    def tile_mask(c):
        key = key_ref[c]
        idx = row + c * LANE
        sel = jnp.logical_or(key > thr_b, jnp.logical_and(key == thr_b, idx <= pos_b))
        sel = jnp.logical_and(sel, key > KEY_NEG_INF)
        mk = jnp.where(sel, 0.0, NEG_INF).T
        return jnp.concatenate([mk, mk], axis=1)

    dq = dq_ref[...].astype(BF16)

    def logits(p, c, mask):
        return (_nt(dq[:, p * LANE:(p + 1) * LANE], kd_ref[(2 * p) // grp, c])
                + mask + bias_ref[p, jnp.clip(qb - c, 0, 2)])

    def values(p, i):
        r0 = pl.multiple_of(i * tile_grp * LANE, tile_grp * LANE)
        return vb_ref[pl.ds(r0, tile_grp * LANE), :]

    outs = _pair_attention(n_grp, tile_grp, n_pair, tq, logits, values, s_ref, m_ref, l_ref, acc_ref,
                           tile_aux=tile_mask)
    for p, (oa, ob) in enumerate(outs):
        if (2 * p) // grp == 0:
            ob = pltpu.roll(ob, hd, 1)
        else:
            oa = pltpu.roll(oa, hd, 1)
        o_ref[:, p * LANE:(p + 1) * LANE] = jnp.where(low_half, oa, ob).astype(o_ref.dtype)


def _dsa_prompt(P, bias, *, B, T, tq, n_top):
    assert tq == LANE
    nq = T // tq
    n_bits = max(1, int(math.ceil(math.log2(T))))
    tile_grp = math.gcd(nq, 4)
    n_pair = bias.shape[0]
    n_tiles = T // LANE
    return pl.pallas_call(
        functools.partial(_dsa_body, tq=tq, n_top=n_top, n_bits=n_bits, tile_grp=tile_grp),
        grid=(B, nq),
        in_specs=[pl.BlockSpec((tq, 4 * LANE), lambda b, i: (b * nq + i, C_DQ // 4)),
                  pl.BlockSpec((tq, D_BLOCK), lambda b, i: (b * nq + i, C_DK * LANE // D_BLOCK)),
                  pl.BlockSpec((tq, LANE), lambda b, i: (b * nq + i, C_TAIL)),
                  pl.BlockSpec((T, LANE), lambda b, i: (b, C_DK)),
                  pl.BlockSpec((T, LANE), lambda b, i: (b, C_DV)),
                  pl.BlockSpec((T, LANE), lambda b, i: (b, C_TAIL)),
                  pl.BlockSpec(bias.shape, lambda b, i: (0, 0, 0, 0))],
        out_specs=pl.BlockSpec((tq, 4 * LANE), lambda b, i: (b * nq + i, 0)),
        out_shape=jax.ShapeDtypeStruct((B * T, 4 * LANE), BF16),
        scratch_shapes=[pltpu.VMEM((n_tiles, tq, LANE), I32),
                        pltpu.VMEM((n_tiles, tq, LANE), jnp.int16),
                        pltpu.VMEM((n_tiles, tq, LANE), jnp.int16),
                        pltpu.VMEM((tq, LANE), I32),
                        pltpu.VMEM((2, n_tiles, 2 * LANE, LANE), BF16),
                        pltpu.VMEM((T, LANE), BF16),
                        pltpu.VMEM((T, LANE), BF16),
                        pltpu.VMEM((n_pair, n_tiles, tq, 2 * LANE), F32),
                        pltpu.VMEM((n_pair, tq, 2 * LANE), F32),
                        pltpu.VMEM((n_pair, tq, 2 * LANE), F32),
                        pltpu.VMEM((n_pair, 2 * tq, LANE), F32)],
        compiler_params=_cparams(("parallel", "arbitrary")),
        name="dsa_prompt")(P, P, P, P, P, P, bias)


def _attn_a_dec_body(pt_ref, scal_ref, q_ref, kn_ref, vn_ref, bl_ref, bn_ref, gout_ref, *rest,
                     pps, n_steps):
    k_refs = rest[:pps]
    v_refs = rest[pps:2 * pps]
    o_ref, m_ref, l_ref, acc_ref = rest[2 * pps:]
    s_i = pl.program_id(1)
    n_h = q_ref.shape[1] // LANE
    rows_h = 2 * T_PAD

    @pl.when(s_i == 0)
    def _():
        m_ref[...] = jnp.full(m_ref.shape, NEG_INF, F32)
        l_ref[...] = jnp.zeros(l_ref.shape, F32)
        acc_ref[...] = jnp.zeros(acc_ref.shape, F32)

    q = q_ref[...]
    lane = lax.broadcasted_iota(I32, (T_PAD, LANE), 1)

    def q_head(h):
        qh = q[:, h * LANE:(h + 1) * LANE]
        return jnp.concatenate([jnp.where(lane < HEAD_DIM, qh, 0.0),
                                jnp.where(lane >= HEAD_DIM, qh, 0.0)], axis=0).astype(BF16)

    qs = [q_head(h) for h in range(n_h)]
    q_all = jnp.concatenate(qs, axis=0)
    is_last = s_i == n_steps - 1

    n_rows = n_h * rows_h
    page_w = PAGE * n_h
    row_head = lax.broadcasted_iota(I32, (n_rows, page_w), 0) // rows_h
    col_head = lax.broadcasted_iota(I32, (n_rows, page_w), 1) % n_h
    head_mask = jnp.where(row_head == col_head, 0.0, NEG_INF)
    last_bias = jnp.where(is_last, bl_ref[...], 0.0) + head_mask
    parts = [_nt(q_all, r[...].astype(BF16)) + (last_bias if i == pps - 1 else head_mask)
             for i, r in enumerate(k_refs)]
    v_all = jnp.concatenate([r[...].astype(BF16) for r in v_refs], axis=0)
    m, l, acc = _softmax_update((m_ref[...], l_ref[...], acc_ref[...]),
                                jnp.concatenate(parts, axis=1), v_all)
    m_ref[...] = m
    l_ref[...] = l
    acc_ref[...] = acc

    @pl.when(is_last)
    def _():
        pad = jnp.zeros((LANE - T_PAD, LANE), BF16)
        m_all, l_all, acc_all = m_ref[...], l_ref[...], acc_ref[...]
        new = []
        for h in range(n_h):
            hs = slice(h * LANE, (h + 1) * LANE)
            rs = slice(h * rows_h, (h + 1) * rows_h)
            kn = jnp.concatenate([kn_ref[:, hs].astype(BF16), pad], axis=0)
            vn = jnp.concatenate([vn_ref[:, hs].astype(BF16), pad], axis=0)
            s = _nt(qs[h], kn) + bn_ref[rs]
            new.append(_softmax_update((m_all[rs], l_all[rs], acc_all[rs]), s, vn))
        m_ref[...] = jnp.concatenate([x[0] for x in new], axis=0)
        l_ref[...] = jnp.concatenate([x[1] for x in new], axis=0)
        acc_ref[...] = jnp.concatenate([x[2] for x in new], axis=0)
        lam = scal_ref[0]
        for h in range(n_h):
            r1 = slice(h * rows_h, h * rows_h + T_PAD)
            r2 = slice(h * rows_h + T_PAD, (h + 1) * rows_h)
            o = acc_ref[r1] / l_ref[r1] - lam * (acc_ref[r2] / l_ref[r2])
            o_ref[:, h * LANE:(h + 1) * LANE] = _rms_rows(o, gout_ref[...]) * scal_ref[1]


def _attn_a_decode(P, cache_k, cache_v, page_table, layer, scal, bias_last, bias_new, gout, *, B, pps):
    n_pages = page_table.shape[1]
    n_steps = n_pages // pps
    W = 4 * LANE

    def page_spec(i):
        return pl.BlockSpec((None, None) + cache_k.shape[2:],
                            lambda b, s, pt: (pt[b, s * pps + i], layer, 0, 0))

    grid_spec = pltpu.PrefetchScalarGridSpec(
        num_scalar_prefetch=1,
        grid=(B, n_steps),
        in_specs=[pl.BlockSpec(memory_space=pltpu.SMEM),
                  pl.BlockSpec((T_PAD, W), lambda b, s, pt: (b, C_AQ // 4)),
                  pl.BlockSpec((T_PAD, W), lambda b, s, pt: (b, C_AK // 4)),
                  pl.BlockSpec((T_PAD, W), lambda b, s, pt: (b, C_AV // 4)),
                  pl.BlockSpec(bias_last.shape, lambda b, s, pt: (0, 0)),
                  pl.BlockSpec(bias_new.shape, lambda b, s, pt: (0, 0)),
                  pl.BlockSpec((1, LANE), lambda b, s, pt: (0, 0))]
        + [page_spec(i) for i in range(pps)] + [page_spec(i) for i in range(pps)],
        out_specs=pl.BlockSpec((T_PAD, W), lambda b, s, pt: (b, 0)),
        scratch_shapes=[pltpu.VMEM((bias_last.shape[0], 1), F32),
                        pltpu.VMEM((bias_last.shape[0], 1), F32),
                        pltpu.VMEM((bias_last.shape[0], LANE), F32)])
    return pl.pallas_call(
        functools.partial(_attn_a_dec_body, pps=pps, n_steps=n_steps),
        grid_spec=grid_spec,
        out_shape=jax.ShapeDtypeStruct((B * T_PAD, W), F32),
        compiler_params=_cparams(("parallel", "arbitrary")),
        name="attn_a_decode")(page_table, scal, P, P, P, bias_last, bias_new, gout,
                              *([cache_k] * pps), *([cache_v] * pps))


def _dsa_sel_body(pt_ref, iq_ref, tq_ref, ikn_ref, *rest, pps, n_steps, n_top, t_real, n_bits):
    ik_refs = rest[:pps]
    keys_ref, keyn_ref, thr_ref, pos_ref, all_ref, wb_ref = rest[pps:]
    s_i = pl.program_id(1)
    hd = HEAD_DIM
    iq = iq_ref[:, D_BLOCK_IQ:D_BLOCK_IQ + IDX_HEADS * hd].astype(BF16)
    q_idx = jnp.concatenate([iq[:, h * hd:(h + 1) * hd] for h in range(IDX_HEADS)], axis=0)
    wscale = IDX_HEADS ** -0.5 * hd ** -0.5
    tail = tq_ref[...]
    for h in range(IDX_HEADS):
        wb_ref[h * T_PAD:(h + 1) * T_PAD, :] = jnp.broadcast_to(
            tail[:, IW_OFF + h:IW_OFF + h + 1] * wscale, (T_PAD, LANE))

    def score(raw):
        s = jnp.maximum(raw, 0.0) * wb_ref[...]
        sc = s[0:T_PAD]
        for h in range(1, IDX_HEADS):
            sc = sc + s[h * T_PAD:(h + 1) * T_PAD]
        return sc

    for i in range(pps):
        raw = jnp.dot(q_idx, ik_refs[i][...].astype(BF16), preferred_element_type=F32)
        key = _score_key(score(raw))
        keys_ref[0, i] = key
        all_ref[s_i * pps + i] = key

    n_tiles = n_steps * pps + 1

    @pl.when(s_i == n_steps - 1)
    def _():
        row = lax.broadcasted_iota(I32, (T_PAD, LANE), 0)
        col = lax.broadcasted_iota(I32, (T_PAD, LANE), 1)
        pad = jnp.zeros((LANE - T_PAD, hd), F32)
        ikn = jnp.concatenate([ikn_ref[...][:, :hd], pad], axis=0)
        valid = jnp.logical_and(col <= row, col < t_real)
        keyn = _score_key(jnp.where(valid, score(_nt(q_idx, ikn.astype(BF16))), NEG_INF))
        keyn_ref[0] = keyn
        all_ref[n_tiles - 1] = keyn
        idx = (lax.broadcasted_iota(I32, all_ref.shape, 0) * LANE
               + lax.broadcasted_iota(I32, all_ref.shape, 2))

        def count(pred):
            acc = jnp.sum(pred(all_ref[...]).astype(I32), axis=0)
            return jnp.sum(acc, axis=-1, keepdims=True)

        def count_ge(cand):
            cb = jnp.broadcast_to(cand, (T_PAD, LANE))
            return count(lambda key: key >= cb)

        thr = _kth_largest(count_ge, (T_PAD, 1), n_top)
        thr_b = jnp.broadcast_to(thr, (T_PAD, LANE))
        n_gt = count(lambda key: key > thr_b)
        r = n_top - n_gt

        def count_tie_below(cand):
            cb = jnp.broadcast_to(cand, (T_PAD, LANE))
            return count(lambda key: jnp.logical_and(key == thr_b, idx < cb))

        pos = _tie_bound(count_tie_below, r, (T_PAD, 1), n_bits)
        thr_ref[0] = thr_b
        pos_ref[0] = jnp.broadcast_to(pos, (T_PAD, LANE))


def _dsa_select_decode(P, cache_idx, page_table, layer, *, B, pps, n_top, t_real):
    n_pages = page_table.shape[1]
    n_steps = n_pages // pps
    n_tiles = n_pages + 1
    n_bits = int(math.ceil(math.log2(n_tiles * LANE)))

    def page_spec(i):
        return pl.BlockSpec((None, None) + cache_idx.shape[2:],
                            lambda b, s, pt: (pt[b, s * pps + i], layer, 0, 0))

    tile_spec = pl.BlockSpec((1, T_PAD, LANE), lambda b, s, pt: (b, 0, 0))
    grid_spec = pltpu.PrefetchScalarGridSpec(
        num_scalar_prefetch=1,
        grid=(B, n_steps),
        in_specs=[pl.BlockSpec((T_PAD, D_BLOCK), lambda b, s, pt: (b, C_DK * LANE // D_BLOCK)),
                  pl.BlockSpec((T_PAD, LANE), lambda b, s, pt: (b, C_TAIL)),
                  pl.BlockSpec((T_PAD, LANE), lambda b, s, pt: (b, C_TAIL))]
        + [page_spec(i) for i in range(pps)],
        out_specs=[pl.BlockSpec((1, pps, T_PAD, LANE), lambda b, s, pt: (b, s, 0, 0)),
                   tile_spec, tile_spec, tile_spec],
        scratch_shapes=[pltpu.VMEM((n_tiles, T_PAD, LANE), I32),
                        pltpu.VMEM((IDX_HEADS * T_PAD, LANE), F32)])
    tile_shape = jax.ShapeDtypeStruct((B, T_PAD, LANE), I32)
    return pl.pallas_call(
        functools.partial(_dsa_sel_body, pps=pps, n_steps=n_steps, n_top=n_top, t_real=t_real,
                          n_bits=n_bits),
        grid_spec=grid_spec,
        out_shape=[jax.ShapeDtypeStruct((B, n_pages, T_PAD, LANE), I32),
                   tile_shape, tile_shape, tile_shape],
        compiler_params=_cparams(("parallel", "arbitrary")),
        name="dsa_select_decode")(page_table, P, P, P, *([cache_idx] * pps))


def _dsa_att_body(pt_ref, dq_ref, kn_ref, vn_ref, keys_ref, keyn_ref, thr_ref, pos_ref,
                  bl_ref, bn_ref, *rest, pps, n_steps):
    k_refs = rest[:pps]
    v_refs = rest[pps:2 * pps]
    o_ref, m_ref, l_ref, acc_ref = rest[2 * pps:]
    s_i = pl.program_id(1)
    hd = HEAD_DIM
    n_kv = kn_ref.shape[1] // hd
    n_q = dq_ref.shape[1] // hd
    grp = n_q // n_kv
    rows_g = grp * T_PAD

    @pl.when(s_i == 0)
    def _():
        m_ref[...] = jnp.full(m_ref.shape, -1e30, F32)
        l_ref[...] = jnp.zeros(l_ref.shape, F32)
        acc_ref[...] = jnp.zeros(acc_ref.shape, F32)

    dq = dq_ref[...].astype(BF16)
    q_g = [jnp.concatenate([dq[:, (g * grp + j) * hd:(g * grp + j + 1) * hd] for j in range(grp)],
                           axis=0) for g in range(n_kv)]
    thr = thr_ref[0]
    pos = pos_ref[0]
    col = lax.broadcasted_iota(I32, (T_PAD, LANE), 1)
    is_last = s_i == n_steps - 1

    def sel_mask(key, tile):
        idx = col + tile * LANE
        sel = jnp.logical_or(key > thr, jnp.logical_and(key == thr, idx <= pos))
        sel = jnp.logical_and(sel, key > KEY_NEG_INF)
        return jnp.concatenate([sel] * grp, axis=0)

    def attend(tiles, masks, bias_tiles):
        m_all, l_all, acc_all = m_ref[...], l_ref[...], acc_ref[...]
        new = []
        for g in range(n_kv):
            rs = slice(g * rows_g, (g + 1) * rows_g)
            logit_tiles, pv = tiles(g)
            parts = []
            for s, mk, bt in zip(logit_tiles, masks, bias_tiles):
                if bt is not None:
                    s = s + bt[rs]
                parts.append(jnp.where(mk, s, NEG_INF))
            s = jnp.concatenate(parts, axis=1) if len(parts) > 1 else parts[0]
            m_new = jnp.maximum(m_all[rs], jnp.max(s, axis=-1, keepdims=True))
            alpha = jnp.exp(m_all[rs] - m_new)
            p = jnp.exp(s - m_new)
            new.append((m_new, alpha * l_all[rs] + jnp.sum(p, axis=-1, keepdims=True),
                        alpha * acc_all[rs] + pv(p.astype(BF16))))
        m_ref[...] = jnp.concatenate([x[0] for x in new], axis=0)
        l_ref[...] = jnp.concatenate([x[1] for x in new], axis=0)
        acc_ref[...] = jnp.concatenate([x[2] for x in new], axis=0)

    masks = [sel_mask(keys_ref[0, i], s_i * pps + i) for i in range(pps)]
    last_bias = jnp.where(is_last, bl_ref[...], 0.0)

    def page_tiles(g):
        vt = jnp.concatenate([r[g].astype(BF16) for r in v_refs], axis=1)
        return ([jnp.dot(q_g[g], r[g].astype(BF16), preferred_element_type=F32) for r in k_refs],
                lambda p: _nt(p, vt))

    attend(page_tiles, masks, [None] * (pps - 1) + [last_bias])

    @pl.when(is_last)
    def _():
        pad = jnp.zeros((LANE - T_PAD, hd), BF16)

        def new_tiles(g):
            hs = slice(g * hd, (g + 1) * hd)
            kn = jnp.concatenate([kn_ref[:, hs].astype(BF16), pad], axis=0)
            vn = jnp.concatenate([vn_ref[:, hs].astype(BF16), pad], axis=0)
            return [_nt(q_g[g], kn)], lambda p: jnp.dot(p, vn, preferred_element_type=F32)

        attend(new_tiles, [sel_mask(keyn_ref[0], n_steps * pps)], [bn_ref[...]])
        for g in range(n_kv):
            rs = slice(g * rows_g, (g + 1) * rows_g)
            og = acc_ref[rs] / l_ref[rs]
            for j in range(grp):
                h = g * grp + j
                o_ref[:, h * hd:(h + 1) * hd] = og[j * T_PAD:(j + 1) * T_PAD]


def _dsa_attend_decode(P, cache_k, cache_v, page_table, layer, keys, keyn, thr, pos,
                       bias_last, bias_new, *, B, pps):
    n_pages = page_table.shape[1]
    n_steps = n_pages // pps
    rows = bias_last.shape[0]

    def page_spec(i):
        return pl.BlockSpec((None, None) + cache_k.shape[2:],
                            lambda b, s, pt: (pt[b, s * pps + i], layer, 0, 0, 0))

    tile_spec = pl.BlockSpec((1, T_PAD, LANE), lambda b, s, pt: (b, 0, 0))
    grid_spec = pltpu.PrefetchScalarGridSpec(
        num_scalar_prefetch=1,
        grid=(B, n_steps),
        in_specs=[pl.BlockSpec((T_PAD, 4 * LANE), lambda b, s, pt: (b, C_DQ // 4)),
                  pl.BlockSpec((T_PAD, LANE), lambda b, s, pt: (b, C_DK)),
                  pl.BlockSpec((T_PAD, LANE), lambda b, s, pt: (b, C_DV)),
                  pl.BlockSpec((1, pps, T_PAD, LANE), lambda b, s, pt: (b, s, 0, 0)),
                  tile_spec, tile_spec, tile_spec,
                  pl.BlockSpec(bias_last.shape, lambda b, s, pt: (0, 0)),
                  pl.BlockSpec(bias_new.shape, lambda b, s, pt: (0, 0))]
        + [page_spec(i) for i in range(pps)] + [page_spec(i) for i in range(pps)],
        out_specs=pl.BlockSpec((T_PAD, 4 * LANE), lambda b, s, pt: (b, 0)),
        scratch_shapes=[pltpu.VMEM((rows, 1), F32), pltpu.VMEM((rows, 1), F32),
                        pltpu.VMEM((rows, HEAD_DIM), F32)])
    return pl.pallas_call(
        functools.partial(_dsa_att_body, pps=pps, n_steps=n_steps),
        grid_spec=grid_spec,
        out_shape=jax.ShapeDtypeStruct((B * T_PAD, 4 * LANE), F32),
        compiler_params=_cparams(("parallel", "arbitrary")),
        name="dsa_attend_decode")(page_table, P, P, P, keys, keyn, thr, pos, bias_last, bias_new,
                                  *([cache_k] * pps), *([cache_v] * pps))


def _bias_minus_far(tab, dist):
    onehot = jax.nn.one_hot(_rel_bucket(dist), N_BUCKETS, dtype=F32)
    bias = jnp.einsum("...k,km->...m", onehot, tab, precision=lax.Precision.HIGHEST)
    return bias - tab[N_BUCKETS - 1]


def _toeplitz_tiles(tab, t):
    r = jnp.arange(t)[:, None]
    c = jnp.arange(t)[None, :]
    tiles = jnp.stack([_bias_minus_far(tab, r - c), _bias_minus_far(tab, t + r - c)], axis=0)
    return jnp.transpose(tiles, (3, 0, 1, 2))


def _decode_bias(tab, past, t_real):
    tq = jnp.arange(T_PAD)[:, None]
    kc = jnp.arange(LANE)[None, :]
    last = _bias_minus_far(tab, past + tq - (past - PAGE + kc))
    new = _bias_minus_far(tab, tq - kc)
    valid = jnp.logical_and(kc <= tq, kc < t_real)
    new = jnp.where(valid[..., None], new, NEG_INF)
    return jnp.transpose(last, (2, 0, 1)), jnp.transpose(new, (2, 0, 1))


def _block_diag(w):
    n, d, e = w.shape
    eye = jnp.eye(n, dtype=w.dtype)
    return (eye[:, None, :, None] * w[:, :, None, :]).reshape(n * d, n * e)


def kernel(x_prompt, x_sample, cache_a_k, cache_a_v, cache_d_k, cache_d_v, cache_d_idx, state_b_h, state_b_conv, state_c_s, page_table, rel_bias, ln1, w_in, a_q_norm, a_k_norm, a_lam_q1, a_lam_k1, a_lam_q2, a_lam_k2, a_out_norm, b_conv_w, b_conv_b, b_wa, b_ba, b_wx, b_bx, b_lambda, c_lb_logits, c_out_norm, d_q_norm, d_k_norm, w_out, ln2, w_up, w_down):
    B, T, D = x_prompt.shape
    Bs, Ts, _ = x_sample.shape
    L = w_in.shape[0]
    n_pages = page_table.shape[1]
    past = n_pages * PAGE
    gw = D // 4
    a_heads = gw // (2 * HEAD_DIM)
    c_heads = c_lb_logits.shape[1] // LANE
    d_heads = gw // HEAD_DIM
    d_kv = cache_d_k.shape[3]
    d_grp = d_heads // d_kv
    n_pool = cache_a_k.shape[0]
    assert Ts <= T_PAD - 0 and Ts >= CONV_W - 1 and past > 0

    tab = rel_bias.astype(F32)
    n_a_maps = 2 * a_heads
    tab_a, tab_d = tab[:, :n_a_maps], tab[:, n_a_maps:]
    tq_a = min(128, T)
    tq_d = min(128, T)
    bias_a = jnp.transpose(_toeplitz_tiles(tab_a, tq_a).reshape(a_heads, 2, 2, tq_a, tq_a),
                           (0, 2, 3, 1, 4)).reshape(a_heads, 2, tq_a, 2 * tq_a)
    causal = jnp.where(jnp.arange(tq_a)[None, :] <= jnp.arange(tq_a)[:, None], 0.0, NEG_INF)
    causal = jnp.tile(causal, (1, 2)).astype(F32)
    bias_a = jnp.stack([jnp.full((a_heads, tq_a, 2 * tq_a), NEG_INF, F32), bias_a[:, 0] + causal,
                        bias_a[:, 1], jnp.zeros((a_heads, tq_a, 2 * tq_a), F32)], axis=1)
    bias_d = jnp.concatenate([_toeplitz_tiles(tab_d, tq_d),
                              jnp.zeros((d_heads, 1, tq_d, tq_d), F32)], axis=1)
    bias_d = jnp.transpose(bias_d.reshape(d_heads // 2, 2, 3, tq_d, tq_d),
                           (0, 2, 3, 1, 4)).reshape(d_heads // 2, 3, tq_d, 2 * tq_d)
    bl_a, bn_a = _decode_bias(tab_a, past, Ts)
    bl_a = jnp.repeat(bl_a.reshape(n_a_maps * T_PAD, LANE), a_heads, axis=1)
    bn_a = bn_a.reshape(n_a_maps * T_PAD, LANE)
    bl_d, bn_d = _decode_bias(tab_d, past, Ts)
    bl_d = bl_d.reshape(d_heads * T_PAD, LANE)
    bn_d = bn_d.reshape(d_heads * T_PAD, LANE)

    lb_cum = jnp.cumsum(jax.nn.softmax(c_lb_logits.astype(F32), axis=0), axis=0)
    lb_all = (lb_cum - lb_cum[0]).reshape(L, c_heads, 1, LANE)

    seg = jnp.kron(jnp.eye(LANE // HEAD_DIM, dtype=F32),
                   jnp.full((HEAD_DIM, HEAD_DIM), 1.0 / HEAD_DIM, F32)).astype(BF16)
    qscale = HEAD_DIM ** -0.5
    zeros = lambda n: jnp.zeros((n,), F32)
    ones = lambda n: jnp.ones((n,), F32)
    flag = jnp.concatenate([ones(2 * gw), zeros(7 * gw), ones(gw), ones(LANE),
                            zeros(P_WIDTH - 10 * gw - LANE)]).reshape(1, P_WIDTH)
    w_in_t = jnp.transpose(w_in, (0, 2, 1))
    n_in = w_in_t.shape[1]
    tail0 = (P_WIDTH // IN_PROJ_TN - 1) * IN_PROJ_TN
    assert tail0 < n_in <= P_WIDTH
    w_in_tail = jnp.pad(w_in_t[:, tail0:], ((0, 0), (0, tail0 + IN_PROJ_TN - n_in), (0, 0)))

    ca_k = cache_a_k.reshape(n_pool, L, PAGE * a_heads, 2 * HEAD_DIM)
    ca_v = cache_a_v.reshape(n_pool, L, PAGE * a_heads, 2 * HEAD_DIM)
    cd_k = jnp.transpose(cache_d_k, (0, 1, 3, 4, 2))
    cd_v = jnp.transpose(cache_d_v, (0, 1, 3, 4, 2))
    cd_idx = jnp.transpose(cache_d_idx, (0, 1, 3, 2))

    xp = x_prompt.reshape(B * T, D)
    xs = jnp.pad(x_sample, ((0, 0), (0, T_PAD - Ts), (0, 0))).reshape(Bs * T_PAD, D)
    s_conv = jnp.pad(state_b_conv, ((0, 0), (0, 0), (SUBLANE - (CONV_W - 1), 0), (0, 0)))

    n_top_p = min(TOPK_MAX, T // 4)
    n_top_s = min(TOPK_MAX, (past + Ts) // 4)
    tm_p = min(1024, B * T)
    tm_f = min(1024, B * T)
    tt_b = min(256, T)
    tt_c = min(512, T)
    ch_c = math.gcd(T, 64)
    pps_a = math.gcd(n_pages, 16)
    pps_d = math.gcd(n_pages, 16)
    pps_att = math.gcd(n_pages, 32)

    outs_p, outs_s = [], []
    for l in range(L):
        gain = jnp.concatenate(
            [jnp.tile(a_q_norm[l], 2 * a_heads) * qscale, jnp.tile(a_k_norm[l], 2 * a_heads),
             ones(7 * gw), jnp.tile(d_q_norm[l], d_heads) * qscale,
             jnp.tile(d_k_norm[l], d_kv), ones(P_WIDTH - 10 * gw - LANE)]).reshape(1, P_WIDTH)
        g1 = ln1[l].reshape(1, D)
        g2 = ln2[l].reshape(1, D)
        lam_init = 0.8 - 0.6 * math.exp(-0.3 * l)
        lam = (jnp.exp(jnp.sum(a_lam_q1[l].astype(F32) * a_lam_k1[l].astype(F32)))
               - jnp.exp(jnp.sum(a_lam_q2[l].astype(F32) * a_lam_k2[l].astype(F32))) + lam_init)
        scal = jnp.stack([lam, jnp.asarray(1.0 - lam_init, F32)]).astype(F32)
        g_a = a_out_norm[l].reshape(1, LANE)
        g_c = c_out_norm[l].reshape(1, LANE)
        b_weights = (b_conv_w[l], b_conv_b[l].reshape(1, gw), _block_diag(b_wa[l]).astype(BF16),
                     b_ba[l].reshape(1, gw), _block_diag(b_wx[l]).astype(BF16),
                     b_bx[l].reshape(1, gw), b_lambda[l].reshape(1, gw))

        def dense_tail(x, mixes, tm, tmf):
            x1 = _out_proj(mixes, w_out, l, x, tm=tm, tn=min(512, D))
            return _ffn(x1, g2, w_up, w_down, l, tm=tmf, tf=min(512, w_up.shape[2]))

        Pp, ak_p, av_p = _in_proj(xp, g1, w_in_t, w_in_tail, l, gain, flag, seg, tm=tm_p)
        mix_a = _attn_a_prompt(Pp, scal, bias_a, g_a, B=B, T=T, tq=tq_a)
        mix_b, hfin, buf = _rglru(Pp, None, b_weights, B=B, T=T, tt=tt_b, t_real=T, out_dtype=BF16)
        mix_c, sfin = _hgrn(Pp, None, lb_all[l], g_c, B=B, T=T, tt=tt_c, ch=ch_c,
                            sub=min(16, ch_c), t_real=T, out_dtype=BF16)
        mix_d = _dsa_prompt(Pp, bias_d, B=B, T=T, tq=tq_d, n_top=n_top_p)
        xp = dense_tail(xp, (mix_a, mix_b, mix_c, mix_d), tm_p, tm_f)
        P3 = Pp.reshape(B, T, P_WIDTH)
        outs_p.append((ak_p.reshape(B, T, a_heads, 2 * HEAD_DIM),
                       av_p.reshape(B, T, a_heads, 2 * HEAD_DIM),
                       P3[..., C_DK * LANE:C_DV * LANE].reshape(B, T, d_kv, HEAD_DIM),
                       P3[..., C_DV * LANE:(C_DV + 1) * LANE].reshape(B, T, d_kv, HEAD_DIM),
                       P3[..., C_TAIL * LANE:C_TAIL * LANE + HEAD_DIM],
                       hfin.reshape(B, gw), buf, sfin))

        Ps, ak_s, av_s = _in_proj(xs, g1, w_in_t, w_in_tail, l, gain, flag, seg, tm=Bs * T_PAD)
        smix_a = _attn_a_decode(Ps, ca_k, ca_v, page_table, l, scal, bl_a, bn_a, g_a, B=Bs, pps=pps_a)
        smix_b, shfin, sbuf = _rglru(
            Ps, (state_b_h[:, l].reshape(Bs, 1, gw), s_conv[:, l]), b_weights,
            B=Bs, T=T_PAD, tt=T_PAD, t_real=Ts, out_dtype=F32)
        smix_c, ssfin = _hgrn(Ps, state_c_s[:, l], lb_all[l], g_c, B=Bs, T=T_PAD, tt=T_PAD,
                              ch=T_PAD, sub=T_PAD, t_real=Ts, out_dtype=F32)
        keys, keyn, thr, pos = _dsa_select_decode(Ps, cd_idx, page_table, l, B=Bs, pps=pps_d,
                                                  n_top=n_top_s, t_real=Ts)
        smix_d = _dsa_attend_decode(Ps, cd_k, cd_v, page_table, l, keys, keyn, thr, pos,
                                    bl_d, bn_d, B=Bs, pps=pps_att)
        xs = dense_tail(xs, (smix_a, smix_b, smix_c, smix_d), Bs * T_PAD, Bs * T_PAD)
        S3 = Ps.reshape(Bs, T_PAD, P_WIDTH)[:, :Ts]
        outs_s.append((ak_s.reshape(Bs, T_PAD, a_heads, 2 * HEAD_DIM)[:, :Ts],
                       av_s.reshape(Bs, T_PAD, a_heads, 2 * HEAD_DIM)[:, :Ts],
                       S3[..., C_DK * LANE:C_DV * LANE].reshape(Bs, Ts, d_kv, HEAD_DIM),
                       S3[..., C_DV * LANE:(C_DV + 1) * LANE].reshape(Bs, Ts, d_kv, HEAD_DIM),
                       S3[..., C_TAIL * LANE:C_TAIL * LANE + HEAD_DIM],
                       shfin.reshape(Bs, gw), sbuf, ssfin))

    y_prompt = xp.reshape(B, T, D)
    y_sample = xs.reshape(Bs, T_PAD, D)[:, :Ts]
    stack = lambda outs: [jnp.stack(s, axis=1) for s in zip(*outs)]
    return (y_prompt, y_sample, *stack(outs_p), *stack(outs_s))
```

```python
import functools
import math

import jax
import jax.numpy as jnp
import numpy as np
from jax import lax
from jax.experimental import pallas as pl
from jax.experimental.pallas import tpu as pltpu

F32 = jnp.float32
BF16 = jnp.bfloat16
I32 = jnp.int32

EPS = 1e-6
HEAD_DIM = 64
PAGE = 128
CONV_W = 4
LRU_C = 8.0
IDX_HEADS = 8
TOPK_MAX = 256
N_BUCKETS = 32
MAX_DIST = 128
LANE = 128
SUBLANE = 8
T_PAD = 8
INT_MIN = -(2 ** 31)
KEY_NEG_INF = -2139095041
VMEM_LIMIT = 56 * 1024 * 1024
NEG_INF = float("-inf")

C_AQ, C_AK, C_AV, C_BX, C_BG, C_CQ, C_CF, C_CI, C_CG, C_DQ = (4 * i for i in range(10))
C_DK, C_DV, C_IQ, C_TAIL = 40, 41, 42, 46
P_WIDTH = 48 * LANE
IW_OFF = 64
D_BLOCK = 8 * LANE
D_BLOCK_IQ = (C_IQ - C_DK) * LANE
D_BLOCK_TAIL = (C_TAIL - C_DK) * LANE


def _cparams(sem):
    return pltpu.CompilerParams(dimension_semantics=sem, vmem_limit_bytes=VMEM_LIMIT)


def _nt(a, b):
    return lax.dot_general(a, b, (((1,), (1,)), ((), ())), preferred_element_type=F32)


def _tn(a, b):
    return lax.dot_general(a, b, (((0,), (0,)), ((), ())), preferred_element_type=F32)


def _rel_bucket(dist):
    n = jnp.maximum(dist, 0)
    exact = N_BUCKETS // 2
    large = exact + (jnp.log(jnp.maximum(n, 1).astype(F32) / exact)
                     / math.log(MAX_DIST / exact) * (N_BUCKETS - exact)).astype(I32)
    return jnp.where(n < exact, n, jnp.minimum(large, N_BUCKETS - 1))


def _in_proj_body(x_ref, g_ref, w_ref, wtail_ref, gain_ref, flag_ref, seg_ref, o_ref, ak_ref, av_ref,
                  xn_ref, *, tn, norm_tiles, head_tiles):
    j = pl.program_id(1)
    n_sub = tn // LANE
    tm = o_ref.shape[0]

    def head_rows(dst_ref):
        for c in range(n_sub):
            dst_ref[pl.ds(c, tm, stride=n_sub), :] = o_ref[:, c * LANE:(c + 1) * LANE]

    @pl.when(j == 0)
    def _():
        x = x_ref[...]
        ms = jnp.mean(x * x, axis=-1, keepdims=True)
        xn_ref[...] = (x * lax.rsqrt(ms + EPS) * g_ref[...]).astype(BF16)

    w = jnp.where(j == pl.num_programs(1) - 1, wtail_ref[...], w_ref[...])
    y = _nt(xn_ref[...], w.astype(BF16))
    is_norm = functools.reduce(jnp.logical_or, [j == t for t in norm_tiles])

    @pl.when(is_norm)
    def _():
        seg = seg_ref[...]
        for c in range(tn // LANE):
            sl = slice(c * LANE, (c + 1) * LANE)
            yc = y[:, sl]
            y2 = yc * yc
            hi = y2.astype(BF16)
            lo = (y2 - hi.astype(F32)).astype(BF16)
            ms = (jnp.dot(hi, seg, preferred_element_type=F32)
                  + jnp.dot(lo, seg, preferred_element_type=F32))
            yn = yc * lax.rsqrt(ms + EPS) * gain_ref[:, sl]
            o_ref[:, sl] = jnp.where(flag_ref[:, sl] > 0, yn, yc)

    @pl.when(jnp.logical_not(is_norm))
    def _():
        o_ref[...] = y

    @pl.when(j == head_tiles[0])
    def _():
        head_rows(ak_ref)

    @pl.when(j == head_tiles[1])
    def _():
        head_rows(av_ref)


IN_PROJ_TN = 4 * LANE


def _in_proj(x, g, wt, wtail, layer, gain, flag, seg, *, tm):
    M, K = x.shape
    tn = IN_PROJ_TN
    n_tiles = P_WIDTH // tn
    norm_ranges = ((C_AQ, 4), (C_AK, 4), (C_DQ, 4), (C_DK, 1))
    norm_tiles = tuple(sorted({t for c, n in norm_ranges
                               for t in range(c * LANE // tn, ((c + n) * LANE - 1) // tn + 1)}))

    n_sub = tn // LANE
    assert C_AK % n_sub == 0 and C_AV % n_sub == 0
    head_spec = pl.BlockSpec((tm * n_sub, LANE), lambda i, j: (i, 0))
    head_shape = jax.ShapeDtypeStruct((M * n_sub, LANE), F32)
    return pl.pallas_call(
        functools.partial(_in_proj_body, tn=tn, norm_tiles=norm_tiles,
                          head_tiles=(C_AK // n_sub, C_AV // n_sub)),
        grid=(M // tm, n_tiles),
        in_specs=[pl.BlockSpec((tm, K), lambda i, j: (i, 0)),
                  pl.BlockSpec((1, K), lambda i, j: (0, 0)),
                  pl.BlockSpec((None, tn, K), lambda i, j: (layer, jnp.minimum(j, n_tiles - 2), 0)),
                  pl.BlockSpec((None, tn, K), lambda i, j: (layer, 0, 0)),
                  pl.BlockSpec((1, tn), lambda i, j: (0, j)),
                  pl.BlockSpec((1, tn), lambda i, j: (0, j)),
                  pl.BlockSpec((LANE, LANE), lambda i, j: (0, 0))],
        out_specs=[pl.BlockSpec((tm, tn), lambda i, j: (i, j)), head_spec, head_spec],
        out_shape=[jax.ShapeDtypeStruct((M, P_WIDTH), F32), head_shape, head_shape],
        scratch_shapes=[pltpu.VMEM((tm, K), BF16)],
        compiler_params=_cparams(("parallel", "arbitrary")),
        name="in_proj")(x, g, wt, wtail, gain, flag, seg)


def _out_proj_body(a_ref, b_ref, c_ref, d_ref, w_ref, x_ref, o_ref, wb_ref, *, gw):
    @pl.when(pl.program_id(1) == 0)
    def _():
        wb_ref[...] = w_ref[...].astype(BF16)

    acc = x_ref[...]
    for g, m_ref in enumerate((a_ref, b_ref, c_ref, d_ref)):
        acc = acc + jnp.dot(m_ref[...].astype(BF16), wb_ref[g * gw:(g + 1) * gw, :],
                            preferred_element_type=F32)
    o_ref[...] = acc


def _out_proj(mixes, w, layer, x, *, tm, tn):
    M, D = x.shape
    gw = mixes[0].shape[1]
    mix_spec = pl.BlockSpec((tm, gw), lambda j, i: (i, 0))
    return pl.pallas_call(
        functools.partial(_out_proj_body, gw=gw),
        grid=(D // tn, M // tm),
        in_specs=[mix_spec, mix_spec, mix_spec, mix_spec,
                  pl.BlockSpec((None, w.shape[1], tn), lambda j, i: (layer, 0, j)),
                  pl.BlockSpec((tm, tn), lambda j, i: (i, j))],
        out_specs=pl.BlockSpec((tm, tn), lambda j, i: (i, j)),
        out_shape=jax.ShapeDtypeStruct((M, D), F32),
        scratch_shapes=[pltpu.VMEM((w.shape[1], tn), BF16)],
        compiler_params=_cparams(("parallel", "arbitrary")),
        name="out_proj")(*mixes, w, x)


def _ffn_body(x_ref, g_ref, wu_ref, wd_ref, o_ref, xn_ref):
    f = pl.program_id(1)

    @pl.when(f == 0)
    def _():
        x = x_ref[...]
        ms = jnp.mean(x * x, axis=-1, keepdims=True)
        xn_ref[...] = (x * lax.rsqrt(ms + EPS) * g_ref[...]).astype(BF16)
        o_ref[...] = x

    h = jnp.dot(xn_ref[...], wu_ref[...].astype(BF16), preferred_element_type=F32)
    h = jnp.maximum(h, 0.0)
    h = (h * h).astype(BF16)
    o_ref[...] += jnp.dot(h, wd_ref[...].astype(BF16), preferred_element_type=F32)


def _ffn(x, g, wu, wd, layer, *, tm, tf):
    M, D = x.shape
    Fd = wu.shape[2]
    return pl.pallas_call(
        _ffn_body,
        grid=(M // tm, Fd // tf),
        in_specs=[pl.BlockSpec((tm, D), lambda i, f: (i, 0), pipeline_mode=pl.Buffered(1)),
                  pl.BlockSpec((1, D), lambda i, f: (0, 0)),
                  pl.BlockSpec((None, D, tf), lambda i, f: (layer, 0, f)),
                  pl.BlockSpec((None, tf, D), lambda i, f: (layer, f, 0))],
        out_specs=pl.BlockSpec((tm, D), lambda i, f: (i, 0)),
        out_shape=jax.ShapeDtypeStruct((M, D), F32),
        scratch_shapes=[pltpu.VMEM((tm, D), BF16)],
        compiler_params=_cparams(("parallel", "arbitrary")),
        name="ffn")(x, g, wu, wd)


def _softmax_update(state, s, v):
    m, l, acc = state
    m_new = jnp.maximum(m, jnp.max(s, axis=-1, keepdims=True))
    alpha = jnp.exp(m - m_new)
    p = jnp.exp(s - m_new)
    l = alpha * l + jnp.sum(p, axis=-1, keepdims=True)
    acc = alpha * acc + jnp.dot(p.astype(BF16), v, preferred_element_type=F32)
    return m_new, l, acc


def _rms_rows(o, gain):
    ms = jnp.mean(o * o, axis=-1, keepdims=True)
    return o * lax.rsqrt(ms + EPS) * gain


def _pair_attention(n_grp, tile_grp, n_pair, tq, logits, values, s_ref, m_ref, l_ref, acc_ref,
                    tile_aux=None):
    def max_pass(i, carry):
        cs = [i * tile_grp + j for j in range(tile_grp)]
        aux = [tile_aux(c) for c in cs] if tile_aux is not None else None
        for p in range(n_pair):
            t = []
            for j, c in enumerate(cs):
                s = logits(p, c) if aux is None else logits(p, c, aux[j])
                s_ref[p, c] = s
                t.append(s)
            while len(t) > 1:
                t = [jnp.maximum(a, b) for a, b in zip(t[::2], t[1::2])]
            m_ref[p] = jnp.maximum(m_ref[p], t[0])
        return carry

    m_ref[...] = jnp.full(m_ref.shape, NEG_INF, F32)
    lax.fori_loop(0, n_grp, max_pass, 0)
    for p in range(n_pair):
        m = m_ref[p]
        m_ref[p] = jnp.concatenate(
            [jnp.broadcast_to(jnp.max(m[:, :LANE], axis=-1, keepdims=True), (tq, LANE)),
             jnp.broadcast_to(jnp.max(m[:, LANE:], axis=-1, keepdims=True), (tq, LANE))], axis=1)
    l_ref[...] = jnp.zeros(l_ref.shape, F32)
    acc_ref[...] = jnp.zeros(acc_ref.shape, F32)

    def sum_pass(i, carry):
        for p in range(n_pair):
            e = [jnp.exp(s_ref[p, i * tile_grp + j] - m_ref[p]) for j in range(tile_grp)]
            l_ref[p] += functools.reduce(lambda a, b: a + b, e)
            pb = jnp.concatenate(
                [jnp.concatenate([x[:, :LANE], x[:, LANE:]], axis=0).astype(BF16) for x in e], axis=1)
            acc_ref[p] += jnp.dot(pb, values(p, i), preferred_element_type=F32)
        return carry

    lax.fori_loop(0, n_grp, sum_pass, 0)
    outs = []
    for p in range(n_pair):
        l = l_ref[p]
        acc = acc_ref[p]
        outs.append((acc[:tq] / jnp.sum(l[:, :LANE], axis=-1, keepdims=True),
                     acc[tq:] / jnp.sum(l[:, LANE:], axis=-1, keepdims=True)))
    return outs


def _attn_a_body(scal_ref, q_ref, k_ref, v_ref, bias_ref, gout_ref, o_ref,
                 kd_ref, vb_ref, s_ref, m_ref, l_ref, acc_ref, *, tq, tile_grp):
    qb = pl.program_id(1)
    n_h = q_ref.shape[1] // LANE
    n_grp = (qb + tile_grp) // tile_grp
    low_half = lax.broadcasted_iota(I32, (tq, LANE), 1) < HEAD_DIM

    @pl.when(qb == 0)
    def _():
        kd_ref[...] = jnp.zeros(kd_ref.shape, BF16)
        vb_ref[...] = jnp.zeros(vb_ref.shape, BF16)

    r_q = pl.multiple_of(qb * LANE, LANE)
    for h in range(n_h):
        kx = k_ref[pl.ds(r_q, LANE), h * LANE:(h + 1) * LANE]
        kd_ref[h, qb] = jnp.concatenate([jnp.where(low_half, kx, 0.0),
                                         jnp.where(low_half, 0.0, kx)], axis=0).astype(BF16)
    vb_ref[pl.ds(r_q, LANE), :] = v_ref[pl.ds(r_q, LANE), :].astype(BF16)
    q = q_ref[...].astype(BF16)

    def logits(h, c):
        return (_nt(q[:, h * LANE:(h + 1) * LANE], kd_ref[h, c])
                + bias_ref[h, jnp.clip(qb - c, -1, 2) + 1])

    def values(h, i):
        r0 = pl.multiple_of(i * tile_grp * LANE, tile_grp * LANE)
        return vb_ref[pl.ds(r0, tile_grp * LANE), h * LANE:(h + 1) * LANE]

    outs = _pair_attention(n_grp, tile_grp, n_h, tq, logits, values, s_ref, m_ref, l_ref, acc_ref)
    lam = scal_ref[0]
    for h, (o1, o2) in enumerate(outs):
        o = _rms_rows(o1 - lam * o2, gout_ref[...]) * scal_ref[1]
        o_ref[:, h * LANE:(h + 1) * LANE] = o.astype(o_ref.dtype)


def _attn_a_prompt(P, scal, bias, gout, *, B, T, tq):
    assert tq == LANE
    H = bias.shape[0]
    W = H * LANE
    nq = T // tq
    tile_grp = math.gcd(nq, 4)
    return pl.pallas_call(
        functools.partial(_attn_a_body, tq=tq, tile_grp=tile_grp),
        grid=(B, nq),
        in_specs=[pl.BlockSpec(memory_space=pltpu.SMEM),
                  pl.BlockSpec((tq, W), lambda b, i: (b * nq + i, C_AQ // H)),
                  pl.BlockSpec((T, W), lambda b, i: (b, C_AK // H)),
                  pl.BlockSpec((T, W), lambda b, i: (b, C_AV // H)),
                  pl.BlockSpec(bias.shape, lambda b, i: (0, 0, 0, 0)),
                  pl.BlockSpec((1, LANE), lambda b, i: (0, 0))],
        out_specs=pl.BlockSpec((tq, W), lambda b, i: (b * nq + i, 0)),
        out_shape=jax.ShapeDtypeStruct((B * T, W), BF16),
        scratch_shapes=[pltpu.VMEM((H, nq, 2 * LANE, LANE), BF16),
                        pltpu.VMEM((T, W), BF16),
                        pltpu.VMEM((H, nq, tq, 2 * LANE), F32),
                        pltpu.VMEM((H, tq, 2 * LANE), F32),
                        pltpu.VMEM((H, tq, 2 * LANE), F32),
                        pltpu.VMEM((H, 2 * tq, LANE), F32)],
        compiler_params=_cparams(("parallel", "arbitrary")),
        name="attn_a_prompt")(scal, P, P, P, bias, gout)


def _rglru_body(*refs, tt, t_real, n_t, has_state, pos0_is_zero):
    if has_state:
        (x_ref, g_ref, h0_ref, buf0_ref, cw_ref, cb_ref, wa_ref, ba_ref, wx_ref, bxb_ref, lam_ref,
         o_ref, hfin_ref, buf_ref, xpad_ref, a_ref, b_ref, hs_ref, hc_ref) = refs
    else:
        (x_ref, g_ref, cw_ref, cb_ref, wa_ref, ba_ref, wx_ref, bxb_ref, lam_ref,
         o_ref, hfin_ref, buf_ref, xpad_ref, a_ref, b_ref, hs_ref, hc_ref) = refs
    ti = pl.program_id(1)
    W = x_ref.shape[1]

    @pl.when(ti == 0)
    def _():
        if has_state:
            xpad_ref[0:SUBLANE, :] = buf0_ref[0]
            hc_ref[...] = jnp.broadcast_to(h0_ref[0], (SUBLANE, W))
        else:
            xpad_ref[0:SUBLANE, :] = jnp.zeros((SUBLANE, W), F32)
            hc_ref[...] = jnp.zeros((SUBLANE, W), F32)

    x = x_ref[...]
    xpad_ref[SUBLANE:SUBLANE + tt, :] = x
    xc = cb_ref[...] + x * cw_ref[CONV_W - 1:CONV_W, :]
    for j in range(CONV_W - 1):
        off = SUBLANE - (CONV_W - 1) + j
        xc = xc + xpad_ref[off:off + tt, :] * cw_ref[j:j + 1, :]
    xcb = xc.astype(BF16)
    r = jax.nn.sigmoid(jnp.dot(xcb, wa_ref[...], preferred_element_type=F32) + ba_ref[...])
    i = jax.nn.sigmoid(jnp.dot(xcb, wx_ref[...], preferred_element_type=F32) + bxb_ref[...])
    nl = -lam_ref[...]
    softplus = jnp.maximum(nl, 0.0) + jnp.log1p(jnp.exp(-jnp.abs(nl)))
    a = jnp.exp(-LRU_C * r * softplus)
    mult = jnp.sqrt(1.0 - a * a)
    if pos0_is_zero:
        rows = lax.broadcasted_iota(I32, (tt, W), 0)
        mult = jnp.where(jnp.logical_and(rows == 0, ti == 0), 1.0, mult)
    a_ref[...] = a
    b_ref[...] = mult * (i * xc)
    rowt = lax.broadcasted_iota(I32, (SUBLANE, W), 0)

    def tile(n, h_prev):
        r0 = pl.multiple_of(n * SUBLANE, SUBLANE)
        at = a_ref[pl.ds(r0, SUBLANE), :]
        bt = b_ref[pl.ds(r0, SUBLANE), :]
        for s in (1, 2, 4):
            keep = rowt >= s
            bt = jnp.where(keep, bt + at * pltpu.roll(bt, s, 0), bt)
            at = jnp.where(keep, at * pltpu.roll(at, s, 0), at)
        ht = bt + at * h_prev
        hs_ref[pl.ds(r0, SUBLANE), :] = ht
        return jnp.broadcast_to(ht[SUBLANE - 1:SUBLANE, :], (SUBLANE, W))

    hc_ref[...] = lax.fori_loop(0, tt // SUBLANE, tile, hc_ref[...])
    o_ref[...] = (hs_ref[...] * jax.nn.gelu(g_ref[...])).astype(o_ref.dtype)
    xpad_ref[0:SUBLANE, :] = xpad_ref[tt:tt + SUBLANE, :]

    @pl.when(ti == n_t - 1)
    def _():
        t_loc = t_real - (n_t - 1) * tt
        hfin_ref[0] = hs_ref[t_loc - 1:t_loc, :]
        buf_ref[0] = x_ref[t_loc - (CONV_W - 1):t_loc, :]


def _rglru(P, state, weights, *, B, T, tt, t_real, out_dtype):
    W = 4 * LANE
    n_t = T // tt
    has_state = state is not None
    xspec = pl.BlockSpec((tt, W), lambda b, i: (b * n_t + i, C_BX // 4))
    gspec = pl.BlockSpec((tt, W), lambda b, i: (b * n_t + i, C_BG // 4))
    full = lambda shape: pl.BlockSpec(shape, lambda b, i: (0,) * len(shape))
    wspecs = [full((CONV_W, W)), full((1, W)), full((W, W)), full((1, W)), full((W, W)),
              full((1, W)), full((1, W))]
    in_specs = [xspec, gspec]
    args = [P, P]
    if has_state:
        in_specs += [pl.BlockSpec((1, 1, W), lambda b, i: (b, 0, 0)),
                     pl.BlockSpec((1, SUBLANE, W), lambda b, i: (b, 0, 0))]
        args += list(state)
    return pl.pallas_call(
        functools.partial(_rglru_body, tt=tt, t_real=t_real, n_t=n_t, has_state=has_state,
                          pos0_is_zero=not has_state),
        grid=(B, n_t),
        in_specs=in_specs + wspecs,
        out_specs=[pl.BlockSpec((tt, W), lambda b, i: (b * n_t + i, 0)),
                   pl.BlockSpec((1, 1, W), lambda b, i: (b, 0, 0)),
                   pl.BlockSpec((1, CONV_W - 1, W), lambda b, i: (b, 0, 0))],
        out_shape=[jax.ShapeDtypeStruct((B * T, W), out_dtype),
                   jax.ShapeDtypeStruct((B, 1, W), F32),
                   jax.ShapeDtypeStruct((B, CONV_W - 1, W), F32)],
        scratch_shapes=[pltpu.VMEM((tt + SUBLANE, W), F32), pltpu.VMEM((tt, W), F32),
                        pltpu.VMEM((tt, W), F32), pltpu.VMEM((tt, W), F32),
                        pltpu.VMEM((SUBLANE, W), F32)],
        compiler_params=_cparams(("parallel", "arbitrary")),
        name="rglru")(*args, *weights)


def _hgrn_body(*refs, tt, ch, sub, t_real, n_t, has_state):
    if has_state:
        q_ref, f_ref, v_ref, g_ref, s0_ref, lb_ref, gain_ref, o_ref, sfin_ref, st_ref = refs
    else:
        q_ref, f_ref, v_ref, g_ref, lb_ref, gain_ref, o_ref, sfin_ref, st_ref = refs
    ti = pl.program_id(1)
    n_h = st_ref.shape[0]
    dk = LANE

    @pl.when(ti == 0)
    def _():
        for h in range(n_h):
            if has_state:
                st_ref[h] = s0_ref[0, h].T
            else:
                st_ref[h] = jnp.zeros((dk, dk), F32)

    rr = lax.broadcasted_iota(I32, (ch, ch), 0)
    cc = lax.broadcasted_iota(I32, (ch, ch), 1)
    tril = (cc <= rr).astype(F32)
    rows = lax.broadcasted_iota(I32, (ch, dk), 0)
    srow = lax.broadcasted_iota(I32, (sub, 1), 0)
    n_sub = ch // sub

    def head_chunk(c, h):
        r0 = pl.multiple_of(c * ch, ch)
        hs = slice(h * dk, (h + 1) * dk)
        lb = lb_ref[h]
        log_lb = jnp.log(lb)
        log_1mlb = jnp.log1p(-lb)
        q = q_ref[pl.ds(r0, ch), hs]
        q = q * jax.nn.sigmoid(q)
        fp = f_ref[pl.ds(r0, ch), hs]
        v = v_ref[pl.ds(r0, ch), hs]
        log_sig = jnp.minimum(fp, 0.0) - jnp.log1p(jnp.exp(-jnp.abs(fp)))
        b = log_1mlb + log_sig
        lf = jnp.maximum(log_lb, b) + jnp.log1p(jnp.exp(-jnp.abs(log_lb - b)))
        kk = (1.0 - lb) * jax.nn.sigmoid(-fp)
        if t_real < tt * n_t:
            live = (rows + (ti * tt + c * ch)) < t_real
            lf = jnp.where(live, lf, 0.0)
            kk = jnp.where(live, kk, 0.0)
        G = jnp.dot(tril, lf, preferred_element_type=F32, precision=lax.Precision.HIGHEST)
        st = st_ref[h]
        vb = v.astype(BF16)
        o = _nt((q * jnp.exp(G)).astype(BF16), st.astype(BF16))
        outs = []
        for i in range(n_sub):
            lo, hi = i * sub, (i + 1) * sub
            qi, Gi, ki, vi = q[lo:hi], G[lo:hi], kk[lo:hi], v[lo:hi]
            oi = o[lo:hi]
            if i > 0:
                R = G[lo - 1:lo]
                qp = (qi * jnp.exp(Gi - R)).astype(BF16)
                kp = (kk[:lo] * jnp.exp(R - G[:lo])).astype(BF16)
                att = _nt(qp, kp)
                oi = oi + jnp.dot(att.astype(BF16), vb[:lo], preferred_element_type=F32)
            for s in range(sub):
                w = jnp.exp(jnp.minimum(Gi - Gi[s:s + 1], 0.0))
                colv = jnp.sum(qi * w * ki[s:s + 1], axis=-1, keepdims=True)
                colv = jnp.where(srow >= s, colv, 0.0)
                oi = oi + colv * vi[s:s + 1]
            outs.append(oi)
        o = jnp.concatenate(outs, axis=0) if n_sub > 1 else outs[0]
        gl = G[ch - 1:ch]
        kpp = (kk * jnp.exp(gl - G)).astype(BF16)
        st_ref[h] = st * jnp.exp(gl) + _tn(vb, kpp)
        gate = g_ref[pl.ds(r0, ch), hs]
        o = _rms_rows(o, gain_ref[...]) * (gate * jax.nn.sigmoid(gate))
        o_ref[pl.ds(r0, ch), hs] = o.astype(o_ref.dtype)

    n_chunks = tt // ch
    per_iter = 2 if n_chunks % 2 == 0 else 1

    def chunk(c, carry):
        for u in range(per_iter):
            for h in range(n_h):
                head_chunk(c * per_iter + u, h)
        return carry

    lax.fori_loop(0, n_chunks // per_iter, chunk, 0)

    @pl.when(ti == n_t - 1)
    def _():
        for h in range(n_h):
            sfin_ref[0, h] = st_ref[h].T


def _hgrn(P, s0, lb, gain, *, B, T, tt, ch, sub, t_real, out_dtype):
    H = lb.shape[0]
    n_t = T // tt
    has_state = s0 is not None

    W = H * LANE

    def col(c0):
        return pl.BlockSpec((tt, W), lambda b, i: (b * n_t + i, c0 // H))

    in_specs = [col(C_CQ), col(C_CF), col(C_CI), col(C_CG)]
    args = [P, P, P, P]
    if has_state:
        in_specs.append(pl.BlockSpec((1, H, LANE, LANE), lambda b, i: (b, 0, 0, 0)))
        args.append(s0)
    in_specs += [pl.BlockSpec((H, 1, LANE), lambda b, i: (0, 0, 0)),
                 pl.BlockSpec((1, LANE), lambda b, i: (0, 0))]
    return pl.pallas_call(
        functools.partial(_hgrn_body, tt=tt, ch=ch, sub=sub, t_real=t_real, n_t=n_t,
                          has_state=has_state),
        grid=(B, n_t),
        in_specs=in_specs,
        out_specs=[pl.BlockSpec((tt, W), lambda b, i: (b * n_t + i, 0)),
                   pl.BlockSpec((1, H, LANE, LANE), lambda b, i: (b, 0, 0, 0))],
        out_shape=[jax.ShapeDtypeStruct((B * T, W), out_dtype),
                   jax.ShapeDtypeStruct((B, H, LANE, LANE), F32)],
        scratch_shapes=[pltpu.VMEM((H, LANE, LANE), F32)],
        compiler_params=_cparams(("parallel", "arbitrary")),
        name="hgrn2")(*args, lb, gain)


def _score_key(score):
    score = jnp.where(score == 0.0, 0.0, score)
    bits = pltpu.bitcast(score, I32)
    return bits ^ ((bits >> 31) & 0x7FFFFFFF)


def _kth_largest(count_ge, shape, k, bits=32):
    zero = jnp.zeros(shape, I32)
    v = jnp.where(count_ge(zero) >= k, zero, jnp.full(shape, -(2 ** (bits - 1)), I32))

    def bit_step(n, v):
        cand = v + (jnp.int32(1) << (bits - 2 - n))
        return jnp.where(count_ge(cand) >= k, cand, v)

    return lax.fori_loop(0, bits - 1, bit_step, v)


def _tie_bound(count_tie_below, r, shape, n_bits):
    def bit_step(n, pos):
        cand = pos + (jnp.int32(1) << (n_bits - 1 - n))
        return jnp.where(count_tie_below(cand) < r, cand, pos)

    return lax.fori_loop(0, n_bits, bit_step, jnp.zeros(shape, I32))


def _dsa_body(dq_ref, iq_ref, tq_ref, dk_ref, dv_ref, ik_ref, bias_ref, o_ref,
              key_ref, hi_ref, lo_ref, pos_ref, kd_ref, vb_ref, ikb_ref,
              s_ref, m_ref, l_ref, acc_ref, *, tq, n_top, n_bits, tile_grp):
    qb = pl.program_id(1)
    n_c = qb + 1
    hd = HEAD_DIM
    n_kv = dk_ref.shape[1] // hd
    n_q = dq_ref.shape[1] // hd
    grp = n_q // n_kv
    n_pair = n_q // 2
    row = lax.broadcasted_iota(I32, (tq, LANE), 0)
    col = lax.broadcasted_iota(I32, (tq, LANE), 1)
    low_half = col < hd

    @pl.when(qb == 0)
    def _():
        kd_ref[...] = jnp.zeros(kd_ref.shape, BF16)
        vb_ref[...] = jnp.zeros(vb_ref.shape, BF16)
        ikb_ref[...] = jnp.zeros(ikb_ref.shape, BF16)

    def pair_blocks(at_low, at_high):
        return jnp.concatenate([jnp.where(low_half, at_low, 0.0),
                                jnp.where(low_half, 0.0, at_high)], axis=0).astype(BF16)

    r_q = pl.multiple_of(qb * LANE, LANE)
    kx = dk_ref[pl.ds(r_q, LANE), :]
    kx_sw = pltpu.roll(kx, hd, 1)
    kd_ref[0, qb] = pair_blocks(kx, kx_sw)
    kd_ref[1, qb] = pair_blocks(kx_sw, kx)
    ikb_ref[pl.ds(r_q, LANE), :] = ik_ref[pl.ds(r_q, LANE), :].astype(BF16)
    vb_ref[pl.ds(r_q, LANE), :] = dv_ref[pl.ds(r_q, LANE), :].astype(BF16)

    iq = iq_ref[:, D_BLOCK_IQ:D_BLOCK_IQ + IDX_HEADS * hd]
    iqz = []
    for p in range(IDX_HEADS // 2):
        pair = iq[:, p * LANE:(p + 1) * LANE]
        iqz.append(jnp.concatenate([jnp.where(low_half, pair, 0.0),
                                    jnp.where(low_half, pltpu.roll(pair, hd, 1), 0.0)],
                                   axis=0).astype(BF16))
    wscale = IDX_HEADS ** -0.5 * hd ** -0.5
    tail_t = tq_ref[...].T
    w_rows = [tail_t[IW_OFF + h:IW_OFF + h + 1, :] * wscale for h in range(IDX_HEADS)]

    n_grp = (n_c + tile_grp - 1) // tile_grp

    def score_tile(c):
        r0 = pl.multiple_of(c * LANE, LANE)
        ikc = ikb_ref[pl.ds(r0, LANE), :]
        sc = None
        for p in range(IDX_HEADS // 2):
            s = jnp.maximum(_nt(ikc, iqz[p]), 0.0)
            t = s[:, :LANE] * w_rows[2 * p] + s[:, LANE:] * w_rows[2 * p + 1]
            sc = t if sc is None else sc + t
        causal = jnp.logical_or(c < qb, jnp.logical_and(c == qb, row <= col))
        key = _score_key(jnp.where(causal, sc, NEG_INF))
        key_ref[c] = key
        hi_ref[c] = (key >> 16).astype(jnp.int16)
        lo_ref[c] = (((key ^ 0x8000) << 16) >> 16).astype(jnp.int16)

    def score_group(i, carry):
        for j in range(tile_grp):
            score_tile(i * tile_grp + j)
        return carry

    lax.fori_loop(0, n_grp, score_group, 0)
    vec = (1, tq)

    def count(pred, ref=key_ref, dtype=I32):
        def body(i, acc):
            for j in range(tile_grp):
                c = i * tile_grp + j
                acc = acc + pred(ref[c], c).astype(dtype)
            return acc
        acc = lax.fori_loop(0, n_grp, body, jnp.zeros((LANE, tq), dtype))
        return jnp.sum(acc.astype(I32), axis=0, keepdims=True)

    def count16_ge(ref):
        def f(cand):
            cb = jnp.broadcast_to(cand, (LANE, tq)).astype(jnp.int16)
            return count(lambda key, c: key >= cb, ref, jnp.int16)
        return f

    min16 = -(2 ** 15)
    t_hi = _kth_largest(count16_ge(hi_ref), vec, n_top, bits=16)
    t_hi_b = jnp.broadcast_to(t_hi, (LANE, tq)).astype(jnp.int16)
    k_lo = n_top - count(lambda key, c: key > t_hi_b, hi_ref, jnp.int16)

    def band_group(i, carry):
        for j in range(tile_grp):
            c = i * tile_grp + j
            lo_ref[c] = jnp.where(hi_ref[c] == t_hi_b, lo_ref[c], jnp.int16(min16))
        return carry

    lax.fori_loop(0, n_grp, band_group, 0)
    t_lo = _kth_largest(count16_ge(lo_ref), vec, k_lo, bits=16)
    thr = (t_hi << 16) | ((t_lo ^ min16) & 0xFFFF)
    thr_b = jnp.broadcast_to(thr, (LANE, tq))
    n_gt = count(lambda key, c: key > thr_b)
    n_ge = count(lambda key, c: key >= thr_b)
    need = jnp.logical_and(n_ge > n_top, thr > KEY_NEG_INF)
    pos_ref[...] = jnp.full((LANE, tq), 2 ** n_bits, I32)

    @pl.when(jnp.max(need.astype(I32)) > 0)
    def _():
        r = n_top - n_gt

        def count_tie_below(cand):
            cb = jnp.broadcast_to(cand, (LANE, tq))
            return count(lambda key, c: jnp.logical_and(key == thr_b, row + c * LANE < cb))

        pos = _tie_bound(count_tie_below, r, vec, n_bits)
        pos_ref[...] = jnp.broadcast_to(pos, (LANE, tq))

    pos_b = pos_ref[...]

    def tile_mask(c):
        key = key_ref[c]
        idx = row + c * LANE
        sel = jnp.logical_or(key > thr_b, jnp.logical_and(key == thr_b, idx <= pos_b))
        sel = jnp.logical_and(sel, key > KEY_NEG_INF)
        mk = jnp.where(sel, 0.0, NEG_INF).T
        return jnp.concatenate([mk, mk], axis=1)

    dq = dq_ref[...].astype(BF16)

    def logits(p, c, mask):
        return (_nt(dq[:, p * LANE:(p + 1) * LANE], kd_ref[(2 * p) // grp, c])
                + mask + bias_ref[p, jnp.clip(qb - c, 0, 2)])

    def values(p, i):
        r0 = pl.multiple_of(i * tile_grp * LANE, tile_grp * LANE)
        return vb_ref[pl.ds(r0, tile_grp * LANE), :]

    outs = _pair_attention(n_grp, tile_grp, n_pair, tq, logits, values, s_ref, m_ref, l_ref, acc_ref,
                           tile_aux=tile_mask)
    for p, (oa, ob) in enumerate(outs):
        if (2 * p) // grp == 0:
            ob = pltpu.roll(ob, hd, 1)
        else:
            oa = pltpu.roll(oa, hd, 1)
        o_ref[:, p * LANE:(p + 1) * LANE] = jnp.where(low_half, oa, ob).astype(o_ref.dtype)


def _dsa_prompt(P, bias, *, B, T, tq, n_top):
    assert tq == LANE
    nq = T // tq
    n_bits = max(1, int(math.ceil(math.log2(T))))
    tile_grp = math.gcd(nq, 4)
    n_pair = bias.shape[0]
    n_tiles = T // LANE
    return pl.pallas_call(
        functools.partial(_dsa_body, tq=tq, n_top=n_top, n_bits=n_bits, tile_grp=tile_grp),
        grid=(B, nq),
        in_specs=[pl.BlockSpec((tq, 4 * LANE), lambda b, i: (b * nq + i, C_DQ // 4)),
                  pl.BlockSpec((tq, D_BLOCK), lambda b, i: (b * nq + i, C_DK * LANE // D_BLOCK)),
                  pl.BlockSpec((tq, LANE), lambda b, i: (b * nq + i, C_TAIL)),
                  pl.BlockSpec((T, LANE), lambda b, i: (b, C_DK)),
                  pl.BlockSpec((T, LANE), lambda b, i: (b, C_DV)),
                  pl.BlockSpec((T, LANE), lambda b, i: (b, C_TAIL)),
                  pl.BlockSpec(bias.shape, lambda b, i: (0, 0, 0, 0))],
        out_specs=pl.BlockSpec((tq, 4 * LANE), lambda b, i: (b * nq + i, 0)),
        out_shape=jax.ShapeDtypeStruct((B * T, 4 * LANE), BF16),
        scratch_shapes=[pltpu.VMEM((n_tiles, tq, LANE), I32),
                        pltpu.VMEM((n_tiles, tq, LANE), jnp.int16),
                        pltpu.VMEM((n_tiles, tq, LANE), jnp.int16),
                        pltpu.VMEM((tq, LANE), I32),
                        pltpu.VMEM((2, n_tiles, 2 * LANE, LANE), BF16),
                        pltpu.VMEM((T, LANE), BF16),
                        pltpu.VMEM((T, LANE), BF16),
                        pltpu.VMEM((n_pair, n_tiles, tq, 2 * LANE), F32),
                        pltpu.VMEM((n_pair, tq, 2 * LANE), F32),
                        pltpu.VMEM((n_pair, tq, 2 * LANE), F32),
                        pltpu.VMEM((n_pair, 2 * tq, LANE), F32)],
        compiler_params=_cparams(("parallel", "arbitrary")),
        name="dsa_prompt")(P, P, P, P, P, P, bias)


def _attn_a_dec_body(pt_ref, scal_ref, q_ref, kn_ref, vn_ref, bl_ref, bn_ref, gout_ref, *rest,
                     pps, n_steps):
    k_refs = rest[:pps]
    v_refs = rest[pps:2 * pps]
    o_ref, m_ref, l_ref, acc_ref = rest[2 * pps:]
    s_i = pl.program_id(1)
    n_h = q_ref.shape[1] // LANE
    rows_h = 2 * T_PAD

    @pl.when(s_i == 0)
    def _():
        m_ref[...] = jnp.full(m_ref.shape, NEG_INF, F32)
        l_ref[...] = jnp.zeros(l_ref.shape, F32)
        acc_ref[...] = jnp.zeros(acc_ref.shape, F32)

    q = q_ref[...]
    lane = lax.broadcasted_iota(I32, (T_PAD, LANE), 1)

    def q_head(h):
        qh = q[:, h * LANE:(h + 1) * LANE]
        return jnp.concatenate([jnp.where(lane < HEAD_DIM, qh, 0.0),
                                jnp.where(lane >= HEAD_DIM, qh, 0.0)], axis=0).astype(BF16)

    qs = [q_head(h) for h in range(n_h)]
    q_all = jnp.concatenate(qs, axis=0)
    is_last = s_i == n_steps - 1

    n_rows = n_h * rows_h
    page_w = PAGE * n_h
    row_head = lax.broadcasted_iota(I32, (n_rows, page_w), 0) // rows_h
    col_head = lax.broadcasted_iota(I32, (n_rows, page_w), 1) % n_h
    head_mask = jnp.where(row_head == col_head, 0.0, NEG_INF)
    last_bias = jnp.where(is_last, bl_ref[...], 0.0) + head_mask
    parts = [_nt(q_all, r[...].astype(BF16)) + (last_bias if i == pps - 1 else head_mask)
             for i, r in enumerate(k_refs)]
    v_all = jnp.concatenate([r[...].astype(BF16) for r in v_refs], axis=0)
    m, l, acc = _softmax_update((m_ref[...], l_ref[...], acc_ref[...]),
                                jnp.concatenate(parts, axis=1), v_all)
    m_ref[...] = m
    l_ref[...] = l
    acc_ref[...] = acc

    @pl.when(is_last)
    def _():
        pad = jnp.zeros((LANE - T_PAD, LANE), BF16)
        m_all, l_all, acc_all = m_ref[...], l_ref[...], acc_ref[...]
        new = []
        for h in range(n_h):
            hs = slice(h * LANE, (h + 1) * LANE)
            rs = slice(h * rows_h, (h + 1) * rows_h)
            kn = jnp.concatenate([kn_ref[:, hs].astype(BF16), pad], axis=0)
            vn = jnp.concatenate([vn_ref[:, hs].astype(BF16), pad], axis=0)
            s = _nt(qs[h], kn) + bn_ref[rs]
            new.append(_softmax_update((m_all[rs], l_all[rs], acc_all[rs]), s, vn))
        m_ref[...] = jnp.concatenate([x[0] for x in new], axis=0)
        l_ref[...] = jnp.concatenate([x[1] for x in new], axis=0)
        acc_ref[...] = jnp.concatenate([x[2] for x in new], axis=0)
        lam = scal_ref[0]
        for h in range(n_h):
            r1 = slice(h * rows_h, h * rows_h + T_PAD)
            r2 = slice(h * rows_h + T_PAD, (h + 1) * rows_h)
            o = acc_ref[r1] / l_ref[r1] - lam * (acc_ref[r2] / l_ref[r2])
            o_ref[:, h * LANE:(h + 1) * LANE] = _rms_rows(o, gout_ref[...]) * scal_ref[1]


def _attn_a_decode(P, cache_k, cache_v, page_table, layer, scal, bias_last, bias_new, gout, *, B, pps):
    n_pages = page_table.shape[1]
    n_steps = n_pages // pps
    W = 4 * LANE

    def page_spec(i):
        return pl.BlockSpec((None, None) + cache_k.shape[2:],
                            lambda b, s, pt: (pt[b, s * pps + i], layer, 0, 0))

    grid_spec = pltpu.PrefetchScalarGridSpec(
        num_scalar_prefetch=1,
        grid=(B, n_steps),
        in_specs=[pl.BlockSpec(memory_space=pltpu.SMEM),
                  pl.BlockSpec((T_PAD, W), lambda b, s, pt: (b, C_AQ // 4)),
                  pl.BlockSpec((T_PAD, W), lambda b, s, pt: (b, C_AK // 4)),
                  pl.BlockSpec((T_PAD, W), lambda b, s, pt: (b, C_AV // 4)),
                  pl.BlockSpec(bias_last.shape, lambda b, s, pt: (0, 0)),
                  pl.BlockSpec(bias_new.shape, lambda b, s, pt: (0, 0)),
                  pl.BlockSpec((1, LANE), lambda b, s, pt: (0, 0))]
        + [page_spec(i) for i in range(pps)] + [page_spec(i) for i in range(pps)],
        out_specs=pl.BlockSpec((T_PAD, W), lambda b, s, pt: (b, 0)),
        scratch_shapes=[pltpu.VMEM((bias_last.shape[0], 1), F32),
                        pltpu.VMEM((bias_last.shape[0], 1), F32),
                        pltpu.VMEM((bias_last.shape[0], LANE), F32)])
    return pl.pallas_call(
        functools.partial(_attn_a_dec_body, pps=pps, n_steps=n_steps),
        grid_spec=grid_spec,
        out_shape=jax.ShapeDtypeStruct((B * T_PAD, W), F32),
        compiler_params=_cparams(("parallel", "arbitrary")),
        name="attn_a_decode")(page_table, scal, P, P, P, bias_last, bias_new, gout,
                              *([cache_k] * pps), *([cache_v] * pps))


def _dsa_sel_body(pt_ref, iq_ref, tq_ref, ikn_ref, *rest, pps, n_steps, n_top, t_real, n_bits):
    ik_refs = rest[:pps]
    keys_ref, keyn_ref, thr_ref, pos_ref, all_ref, wb_ref = rest[pps:]
    s_i = pl.program_id(1)
    hd = HEAD_DIM
    iq = iq_ref[:, D_BLOCK_IQ:D_BLOCK_IQ + IDX_HEADS * hd].astype(BF16)
    q_idx = jnp.concatenate([iq[:, h * hd:(h + 1) * hd] for h in range(IDX_HEADS)], axis=0)
    wscale = IDX_HEADS ** -0.5 * hd ** -0.5
    tail = tq_ref[...]
    for h in range(IDX_HEADS):
        wb_ref[h * T_PAD:(h + 1) * T_PAD, :] = jnp.broadcast_to(
            tail[:, IW_OFF + h:IW_OFF + h + 1] * wscale, (T_PAD, LANE))

    def score(raw):
        s = jnp.maximum(raw, 0.0) * wb_ref[...]
        sc = s[0:T_PAD]
        for h in range(1, IDX_HEADS):
            sc = sc + s[h * T_PAD:(h + 1) * T_PAD]
        return sc

    for i in range(pps):
        raw = jnp.dot(q_idx, ik_refs[i][...].astype(BF16), preferred_element_type=F32)
        key = _score_key(score(raw))
        keys_ref[0, i] = key
        all_ref[s_i * pps + i] = key

    n_tiles = n_steps * pps + 1

    @pl.when(s_i == n_steps - 1)
    def _():
        row = lax.broadcasted_iota(I32, (T_PAD, LANE), 0)
        col = lax.broadcasted_iota(I32, (T_PAD, LANE), 1)
        pad = jnp.zeros((LANE - T_PAD, hd), F32)
        ikn = jnp.concatenate([ikn_ref[...][:, :hd], pad], axis=0)
        valid = jnp.logical_and(col <= row, col < t_real)
        keyn = _score_key(jnp.where(valid, score(_nt(q_idx, ikn.astype(BF16))), NEG_INF))
        keyn_ref[0] = keyn
        all_ref[n_tiles - 1] = keyn
        idx = (lax.broadcasted_iota(I32, all_ref.shape, 0) * LANE
               + lax.broadcasted_iota(I32, all_ref.shape, 2))

        def count(pred):
            acc = jnp.sum(pred(all_ref[...]).astype(I32), axis=0)
            return jnp.sum(acc, axis=-1, keepdims=True)

        def count_ge(cand):
            cb = jnp.broadcast_to(cand, (T_PAD, LANE))
            return count(lambda key: key >= cb)

        thr = _kth_largest(count_ge, (T_PAD, 1), n_top)
        thr_b = jnp.broadcast_to(thr, (T_PAD, LANE))
        n_gt = count(lambda key: key > thr_b)
        r = n_top - n_gt

        def count_tie_below(cand):
            cb = jnp.broadcast_to(cand, (T_PAD, LANE))
            return count(lambda key: jnp.logical_and(key == thr_b, idx < cb))

        pos = _tie_bound(count_tie_below, r, (T_PAD, 1), n_bits)
        thr_ref[0] = thr_b
        pos_ref[0] = jnp.broadcast_to(pos, (T_PAD, LANE))


def _dsa_select_decode(P, cache_idx, page_table, layer, *, B, pps, n_top, t_real):
    n_pages = page_table.shape[1]
    n_steps = n_pages // pps
    n_tiles = n_pages + 1
    n_bits = int(math.ceil(math.log2(n_tiles * LANE)))

    def page_spec(i):
        return pl.BlockSpec((None, None) + cache_idx.shape[2:],
                            lambda b, s, pt: (pt[b, s * pps + i], layer, 0, 0))

    tile_spec = pl.BlockSpec((1, T_PAD, LANE), lambda b, s, pt: (b, 0, 0))
    grid_spec = pltpu.PrefetchScalarGridSpec(
        num_scalar_prefetch=1,
        grid=(B, n_steps),
        in_specs=[pl.BlockSpec((T_PAD, D_BLOCK), lambda b, s, pt: (b, C_DK * LANE // D_BLOCK)),
                  pl.BlockSpec((T_PAD, LANE), lambda b, s, pt: (b, C_TAIL)),
                  pl.BlockSpec((T_PAD, LANE), lambda b, s, pt: (b, C_TAIL))]
        + [page_spec(i) for i in range(pps)],
        out_specs=[pl.BlockSpec((1, pps, T_PAD, LANE), lambda b, s, pt: (b, s, 0, 0)),
                   tile_spec, tile_spec, tile_spec],
        scratch_shapes=[pltpu.VMEM((n_tiles, T_PAD, LANE), I32),
                        pltpu.VMEM((IDX_HEADS * T_PAD, LANE), F32)])
    tile_shape = jax.ShapeDtypeStruct((B, T_PAD, LANE), I32)
    return pl.pallas_call(
        functools.partial(_dsa_sel_body, pps=pps, n_steps=n_steps, n_top=n_top, t_real=t_real,
                          n_bits=n_bits),
        grid_spec=grid_spec,
        out_shape=[jax.ShapeDtypeStruct((B, n_pages, T_PAD, LANE), I32),
                   tile_shape, tile_shape, tile_shape],
        compiler_params=_cparams(("parallel", "arbitrary")),
        name="dsa_select_decode")(page_table, P, P, P, *([cache_idx] * pps))


def _dsa_att_body(pt_ref, dq_ref, kn_ref, vn_ref, keys_ref, keyn_ref, thr_ref, pos_ref,
                  bl_ref, bn_ref, *rest, pps, n_steps):
    k_refs = rest[:pps]
    v_refs = rest[pps:2 * pps]
    o_ref, m_ref, l_ref, acc_ref = rest[2 * pps:]
    s_i = pl.program_id(1)
    hd = HEAD_DIM
    n_kv = kn_ref.shape[1] // hd
    n_q = dq_ref.shape[1] // hd
    grp = n_q // n_kv
    rows_g = grp * T_PAD

    @pl.when(s_i == 0)
    def _():
        m_ref[...] = jnp.full(m_ref.shape, -1e30, F32)
        l_ref[...] = jnp.zeros(l_ref.shape, F32)
        acc_ref[...] = jnp.zeros(acc_ref.shape, F32)

    dq = dq_ref[...].astype(BF16)
    q_g = [jnp.concatenate([dq[:, (g * grp + j) * hd:(g * grp + j + 1) * hd] for j in range(grp)],
                           axis=0) for g in range(n_kv)]
    thr = thr_ref[0]
    pos = pos_ref[0]
    col = lax.broadcasted_iota(I32, (T_PAD, LANE), 1)
    is_last = s_i == n_steps - 1

    def sel_mask(key, tile):
        idx = col + tile * LANE
        sel = jnp.logical_or(key > thr, jnp.logical_and(key == thr, idx <= pos))
        sel = jnp.logical_and(sel, key > KEY_NEG_INF)
        return jnp.concatenate([sel] * grp, axis=0)

    def attend(tiles, masks, bias_tiles):
        m_all, l_all, acc_all = m_ref[...], l_ref[...], acc_ref[...]
        new = []
        for g in range(n_kv):
            rs = slice(g * rows_g, (g + 1) * rows_g)
            logit_tiles, pv = tiles(g)
            parts = []
            for s, mk, bt in zip(logit_tiles, masks, bias_tiles):
                if bt is not None:
                    s = s + bt[rs]
                parts.append(jnp.where(mk, s, NEG_INF))
            s = jnp.concatenate(parts, axis=1) if len(parts) > 1 else parts[0]
            m_new = jnp.maximum(m_all[rs], jnp.max(s, axis=-1, keepdims=True))
            alpha = jnp.exp(m_all[rs] - m_new)
            p = jnp.exp(s - m_new)
            new.append((m_new, alpha * l_all[rs] + jnp.sum(p, axis=-1, keepdims=True),
                        alpha * acc_all[rs] + pv(p.astype(BF16))))
        m_ref[...] = jnp.concatenate([x[0] for x in new], axis=0)
        l_ref[...] = jnp.concatenate([x[1] for x in new], axis=0)
        acc_ref[...] = jnp.concatenate([x[2] for x in new], axis=0)

    masks = [sel_mask(keys_ref[0, i], s_i * pps + i) for i in range(pps)]
    last_bias = jnp.where(is_last, bl_ref[...], 0.0)

    def page_tiles(g):
        vt = jnp.concatenate([r[g].astype(BF16) for r in v_refs], axis=1)
        return ([jnp.dot(q_g[g], r[g].astype(BF16), preferred_element_type=F32) for r in k_refs],
                lambda p: _nt(p, vt))

    attend(page_tiles, masks, [None] * (pps - 1) + [last_bias])

    @pl.when(is_last)
    def _():
        pad = jnp.zeros((LANE - T_PAD, hd), BF16)

        def new_tiles(g):
            hs = slice(g * hd, (g + 1) * hd)
            kn = jnp.concatenate([kn_ref[:, hs].astype(BF16), pad], axis=0)
            vn = jnp.concatenate([vn_ref[:, hs].astype(BF16), pad], axis=0)
            return [_nt(q_g[g], kn)], lambda p: jnp.dot(p, vn, preferred_element_type=F32)

        attend(new_tiles, [sel_mask(keyn_ref[0], n_steps * pps)], [bn_ref[...]])
        for g in range(n_kv):
            rs = slice(g * rows_g, (g + 1) * rows_g)
            og = acc_ref[rs] / l_ref[rs]
            for j in range(grp):
                h = g * grp + j
                o_ref[:, h * hd:(h + 1) * hd] = og[j * T_PAD:(j + 1) * T_PAD]


def _dsa_attend_decode(P, cache_k, cache_v, page_table, layer, keys, keyn, thr, pos,
                       bias_last, bias_new, *, B, pps):
    n_pages = page_table.shape[1]
    n_steps = n_pages // pps
    rows = bias_last.shape[0]

    def page_spec(i):
        return pl.BlockSpec((None, None) + cache_k.shape[2:],
                            lambda b, s, pt: (pt[b, s * pps + i], layer, 0, 0, 0))

    tile_spec = pl.BlockSpec((1, T_PAD, LANE), lambda b, s, pt: (b, 0, 0))
    grid_spec = pltpu.PrefetchScalarGridSpec(
        num_scalar_prefetch=1,
        grid=(B, n_steps),
        in_specs=[pl.BlockSpec((T_PAD, 4 * LANE), lambda b, s, pt: (b, C_DQ // 4)),
                  pl.BlockSpec((T_PAD, LANE), lambda b, s, pt: (b, C_DK)),
                  pl.BlockSpec((T_PAD, LANE), lambda b, s, pt: (b, C_DV)),
                  pl.BlockSpec((1, pps, T_PAD, LANE), lambda b, s, pt: (b, s, 0, 0)),
                  tile_spec, tile_spec, tile_spec,
                  pl.BlockSpec(bias_last.shape, lambda b, s, pt: (0, 0)),
                  pl.BlockSpec(bias_new.shape, lambda b, s, pt: (0, 0))]
        + [page_spec(i) for i in range(pps)] + [page_spec(i) for i in range(pps)],
        out_specs=pl.BlockSpec((T_PAD, 4 * LANE), lambda b, s, pt: (b, 0)),
        scratch_shapes=[pltpu.VMEM((rows, 1), F32), pltpu.VMEM((rows, 1), F32),
                        pltpu.VMEM((rows, HEAD_DIM), F32)])
    return pl.pallas_call(
        functools.partial(_dsa_att_body, pps=pps, n_steps=n_steps),
        grid_spec=grid_spec,
        out_shape=jax.ShapeDtypeStruct((B * T_PAD, 4 * LANE), F32),
        compiler_params=_cparams(("parallel", "arbitrary")),
        name="dsa_attend_decode")(page_table, P, P, P, keys, keyn, thr, pos, bias_last, bias_new,
                                  *([cache_k] * pps), *([cache_v] * pps))


def _bias_minus_far(tab, dist):
    onehot = jax.nn.one_hot(_rel_bucket(dist), N_BUCKETS, dtype=F32)
    bias = jnp.einsum("...k,km->...m", onehot, tab, precision=lax.Precision.HIGHEST)
    return bias - tab[N_BUCKETS - 1]


def _toeplitz_tiles(tab, t):
    r = jnp.arange(t)[:, None]
    c = jnp.arange(t)[None, :]
    tiles = jnp.stack([_bias_minus_far(tab, r - c), _bias_minus_far(tab, t + r - c)], axis=0)
    return jnp.transpose(tiles, (3, 0, 1, 2))


def _decode_bias(tab, past, t_real):
    tq = jnp.arange(T_PAD)[:, None]
    kc = jnp.arange(LANE)[None, :]
    last = _bias_minus_far(tab, past + tq - (past - PAGE + kc))
    new = _bias_minus_far(tab, tq - kc)
    valid = jnp.logical_and(kc <= tq, kc < t_real)
    new = jnp.where(valid[..., None], new, NEG_INF)
    return jnp.transpose(last, (2, 0, 1)), jnp.transpose(new, (2, 0, 1))


def _block_diag(w):
    n, d, e = w.shape
    eye = jnp.eye(n, dtype=w.dtype)
    return (eye[:, None, :, None] * w[:, :, None, :]).reshape(n * d, n * e)


def kernel(x_prompt, x_sample, cache_a_k, cache_a_v, cache_d_k, cache_d_v, cache_d_idx, state_b_h, state_b_conv, state_c_s, page_table, rel_bias, ln1, w_in, a_q_norm, a_k_norm, a_lam_q1, a_lam_k1, a_lam_q2, a_lam_k2, a_out_norm, b_conv_w, b_conv_b, b_wa, b_ba, b_wx, b_bx, b_lambda, c_lb_logits, c_out_norm, d_q_norm, d_k_norm, w_out, ln2, w_up, w_down):
    B, T, D = x_prompt.shape
    Bs, Ts, _ = x_sample.shape
    L = w_in.shape[0]
    n_pages = page_table.shape[1]
    past = n_pages * PAGE
    gw = D // 4
    a_heads = gw // (2 * HEAD_DIM)
    c_heads = c_lb_logits.shape[1] // LANE
    d_heads = gw // HEAD_DIM
    d_kv = cache_d_k.shape[3]
    d_grp = d_heads // d_kv
    n_pool = cache_a_k.shape[0]
    assert Ts <= T_PAD - 0 and Ts >= CONV_W - 1 and past > 0

    tab = rel_bias.astype(F32)
    n_a_maps = 2 * a_heads
    tab_a, tab_d = tab[:, :n_a_maps], tab[:, n_a_maps:]
    tq_a = min(128, T)
    tq_d = min(128, T)
    bias_a = jnp.transpose(_toeplitz_tiles(tab_a, tq_a).reshape(a_heads, 2, 2, tq_a, tq_a),
                           (0, 2, 3, 1, 4)).reshape(a_heads, 2, tq_a, 2 * tq_a)
    causal = jnp.where(jnp.arange(tq_a)[None, :] <= jnp.arange(tq_a)[:, None], 0.0, NEG_INF)
    causal = jnp.tile(causal, (1, 2)).astype(F32)
    bias_a = jnp.stack([jnp.full((a_heads, tq_a, 2 * tq_a), NEG_INF, F32), bias_a[:, 0] + causal,
                        bias_a[:, 1], jnp.zeros((a_heads, tq_a, 2 * tq_a), F32)], axis=1)
    bias_d = jnp.concatenate([_toeplitz_tiles(tab_d, tq_d),
                              jnp.zeros((d_heads, 1, tq_d, tq_d), F32)], axis=1)
    bias_d = jnp.transpose(bias_d.reshape(d_heads // 2, 2, 3, tq_d, tq_d),
                           (0, 2, 3, 1, 4)).reshape(d_heads // 2, 3, tq_d, 2 * tq_d)
    bl_a, bn_a = _decode_bias(tab_a, past, Ts)
    bl_a = jnp.repeat(bl_a.reshape(n_a_maps * T_PAD, LANE), a_heads, axis=1)
    bn_a = bn_a.reshape(n_a_maps * T_PAD, LANE)
    bl_d, bn_d = _decode_bias(tab_d, past, Ts)
    bl_d = bl_d.reshape(d_heads * T_PAD, LANE)
    bn_d = bn_d.reshape(d_heads * T_PAD, LANE)

    lb_cum = jnp.cumsum(jax.nn.softmax(c_lb_logits.astype(F32), axis=0), axis=0)
    lb_all = (lb_cum - lb_cum[0]).reshape(L, c_heads, 1, LANE)

    seg = jnp.kron(jnp.eye(LANE // HEAD_DIM, dtype=F32),
                   jnp.full((HEAD_DIM, HEAD_DIM), 1.0 / HEAD_DIM, F32)).astype(BF16)
    qscale = HEAD_DIM ** -0.5
    zeros = lambda n: jnp.zeros((n,), F32)
    ones = lambda n: jnp.ones((n,), F32)
    flag = jnp.concatenate([ones(2 * gw), zeros(7 * gw), ones(gw), ones(LANE),
                            zeros(P_WIDTH - 10 * gw - LANE)]).reshape(1, P_WIDTH)
    w_in_t = jnp.transpose(w_in, (0, 2, 1))
    n_in = w_in_t.shape[1]
    tail0 = (P_WIDTH // IN_PROJ_TN - 1) * IN_PROJ_TN
    assert tail0 < n_in <= P_WIDTH
    w_in_tail = jnp.pad(w_in_t[:, tail0:], ((0, 0), (0, tail0 + IN_PROJ_TN - n_in), (0, 0)))

    ca_k = cache_a_k.reshape(n_pool, L, PAGE * a_heads, 2 * HEAD_DIM)
    ca_v = cache_a_v.reshape(n_pool, L, PAGE * a_heads, 2 * HEAD_DIM)
    cd_k = jnp.transpose(cache_d_k, (0, 1, 3, 4, 2))
    cd_v = jnp.transpose(cache_d_v, (0, 1, 3, 4, 2))
    cd_idx = jnp.transpose(cache_d_idx, (0, 1, 3, 2))

    xp = x_prompt.reshape(B * T, D)
    xs = jnp.pad(x_sample, ((0, 0), (0, T_PAD - Ts), (0, 0))).reshape(Bs * T_PAD, D)
    s_conv = jnp.pad(state_b_conv, ((0, 0), (0, 0), (SUBLANE - (CONV_W - 1), 0), (0, 0)))

    n_top_p = min(TOPK_MAX, T // 4)
    n_top_s = min(TOPK_MAX, (past + Ts) // 4)
    tm_p = min(1024, B * T)
    tm_f = min(1024, B * T)
    tt_b = min(256, T)
    tt_c = min(512, T)
    ch_c = math.gcd(T, 64)
    pps_a = math.gcd(n_pages, 16)
    pps_d = math.gcd(n_pages, 32)
    pps_att = pps_d

    outs_p, outs_s = [], []
    for l in range(L):
        gain = jnp.concatenate(
            [jnp.tile(a_q_norm[l], 2 * a_heads) * qscale, jnp.tile(a_k_norm[l], 2 * a_heads),
             ones(7 * gw), jnp.tile(d_q_norm[l], d_heads) * qscale,
             jnp.tile(d_k_norm[l], d_kv), ones(P_WIDTH - 10 * gw - LANE)]).reshape(1, P_WIDTH)
        g1 = ln1[l].reshape(1, D)
        g2 = ln2[l].reshape(1, D)
        lam_init = 0.8 - 0.6 * math.exp(-0.3 * l)
        lam = (jnp.exp(jnp.sum(a_lam_q1[l].astype(F32) * a_lam_k1[l].astype(F32)))
               - jnp.exp(jnp.sum(a_lam_q2[l].astype(F32) * a_lam_k2[l].astype(F32))) + lam_init)
        scal = jnp.stack([lam, jnp.asarray(1.0 - lam_init, F32)]).astype(F32)
        g_a = a_out_norm[l].reshape(1, LANE)
        g_c = c_out_norm[l].reshape(1, LANE)
        b_weights = (b_conv_w[l], b_conv_b[l].reshape(1, gw), _block_diag(b_wa[l]).astype(BF16),
                     b_ba[l].reshape(1, gw), _block_diag(b_wx[l]).astype(BF16),
                     b_bx[l].reshape(1, gw), b_lambda[l].reshape(1, gw))

        def dense_tail(x, mixes, tm, tmf):
            x1 = _out_proj(mixes, w_out, l, x, tm=tm, tn=min(512, D))
            return _ffn(x1, g2, w_up, w_down, l, tm=tmf, tf=min(512, w_up.shape[2]))

        Pp, ak_p, av_p = _in_proj(xp, g1, w_in_t, w_in_tail, l, gain, flag, seg, tm=tm_p)
        mix_a = _attn_a_prompt(Pp, scal, bias_a, g_a, B=B, T=T, tq=tq_a)
        mix_b, hfin, buf = _rglru(Pp, None, b_weights, B=B, T=T, tt=tt_b, t_real=T, out_dtype=BF16)
        mix_c, sfin = _hgrn(Pp, None, lb_all[l], g_c, B=B, T=T, tt=tt_c, ch=ch_c,
                            sub=min(16, ch_c), t_real=T, out_dtype=BF16)
        mix_d = _dsa_prompt(Pp, bias_d, B=B, T=T, tq=tq_d, n_top=n_top_p)
        xp = dense_tail(xp, (mix_a, mix_b, mix_c, mix_d), tm_p, tm_f)
        P3 = Pp.reshape(B, T, P_WIDTH)
        outs_p.append((ak_p.reshape(B, T, a_heads, 2 * HEAD_DIM),
                       av_p.reshape(B, T, a_heads, 2 * HEAD_DIM),
                       P3[..., C_DK * LANE:C_DV * LANE].reshape(B, T, d_kv, HEAD_DIM),
                       P3[..., C_DV * LANE:(C_DV + 1) * LANE].reshape(B, T, d_kv, HEAD_DIM),
                       P3[..., C_TAIL * LANE:C_TAIL * LANE + HEAD_DIM],
                       hfin.reshape(B, gw), buf, sfin))

        Ps, ak_s, av_s = _in_proj(xs, g1, w_in_t, w_in_tail, l, gain, flag, seg, tm=Bs * T_PAD)
        smix_a = _attn_a_decode(Ps, ca_k, ca_v, page_table, l, scal, bl_a, bn_a, g_a, B=Bs, pps=pps_a)
        smix_b, shfin, sbuf = _rglru(
            Ps, (state_b_h[:, l].reshape(Bs, 1, gw), s_conv[:, l]), b_weights,
            B=Bs, T=T_PAD, tt=T_PAD, t_real=Ts, out_dtype=F32)
        smix_c, ssfin = _hgrn(Ps, state_c_s[:, l], lb_all[l], g_c, B=Bs, T=T_PAD, tt=T_PAD,
                              ch=T_PAD, sub=T_PAD, t_real=Ts, out_dtype=F32)
        keys, keyn, thr, pos = _dsa_select_decode(Ps, cd_idx, page_table, l, B=Bs, pps=pps_d,
                                                  n_top=n_top_s, t_real=Ts)
        smix_d = _dsa_attend_decode(Ps, cd_k, cd_v, page_table, l, keys, keyn, thr, pos,
                                    bl_d, bn_d, B=Bs, pps=pps_att)
        xs = dense_tail(xs, (smix_a, smix_b, smix_c, smix_d), Bs * T_PAD, Bs * T_PAD)
        S3 = Ps.reshape(Bs, T_PAD, P_WIDTH)[:, :Ts]
        outs_s.append((ak_s.reshape(Bs, T_PAD, a_heads, 2 * HEAD_DIM)[:, :Ts],
                       av_s.reshape(Bs, T_PAD, a_heads, 2 * HEAD_DIM)[:, :Ts],
                       S3[..., C_DK * LANE:C_DV * LANE].reshape(Bs, Ts, d_kv, HEAD_DIM),
                       S3[..., C_DV * LANE:(C_DV + 1) * LANE].reshape(Bs, Ts, d_kv, HEAD_DIM),
                       S3[..., C_TAIL * LANE:C_TAIL * LANE + HEAD_DIM],
                       shfin.reshape(Bs, gw), sbuf, ssfin))

    y_prompt = xp.reshape(B, T, D)
    y_sample = xs.reshape(Bs, T_PAD, D)[:, :Ts]
    stack = lambda outs: [jnp.stack(s, axis=1) for s in zip(*outs)]
    return (y_prompt, y_sample, *stack(outs_p), *stack(outs_s))
```

```python
import functools
import math

import jax
import jax.numpy as jnp
import numpy as np
from jax import lax
from jax.experimental import pallas as pl
from jax.experimental.pallas import tpu as pltpu

F32 = jnp.float32
BF16 = jnp.bfloat16
I32 = jnp.int32

EPS = 1e-6
HEAD_DIM = 64
PAGE = 128
CONV_W = 4
LRU_C = 8.0
IDX_HEADS = 8
TOPK_MAX = 256
N_BUCKETS = 32
MAX_DIST = 128
LANE = 128
SUBLANE = 8
T_PAD = 8
INT_MIN = -(2 ** 31)
KEY_NEG_INF = -2139095041
VMEM_LIMIT = 56 * 1024 * 1024
NEG_INF = float("-inf")

C_AQ, C_AK, C_AV, C_BX, C_BG, C_CQ, C_CF, C_CI, C_CG, C_DQ = (4 * i for i in range(10))
C_DK, C_DV, C_IQ, C_TAIL = 40, 41, 42, 46
P_WIDTH = 48 * LANE
IW_OFF = 64
D_BLOCK = 8 * LANE
D_BLOCK_IQ = (C_IQ - C_DK) * LANE
D_BLOCK_TAIL = (C_TAIL - C_DK) * LANE


def _cparams(sem):
    return pltpu.CompilerParams(dimension_semantics=sem, vmem_limit_bytes=VMEM_LIMIT)


def _nt(a, b):
    return lax.dot_general(a, b, (((1,), (1,)), ((), ())), preferred_element_type=F32)


def _tn(a, b):
    return lax.dot_general(a, b, (((0,), (0,)), ((), ())), preferred_element_type=F32)


def _rel_bucket(dist):
    n = jnp.maximum(dist, 0)
    exact = N_BUCKETS // 2
    large = exact + (jnp.log(jnp.maximum(n, 1).astype(F32) / exact)
                     / math.log(MAX_DIST / exact) * (N_BUCKETS - exact)).astype(I32)
    return jnp.where(n < exact, n, jnp.minimum(large, N_BUCKETS - 1))


def _in_proj_body(x_ref, g_ref, w_ref, wtail_ref, gain_ref, flag_ref, seg_ref, o_ref, ak_ref, av_ref,
                  xn_ref, *, tn, norm_tiles, head_tiles):
    j = pl.program_id(1)
    n_sub = tn // LANE
    tm = o_ref.shape[0]

    def head_rows(dst_ref):
        for c in range(n_sub):
            dst_ref[pl.ds(c, tm, stride=n_sub), :] = o_ref[:, c * LANE:(c + 1) * LANE]

    @pl.when(j == 0)
    def _():
        x = x_ref[...]
        ms = jnp.mean(x * x, axis=-1, keepdims=True)
        xn_ref[...] = (x * lax.rsqrt(ms + EPS) * g_ref[...]).astype(BF16)

    w = jnp.where(j == pl.num_programs(1) - 1, wtail_ref[...], w_ref[...])
    y = _nt(xn_ref[...], w.astype(BF16))
    is_norm = functools.reduce(jnp.logical_or, [j == t for t in norm_tiles])

    @pl.when(is_norm)
    def _():
        seg = seg_ref[...]
        for c in range(tn // LANE):
            sl = slice(c * LANE, (c + 1) * LANE)
            yc = y[:, sl]
            y2 = yc * yc
            hi = y2.astype(BF16)
            lo = (y2 - hi.astype(F32)).astype(BF16)
            ms = (jnp.dot(hi, seg, preferred_element_type=F32)
                  + jnp.dot(lo, seg, preferred_element_type=F32))
            yn = yc * lax.rsqrt(ms + EPS) * gain_ref[:, sl]
            o_ref[:, sl] = jnp.where(flag_ref[:, sl] > 0, yn, yc)

    @pl.when(jnp.logical_not(is_norm))
    def _():
        o_ref[...] = y

    @pl.when(j == head_tiles[0])
    def _():
        head_rows(ak_ref)

    @pl.when(j == head_tiles[1])
    def _():
        head_rows(av_ref)


IN_PROJ_TN = 4 * LANE


def _in_proj(x, g, wt, wtail, layer, gain, flag, seg, *, tm):
    M, K = x.shape
    tn = IN_PROJ_TN
    n_tiles = P_WIDTH // tn
    norm_ranges = ((C_AQ, 4), (C_AK, 4), (C_DQ, 4), (C_DK, 1))
    norm_tiles = tuple(sorted({t for c, n in norm_ranges
                               for t in range(c * LANE // tn, ((c + n) * LANE - 1) // tn + 1)}))

    n_sub = tn // LANE
    assert C_AK % n_sub == 0 and C_AV % n_sub == 0
    head_spec = pl.BlockSpec((tm * n_sub, LANE), lambda i, j: (i, 0))
    head_shape = jax.ShapeDtypeStruct((M * n_sub, LANE), F32)
    return pl.pallas_call(
        functools.partial(_in_proj_body, tn=tn, norm_tiles=norm_tiles,
                          head_tiles=(C_AK // n_sub, C_AV // n_sub)),
        grid=(M // tm, n_tiles),
        in_specs=[pl.BlockSpec((tm, K), lambda i, j: (i, 0)),
                  pl.BlockSpec((1, K), lambda i, j: (0, 0)),
                  pl.BlockSpec((None, tn, K), lambda i, j: (layer, jnp.minimum(j, n_tiles - 2), 0)),
                  pl.BlockSpec((None, tn, K), lambda i, j: (layer, 0, 0)),
                  pl.BlockSpec((1, tn), lambda i, j: (0, j)),
                  pl.BlockSpec((1, tn), lambda i, j: (0, j)),
                  pl.BlockSpec((LANE, LANE), lambda i, j: (0, 0))],
        out_specs=[pl.BlockSpec((tm, tn), lambda i, j: (i, j)), head_spec, head_spec],
        out_shape=[jax.ShapeDtypeStruct((M, P_WIDTH), F32), head_shape, head_shape],
        scratch_shapes=[pltpu.VMEM((tm, K), BF16)],
        compiler_params=_cparams(("parallel", "arbitrary")),
        name="in_proj")(x, g, wt, wtail, gain, flag, seg)


def _out_proj_body(a_ref, b_ref, c_ref, d_ref, w_ref, x_ref, o_ref, wb_ref, *, gw):
    @pl.when(pl.program_id(1) == 0)
    def _():
        wb_ref[...] = w_ref[...].astype(BF16)

    acc = x_ref[...]
    for g, m_ref in enumerate((a_ref, b_ref, c_ref, d_ref)):
        acc = acc + jnp.dot(m_ref[...].astype(BF16), wb_ref[g * gw:(g + 1) * gw, :],
                            preferred_element_type=F32)
    o_ref[...] = acc


def _out_proj(mixes, w, layer, x, *, tm, tn):
    M, D = x.shape
    gw = mixes[0].shape[1]
    mix_spec = pl.BlockSpec((tm, gw), lambda j, i: (i, 0))
    return pl.pallas_call(
        functools.partial(_out_proj_body, gw=gw),
        grid=(D // tn, M // tm),
        in_specs=[mix_spec, mix_spec, mix_spec, mix_spec,
                  pl.BlockSpec((None, w.shape[1], tn), lambda j, i: (layer, 0, j)),
                  pl.BlockSpec((tm, tn), lambda j, i: (i, j))],
        out_specs=pl.BlockSpec((tm, tn), lambda j, i: (i, j)),
        out_shape=jax.ShapeDtypeStruct((M, D), F32),
        scratch_shapes=[pltpu.VMEM((w.shape[1], tn), BF16)],
        compiler_params=_cparams(("parallel", "arbitrary")),
        name="out_proj")(*mixes, w, x)


def _ffn_body(x_ref, g_ref, wu_ref, wd_ref, o_ref, xn_ref):
    f = pl.program_id(1)

    @pl.when(f == 0)
    def _():
        x = x_ref[...]
        ms = jnp.mean(x * x, axis=-1, keepdims=True)
        xn_ref[...] = (x * lax.rsqrt(ms + EPS) * g_ref[...]).astype(BF16)
        o_ref[...] = x

    h = jnp.dot(xn_ref[...], wu_ref[...].astype(BF16), preferred_element_type=F32)
    h = jnp.maximum(h, 0.0)
    h = (h * h).astype(BF16)
    o_ref[...] += jnp.dot(h, wd_ref[...].astype(BF16), preferred_element_type=F32)


def _ffn(x, g, wu, wd, layer, *, tm, tf):
    M, D = x.shape
    Fd = wu.shape[2]
    return pl.pallas_call(
        _ffn_body,
        grid=(M // tm, Fd // tf),
        in_specs=[pl.BlockSpec((tm, D), lambda i, f: (i, 0), pipeline_mode=pl.Buffered(1)),
                  pl.BlockSpec((1, D), lambda i, f: (0, 0)),
                  pl.BlockSpec((None, D, tf), lambda i, f: (layer, 0, f)),
                  pl.BlockSpec((None, tf, D), lambda i, f: (layer, f, 0))],
        out_specs=pl.BlockSpec((tm, D), lambda i, f: (i, 0)),
        out_shape=jax.ShapeDtypeStruct((M, D), F32),
        scratch_shapes=[pltpu.VMEM((tm, D), BF16)],
        compiler_params=_cparams(("parallel", "arbitrary")),
        name="ffn")(x, g, wu, wd)


def _softmax_update(state, s, v):
    m, l, acc = state
    m_new = jnp.maximum(m, jnp.max(s, axis=-1, keepdims=True))
    alpha = jnp.exp(m - m_new)
    p = jnp.exp(s - m_new)
    l = alpha * l + jnp.sum(p, axis=-1, keepdims=True)
    acc = alpha * acc + jnp.dot(p.astype(BF16), v, preferred_element_type=F32)
    return m_new, l, acc


def _rms_rows(o, gain):
    ms = jnp.mean(o * o, axis=-1, keepdims=True)
    return o * lax.rsqrt(ms + EPS) * gain


def _pair_attention(n_grp, tile_grp, n_pair, tq, logits, values, s_ref, m_ref, l_ref, acc_ref,
                    tile_aux=None):
    def max_pass(i, carry):
        cs = [i * tile_grp + j for j in range(tile_grp)]
        aux = [tile_aux(c) for c in cs] if tile_aux is not None else None
        for p in range(n_pair):
            t = []
            for j, c in enumerate(cs):
                s = logits(p, c) if aux is None else logits(p, c, aux[j])
                s_ref[p, c] = s
                t.append(s)
            while len(t) > 1:
                t = [jnp.maximum(a, b) for a, b in zip(t[::2], t[1::2])]
            m_ref[p] = jnp.maximum(m_ref[p], t[0])
        return carry

    m_ref[...] = jnp.full(m_ref.shape, NEG_INF, F32)
    lax.fori_loop(0, n_grp, max_pass, 0)
    for p in range(n_pair):
        m = m_ref[p]
        m_ref[p] = jnp.concatenate(
            [jnp.broadcast_to(jnp.max(m[:, :LANE], axis=-1, keepdims=True), (tq, LANE)),
             jnp.broadcast_to(jnp.max(m[:, LANE:], axis=-1, keepdims=True), (tq, LANE))], axis=1)
    l_ref[...] = jnp.zeros(l_ref.shape, F32)
    acc_ref[...] = jnp.zeros(acc_ref.shape, F32)

    def sum_pass(i, carry):
        for p in range(n_pair):
            e = [jnp.exp(s_ref[p, i * tile_grp + j] - m_ref[p]) for j in range(tile_grp)]
            l_ref[p] += functools.reduce(lambda a, b: a + b, e)
            pb = jnp.concatenate(
                [jnp.concatenate([x[:, :LANE], x[:, LANE:]], axis=0).astype(BF16) for x in e], axis=1)
            acc_ref[p] += jnp.dot(pb, values(p, i), preferred_element_type=F32)
        return carry

    lax.fori_loop(0, n_grp, sum_pass, 0)
    outs = []
    for p in range(n_pair):
        l = l_ref[p]
        acc = acc_ref[p]
        outs.append((acc[:tq] / jnp.sum(l[:, :LANE], axis=-1, keepdims=True),
                     acc[tq:] / jnp.sum(l[:, LANE:], axis=-1, keepdims=True)))
    return outs


def _attn_a_body(scal_ref, q_ref, k_ref, v_ref, bias_ref, gout_ref, o_ref,
                 kd_ref, vb_ref, s_ref, m_ref, l_ref, acc_ref, *, tq, tile_grp):
    qb = pl.program_id(1)
    n_h = q_ref.shape[1] // LANE
    n_grp = (qb + tile_grp) // tile_grp
    low_half = lax.broadcasted_iota(I32, (tq, LANE), 1) < HEAD_DIM

    @pl.when(qb == 0)
    def _():
        kd_ref[...] = jnp.zeros(kd_ref.shape, BF16)
        vb_ref[...] = jnp.zeros(vb_ref.shape, BF16)

    r_q = pl.multiple_of(qb * LANE, LANE)
    for h in range(n_h):
        kx = k_ref[pl.ds(r_q, LANE), h * LANE:(h + 1) * LANE]
        kd_ref[h, qb] = jnp.concatenate([jnp.where(low_half, kx, 0.0),
                                         jnp.where(low_half, 0.0, kx)], axis=0).astype(BF16)
    vb_ref[pl.ds(r_q, LANE), :] = v_ref[pl.ds(r_q, LANE), :].astype(BF16)
    q = q_ref[...].astype(BF16)

    def logits(h, c):
        return (_nt(q[:, h * LANE:(h + 1) * LANE], kd_ref[h, c])
                + bias_ref[h, jnp.clip(qb - c, -1, 2) + 1])

    def values(h, i):
        r0 = pl.multiple_of(i * tile_grp * LANE, tile_grp * LANE)
        return vb_ref[pl.ds(r0, tile_grp * LANE), h * LANE:(h + 1) * LANE]

    outs = _pair_attention(n_grp, tile_grp, n_h, tq, logits, values, s_ref, m_ref, l_ref, acc_ref)
    lam = scal_ref[0]
    for h, (o1, o2) in enumerate(outs):
        o = _rms_rows(o1 - lam * o2, gout_ref[...]) * scal_ref[1]
        o_ref[:, h * LANE:(h + 1) * LANE] = o.astype(o_ref.dtype)


def _attn_a_prompt(P, scal, bias, gout, *, B, T, tq):
    assert tq == LANE
    H = bias.shape[0]
    W = H * LANE
    nq = T // tq
    tile_grp = math.gcd(nq, 4)
    return pl.pallas_call(
        functools.partial(_attn_a_body, tq=tq, tile_grp=tile_grp),
        grid=(B, nq),
        in_specs=[pl.BlockSpec(memory_space=pltpu.SMEM),
                  pl.BlockSpec((tq, W), lambda b, i: (b * nq + i, C_AQ // H)),
                  pl.BlockSpec((T, W), lambda b, i: (b, C_AK // H)),
                  pl.BlockSpec((T, W), lambda b, i: (b, C_AV // H)),
                  pl.BlockSpec(bias.shape, lambda b, i: (0, 0, 0, 0)),
                  pl.BlockSpec((1, LANE), lambda b, i: (0, 0))],
        out_specs=pl.BlockSpec((tq, W), lambda b, i: (b * nq + i, 0)),
        out_shape=jax.ShapeDtypeStruct((B * T, W), BF16),
        scratch_shapes=[pltpu.VMEM((H, nq, 2 * LANE, LANE), BF16),
                        pltpu.VMEM((T, W), BF16),
                        pltpu.VMEM((H, nq, tq, 2 * LANE), F32),
                        pltpu.VMEM((H, tq, 2 * LANE), F32),
                        pltpu.VMEM((H, tq, 2 * LANE), F32),
                        pltpu.VMEM((H, 2 * tq, LANE), F32)],
        compiler_params=_cparams(("parallel", "arbitrary")),
        name="attn_a_prompt")(scal, P, P, P, bias, gout)


def _rglru_body(*refs, tt, t_real, n_t, has_state, pos0_is_zero):
    if has_state:
        (x_ref, g_ref, h0_ref, buf0_ref, cw_ref, cb_ref, wa_ref, ba_ref, wx_ref, bxb_ref, lam_ref,
         o_ref, hfin_ref, buf_ref, xpad_ref, a_ref, b_ref, hs_ref, hc_ref) = refs
    else:
        (x_ref, g_ref, cw_ref, cb_ref, wa_ref, ba_ref, wx_ref, bxb_ref, lam_ref,
         o_ref, hfin_ref, buf_ref, xpad_ref, a_ref, b_ref, hs_ref, hc_ref) = refs
    ti = pl.program_id(1)
    W = x_ref.shape[1]

    @pl.when(ti == 0)
    def _():
        if has_state:
            xpad_ref[0:SUBLANE, :] = buf0_ref[0]
            hc_ref[...] = jnp.broadcast_to(h0_ref[0], (SUBLANE, W))
        else:
            xpad_ref[0:SUBLANE, :] = jnp.zeros((SUBLANE, W), F32)
            hc_ref[...] = jnp.zeros((SUBLANE, W), F32)

    x = x_ref[...]
    xpad_ref[SUBLANE:SUBLANE + tt, :] = x
    xc = cb_ref[...] + x * cw_ref[CONV_W - 1:CONV_W, :]
    for j in range(CONV_W - 1):
        off = SUBLANE - (CONV_W - 1) + j
        xc = xc + xpad_ref[off:off + tt, :] * cw_ref[j:j + 1, :]
    xcb = xc.astype(BF16)
    r = jax.nn.sigmoid(jnp.dot(xcb, wa_ref[...], preferred_element_type=F32) + ba_ref[...])
    i = jax.nn.sigmoid(jnp.dot(xcb, wx_ref[...], preferred_element_type=F32) + bxb_ref[...])
    nl = -lam_ref[...]
    softplus = jnp.maximum(nl, 0.0) + jnp.log1p(jnp.exp(-jnp.abs(nl)))
    a = jnp.exp(-LRU_C * r * softplus)
    mult = jnp.sqrt(1.0 - a * a)
    if pos0_is_zero:
        rows = lax.broadcasted_iota(I32, (tt, W), 0)
        mult = jnp.where(jnp.logical_and(rows == 0, ti == 0), 1.0, mult)
    a_ref[...] = a
    b_ref[...] = mult * (i * xc)
    rowt = lax.broadcasted_iota(I32, (SUBLANE, W), 0)

    def tile(n, h_prev):
        r0 = pl.multiple_of(n * SUBLANE, SUBLANE)
        at = a_ref[pl.ds(r0, SUBLANE), :]
        bt = b_ref[pl.ds(r0, SUBLANE), :]
        for s in (1, 2, 4):
            keep = rowt >= s
            bt = jnp.where(keep, bt + at * pltpu.roll(bt, s, 0), bt)
            at = jnp.where(keep, at * pltpu.roll(at, s, 0), at)
        ht = bt + at * h_prev
        hs_ref[pl.ds(r0, SUBLANE), :] = ht
        return jnp.broadcast_to(ht[SUBLANE - 1:SUBLANE, :], (SUBLANE, W))

    hc_ref[...] = lax.fori_loop(0, tt // SUBLANE, tile, hc_ref[...])
    o_ref[...] = (hs_ref[...] * jax.nn.gelu(g_ref[...])).astype(o_ref.dtype)
    xpad_ref[0:SUBLANE, :] = xpad_ref[tt:tt + SUBLANE, :]

    @pl.when(ti == n_t - 1)
    def _():
        t_loc = t_real - (n_t - 1) * tt
        hfin_ref[0] = hs_ref[t_loc - 1:t_loc, :]
        buf_ref[0] = x_ref[t_loc - (CONV_W - 1):t_loc, :]


def _rglru(P, state, weights, *, B, T, tt, t_real, out_dtype):
    W = 4 * LANE
    n_t = T // tt
    has_state = state is not None
    xspec = pl.BlockSpec((tt, W), lambda b, i: (b * n_t + i, C_BX // 4))
    gspec = pl.BlockSpec((tt, W), lambda b, i: (b * n_t + i, C_BG // 4))
    full = lambda shape: pl.BlockSpec(shape, lambda b, i: (0,) * len(shape))
    wspecs = [full((CONV_W, W)), full((1, W)), full((W, W)), full((1, W)), full((W, W)),
              full((1, W)), full((1, W))]
    in_specs = [xspec, gspec]
    args = [P, P]
    if has_state:
        in_specs += [pl.BlockSpec((1, 1, W), lambda b, i: (b, 0, 0)),
                     pl.BlockSpec((1, SUBLANE, W), lambda b, i: (b, 0, 0))]
        args += list(state)
    return pl.pallas_call(
        functools.partial(_rglru_body, tt=tt, t_real=t_real, n_t=n_t, has_state=has_state,
                          pos0_is_zero=not has_state),
        grid=(B, n_t),
        in_specs=in_specs + wspecs,
        out_specs=[pl.BlockSpec((tt, W), lambda b, i: (b * n_t + i, 0)),
                   pl.BlockSpec((1, 1, W), lambda b, i: (b, 0, 0)),
                   pl.BlockSpec((1, CONV_W - 1, W), lambda b, i: (b, 0, 0))],
        out_shape=[jax.ShapeDtypeStruct((B * T, W), out_dtype),
                   jax.ShapeDtypeStruct((B, 1, W), F32),
                   jax.ShapeDtypeStruct((B, CONV_W - 1, W), F32)],
        scratch_shapes=[pltpu.VMEM((tt + SUBLANE, W), F32), pltpu.VMEM((tt, W), F32),
                        pltpu.VMEM((tt, W), F32), pltpu.VMEM((tt, W), F32),
                        pltpu.VMEM((SUBLANE, W), F32)],
        compiler_params=_cparams(("parallel", "arbitrary")),
        name="rglru")(*args, *weights)


def _hgrn_body(*refs, tt, ch, sub, t_real, n_t, has_state):
    if has_state:
        q_ref, f_ref, v_ref, g_ref, s0_ref, lb_ref, gain_ref, o_ref, sfin_ref, st_ref = refs
    else:
        q_ref, f_ref, v_ref, g_ref, lb_ref, gain_ref, o_ref, sfin_ref, st_ref = refs
    ti = pl.program_id(1)
    n_h = st_ref.shape[0]
    dk = LANE

    @pl.when(ti == 0)
    def _():
        for h in range(n_h):
            if has_state:
                st_ref[h] = s0_ref[0, h].T
            else:
                st_ref[h] = jnp.zeros((dk, dk), F32)

    rr = lax.broadcasted_iota(I32, (ch, ch), 0)
    cc = lax.broadcasted_iota(I32, (ch, ch), 1)
    tril = (cc <= rr).astype(F32)
    rows = lax.broadcasted_iota(I32, (ch, dk), 0)
    srow = lax.broadcasted_iota(I32, (sub, 1), 0)
    n_sub = ch // sub

    def head_chunk(c, h):
        r0 = pl.multiple_of(c * ch, ch)
        hs = slice(h * dk, (h + 1) * dk)
        lb = lb_ref[h]
        log_lb = jnp.log(lb)
        log_1mlb = jnp.log1p(-lb)
        q = q_ref[pl.ds(r0, ch), hs]
        q = q * jax.nn.sigmoid(q)
        fp = f_ref[pl.ds(r0, ch), hs]
        v = v_ref[pl.ds(r0, ch), hs]
        log_sig = jnp.minimum(fp, 0.0) - jnp.log1p(jnp.exp(-jnp.abs(fp)))
        b = log_1mlb + log_sig
        lf = jnp.maximum(log_lb, b) + jnp.log1p(jnp.exp(-jnp.abs(log_lb - b)))
        kk = (1.0 - lb) * jax.nn.sigmoid(-fp)
        if t_real < tt * n_t:
            live = (rows + (ti * tt + c * ch)) < t_real
            lf = jnp.where(live, lf, 0.0)
            kk = jnp.where(live, kk, 0.0)
        G = jnp.dot(tril, lf, preferred_element_type=F32, precision=lax.Precision.HIGHEST)
        st = st_ref[h]
        vb = v.astype(BF16)
        o = _nt((q * jnp.exp(G)).astype(BF16), st.astype(BF16))
        outs = []
        for i in range(n_sub):
            lo, hi = i * sub, (i + 1) * sub
            qi, Gi, ki, vi = q[lo:hi], G[lo:hi], kk[lo:hi], v[lo:hi]
            oi = o[lo:hi]
            if i > 0:
                R = G[lo - 1:lo]
                qp = (qi * jnp.exp(Gi - R)).astype(BF16)
                kp = (kk[:lo] * jnp.exp(R - G[:lo])).astype(BF16)
                att = _nt(qp, kp)
                oi = oi + jnp.dot(att.astype(BF16), vb[:lo], preferred_element_type=F32)
            for s in range(sub):
                w = jnp.exp(jnp.minimum(Gi - Gi[s:s + 1], 0.0))
                colv = jnp.sum(qi * w * ki[s:s + 1], axis=-1, keepdims=True)
                colv = jnp.where(srow >= s, colv, 0.0)
                oi = oi + colv * vi[s:s + 1]
            outs.append(oi)
        o = jnp.concatenate(outs, axis=0) if n_sub > 1 else outs[0]
        gl = G[ch - 1:ch]
        kpp = (kk * jnp.exp(gl - G)).astype(BF16)
        st_ref[h] = st * jnp.exp(gl) + _tn(vb, kpp)
        gate = g_ref[pl.ds(r0, ch), hs]
        o = _rms_rows(o, gain_ref[...]) * (gate * jax.nn.sigmoid(gate))
        o_ref[pl.ds(r0, ch), hs] = o.astype(o_ref.dtype)

    n_chunks = tt // ch
    per_iter = 2 if n_chunks % 2 == 0 else 1

    def chunk(c, carry):
        for u in range(per_iter):
            for h in range(n_h):
                head_chunk(c * per_iter + u, h)
        return carry

    lax.fori_loop(0, n_chunks // per_iter, chunk, 0)

    @pl.when(ti == n_t - 1)
    def _():
        for h in range(n_h):
            sfin_ref[0, h] = st_ref[h].T


def _hgrn(P, s0, lb, gain, *, B, T, tt, ch, sub, t_real, out_dtype):
    H = lb.shape[0]
    n_t = T // tt
    has_state = s0 is not None

    W = H * LANE

    def col(c0):
        return pl.BlockSpec((tt, W), lambda b, i: (b * n_t + i, c0 // H))

    in_specs = [col(C_CQ), col(C_CF), col(C_CI), col(C_CG)]
    args = [P, P, P, P]
    if has_state:
        in_specs.append(pl.BlockSpec((1, H, LANE, LANE), lambda b, i: (b, 0, 0, 0)))
        args.append(s0)
    in_specs += [pl.BlockSpec((H, 1, LANE), lambda b, i: (0, 0, 0)),
                 pl.BlockSpec((1, LANE), lambda b, i: (0, 0))]
    return pl.pallas_call(
        functools.partial(_hgrn_body, tt=tt, ch=ch, sub=sub, t_real=t_real, n_t=n_t,
                          has_state=has_state),
        grid=(B, n_t),
        in_specs=in_specs,
        out_specs=[pl.BlockSpec((tt, W), lambda b, i: (b * n_t + i, 0)),
                   pl.BlockSpec((1, H, LANE, LANE), lambda b, i: (b, 0, 0, 0))],
        out_shape=[jax.ShapeDtypeStruct((B * T, W), out_dtype),
                   jax.ShapeDtypeStruct((B, H, LANE, LANE), F32)],
        scratch_shapes=[pltpu.VMEM((H, LANE, LANE), F32)],
        compiler_params=_cparams(("parallel", "arbitrary")),
        name="hgrn2")(*args, lb, gain)


def _score_key(score):
    score = jnp.where(score == 0.0, 0.0, score)
    bits = pltpu.bitcast(score, I32)
    return bits ^ ((bits >> 31) & 0x7FFFFFFF)


def _kth_largest(count_ge, shape, k, bits=32):
    zero = jnp.zeros(shape, I32)
    v = jnp.where(count_ge(zero) >= k, zero, jnp.full(shape, -(2 ** (bits - 1)), I32))

    def bit_step(n, v):
        cand = v + (jnp.int32(1) << (bits - 2 - n))
        return jnp.where(count_ge(cand) >= k, cand, v)

    return lax.fori_loop(0, bits - 1, bit_step, v)


def _tie_bound(count_tie_below, r, shape, n_bits):
    def bit_step(n, pos):
        cand = pos + (jnp.int32(1) << (n_bits - 1 - n))
        return jnp.where(count_tie_below(cand) < r, cand, pos)

    return lax.fori_loop(0, n_bits, bit_step, jnp.zeros(shape, I32))


def _dsa_body(dq_ref, iq_ref, tq_ref, dk_ref, dv_ref, ik_ref, bias_ref, o_ref,
              key_ref, hi_ref, lo_ref, pos_ref, kd_ref, vb_ref, ikb_ref,
              s_ref, m_ref, l_ref, acc_ref, *, tq, n_top, n_bits, tile_grp):
    qb = pl.program_id(1)
    n_c = qb + 1
    hd = HEAD_DIM
    n_kv = dk_ref.shape[1] // hd
    n_q = dq_ref.shape[1] // hd
    grp = n_q // n_kv
    n_pair = n_q // 2
    row = lax.broadcasted_iota(I32, (tq, LANE), 0)
    col = lax.broadcasted_iota(I32, (tq, LANE), 1)
    low_half = col < hd

    @pl.when(qb == 0)
    def _():
        kd_ref[...] = jnp.zeros(kd_ref.shape, BF16)
        vb_ref[...] = jnp.zeros(vb_ref.shape, BF16)
        ikb_ref[...] = jnp.zeros(ikb_ref.shape, BF16)

    def pair_blocks(at_low, at_high):
        return jnp.concatenate([jnp.where(low_half, at_low, 0.0),
                                jnp.where(low_half, 0.0, at_high)], axis=0).astype(BF16)

    r_q = pl.multiple_of(qb * LANE, LANE)
    kx = dk_ref[pl.ds(r_q, LANE), :]
    kx_sw = pltpu.roll(kx, hd, 1)
    kd_ref[0, qb] = pair_blocks(kx, kx_sw)
    kd_ref[1, qb] = pair_blocks(kx_sw, kx)
    ikb_ref[pl.ds(r_q, LANE), :] = ik_ref[pl.ds(r_q, LANE), :].astype(BF16)
    vb_ref[pl.ds(r_q, LANE), :] = dv_ref[pl.ds(r_q, LANE), :].astype(BF16)

    iq = iq_ref[:, D_BLOCK_IQ:D_BLOCK_IQ + IDX_HEADS * hd]
    iqz = []
    for p in range(IDX_HEADS // 2):
        pair = iq[:, p * LANE:(p + 1) * LANE]
        iqz.append(jnp.concatenate([jnp.where(low_half, pair, 0.0),
                                    jnp.where(low_half, pltpu.roll(pair, hd, 1), 0.0)],
                                   axis=0).astype(BF16))
    wscale = IDX_HEADS ** -0.5 * hd ** -0.5
    tail_t = tq_ref[...].T
    w_rows = [tail_t[IW_OFF + h:IW_OFF + h + 1, :] * wscale for h in range(IDX_HEADS)]

    n_grp = (n_c + tile_grp - 1) // tile_grp

    def score_tile(c):
        r0 = pl.multiple_of(c * LANE, LANE)
        ikc = ikb_ref[pl.ds(r0, LANE), :]
        sc = None
        for p in range(IDX_HEADS // 2):
            s = jnp.maximum(_nt(ikc, iqz[p]), 0.0)
            t = s[:, :LANE] * w_rows[2 * p] + s[:, LANE:] * w_rows[2 * p + 1]
            sc = t if sc is None else sc + t
        causal = jnp.logical_or(c < qb, jnp.logical_and(c == qb, row <= col))
        key = _score_key(jnp.where(causal, sc, NEG_INF))
        key_ref[c] = key
        hi_ref[c] = (key >> 16).astype(jnp.int16)
        lo_ref[c] = (((key ^ 0x8000) << 16) >> 16).astype(jnp.int16)

    def score_group(i, carry):
        for j in range(tile_grp):
            score_tile(i * tile_grp + j)
        return carry

    lax.fori_loop(0, n_grp, score_group, 0)
    vec = (1, tq)

    def count(pred, ref=key_ref, dtype=I32):
        def body(i, acc):
            for j in range(tile_grp):
                c = i * tile_grp + j
                acc = acc + pred(ref[c], c).astype(dtype)
            return acc
        acc = lax.fori_loop(0, n_grp, body, jnp.zeros((LANE, tq), dtype))
        return jnp.sum(acc.astype(I32), axis=0, keepdims=True)

    def count16_ge(ref):
        def f(cand):
            cb = jnp.broadcast_to(cand, (LANE, tq)).astype(jnp.int16)
            return count(lambda key, c: key >= cb, ref, jnp.int16)
        return f

    min16 = -(2 ** 15)
    t_hi = _kth_largest(count16_ge(hi_ref), vec, n_top, bits=16)
    t_hi_b = jnp.broadcast_to(t_hi, (LANE, tq)).astype(jnp.int16)
    k_lo = n_top - count(lambda key, c: key > t_hi_b, hi_ref, jnp.int16)

    def band_group(i, carry):
        for j in range(tile_grp):
            c = i * tile_grp + j
            lo_ref[c] = jnp.where(hi_ref[c] == t_hi_b, lo_ref[c], jnp.int16(min16))
        return carry

    lax.fori_loop(0, n_grp, band_group, 0)
    t_lo = _kth_largest(count16_ge(lo_ref), vec, k_lo, bits=16)
    thr = (t_hi << 16) | ((t_lo ^ min16) & 0xFFFF)
    thr_b = jnp.broadcast_to(thr, (LANE, tq))
    n_gt = count(lambda key, c: key > thr_b)
    n_ge = count(lambda key, c: key >= thr_b)
    need = jnp.logical_and(n_ge > n_top, thr > KEY_NEG_INF)
    pos_ref[...] = jnp.full((LANE, tq), 2 ** n_bits, I32)

    @pl.when(jnp.max(need.astype(I32)) > 0)
    def _():
        r = n_top - n_gt

        def count_tie_below(cand):
            cb = jnp.broadcast_to(cand, (LANE, tq))
            return count(lambda key, c: jnp.logical_and(key == thr_b, row + c * LANE < cb))

        pos = _tie_bound(count_tie_below, r, vec, n_bits)
        pos_ref[...] = jnp.broadcast_to(pos, (LANE, tq))

    pos_b = pos_ref[...]

    def tile_mask(c):
        key = key_ref[c]
        idx = row + c * LANE
        sel = jnp.logical_or(key > thr_b, jnp.logical_and(key == thr_b, idx <= pos_b))
        sel = jnp.logical_and(sel, key > KEY_NEG_INF)
        mk = jnp.where(sel, 0.0, NEG_INF).T
        return jnp.concatenate([mk, mk], axis=1)

    dq = dq_ref[...].astype(BF16)

    def logits(p, c, mask):
        return (_nt(dq[:, p * LANE:(p + 1) * LANE], kd_ref[(2 * p) // grp, c])
                + mask + bias_ref[p, jnp.clip(qb - c, 0, 2)])

    def values(p, i):
        r0 = pl.multiple_of(i * tile_grp * LANE, tile_grp * LANE)
        return vb_ref[pl.ds(r0, tile_grp * LANE), :]

    outs = _pair_attention(n_grp, tile_grp, n_pair, tq, logits, values, s_ref, m_ref, l_ref, acc_ref,
                           tile_aux=tile_mask)
    for p, (oa, ob) in enumerate(outs):
        if (2 * p) // grp == 0:
            ob = pltpu.roll(ob, hd, 1)
        else:
            oa = pltpu.roll(oa, hd, 1)
        o_ref[:, p * LANE:(p + 1) * LANE] = jnp.where(low_half, oa, ob).astype(o_ref.dtype)


def _dsa_prompt(P, bias, *, B, T, tq, n_top):
    assert tq == LANE
    nq = T // tq
    n_bits = max(1, int(math.ceil(math.log2(T))))
    tile_grp = math.gcd(nq, 4)
    n_pair = bias.shape[0]
    n_tiles = T // LANE
    return pl.pallas_call(
        functools.partial(_dsa_body, tq=tq, n_top=n_top, n_bits=n_bits, tile_grp=tile_grp),
        grid=(B, nq),
        in_specs=[pl.BlockSpec((tq, 4 * LANE), lambda b, i: (b * nq + i, C_DQ // 4)),
                  pl.BlockSpec((tq, D_BLOCK), lambda b, i: (b * nq + i, C_DK * LANE // D_BLOCK)),
                  pl.BlockSpec((tq, LANE), lambda b, i: (b * nq + i, C_TAIL)),
                  pl.BlockSpec((T, LANE), lambda b, i: (b, C_DK)),
                  pl.BlockSpec((T, LANE), lambda b, i: (b, C_DV)),
                  pl.BlockSpec((T, LANE), lambda b, i: (b, C_TAIL)),
                  pl.BlockSpec(bias.shape, lambda b, i: (0, 0, 0, 0))],
        out_specs=pl.BlockSpec((tq, 4 * LANE), lambda b, i: (b * nq + i, 0)),
        out_shape=jax.ShapeDtypeStruct((B * T, 4 * LANE), BF16),
        scratch_shapes=[pltpu.VMEM((n_tiles, tq, LANE), I32),
                        pltpu.VMEM((n_tiles, tq, LANE), jnp.int16),
                        pltpu.VMEM((n_tiles, tq, LANE), jnp.int16),
                        pltpu.VMEM((tq, LANE), I32),
                        pltpu.VMEM((2, n_tiles, 2 * LANE, LANE), BF16),
                        pltpu.VMEM((T, LANE), BF16),
                        pltpu.VMEM((T, LANE), BF16),
                        pltpu.VMEM((n_pair, n_tiles, tq, 2 * LANE), F32),
                        pltpu.VMEM((n_pair, tq, 2 * LANE), F32),
                        pltpu.VMEM((n_pair, tq, 2 * LANE), F32),
                        pltpu.VMEM((n_pair, 2 * tq, LANE), F32)],
        compiler_params=_cparams(("parallel", "arbitrary")),
        name="dsa_prompt")(P, P, P, P, P, P, bias)


def _attn_a_dec_body(pt_ref, scal_ref, q_ref, kn_ref, vn_ref, bl_ref, bn_ref, gout_ref, *rest,
                     pps, n_steps):
    k_refs = rest[:pps]
    v_refs = rest[pps:2 * pps]
    o_ref, m_ref, l_ref, acc_ref = rest[2 * pps:]
    s_i = pl.program_id(1)
    n_h = q_ref.shape[1] // LANE
    rows_h = 2 * T_PAD

    @pl.when(s_i == 0)
    def _():
        m_ref[...] = jnp.full(m_ref.shape, NEG_INF, F32)
        l_ref[...] = jnp.zeros(l_ref.shape, F32)
        acc_ref[...] = jnp.zeros(acc_ref.shape, F32)

    q = q_ref[...]
    lane = lax.broadcasted_iota(I32, (T_PAD, LANE), 1)

    def q_head(h):
        qh = q[:, h * LANE:(h + 1) * LANE]
        return jnp.concatenate([jnp.where(lane < HEAD_DIM, qh, 0.0),
                                jnp.where(lane >= HEAD_DIM, qh, 0.0)], axis=0).astype(BF16)

    qs = [q_head(h) for h in range(n_h)]
    q_all = jnp.concatenate(qs, axis=0)
    is_last = s_i == n_steps - 1

    n_rows = n_h * rows_h
    page_w = PAGE * n_h
    row_head = lax.broadcasted_iota(I32, (n_rows, page_w), 0) // rows_h
    col_head = lax.broadcasted_iota(I32, (n_rows, page_w), 1) % n_h
    head_mask = jnp.where(row_head == col_head, 0.0, NEG_INF)
    last_bias = jnp.where(is_last, bl_ref[...], 0.0) + head_mask
    parts = [_nt(q_all, r[...].astype(BF16)) + (last_bias if i == pps - 1 else head_mask)
             for i, r in enumerate(k_refs)]
    v_all = jnp.concatenate([r[...].astype(BF16) for r in v_refs], axis=0)
    m, l, acc = _softmax_update((m_ref[...], l_ref[...], acc_ref[...]),
                                jnp.concatenate(parts, axis=1), v_all)
    m_ref[...] = m
    l_ref[...] = l
    acc_ref[...] = acc

    @pl.when(is_last)
    def _():
        pad = jnp.zeros((LANE - T_PAD, LANE), BF16)
        m_all, l_all, acc_all = m_ref[...], l_ref[...], acc_ref[...]
        new = []
        for h in range(n_h):
            hs = slice(h * LANE, (h + 1) * LANE)
            rs = slice(h * rows_h, (h + 1) * rows_h)
            kn = jnp.concatenate([kn_ref[:, hs].astype(BF16), pad], axis=0)
            vn = jnp.concatenate([vn_ref[:, hs].astype(BF16), pad], axis=0)
            s = _nt(qs[h], kn) + bn_ref[rs]
            new.append(_softmax_update((m_all[rs], l_all[rs], acc_all[rs]), s, vn))
        m_ref[...] = jnp.concatenate([x[0] for x in new], axis=0)
        l_ref[...] = jnp.concatenate([x[1] for x in new], axis=0)
        acc_ref[...] = jnp.concatenate([x[2] for x in new], axis=0)
        lam = scal_ref[0]
        for h in range(n_h):
            r1 = slice(h * rows_h, h * rows_h + T_PAD)
            r2 = slice(h * rows_h + T_PAD, (h + 1) * rows_h)
            o = acc_ref[r1] / l_ref[r1] - lam * (acc_ref[r2] / l_ref[r2])
            o_ref[:, h * LANE:(h + 1) * LANE] = _rms_rows(o, gout_ref[...]) * scal_ref[1]


def _attn_a_decode(P, cache_k, cache_v, page_table, layer, scal, bias_last, bias_new, gout, *, B, pps):
    n_pages = page_table.shape[1]
    n_steps = n_pages // pps
    W = 4 * LANE

    def page_spec(i):
        return pl.BlockSpec((None, None) + cache_k.shape[2:],
                            lambda b, s, pt: (pt[b, s * pps + i], layer, 0, 0))

    grid_spec = pltpu.PrefetchScalarGridSpec(
        num_scalar_prefetch=1,
        grid=(B, n_steps),
        in_specs=[pl.BlockSpec(memory_space=pltpu.SMEM),
                  pl.BlockSpec((T_PAD, W), lambda b, s, pt: (b, C_AQ // 4)),
                  pl.BlockSpec((T_PAD, W), lambda b, s, pt: (b, C_AK // 4)),
                  pl.BlockSpec((T_PAD, W), lambda b, s, pt: (b, C_AV // 4)),
                  pl.BlockSpec(bias_last.shape, lambda b, s, pt: (0, 0)),
                  pl.BlockSpec(bias_new.shape, lambda b, s, pt: (0, 0)),
                  pl.BlockSpec((1, LANE), lambda b, s, pt: (0, 0))]
        + [page_spec(i) for i in range(pps)] + [page_spec(i) for i in range(pps)],
        out_specs=pl.BlockSpec((T_PAD, W), lambda b, s, pt: (b, 0)),
        scratch_shapes=[pltpu.VMEM((bias_last.shape[0], 1), F32),
                        pltpu.VMEM((bias_last.shape[0], 1), F32),
                        pltpu.VMEM((bias_last.shape[0], LANE), F32)])
    return pl.pallas_call(
        functools.partial(_attn_a_dec_body, pps=pps, n_steps=n_steps),
        grid_spec=grid_spec,
        out_shape=jax.ShapeDtypeStruct((B * T_PAD, W), F32),
        compiler_params=_cparams(("parallel", "arbitrary")),
        name="attn_a_decode")(page_table, scal, P, P, P, bias_last, bias_new, gout,
                              *([cache_k] * pps), *([cache_v] * pps))


def _dsa_sel_body(pt_ref, iq_ref, tq_ref, ikn_ref, *rest, pps, n_steps, n_top, t_real, n_bits):
    ik_refs = rest[:pps]
    keys_ref, keyn_ref, thr_ref, pos_ref, all_ref, wb_ref = rest[pps:]
    s_i = pl.program_id(1)
    hd = HEAD_DIM
    iq = iq_ref[:, D_BLOCK_IQ:D_BLOCK_IQ + IDX_HEADS * hd].astype(BF16)
    q_idx = jnp.concatenate([iq[:, h * hd:(h + 1) * hd] for h in range(IDX_HEADS)], axis=0)
    wscale = IDX_HEADS ** -0.5 * hd ** -0.5
    tail = tq_ref[...]
    for h in range(IDX_HEADS):
        wb_ref[h * T_PAD:(h + 1) * T_PAD, :] = jnp.broadcast_to(
            tail[:, IW_OFF + h:IW_OFF + h + 1] * wscale, (T_PAD, LANE))

    def score(raw):
        s = jnp.maximum(raw, 0.0) * wb_ref[...]
        sc = s[0:T_PAD]
        for h in range(1, IDX_HEADS):
            sc = sc + s[h * T_PAD:(h + 1) * T_PAD]
        return sc

    for i in range(pps):
        raw = jnp.dot(q_idx, ik_refs[i][...].astype(BF16), preferred_element_type=F32)
        key = _score_key(score(raw))
        keys_ref[0, i] = key
        all_ref[s_i * pps + i] = key

    n_tiles = n_steps * pps + 1

    @pl.when(s_i == n_steps - 1)
    def _():
        row = lax.broadcasted_iota(I32, (T_PAD, LANE), 0)
        col = lax.broadcasted_iota(I32, (T_PAD, LANE), 1)
        pad = jnp.zeros((LANE - T_PAD, hd), F32)
        ikn = jnp.concatenate([ikn_ref[...][:, :hd], pad], axis=0)
        valid = jnp.logical_and(col <= row, col < t_real)
        keyn = _score_key(jnp.where(valid, score(_nt(q_idx, ikn.astype(BF16))), NEG_INF))
        keyn_ref[0] = keyn
        all_ref[n_tiles - 1] = keyn
        idx = (lax.broadcasted_iota(I32, all_ref.shape, 0) * LANE
               + lax.broadcasted_iota(I32, all_ref.shape, 2))

        def count(pred):
            acc = jnp.sum(pred(all_ref[...]).astype(I32), axis=0)
            return jnp.sum(acc, axis=-1, keepdims=True)

        def count_ge(cand):
            cb = jnp.broadcast_to(cand, (T_PAD, LANE))
            return count(lambda key: key >= cb)

        thr = _kth_largest(count_ge, (T_PAD, 1), n_top)
        thr_b = jnp.broadcast_to(thr, (T_PAD, LANE))
        n_gt = count(lambda key: key > thr_b)
        r = n_top - n_gt

        def count_tie_below(cand):
            cb = jnp.broadcast_to(cand, (T_PAD, LANE))
            return count(lambda key: jnp.logical_and(key == thr_b, idx < cb))

        pos = _tie_bound(count_tie_below, r, (T_PAD, 1), n_bits)
        thr_ref[0] = thr_b
        pos_ref[0] = jnp.broadcast_to(pos, (T_PAD, LANE))


def _dsa_select_decode(P, cache_idx, page_table, layer, *, B, pps, n_top, t_real):
    n_pages = page_table.shape[1]
    n_steps = n_pages // pps
    n_tiles = n_pages + 1
    n_bits = int(math.ceil(math.log2(n_tiles * LANE)))

    def page_spec(i):
        return pl.BlockSpec((None, None) + cache_idx.shape[2:],
                            lambda b, s, pt: (pt[b, s * pps + i], layer, 0, 0))

    tile_spec = pl.BlockSpec((1, T_PAD, LANE), lambda b, s, pt: (b, 0, 0))
    grid_spec = pltpu.PrefetchScalarGridSpec(
        num_scalar_prefetch=1,
        grid=(B, n_steps),
        in_specs=[pl.BlockSpec((T_PAD, D_BLOCK), lambda b, s, pt: (b, C_DK * LANE // D_BLOCK)),
                  pl.BlockSpec((T_PAD, LANE), lambda b, s, pt: (b, C_TAIL)),
                  pl.BlockSpec((T_PAD, LANE), lambda b, s, pt: (b, C_TAIL))]
        + [page_spec(i) for i in range(pps)],
        out_specs=[pl.BlockSpec((1, pps, T_PAD, LANE), lambda b, s, pt: (b, s, 0, 0)),
                   tile_spec, tile_spec, tile_spec],
        scratch_shapes=[pltpu.VMEM((n_tiles, T_PAD, LANE), I32),
                        pltpu.VMEM((IDX_HEADS * T_PAD, LANE), F32)])
    tile_shape = jax.ShapeDtypeStruct((B, T_PAD, LANE), I32)
    return pl.pallas_call(
        functools.partial(_dsa_sel_body, pps=pps, n_steps=n_steps, n_top=n_top, t_real=t_real,
                          n_bits=n_bits),
        grid_spec=grid_spec,
        out_shape=[jax.ShapeDtypeStruct((B, n_pages, T_PAD, LANE), I32),
                   tile_shape, tile_shape, tile_shape],
        compiler_params=_cparams(("parallel", "arbitrary")),
        name="dsa_select_decode")(page_table, P, P, P, *([cache_idx] * pps))


def _dsa_att_body(pt_ref, dq_ref, kn_ref, vn_ref, keys_ref, keyn_ref, thr_ref, pos_ref,
                  bl_ref, bn_ref, *rest, pps, n_steps):
    k_refs = rest[:pps]
    v_refs = rest[pps:2 * pps]
    o_ref, m_ref, l_ref, acc_ref = rest[2 * pps:]
    s_i = pl.program_id(1)
    hd = HEAD_DIM
    n_kv = kn_ref.shape[1] // hd
    n_q = dq_ref.shape[1] // hd
    grp = n_q // n_kv
    rows_g = grp * T_PAD

    @pl.when(s_i == 0)
    def _():
        m_ref[...] = jnp.full(m_ref.shape, -1e30, F32)
        l_ref[...] = jnp.zeros(l_ref.shape, F32)
        acc_ref[...] = jnp.zeros(acc_ref.shape, F32)

    dq = dq_ref[...].astype(BF16)
    q_g = [jnp.concatenate([dq[:, (g * grp + j) * hd:(g * grp + j + 1) * hd] for j in range(grp)],
                           axis=0) for g in range(n_kv)]
    thr = thr_ref[0]
    pos = pos_ref[0]
    col = lax.broadcasted_iota(I32, (T_PAD, LANE), 1)
    is_last = s_i == n_steps - 1

    def sel_mask(key, tile):
        idx = col + tile * LANE
        sel = jnp.logical_or(key > thr, jnp.logical_and(key == thr, idx <= pos))
        sel = jnp.logical_and(sel, key > KEY_NEG_INF)
        return jnp.concatenate([sel] * grp, axis=0)

    def attend(tiles, masks, bias_tiles):
        m_all, l_all, acc_all = m_ref[...], l_ref[...], acc_ref[...]
        new = []
        for g in range(n_kv):
            rs = slice(g * rows_g, (g + 1) * rows_g)
            logit_tiles, pv = tiles(g)
            parts = []
            for s, mk, bt in zip(logit_tiles, masks, bias_tiles):
                if bt is not None:
                    s = s + bt[rs]
                parts.append(jnp.where(mk, s, NEG_INF))
            s = jnp.concatenate(parts, axis=1) if len(parts) > 1 else parts[0]
            m_new = jnp.maximum(m_all[rs], jnp.max(s, axis=-1, keepdims=True))
            alpha = jnp.exp(m_all[rs] - m_new)
            p = jnp.exp(s - m_new)
            new.append((m_new, alpha * l_all[rs] + jnp.sum(p, axis=-1, keepdims=True),
                        alpha * acc_all[rs] + pv(p.astype(BF16))))
        m_ref[...] = jnp.concatenate([x[0] for x in new], axis=0)
        l_ref[...] = jnp.concatenate([x[1] for x in new], axis=0)
        acc_ref[...] = jnp.concatenate([x[2] for x in new], axis=0)

    masks = [sel_mask(keys_ref[0, i], s_i * pps + i) for i in range(pps)]
    last_bias = jnp.where(is_last, bl_ref[...], 0.0)

    def page_tiles(g):
        vt = jnp.concatenate([r[g].astype(BF16) for r in v_refs], axis=1)
        return ([jnp.dot(q_g[g], r[g].astype(BF16), preferred_element_type=F32) for r in k_refs],
                lambda p: _nt(p, vt))

    attend(page_tiles, masks, [None] * (pps - 1) + [last_bias])

    @pl.when(is_last)
    def _():
        pad = jnp.zeros((LANE - T_PAD, hd), BF16)

        def new_tiles(g):
            hs = slice(g * hd, (g + 1) * hd)
            kn = jnp.concatenate([kn_ref[:, hs].astype(BF16), pad], axis=0)
            vn = jnp.concatenate([vn_ref[:, hs].astype(BF16), pad], axis=0)
            return [_nt(q_g[g], kn)], lambda p: jnp.dot(p, vn, preferred_element_type=F32)

        attend(new_tiles, [sel_mask(keyn_ref[0], n_steps * pps)], [bn_ref[...]])
        for g in range(n_kv):
            rs = slice(g * rows_g, (g + 1) * rows_g)
            og = acc_ref[rs] / l_ref[rs]
            for j in range(grp):
                h = g * grp + j
                o_ref[:, h * hd:(h + 1) * hd] = og[j * T_PAD:(j + 1) * T_PAD]


def _dsa_attend_decode(P, cache_k, cache_v, page_table, layer, keys, keyn, thr, pos,
                       bias_last, bias_new, *, B, pps):
    n_pages = page_table.shape[1]
    n_steps = n_pages // pps
    rows = bias_last.shape[0]

    def page_spec(i):
        return pl.BlockSpec((None, None) + cache_k.shape[2:],
                            lambda b, s, pt: (pt[b, s * pps + i], layer, 0, 0, 0))

    tile_spec = pl.BlockSpec((1, T_PAD, LANE), lambda b, s, pt: (b, 0, 0))
    grid_spec = pltpu.PrefetchScalarGridSpec(
        num_scalar_prefetch=1,
        grid=(B, n_steps),
        in_specs=[pl.BlockSpec((T_PAD, 4 * LANE), lambda b, s, pt: (b, C_DQ // 4)),
                  pl.BlockSpec((T_PAD, LANE), lambda b, s, pt: (b, C_DK)),
                  pl.BlockSpec((T_PAD, LANE), lambda b, s, pt: (b, C_DV)),
                  pl.BlockSpec((1, pps, T_PAD, LANE), lambda b, s, pt: (b, s, 0, 0)),
                  tile_spec, tile_spec, tile_spec,
                  pl.BlockSpec(bias_last.shape, lambda b, s, pt: (0, 0)),
                  pl.BlockSpec(bias_new.shape, lambda b, s, pt: (0, 0))]
        + [page_spec(i) for i in range(pps)] + [page_spec(i) for i in range(pps)],
        out_specs=pl.BlockSpec((T_PAD, 4 * LANE), lambda b, s, pt: (b, 0)),
        scratch_shapes=[pltpu.VMEM((rows, 1), F32), pltpu.VMEM((rows, 1), F32),
                        pltpu.VMEM((rows, HEAD_DIM), F32)])
    return pl.pallas_call(
        functools.partial(_dsa_att_body, pps=pps, n_steps=n_steps),
        grid_spec=grid_spec,
        out_shape=jax.ShapeDtypeStruct((B * T_PAD, 4 * LANE), F32),
        compiler_params=_cparams(("parallel", "arbitrary")),
        name="dsa_attend_decode")(page_table, P, P, P, keys, keyn, thr, pos, bias_last, bias_new,
                                  *([cache_k] * pps), *([cache_v] * pps))


def _bias_minus_far(tab, dist):
    onehot = jax.nn.one_hot(_rel_bucket(dist), N_BUCKETS, dtype=F32)
    bias = jnp.einsum("...k,km->...m", onehot, tab, precision=lax.Precision.HIGHEST)
    return bias - tab[N_BUCKETS - 1]


def _toeplitz_tiles(tab, t):
    r = jnp.arange(t)[:, None]
    c = jnp.arange(t)[None, :]
    tiles = jnp.stack([_bias_minus_far(tab, r - c), _bias_minus_far(tab, t + r - c)], axis=0)
    return jnp.transpose(tiles, (3, 0, 1, 2))


def _decode_bias(tab, past, t_real):
    tq = jnp.arange(T_PAD)[:, None]
    kc = jnp.arange(LANE)[None, :]
    last = _bias_minus_far(tab, past + tq - (past - PAGE + kc))
    new = _bias_minus_far(tab, tq - kc)
    valid = jnp.logical_and(kc <= tq, kc < t_real)
    new = jnp.where(valid[..., None], new, NEG_INF)
    return jnp.transpose(last, (2, 0, 1)), jnp.transpose(new, (2, 0, 1))


def _block_diag(w):
    n, d, e = w.shape
    eye = jnp.eye(n, dtype=w.dtype)
    return (eye[:, None, :, None] * w[:, :, None, :]).reshape(n * d, n * e)


def kernel(x_prompt, x_sample, cache_a_k, cache_a_v, cache_d_k, cache_d_v, cache_d_idx, state_b_h, state_b_conv, state_c_s, page_table, rel_bias, ln1, w_in, a_q_norm, a_k_norm, a_lam_q1, a_lam_k1, a_lam_q2, a_lam_k2, a_out_norm, b_conv_w, b_conv_b, b_wa, b_ba, b_wx, b_bx, b_lambda, c_lb_logits, c_out_norm, d_q_norm, d_k_norm, w_out, ln2, w_up, w_down):
    B, T, D = x_prompt.shape
    Bs, Ts, _ = x_sample.shape
    L = w_in.shape[0]
    n_pages = page_table.shape[1]
    past = n_pages * PAGE
    gw = D // 4
    a_heads = gw // (2 * HEAD_DIM)
    c_heads = c_lb_logits.shape[1] // LANE
    d_heads = gw // HEAD_DIM
    d_kv = cache_d_k.shape[3]
    d_grp = d_heads // d_kv
    n_pool = cache_a_k.shape[0]
    assert Ts <= T_PAD - 0 and Ts >= CONV_W - 1 and past > 0

    tab = rel_bias.astype(F32)
    n_a_maps = 2 * a_heads
    tab_a, tab_d = tab[:, :n_a_maps], tab[:, n_a_maps:]
    tq_a = min(128, T)
    tq_d = min(128, T)
    bias_a = jnp.transpose(_toeplitz_tiles(tab_a, tq_a).reshape(a_heads, 2, 2, tq_a, tq_a),
                           (0, 2, 3, 1, 4)).reshape(a_heads, 2, tq_a, 2 * tq_a)
    causal = jnp.where(jnp.arange(tq_a)[None, :] <= jnp.arange(tq_a)[:, None], 0.0, NEG_INF)
    causal = jnp.tile(causal, (1, 2)).astype(F32)
    bias_a = jnp.stack([jnp.full((a_heads, tq_a, 2 * tq_a), NEG_INF, F32), bias_a[:, 0] + causal,
                        bias_a[:, 1], jnp.zeros((a_heads, tq_a, 2 * tq_a), F32)], axis=1)
    bias_d = jnp.concatenate([_toeplitz_tiles(tab_d, tq_d),
                              jnp.zeros((d_heads, 1, tq_d, tq_d), F32)], axis=1)
    bias_d = jnp.transpose(bias_d.reshape(d_heads // 2, 2, 3, tq_d, tq_d),
                           (0, 2, 3, 1, 4)).reshape(d_heads // 2, 3, tq_d, 2 * tq_d)
    bl_a, bn_a = _decode_bias(tab_a, past, Ts)
    bl_a = jnp.repeat(bl_a.reshape(n_a_maps * T_PAD, LANE), a_heads, axis=1)
    bn_a = bn_a.reshape(n_a_maps * T_PAD, LANE)
    bl_d, bn_d = _decode_bias(tab_d, past, Ts)
    bl_d = bl_d.reshape(d_heads * T_PAD, LANE)
    bn_d = bn_d.reshape(d_heads * T_PAD, LANE)

    lb_cum = jnp.cumsum(jax.nn.softmax(c_lb_logits.astype(F32), axis=0), axis=0)
    lb_all = (lb_cum - lb_cum[0]).reshape(L, c_heads, 1, LANE)

    seg = jnp.kron(jnp.eye(LANE // HEAD_DIM, dtype=F32),
                   jnp.full((HEAD_DIM, HEAD_DIM), 1.0 / HEAD_DIM, F32)).astype(BF16)
    qscale = HEAD_DIM ** -0.5
    zeros = lambda n: jnp.zeros((n,), F32)
    ones = lambda n: jnp.ones((n,), F32)
    flag = jnp.concatenate([ones(2 * gw), zeros(7 * gw), ones(gw), ones(LANE),
                            zeros(P_WIDTH - 10 * gw - LANE)]).reshape(1, P_WIDTH)
    w_in_t = jnp.transpose(w_in, (0, 2, 1))
    n_in = w_in_t.shape[1]
    tail0 = (P_WIDTH // IN_PROJ_TN - 1) * IN_PROJ_TN
    assert tail0 < n_in <= P_WIDTH
    w_in_tail = jnp.pad(w_in_t[:, tail0:], ((0, 0), (0, tail0 + IN_PROJ_TN - n_in), (0, 0)))

    ca_k = cache_a_k.reshape(n_pool, L, PAGE * a_heads, 2 * HEAD_DIM)
    ca_v = cache_a_v.reshape(n_pool, L, PAGE * a_heads, 2 * HEAD_DIM)
    cd_k = jnp.transpose(cache_d_k, (0, 1, 3, 4, 2))
    cd_v = jnp.transpose(cache_d_v, (0, 1, 3, 4, 2))
    cd_idx = jnp.transpose(cache_d_idx, (0, 1, 3, 2))

    xp = x_prompt.reshape(B * T, D)
    xs = jnp.pad(x_sample, ((0, 0), (0, T_PAD - Ts), (0, 0))).reshape(Bs * T_PAD, D)
    s_conv = jnp.pad(state_b_conv, ((0, 0), (0, 0), (SUBLANE - (CONV_W - 1), 0), (0, 0)))

    n_top_p = min(TOPK_MAX, T // 4)
    n_top_s = min(TOPK_MAX, (past + Ts) // 4)
    tm_p = min(1024, B * T)
    tm_f = min(1024, B * T)
    tt_b = min(256, T)
    tt_c = min(512, T)
    ch_c = math.gcd(T, 64)
    pps_a = math.gcd(n_pages, 32)
    pps_d = math.gcd(n_pages, 32)
    pps_att = pps_d

    outs_p, outs_s = [], []
    for l in range(L):
        gain = jnp.concatenate(
            [jnp.tile(a_q_norm[l], 2 * a_heads) * qscale, jnp.tile(a_k_norm[l], 2 * a_heads),
             ones(7 * gw), jnp.tile(d_q_norm[l], d_heads) * qscale,
             jnp.tile(d_k_norm[l], d_kv), ones(P_WIDTH - 10 * gw - LANE)]).reshape(1, P_WIDTH)
        g1 = ln1[l].reshape(1, D)
        g2 = ln2[l].reshape(1, D)
        lam_init = 0.8 - 0.6 * math.exp(-0.3 * l)
        lam = (jnp.exp(jnp.sum(a_lam_q1[l].astype(F32) * a_lam_k1[l].astype(F32)))
               - jnp.exp(jnp.sum(a_lam_q2[l].astype(F32) * a_lam_k2[l].astype(F32))) + lam_init)
        scal = jnp.stack([lam, jnp.asarray(1.0 - lam_init, F32)]).astype(F32)
        g_a = a_out_norm[l].reshape(1, LANE)
        g_c = c_out_norm[l].reshape(1, LANE)
        b_weights = (b_conv_w[l], b_conv_b[l].reshape(1, gw), _block_diag(b_wa[l]).astype(BF16),
                     b_ba[l].reshape(1, gw), _block_diag(b_wx[l]).astype(BF16),
                     b_bx[l].reshape(1, gw), b_lambda[l].reshape(1, gw))

        def dense_tail(x, mixes, tm, tmf):
            x1 = _out_proj(mixes, w_out, l, x, tm=tm, tn=min(512, D))
            return _ffn(x1, g2, w_up, w_down, l, tm=tmf, tf=min(512, w_up.shape[2]))

        Pp, ak_p, av_p = _in_proj(xp, g1, w_in_t, w_in_tail, l, gain, flag, seg, tm=tm_p)
        mix_a = _attn_a_prompt(Pp, scal, bias_a, g_a, B=B, T=T, tq=tq_a)
        mix_b, hfin, buf = _rglru(Pp, None, b_weights, B=B, T=T, tt=tt_b, t_real=T, out_dtype=BF16)
        mix_c, sfin = _hgrn(Pp, None, lb_all[l], g_c, B=B, T=T, tt=tt_c, ch=ch_c,
                            sub=min(16, ch_c), t_real=T, out_dtype=BF16)
        mix_d = _dsa_prompt(Pp, bias_d, B=B, T=T, tq=tq_d, n_top=n_top_p)
        xp = dense_tail(xp, (mix_a, mix_b, mix_c, mix_d), tm_p, tm_f)
        P3 = Pp.reshape(B, T, P_WIDTH)
        outs_p.append((ak_p.reshape(B, T, a_heads, 2 * HEAD_DIM),
                       av_p.reshape(B, T, a_heads, 2 * HEAD_DIM),
                       P3[..., C_DK * LANE:C_DV * LANE].reshape(B, T, d_kv, HEAD_DIM),
                       P3[..., C_DV * LANE:(C_DV + 1) * LANE].reshape(B, T, d_kv, HEAD_DIM),
                       P3[..., C_TAIL * LANE:C_TAIL * LANE + HEAD_DIM],
                       hfin.reshape(B, gw), buf, sfin))

        Ps, ak_s, av_s = _in_proj(xs, g1, w_in_t, w_in_tail, l, gain, flag, seg, tm=Bs * T_PAD)
        smix_a = _attn_a_decode(Ps, ca_k, ca_v, page_table, l, scal, bl_a, bn_a, g_a, B=Bs, pps=pps_a)
        smix_b, shfin, sbuf = _rglru(
            Ps, (state_b_h[:, l].reshape(Bs, 1, gw), s_conv[:, l]), b_weights,
            B=Bs, T=T_PAD, tt=T_PAD, t_real=Ts, out_dtype=F32)
        smix_c, ssfin = _hgrn(Ps, state_c_s[:, l], lb_all[l], g_c, B=Bs, T=T_PAD, tt=T_PAD,
                              ch=T_PAD, sub=T_PAD, t_real=Ts, out_dtype=F32)
        keys, keyn, thr, pos = _dsa_select_decode(Ps, cd_idx, page_table, l, B=Bs, pps=pps_d,
                                                  n_top=n_top_s, t_real=Ts)
        smix_d = _dsa_attend_decode(Ps, cd_k, cd_v, page_table, l, keys, keyn, thr, pos,
                                    bl_d, bn_d, B=Bs, pps=pps_att)
        xs = dense_tail(xs, (smix_a, smix_b, smix_c, smix_d), Bs * T_PAD, Bs * T_PAD)
        S3 = Ps.reshape(Bs, T_PAD, P_WIDTH)[:, :Ts]
        outs_s.append((ak_s.reshape(Bs, T_PAD, a_heads, 2 * HEAD_DIM)[:, :Ts],
                       av_s.reshape(Bs, T_PAD, a_heads, 2 * HEAD_DIM)[:, :Ts],
                       S3[..., C_DK * LANE:C_DV * LANE].reshape(Bs, Ts, d_kv, HEAD_DIM),
                       S3[..., C_DV * LANE:(C_DV + 1) * LANE].reshape(Bs, Ts, d_kv, HEAD_DIM),
                       S3[..., C_TAIL * LANE:C_TAIL * LANE + HEAD_DIM],
                       shfin.reshape(Bs, gw), sbuf, ssfin))

    y_prompt = xp.reshape(B, T, D)
    y_sample = xs.reshape(Bs, T_PAD, D)[:, :Ts]
    stack = lambda outs: [jnp.stack(s, axis=1) for s in zip(*outs)]
    return (y_prompt, y_sample, *stack(outs_p), *stack(outs_s))
```

```python
import functools
import math

import jax
import jax.numpy as jnp
import numpy as np
from jax import lax
from jax.experimental import pallas as pl
from jax.experimental.pallas import tpu as pltpu

F32 = jnp.float32
BF16 = jnp.bfloat16
I32 = jnp.int32

EPS = 1e-6
HEAD_DIM = 64
PAGE = 128
CONV_W = 4
LRU_C = 8.0
IDX_HEADS = 8
TOPK_MAX = 256
N_BUCKETS = 32
MAX_DIST = 128
LANE = 128
SUBLANE = 8
T_PAD = 8
INT_MIN = -(2 ** 31)
KEY_NEG_INF = -2139095041
VMEM_LIMIT = 56 * 1024 * 1024
NEG_INF = float("-inf")

C_AQ, C_AK, C_AV, C_BX, C_BG, C_CQ, C_CF, C_CI, C_CG, C_DQ = (4 * i for i in range(10))
C_DK, C_DV, C_IQ, C_TAIL = 40, 41, 42, 46
P_WIDTH = 48 * LANE
IW_OFF = 64
D_BLOCK = 8 * LANE
D_BLOCK_IQ = (C_IQ - C_DK) * LANE
D_BLOCK_TAIL = (C_TAIL - C_DK) * LANE


def _cparams(sem):
    return pltpu.CompilerParams(dimension_semantics=sem, vmem_limit_bytes=VMEM_LIMIT)


def _nt(a, b):
    return lax.dot_general(a, b, (((1,), (1,)), ((), ())), preferred_element_type=F32)


def _tn(a, b):
    return lax.dot_general(a, b, (((0,), (0,)), ((), ())), preferred_element_type=F32)


def _rel_bucket(dist):
    n = jnp.maximum(dist, 0)
    exact = N_BUCKETS // 2
    large = exact + (jnp.log(jnp.maximum(n, 1).astype(F32) / exact)
                     / math.log(MAX_DIST / exact) * (N_BUCKETS - exact)).astype(I32)
    return jnp.where(n < exact, n, jnp.minimum(large, N_BUCKETS - 1))


def _in_proj_body(x_ref, g_ref, w_ref, wtail_ref, gain_ref, flag_ref, seg_ref, o_ref, ak_ref, av_ref,
                  xn_ref, *, tn, norm_tiles, head_tiles):
    j = pl.program_id(1)
    n_sub = tn // LANE
    tm = o_ref.shape[0]

    def head_rows(dst_ref):
        for c in range(n_sub):
            dst_ref[pl.ds(c, tm, stride=n_sub), :] = o_ref[:, c * LANE:(c + 1) * LANE]

    @pl.when(j == 0)
    def _():
        x = x_ref[...]
        ms = jnp.mean(x * x, axis=-1, keepdims=True)
        xn_ref[...] = (x * lax.rsqrt(ms + EPS) * g_ref[...]).astype(BF16)

    w = jnp.where(j == pl.num_programs(1) - 1, wtail_ref[...], w_ref[...])
    y = _nt(xn_ref[...], w.astype(BF16))
    is_norm = functools.reduce(jnp.logical_or, [j == t for t in norm_tiles])

    @pl.when(is_norm)
    def _():
        seg = seg_ref[...]
        for c in range(tn // LANE):
            sl = slice(c * LANE, (c + 1) * LANE)
            yc = y[:, sl]
            y2 = yc * yc
            hi = y2.astype(BF16)
            lo = (y2 - hi.astype(F32)).astype(BF16)
            ms = (jnp.dot(hi, seg, preferred_element_type=F32)
                  + jnp.dot(lo, seg, preferred_element_type=F32))
            yn = yc * lax.rsqrt(ms + EPS) * gain_ref[:, sl]
            o_ref[:, sl] = jnp.where(flag_ref[:, sl] > 0, yn, yc)

    @pl.when(jnp.logical_not(is_norm))
    def _():
        o_ref[...] = y

    @pl.when(j == head_tiles[0])
    def _():
        head_rows(ak_ref)

    @pl.when(j == head_tiles[1])
    def _():
        head_rows(av_ref)


IN_PROJ_TN = 4 * LANE


def _in_proj(x, g, wt, wtail, layer, gain, flag, seg, *, tm):
    M, K = x.shape
    tn = IN_PROJ_TN
    n_tiles = P_WIDTH // tn
    norm_ranges = ((C_AQ, 4), (C_AK, 4), (C_DQ, 4), (C_DK, 1))
    norm_tiles = tuple(sorted({t for c, n in norm_ranges
                               for t in range(c * LANE // tn, ((c + n) * LANE - 1) // tn + 1)}))

    n_sub = tn // LANE
    assert C_AK % n_sub == 0 and C_AV % n_sub == 0
    head_spec = pl.BlockSpec((tm * n_sub, LANE), lambda i, j: (i, 0))
    head_shape = jax.ShapeDtypeStruct((M * n_sub, LANE), F32)
    return pl.pallas_call(
        functools.partial(_in_proj_body, tn=tn, norm_tiles=norm_tiles,
                          head_tiles=(C_AK // n_sub, C_AV // n_sub)),
        grid=(M // tm, n_tiles),
        in_specs=[pl.BlockSpec((tm, K), lambda i, j: (i, 0)),
                  pl.BlockSpec((1, K), lambda i, j: (0, 0)),
                  pl.BlockSpec((None, tn, K), lambda i, j: (layer, jnp.minimum(j, n_tiles - 2), 0)),
                  pl.BlockSpec((None, tn, K), lambda i, j: (layer, 0, 0)),
                  pl.BlockSpec((1, tn), lambda i, j: (0, j)),
                  pl.BlockSpec((1, tn), lambda i, j: (0, j)),
                  pl.BlockSpec((LANE, LANE), lambda i, j: (0, 0))],
        out_specs=[pl.BlockSpec((tm, tn), lambda i, j: (i, j)), head_spec, head_spec],
        out_shape=[jax.ShapeDtypeStruct((M, P_WIDTH), F32), head_shape, head_shape],
        scratch_shapes=[pltpu.VMEM((tm, K), BF16)],
        compiler_params=_cparams(("parallel", "arbitrary")),
        name="in_proj")(x, g, wt, wtail, gain, flag, seg)


def _out_proj_body(a_ref, b_ref, c_ref, d_ref, w_ref, x_ref, o_ref, wb_ref, *, gw):
    @pl.when(pl.program_id(1) == 0)
    def _():
        wb_ref[...] = w_ref[...].astype(BF16)

    acc = x_ref[...]
    for g, m_ref in enumerate((a_ref, b_ref, c_ref, d_ref)):
        acc = acc + jnp.dot(m_ref[...].astype(BF16), wb_ref[g * gw:(g + 1) * gw, :],
                            preferred_element_type=F32)
    o_ref[...] = acc


def _out_proj(mixes, w, layer, x, *, tm, tn):
    M, D = x.shape
    gw = mixes[0].shape[1]
    mix_spec = pl.BlockSpec((tm, gw), lambda j, i: (i, 0))
    return pl.pallas_call(
        functools.partial(_out_proj_body, gw=gw),
        grid=(D // tn, M // tm),
        in_specs=[mix_spec, mix_spec, mix_spec, mix_spec,
                  pl.BlockSpec((None, w.shape[1], tn), lambda j, i: (layer, 0, j)),
                  pl.BlockSpec((tm, tn), lambda j, i: (i, j))],
        out_specs=pl.BlockSpec((tm, tn), lambda j, i: (i, j)),
        out_shape=jax.ShapeDtypeStruct((M, D), F32),
        scratch_shapes=[pltpu.VMEM((w.shape[1], tn), BF16)],
        compiler_params=_cparams(("parallel", "arbitrary")),
        name="out_proj")(*mixes, w, x)


def _ffn_body(x_ref, g_ref, wu_ref, wd_ref, o_ref, xn_ref):
    f = pl.program_id(1)

    @pl.when(f == 0)
    def _():
        x = x_ref[...]
        ms = jnp.mean(x * x, axis=-1, keepdims=True)
        xn_ref[...] = (x * lax.rsqrt(ms + EPS) * g_ref[...]).astype(BF16)
        o_ref[...] = x

    h = jnp.dot(xn_ref[...], wu_ref[...].astype(BF16), preferred_element_type=F32)
    h = jnp.maximum(h, 0.0)
    h = (h * h).astype(BF16)
    o_ref[...] += jnp.dot(h, wd_ref[...].astype(BF16), preferred_element_type=F32)


def _ffn(x, g, wu, wd, layer, *, tm, tf):
    M, D = x.shape
    Fd = wu.shape[2]
    return pl.pallas_call(
        _ffn_body,
        grid=(M // tm, Fd // tf),
        in_specs=[pl.BlockSpec((tm, D), lambda i, f: (i, 0), pipeline_mode=pl.Buffered(1)),
                  pl.BlockSpec((1, D), lambda i, f: (0, 0)),
                  pl.BlockSpec((None, D, tf), lambda i, f: (layer, 0, f)),
                  pl.BlockSpec((None, tf, D), lambda i, f: (layer, f, 0))],
        out_specs=pl.BlockSpec((tm, D), lambda i, f: (i, 0)),
        out_shape=jax.ShapeDtypeStruct((M, D), F32),
        scratch_shapes=[pltpu.VMEM((tm, D), BF16)],
        compiler_params=_cparams(("parallel", "arbitrary")),
        name="ffn")(x, g, wu, wd)


def _softmax_update(state, s, v):
    m, l, acc = state
    m_new = jnp.maximum(m, jnp.max(s, axis=-1, keepdims=True))
    alpha = jnp.exp(m - m_new)
    p = jnp.exp(s - m_new)
    l = alpha * l + jnp.sum(p, axis=-1, keepdims=True)
    acc = alpha * acc + jnp.dot(p.astype(BF16), v, preferred_element_type=F32)
    return m_new, l, acc


def _cumsum_rows(x):
    n, w = x.shape
    rowt = lax.broadcasted_iota(I32, (SUBLANE, w), 0)
    off = jnp.zeros((1, w), F32)
    outs = []
    for b in range(n // SUBLANE):
        t = x[b * SUBLANE:(b + 1) * SUBLANE]
        for s in (1, 2, 4):
            t = jnp.where(rowt >= s, t + pltpu.roll(t, s, 0), t)
        t = t + off
        off = t[SUBLANE - 1:SUBLANE]
        outs.append(t)
    return jnp.concatenate(outs, axis=0) if len(outs) > 1 else outs[0]


def _rms_rows(o, gain):
    ms = jnp.mean(o * o, axis=-1, keepdims=True)
    return o * lax.rsqrt(ms + EPS) * gain


def _pair_attention(n_grp, tile_grp, n_pair, tq, logits, values, s_ref, m_ref, l_ref, acc_ref,
                    tile_aux=None):
    def max_pass(i, carry):
        cs = [i * tile_grp + j for j in range(tile_grp)]
        aux = [tile_aux(c) for c in cs] if tile_aux is not None else None
        for p in range(n_pair):
            t = []
            for j, c in enumerate(cs):
                s = logits(p, c) if aux is None else logits(p, c, aux[j])
                s_ref[p, c] = s
                t.append(s)
            while len(t) > 1:
                t = [jnp.maximum(a, b) for a, b in zip(t[::2], t[1::2])]
            m_ref[p] = jnp.maximum(m_ref[p], t[0])
        return carry

    m_ref[...] = jnp.full(m_ref.shape, NEG_INF, F32)
    lax.fori_loop(0, n_grp, max_pass, 0)
    for p in range(n_pair):
        m = m_ref[p]
        m_ref[p] = jnp.concatenate(
            [jnp.broadcast_to(jnp.max(m[:, :LANE], axis=-1, keepdims=True), (tq, LANE)),
             jnp.broadcast_to(jnp.max(m[:, LANE:], axis=-1, keepdims=True), (tq, LANE))], axis=1)
    l_ref[...] = jnp.zeros(l_ref.shape, F32)
    acc_ref[...] = jnp.zeros(acc_ref.shape, F32)

    def sum_pass(i, carry):
        for p in range(n_pair):
            e = [jnp.exp(s_ref[p, i * tile_grp + j] - m_ref[p]) for j in range(tile_grp)]
            l_ref[p] += functools.reduce(lambda a, b: a + b, e)
            pb = jnp.concatenate(
                [jnp.concatenate([x[:, :LANE], x[:, LANE:]], axis=0).astype(BF16) for x in e], axis=1)
            acc_ref[p] += jnp.dot(pb, values(p, i), preferred_element_type=F32)
        return carry

    lax.fori_loop(0, n_grp, sum_pass, 0)
    outs = []
    for p in range(n_pair):
        l = l_ref[p]
        acc = acc_ref[p]
        outs.append((acc[:tq] / jnp.sum(l[:, :LANE], axis=-1, keepdims=True),
                     acc[tq:] / jnp.sum(l[:, LANE:], axis=-1, keepdims=True)))
    return outs


def _attn_a_body(scal_ref, q_ref, k_ref, v_ref, bias_ref, gout_ref, o_ref,
                 kd_ref, vb_ref, s_ref, m_ref, l_ref, acc_ref, *, tq, tile_grp):
    qb = pl.program_id(1)
    n_h = q_ref.shape[1] // LANE
    n_grp = (qb + tile_grp) // tile_grp
    low_half = lax.broadcasted_iota(I32, (tq, LANE), 1) < HEAD_DIM

    @pl.when(qb == 0)
    def _():
        kd_ref[...] = jnp.zeros(kd_ref.shape, BF16)
        vb_ref[...] = jnp.zeros(vb_ref.shape, BF16)

    r_q = pl.multiple_of(qb * LANE, LANE)
    for h in range(n_h):
        kx = k_ref[pl.ds(r_q, LANE), h * LANE:(h + 1) * LANE]
        kd_ref[h, qb] = jnp.concatenate([jnp.where(low_half, kx, 0.0),
                                         jnp.where(low_half, 0.0, kx)], axis=0).astype(BF16)
    vb_ref[pl.ds(r_q, LANE), :] = v_ref[pl.ds(r_q, LANE), :].astype(BF16)
    q = q_ref[...].astype(BF16)

    def logits(h, c):
        return (_nt(q[:, h * LANE:(h + 1) * LANE], kd_ref[h, c])
                + bias_ref[h, jnp.clip(qb - c, -1, 2) + 1])

    def values(h, i):
        r0 = pl.multiple_of(i * tile_grp * LANE, tile_grp * LANE)
        return vb_ref[pl.ds(r0, tile_grp * LANE), h * LANE:(h + 1) * LANE]

    outs = _pair_attention(n_grp, tile_grp, n_h, tq, logits, values, s_ref, m_ref, l_ref, acc_ref)
    lam = scal_ref[0]
    for h, (o1, o2) in enumerate(outs):
        o = _rms_rows(o1 - lam * o2, gout_ref[...]) * scal_ref[1]
        o_ref[:, h * LANE:(h + 1) * LANE] = o.astype(o_ref.dtype)


def _attn_a_prompt(P, scal, bias, gout, *, B, T, tq):
    assert tq == LANE
    H = bias.shape[0]
    W = H * LANE
    nq = T // tq
    tile_grp = math.gcd(nq, 4)
    return pl.pallas_call(
        functools.partial(_attn_a_body, tq=tq, tile_grp=tile_grp),
        grid=(B, nq),
        in_specs=[pl.BlockSpec(memory_space=pltpu.SMEM),
                  pl.BlockSpec((tq, W), lambda b, i: (b * nq + i, C_AQ // H)),
                  pl.BlockSpec((T, W), lambda b, i: (b, C_AK // H)),
                  pl.BlockSpec((T, W), lambda b, i: (b, C_AV // H)),
                  pl.BlockSpec(bias.shape, lambda b, i: (0, 0, 0, 0)),
                  pl.BlockSpec((1, LANE), lambda b, i: (0, 0))],
        out_specs=pl.BlockSpec((tq, W), lambda b, i: (b * nq + i, 0)),
        out_shape=jax.ShapeDtypeStruct((B * T, W), BF16),
        scratch_shapes=[pltpu.VMEM((H, nq, 2 * LANE, LANE), BF16),
                        pltpu.VMEM((T, W), BF16),
                        pltpu.VMEM((H, nq, tq, 2 * LANE), F32),
                        pltpu.VMEM((H, tq, 2 * LANE), F32),
                        pltpu.VMEM((H, tq, 2 * LANE), F32),
                        pltpu.VMEM((H, 2 * tq, LANE), F32)],
        compiler_params=_cparams(("parallel", "arbitrary")),
        name="attn_a_prompt")(scal, P, P, P, bias, gout)


def _rglru_body(*refs, tt, t_real, n_t, has_state, pos0_is_zero):
    if has_state:
        (x_ref, g_ref, h0_ref, buf0_ref, cw_ref, cb_ref, wa_ref, ba_ref, wx_ref, bxb_ref, lam_ref,
         o_ref, hfin_ref, buf_ref, xpad_ref, a_ref, b_ref, hs_ref, hc_ref) = refs
    else:
        (x_ref, g_ref, cw_ref, cb_ref, wa_ref, ba_ref, wx_ref, bxb_ref, lam_ref,
         o_ref, hfin_ref, buf_ref, xpad_ref, a_ref, b_ref, hs_ref, hc_ref) = refs
    ti = pl.program_id(1)
    W = x_ref.shape[1]

    @pl.when(ti == 0)
    def _():
        if has_state:
            xpad_ref[0:SUBLANE, :] = buf0_ref[0]
            hc_ref[...] = jnp.broadcast_to(h0_ref[0], (SUBLANE, W))
        else:
            xpad_ref[0:SUBLANE, :] = jnp.zeros((SUBLANE, W), F32)
            hc_ref[...] = jnp.zeros((SUBLANE, W), F32)

    x = x_ref[...]
    xpad_ref[SUBLANE:SUBLANE + tt, :] = x
    xc = cb_ref[...] + x * cw_ref[CONV_W - 1:CONV_W, :]
    for j in range(CONV_W - 1):
        off = SUBLANE - (CONV_W - 1) + j
        xc = xc + xpad_ref[off:off + tt, :] * cw_ref[j:j + 1, :]
    xcb = xc.astype(BF16)
    r = jax.nn.sigmoid(jnp.dot(xcb, wa_ref[...], preferred_element_type=F32) + ba_ref[...])
    i = jax.nn.sigmoid(jnp.dot(xcb, wx_ref[...], preferred_element_type=F32) + bxb_ref[...])
    nl = -lam_ref[...]
    softplus = jnp.maximum(nl, 0.0) + jnp.log1p(jnp.exp(-jnp.abs(nl)))
    a = jnp.exp(-LRU_C * r * softplus)
    mult = jnp.sqrt(1.0 - a * a)
    if pos0_is_zero:
        rows = lax.broadcasted_iota(I32, (tt, W), 0)
        mult = jnp.where(jnp.logical_and(rows == 0, ti == 0), 1.0, mult)
    a_ref[...] = a
    b_ref[...] = mult * (i * xc)
    rowt = lax.broadcasted_iota(I32, (SUBLANE, W), 0)

    def tile(n, h_prev):
        r0 = pl.multiple_of(n * SUBLANE, SUBLANE)
        at = a_ref[pl.ds(r0, SUBLANE), :]
        bt = b_ref[pl.ds(r0, SUBLANE), :]
        for s in (1, 2, 4):
            keep = rowt >= s
            bt = jnp.where(keep, bt + at * pltpu.roll(bt, s, 0), bt)
            at = jnp.where(keep, at * pltpu.roll(at, s, 0), at)
        ht = bt + at * h_prev
        hs_ref[pl.ds(r0, SUBLANE), :] = ht
        return jnp.broadcast_to(ht[SUBLANE - 1:SUBLANE, :], (SUBLANE, W))

    hc_ref[...] = lax.fori_loop(0, tt // SUBLANE, tile, hc_ref[...])
    o_ref[...] = (hs_ref[...] * jax.nn.gelu(g_ref[...])).astype(o_ref.dtype)
    xpad_ref[0:SUBLANE, :] = xpad_ref[tt:tt + SUBLANE, :]

    @pl.when(ti == n_t - 1)
    def _():
        t_loc = t_real - (n_t - 1) * tt
        hfin_ref[0] = hs_ref[t_loc - 1:t_loc, :]
        buf_ref[0] = x_ref[t_loc - (CONV_W - 1):t_loc, :]


def _rglru(P, state, weights, *, B, T, tt, t_real, out_dtype):
    W = 4 * LANE
    n_t = T // tt
    has_state = state is not None
    xspec = pl.BlockSpec((tt, W), lambda b, i: (b * n_t + i, C_BX // 4))
    gspec = pl.BlockSpec((tt, W), lambda b, i: (b * n_t + i, C_BG // 4))
    full = lambda shape: pl.BlockSpec(shape, lambda b, i: (0,) * len(shape))
    wspecs = [full((CONV_W, W)), full((1, W)), full((W, W)), full((1, W)), full((W, W)),
              full((1, W)), full((1, W))]
    in_specs = [xspec, gspec]
    args = [P, P]
    if has_state:
        in_specs += [pl.BlockSpec((1, 1, W), lambda b, i: (b, 0, 0)),
                     pl.BlockSpec((1, SUBLANE, W), lambda b, i: (b, 0, 0))]
        args += list(state)
    return pl.pallas_call(
        functools.partial(_rglru_body, tt=tt, t_real=t_real, n_t=n_t, has_state=has_state,
                          pos0_is_zero=not has_state),
        grid=(B, n_t),
        in_specs=in_specs + wspecs,
        out_specs=[pl.BlockSpec((tt, W), lambda b, i: (b * n_t + i, 0)),
                   pl.BlockSpec((1, 1, W), lambda b, i: (b, 0, 0)),
                   pl.BlockSpec((1, CONV_W - 1, W), lambda b, i: (b, 0, 0))],
        out_shape=[jax.ShapeDtypeStruct((B * T, W), out_dtype),
                   jax.ShapeDtypeStruct((B, 1, W), F32),
                   jax.ShapeDtypeStruct((B, CONV_W - 1, W), F32)],
        scratch_shapes=[pltpu.VMEM((tt + SUBLANE, W), F32), pltpu.VMEM((tt, W), F32),
                        pltpu.VMEM((tt, W), F32), pltpu.VMEM((tt, W), F32),
                        pltpu.VMEM((SUBLANE, W), F32)],
        compiler_params=_cparams(("parallel", "arbitrary")),
        name="rglru")(*args, *weights)


def _hgrn_body(*refs, tt, ch, sub, t_real, n_t, has_state):
    if has_state:
        q_ref, f_ref, v_ref, g_ref, s0_ref, lb_ref, gain_ref, o_ref, sfin_ref, st_ref = refs
    else:
        q_ref, f_ref, v_ref, g_ref, lb_ref, gain_ref, o_ref, sfin_ref, st_ref = refs
    ti = pl.program_id(1)
    n_h = st_ref.shape[0]
    dk = LANE

    @pl.when(ti == 0)
    def _():
        for h in range(n_h):
            if has_state:
                st_ref[h] = s0_ref[0, h].T
            else:
                st_ref[h] = jnp.zeros((dk, dk), F32)

    rr = lax.broadcasted_iota(I32, (ch, ch), 0)
    cc = lax.broadcasted_iota(I32, (ch, ch), 1)
    tril = (cc <= rr).astype(F32)
    rows = lax.broadcasted_iota(I32, (ch, dk), 0)
    srow = lax.broadcasted_iota(I32, (sub, 1), 0)
    n_sub = ch // sub

    def head_chunk(c, h):
        r0 = pl.multiple_of(c * ch, ch)
        hs = slice(h * dk, (h + 1) * dk)
        lb = lb_ref[h]
        log_lb = jnp.log(lb)
        log_1mlb = jnp.log1p(-lb)
        q = q_ref[pl.ds(r0, ch), hs]
        q = q * jax.nn.sigmoid(q)
        fp = f_ref[pl.ds(r0, ch), hs]
        v = v_ref[pl.ds(r0, ch), hs]
        log_sig = jnp.minimum(fp, 0.0) - jnp.log1p(jnp.exp(-jnp.abs(fp)))
        b = log_1mlb + log_sig
        lf = jnp.maximum(log_lb, b) + jnp.log1p(jnp.exp(-jnp.abs(log_lb - b)))
        kk = (1.0 - lb) * jax.nn.sigmoid(-fp)
        if t_real < tt * n_t:
            live = (rows + (ti * tt + c * ch)) < t_real
            lf = jnp.where(live, lf, 0.0)
            kk = jnp.where(live, kk, 0.0)
        G = _cumsum_rows(lf)
        st = st_ref[h]
        vb = v.astype(BF16)
        o = _nt((q * jnp.exp(G)).astype(BF16), st.astype(BF16))
        outs = []
        for i in range(n_sub):
            lo, hi = i * sub, (i + 1) * sub
            qi, Gi, ki, vi = q[lo:hi], G[lo:hi], kk[lo:hi], v[lo:hi]
            oi = o[lo:hi]
            if i > 0:
                R = G[lo - 1:lo]
                qp = (qi * jnp.exp(Gi - R)).astype(BF16)
                kp = (kk[:lo] * jnp.exp(R - G[:lo])).astype(BF16)
                att = _nt(qp, kp)
                oi = oi + jnp.dot(att.astype(BF16), vb[:lo], preferred_element_type=F32)
            for s in range(sub):
                w = jnp.exp(jnp.minimum(Gi - Gi[s:s + 1], 0.0))
                colv = jnp.sum(qi * w * ki[s:s + 1], axis=-1, keepdims=True)
                colv = jnp.where(srow >= s, colv, 0.0)
                oi = oi + colv * vi[s:s + 1]
            outs.append(oi)
        o = jnp.concatenate(outs, axis=0) if n_sub > 1 else outs[0]
        gl = G[ch - 1:ch]
        kpp = (kk * jnp.exp(gl - G)).astype(BF16)
        st_ref[h] = st * jnp.exp(gl) + _tn(vb, kpp)
        gate = g_ref[pl.ds(r0, ch), hs]
        o = _rms_rows(o, gain_ref[...]) * (gate * jax.nn.sigmoid(gate))
        o_ref[pl.ds(r0, ch), hs] = o.astype(o_ref.dtype)

    n_chunks = tt // ch
    per_iter = 2 if n_chunks % 2 == 0 else 1

    def chunk(c, carry):
        for u in range(per_iter):
            for h in range(n_h):
                head_chunk(c * per_iter + u, h)
        return carry

    lax.fori_loop(0, n_chunks // per_iter, chunk, 0)

    @pl.when(ti == n_t - 1)
    def _():
        for h in range(n_h):
            sfin_ref[0, h] = st_ref[h].T


def _hgrn(P, s0, lb, gain, *, B, T, tt, ch, sub, t_real, out_dtype):
    H = lb.shape[0]
    n_t = T // tt
    has_state = s0 is not None

    W = H * LANE

    def col(c0):
        return pl.BlockSpec((tt, W), lambda b, i: (b * n_t + i, c0 // H))

    in_specs = [col(C_CQ), col(C_CF), col(C_CI), col(C_CG)]
    args = [P, P, P, P]
    if has_state:
        in_specs.append(pl.BlockSpec((1, H, LANE, LANE), lambda b, i: (b, 0, 0, 0)))
        args.append(s0)
    in_specs += [pl.BlockSpec((H, 1, LANE), lambda b, i: (0, 0, 0)),
                 pl.BlockSpec((1, LANE), lambda b, i: (0, 0))]
    return pl.pallas_call(
        functools.partial(_hgrn_body, tt=tt, ch=ch, sub=sub, t_real=t_real, n_t=n_t,
                          has_state=has_state),
        grid=(B, n_t),
        in_specs=in_specs,
        out_specs=[pl.BlockSpec((tt, W), lambda b, i: (b * n_t + i, 0)),
                   pl.BlockSpec((1, H, LANE, LANE), lambda b, i: (b, 0, 0, 0))],
        out_shape=[jax.ShapeDtypeStruct((B * T, W), out_dtype),
                   jax.ShapeDtypeStruct((B, H, LANE, LANE), F32)],
        scratch_shapes=[pltpu.VMEM((H, LANE, LANE), F32)],
        compiler_params=_cparams(("parallel", "arbitrary")),
        name="hgrn2")(*args, lb, gain)


def _score_key(score):
    score = jnp.where(score == 0.0, 0.0, score)
    bits = pltpu.bitcast(score, I32)
    return bits ^ ((bits >> 31) & 0x7FFFFFFF)


def _kth_largest(count_ge, shape, k, bits=32):
    zero = jnp.zeros(shape, I32)
    v = jnp.where(count_ge(zero) >= k, zero, jnp.full(shape, -(2 ** (bits - 1)), I32))

    def bit_step(n, v):
        cand = v + (jnp.int32(1) << (bits - 2 - n))
        return jnp.where(count_ge(cand) >= k, cand, v)

    return lax.fori_loop(0, bits - 1, bit_step, v)


def _tie_bound(count_tie_below, r, shape, n_bits):
    def bit_step(n, pos):
        cand = pos + (jnp.int32(1) << (n_bits - 1 - n))
        return jnp.where(count_tie_below(cand) < r, cand, pos)

    return lax.fori_loop(0, n_bits, bit_step, jnp.zeros(shape, I32))


def _dsa_body(dq_ref, iq_ref, tq_ref, dk_ref, dv_ref, ik_ref, bias_ref, o_ref,
              key_ref, hi_ref, lo_ref, pos_ref, kd_ref, vb_ref, ikb_ref,
              s_ref, m_ref, l_ref, acc_ref, *, tq, n_top, n_bits, tile_grp):
    qb = pl.program_id(1)
    n_c = qb + 1
    hd = HEAD_DIM
    n_kv = dk_ref.shape[1] // hd
    n_q = dq_ref.shape[1] // hd
    grp = n_q // n_kv
    n_pair = n_q // 2
    row = lax.broadcasted_iota(I32, (tq, LANE), 0)
    col = lax.broadcasted_iota(I32, (tq, LANE), 1)
    low_half = col < hd

    @pl.when(qb == 0)
    def _():
        kd_ref[...] = jnp.zeros(kd_ref.shape, BF16)
        vb_ref[...] = jnp.zeros(vb_ref.shape, BF16)
        ikb_ref[...] = jnp.zeros(ikb_ref.shape, BF16)

    def pair_blocks(at_low, at_high):
        return jnp.concatenate([jnp.where(low_half, at_low, 0.0),
                                jnp.where(low_half, 0.0, at_high)], axis=0).astype(BF16)

    r_q = pl.multiple_of(qb * LANE, LANE)
    kx = dk_ref[pl.ds(r_q, LANE), :]
    kx_sw = pltpu.roll(kx, hd, 1)
    kd_ref[0, qb] = pair_blocks(kx, kx_sw)
    kd_ref[1, qb] = pair_blocks(kx_sw, kx)
    ikb_ref[pl.ds(r_q, LANE), :] = ik_ref[pl.ds(r_q, LANE), :].astype(BF16)
    vb_ref[pl.ds(r_q, LANE), :] = dv_ref[pl.ds(r_q, LANE), :].astype(BF16)

    iq = iq_ref[:, D_BLOCK_IQ:D_BLOCK_IQ + IDX_HEADS * hd]
    iqz = []
    for p in range(IDX_HEADS // 2):
        pair = iq[:, p * LANE:(p + 1) * LANE]
        iqz.append(jnp.concatenate([jnp.where(low_half, pair, 0.0),
                                    jnp.where(low_half, pltpu.roll(pair, hd, 1), 0.0)],
                                   axis=0).astype(BF16))
    wscale = IDX_HEADS ** -0.5 * hd ** -0.5
    tail_t = tq_ref[...].T
    w_rows = [tail_t[IW_OFF + h:IW_OFF + h + 1, :] * wscale for h in range(IDX_HEADS)]

    n_grp = (n_c + tile_grp - 1) // tile_grp

    def score_tile(c):
        r0 = pl.multiple_of(c * LANE, LANE)
        ikc = ikb_ref[pl.ds(r0, LANE), :]
        sc = None
        for p in range(IDX_HEADS // 2):
            s = jnp.maximum(_nt(ikc, iqz[p]), 0.0)
            t = s[:, :LANE] * w_rows[2 * p] + s[:, LANE:] * w_rows[2 * p + 1]
            sc = t if sc is None else sc + t
        causal = jnp.logical_or(c < qb, jnp.logical_and(c == qb, row <= col))
        key = _score_key(jnp.where(causal, sc, NEG_INF))
        key_ref[c] = key
        hi_ref[c] = (key >> 16).astype(jnp.int16)
        lo_ref[c] = (((key ^ 0x8000) << 16) >> 16).astype(jnp.int16)

    def score_group(i, carry):
        for j in range(tile_grp):
            score_tile(i * tile_grp + j)
        return carry

    lax.fori_loop(0, n_grp, score_group, 0)
    vec = (1, tq)

    def count(pred, ref=key_ref, dtype=I32):
        def body(i, acc):
            for j in range(tile_grp):
                c = i * tile_grp + j
                acc = acc + pred(ref[c], c).astype(dtype)
            return acc
        acc = lax.fori_loop(0, n_grp, body, jnp.zeros((LANE, tq), dtype))
        return jnp.sum(acc.astype(I32), axis=0, keepdims=True)

    def count16_ge(ref):
        def f(cand):
            cb = jnp.broadcast_to(cand, (LANE, tq)).astype(jnp.int16)
            return count(lambda key, c: key >= cb, ref, jnp.int16)
        return f

    min16 = -(2 ** 15)
    t_hi = _kth_largest(count16_ge(hi_ref), vec, n_top, bits=16)
    t_hi_b = jnp.broadcast_to(t_hi, (LANE, tq)).astype(jnp.int16)
    k_lo = n_top - count(lambda key, c: key > t_hi_b, hi_ref, jnp.int16)

    def band_group(i, carry):
        for j in range(tile_grp):
            c = i * tile_grp + j
            lo_ref[c] = jnp.where(hi_ref[c] == t_hi_b, lo_ref[c], jnp.int16(min16))
        return carry

    lax.fori_loop(0, n_grp, band_group, 0)
    t_lo = _kth_largest(count16_ge(lo_ref), vec, k_lo, bits=16)
    thr = (t_hi << 16) | ((t_lo ^ min16) & 0xFFFF)
    thr_b = jnp.broadcast_to(thr, (LANE, tq))
    n_gt = count(lambda key, c: key > thr_b)
    n_ge = count(lambda key, c: key >= thr_b)
    need = jnp.logical_and(n_ge > n_top, thr > KEY_NEG_INF)
    pos_ref[...] = jnp.full((LANE, tq), 2 ** n_bits, I32)

    @pl.when(jnp.max(need.astype(I32)) > 0)
    def _():
        r = n_top - n_gt

        def count_tie_below(cand):
            cb = jnp.broadcast_to(cand, (LANE, tq))
            return count(lambda key, c: jnp.logical_and(key == thr_b, row + c * LANE < cb))

        pos = _tie_bound(count_tie_below, r, vec, n_bits)
        pos_ref[...] = jnp.broadcast_to(pos, (LANE, tq))

    pos_b = pos_ref[...]

    def tile_mask(c):
        key = key_ref[c]
        idx = row + c * LANE
        sel = jnp.logical_or(key > thr_b, jnp.logical_and(key == thr_b, idx <= pos_b))
        sel = jnp.logical_and(sel, key > KEY_NEG_INF)
        mk = jnp.where(sel, 0.0, NEG_INF).T
        return jnp.concatenate([mk, mk], axis=1)

    dq = dq_ref[...].astype(BF16)

    def logits(p, c, mask):
        return (_nt(dq[:, p * LANE:(p + 1) * LANE], kd_ref[(2 * p) // grp, c])
                + mask + bias_ref[p, jnp.clip(qb - c, 0, 2)])

    def values(p, i):
        r0 = pl.multiple_of(i * tile_grp * LANE, tile_grp * LANE)
        return vb_ref[pl.ds(r0, tile_grp * LANE), :]

    outs = _pair_attention(n_grp, tile_grp, n_pair, tq, logits, values, s_ref, m_ref, l_ref, acc_ref,
                           tile_aux=tile_mask)
    for p, (oa, ob) in enumerate(outs):
        if (2 * p) // grp == 0:
            ob = pltpu.roll(ob, hd, 1)
        else:
            oa = pltpu.roll(oa, hd, 1)
        o_ref[:, p * LANE:(p + 1) * LANE] = jnp.where(low_half, oa, ob).astype(o_ref.dtype)


def _dsa_prompt(P, bias, *, B, T, tq, n_top):
    assert tq == LANE
    nq = T // tq
    n_bits = max(1, int(math.ceil(math.log2(T))))
    tile_grp = math.gcd(nq, 4)
    n_pair = bias.shape[0]
    n_tiles = T // LANE
    return pl.pallas_call(
        functools.partial(_dsa_body, tq=tq, n_top=n_top, n_bits=n_bits, tile_grp=tile_grp),
        grid=(B, nq),
        in_specs=[pl.BlockSpec((tq, 4 * LANE), lambda b, i: (b * nq + i, C_DQ // 4)),
                  pl.BlockSpec((tq, D_BLOCK), lambda b, i: (b * nq + i, C_DK * LANE // D_BLOCK)),
                  pl.BlockSpec((tq, LANE), lambda b, i: (b * nq + i, C_TAIL)),
                  pl.BlockSpec((T, LANE), lambda b, i: (b, C_DK)),
                  pl.BlockSpec((T, LANE), lambda b, i: (b, C_DV)),
                  pl.BlockSpec((T, LANE), lambda b, i: (b, C_TAIL)),
                  pl.BlockSpec(bias.shape, lambda b, i: (0, 0, 0, 0))],
        out_specs=pl.BlockSpec((tq, 4 * LANE), lambda b, i: (b * nq + i, 0)),
        out_shape=jax.ShapeDtypeStruct((B * T, 4 * LANE), BF16),
        scratch_shapes=[pltpu.VMEM((n_tiles, tq, LANE), I32),
                        pltpu.VMEM((n_tiles, tq, LANE), jnp.int16),
                        pltpu.VMEM((n_tiles, tq, LANE), jnp.int16),
                        pltpu.VMEM((tq, LANE), I32),
                        pltpu.VMEM((2, n_tiles, 2 * LANE, LANE), BF16),
                        pltpu.VMEM((T, LANE), BF16),
                        pltpu.VMEM((T, LANE), BF16),
                        pltpu.VMEM((n_pair, n_tiles, tq, 2 * LANE), F32),
                        pltpu.VMEM((n_pair, tq, 2 * LANE), F32),
                        pltpu.VMEM((n_pair, tq, 2 * LANE), F32),
                        pltpu.VMEM((n_pair, 2 * tq, LANE), F32)],
        compiler_params=_cparams(("parallel", "arbitrary")),
        name="dsa_prompt")(P, P, P, P, P, P, bias)


def _attn_a_dec_body(pt_ref, scal_ref, q_ref, kn_ref, vn_ref, bl_ref, bn_ref, gout_ref, *rest,
                     pps, n_steps):
    k_refs = rest[:pps]
    v_refs = rest[pps:2 * pps]
    o_ref, m_ref, l_ref, acc_ref = rest[2 * pps:]
    s_i = pl.program_id(1)
    n_h = q_ref.shape[1] // LANE
    rows_h = 2 * T_PAD

    @pl.when(s_i == 0)
    def _():
        m_ref[...] = jnp.full(m_ref.shape, NEG_INF, F32)
        l_ref[...] = jnp.zeros(l_ref.shape, F32)
        acc_ref[...] = jnp.zeros(acc_ref.shape, F32)

    q = q_ref[...]
    lane = lax.broadcasted_iota(I32, (T_PAD, LANE), 1)

    def q_head(h):
        qh = q[:, h * LANE:(h + 1) * LANE]
        return jnp.concatenate([jnp.where(lane < HEAD_DIM, qh, 0.0),
                                jnp.where(lane >= HEAD_DIM, qh, 0.0)], axis=0).astype(BF16)

    qs = [q_head(h) for h in range(n_h)]
    q_all = jnp.concatenate(qs, axis=0)
    is_last = s_i == n_steps - 1

    n_rows = n_h * rows_h
    page_w = PAGE * n_h
    row_head = lax.broadcasted_iota(I32, (n_rows, page_w), 0) // rows_h
    col_head = lax.broadcasted_iota(I32, (n_rows, page_w), 1) % n_h
    head_mask = jnp.where(row_head == col_head, 0.0, NEG_INF)
    last_bias = jnp.where(is_last, bl_ref[...], 0.0) + head_mask
    parts = [_nt(q_all, r[...].astype(BF16)) + (last_bias if i == pps - 1 else head_mask)
             for i, r in enumerate(k_refs)]
    v_all = jnp.concatenate([r[...].astype(BF16) for r in v_refs], axis=0)
    m, l, acc = _softmax_update((m_ref[...], l_ref[...], acc_ref[...]),
                                jnp.concatenate(parts, axis=1), v_all)
    m_ref[...] = m
    l_ref[...] = l
    acc_ref[...] = acc

    @pl.when(is_last)
    def _():
        pad = jnp.zeros((LANE - T_PAD, LANE), BF16)
        m_all, l_all, acc_all = m_ref[...], l_ref[...], acc_ref[...]
        new = []
        for h in range(n_h):
            hs = slice(h * LANE, (h + 1) * LANE)
            rs = slice(h * rows_h, (h + 1) * rows_h)
            kn = jnp.concatenate([kn_ref[:, hs].astype(BF16), pad], axis=0)
            vn = jnp.concatenate([vn_ref[:, hs].astype(BF16), pad], axis=0)
            s = _nt(qs[h], kn) + bn_ref[rs]
            new.append(_softmax_update((m_all[rs], l_all[rs], acc_all[rs]), s, vn))
        m_ref[...] = jnp.concatenate([x[0] for x in new], axis=0)
        l_ref[...] = jnp.concatenate([x[1] for x in new], axis=0)
        acc_ref[...] = jnp.concatenate([x[2] for x in new], axis=0)
        lam = scal_ref[0]
        for h in range(n_h):
            r1 = slice(h * rows_h, h * rows_h + T_PAD)
            r2 = slice(h * rows_h + T_PAD, (h + 1) * rows_h)
            o = acc_ref[r1] / l_ref[r1] - lam * (acc_ref[r2] / l_ref[r2])
            o_ref[:, h * LANE:(h + 1) * LANE] = _rms_rows(o, gout_ref[...]) * scal_ref[1]


def _attn_a_decode(P, cache_k, cache_v, page_table, layer, scal, bias_last, bias_new, gout, *, B, pps):
    n_pages = page_table.shape[1]
    n_steps = n_pages // pps
    W = 4 * LANE

    def page_spec(i):
        return pl.BlockSpec((None, None) + cache_k.shape[2:],
                            lambda b, s, pt: (pt[b, s * pps + i], layer, 0, 0))

    grid_spec = pltpu.PrefetchScalarGridSpec(
        num_scalar_prefetch=1,
        grid=(B, n_steps),
        in_specs=[pl.BlockSpec(memory_space=pltpu.SMEM),
                  pl.BlockSpec((T_PAD, W), lambda b, s, pt: (b, C_AQ // 4)),
                  pl.BlockSpec((T_PAD, W), lambda b, s, pt: (b, C_AK // 4)),
                  pl.BlockSpec((T_PAD, W), lambda b, s, pt: (b, C_AV // 4)),
                  pl.BlockSpec(bias_last.shape, lambda b, s, pt: (0, 0)),
                  pl.BlockSpec(bias_new.shape, lambda b, s, pt: (0, 0)),
                  pl.BlockSpec((1, LANE), lambda b, s, pt: (0, 0))]
        + [page_spec(i) for i in range(pps)] + [page_spec(i) for i in range(pps)],
        out_specs=pl.BlockSpec((T_PAD, W), lambda b, s, pt: (b, 0)),
        scratch_shapes=[pltpu.VMEM((bias_last.shape[0], 1), F32),
                        pltpu.VMEM((bias_last.shape[0], 1), F32),
                        pltpu.VMEM((bias_last.shape[0], LANE), F32)])
    return pl.pallas_call(
        functools.partial(_attn_a_dec_body, pps=pps, n_steps=n_steps),
        grid_spec=grid_spec,
        out_shape=jax.ShapeDtypeStruct((B * T_PAD, W), F32),
        compiler_params=_cparams(("parallel", "arbitrary")),
        name="attn_a_decode")(page_table, scal, P, P, P, bias_last, bias_new, gout,
                              *([cache_k] * pps), *([cache_v] * pps))


def _dsa_sel_body(pt_ref, iq_ref, tq_ref, ikn_ref, *rest, pps, n_steps, n_top, t_real, n_bits):
    ik_refs = rest[:pps]
    keys_ref, keyn_ref, thr_ref, pos_ref, all_ref, wb_ref = rest[pps:]
    s_i = pl.program_id(1)
    hd = HEAD_DIM
    iq = iq_ref[:, D_BLOCK_IQ:D_BLOCK_IQ + IDX_HEADS * hd].astype(BF16)
    q_idx = jnp.concatenate([iq[:, h * hd:(h + 1) * hd] for h in range(IDX_HEADS)], axis=0)
    wscale = IDX_HEADS ** -0.5 * hd ** -0.5
    tail = tq_ref[...]
    for h in range(IDX_HEADS):
        wb_ref[h * T_PAD:(h + 1) * T_PAD, :] = jnp.broadcast_to(
            tail[:, IW_OFF + h:IW_OFF + h + 1] * wscale, (T_PAD, LANE))

    def score(raw):
        s = jnp.maximum(raw, 0.0) * wb_ref[...]
        sc = s[0:T_PAD]
        for h in range(1, IDX_HEADS):
            sc = sc + s[h * T_PAD:(h + 1) * T_PAD]
        return sc

    for i in range(pps):
        raw = jnp.dot(q_idx, ik_refs[i][...].astype(BF16), preferred_element_type=F32)
        key = _score_key(score(raw))
        keys_ref[0, i] = key
        all_ref[s_i * pps + i] = key

    n_tiles = n_steps * pps + 1

    @pl.when(s_i == n_steps - 1)
    def _():
        row = lax.broadcasted_iota(I32, (T_PAD, LANE), 0)
        col = lax.broadcasted_iota(I32, (T_PAD, LANE), 1)
        pad = jnp.zeros((LANE - T_PAD, hd), F32)
        ikn = jnp.concatenate([ikn_ref[...][:, :hd], pad], axis=0)
        valid = jnp.logical_and(col <= row, col < t_real)
        keyn = _score_key(jnp.where(valid, score(_nt(q_idx, ikn.astype(BF16))), NEG_INF))
        keyn_ref[0] = keyn
        all_ref[n_tiles - 1] = keyn
        idx = (lax.broadcasted_iota(I32, all_ref.shape, 0) * LANE
               + lax.broadcasted_iota(I32, all_ref.shape, 2))

        def count(pred):
            acc = jnp.sum(pred(all_ref[...]).astype(I32), axis=0)
            return jnp.sum(acc, axis=-1, keepdims=True)

        def count_ge(cand):
            cb = jnp.broadcast_to(cand, (T_PAD, LANE))
            return count(lambda key: key >= cb)

        thr = _kth_largest(count_ge, (T_PAD, 1), n_top)
        thr_b = jnp.broadcast_to(thr, (T_PAD, LANE))
        n_gt = count(lambda key: key > thr_b)
        r = n_top - n_gt

        def count_tie_below(cand):
            cb = jnp.broadcast_to(cand, (T_PAD, LANE))
            return count(lambda key: jnp.logical_and(key == thr_b, idx < cb))

        pos = _tie_bound(count_tie_below, r, (T_PAD, 1), n_bits)
        thr_ref[0] = thr_b
        pos_ref[0] = jnp.broadcast_to(pos, (T_PAD, LANE))


def _dsa_select_decode(P, cache_idx, page_table, layer, *, B, pps, n_top, t_real):
    n_pages = page_table.shape[1]
    n_steps = n_pages // pps
    n_tiles = n_pages + 1
    n_bits = int(math.ceil(math.log2(n_tiles * LANE)))

    def page_spec(i):
        return pl.BlockSpec((None, None) + cache_idx.shape[2:],
                            lambda b, s, pt: (pt[b, s * pps + i], layer, 0, 0))

    tile_spec = pl.BlockSpec((1, T_PAD, LANE), lambda b, s, pt: (b, 0, 0))
    grid_spec = pltpu.PrefetchScalarGridSpec(
        num_scalar_prefetch=1,
        grid=(B, n_steps),
        in_specs=[pl.BlockSpec((T_PAD, D_BLOCK), lambda b, s, pt: (b, C_DK * LANE // D_BLOCK)),
                  pl.BlockSpec((T_PAD, LANE), lambda b, s, pt: (b, C_TAIL)),
                  pl.BlockSpec((T_PAD, LANE), lambda b, s, pt: (b, C_TAIL))]
        + [page_spec(i) for i in range(pps)],
        out_specs=[pl.BlockSpec((1, pps, T_PAD, LANE), lambda b, s, pt: (b, s, 0, 0)),
                   tile_spec, tile_spec, tile_spec],
        scratch_shapes=[pltpu.VMEM((n_tiles, T_PAD, LANE), I32),
                        pltpu.VMEM((IDX_HEADS * T_PAD, LANE), F32)])
    tile_shape = jax.ShapeDtypeStruct((B, T_PAD, LANE), I32)
    return pl.pallas_call(
        functools.partial(_dsa_sel_body, pps=pps, n_steps=n_steps, n_top=n_top, t_real=t_real,
                          n_bits=n_bits),
        grid_spec=grid_spec,
        out_shape=[jax.ShapeDtypeStruct((B, n_pages, T_PAD, LANE), I32),
                   tile_shape, tile_shape, tile_shape],
        compiler_params=_cparams(("parallel", "arbitrary")),
        name="dsa_select_decode")(page_table, P, P, P, *([cache_idx] * pps))


def _dsa_att_body(pt_ref, dq_ref, kn_ref, vn_ref, keys_ref, keyn_ref, thr_ref, pos_ref,
                  bl_ref, bn_ref, *rest, pps, n_steps):
    k_refs = rest[:pps]
    v_refs = rest[pps:2 * pps]
    o_ref, m_ref, l_ref, acc_ref = rest[2 * pps:]
    s_i = pl.program_id(1)
    hd = HEAD_DIM
    n_kv = kn_ref.shape[1] // hd
    n_q = dq_ref.shape[1] // hd
    grp = n_q // n_kv
    rows_g = grp * T_PAD

    @pl.when(s_i == 0)
    def _():
        m_ref[...] = jnp.full(m_ref.shape, -1e30, F32)
        l_ref[...] = jnp.zeros(l_ref.shape, F32)
        acc_ref[...] = jnp.zeros(acc_ref.shape, F32)

    dq = dq_ref[...].astype(BF16)
    q_g = [jnp.concatenate([dq[:, (g * grp + j) * hd:(g * grp + j + 1) * hd] for j in range(grp)],
                           axis=0) for g in range(n_kv)]
    thr = thr_ref[0]
    pos = pos_ref[0]
    col = lax.broadcasted_iota(I32, (T_PAD, LANE), 1)
    is_last = s_i == n_steps - 1

    def sel_mask(key, tile):
        idx = col + tile * LANE
        sel = jnp.logical_or(key > thr, jnp.logical_and(key == thr, idx <= pos))
        sel = jnp.logical_and(sel, key > KEY_NEG_INF)
        return jnp.concatenate([sel] * grp, axis=0)

    def attend(tiles, masks, bias_tiles):
        m_all, l_all, acc_all = m_ref[...], l_ref[...], acc_ref[...]
        new = []
        for g in range(n_kv):
            rs = slice(g * rows_g, (g + 1) * rows_g)
            logit_tiles, pv = tiles(g)
            parts = []
            for s, mk, bt in zip(logit_tiles, masks, bias_tiles):
                if bt is not None:
                    s = s + bt[rs]
                parts.append(jnp.where(mk, s, NEG_INF))
            s = jnp.concatenate(parts, axis=1) if len(parts) > 1 else parts[0]
            m_new = jnp.maximum(m_all[rs], jnp.max(s, axis=-1, keepdims=True))
            alpha = jnp.exp(m_all[rs] - m_new)
            p = jnp.exp(s - m_new)
            new.append((m_new, alpha * l_all[rs] + jnp.sum(p, axis=-1, keepdims=True),
                        alpha * acc_all[rs] + pv(p.astype(BF16))))
        m_ref[...] = jnp.concatenate([x[0] for x in new], axis=0)
        l_ref[...] = jnp.concatenate([x[1] for x in new], axis=0)
        acc_ref[...] = jnp.concatenate([x[2] for x in new], axis=0)

    masks = [sel_mask(keys_ref[0, i], s_i * pps + i) for i in range(pps)]
    last_bias = jnp.where(is_last, bl_ref[...], 0.0)

    def page_tiles(g):
        vt = jnp.concatenate([r[g].astype(BF16) for r in v_refs], axis=1)
        return ([jnp.dot(q_g[g], r[g].astype(BF16), preferred_element_type=F32) for r in k_refs],
                lambda p: _nt(p, vt))

    attend(page_tiles, masks, [None] * (pps - 1) + [last_bias])

    @pl.when(is_last)
    def _():
        pad = jnp.zeros((LANE - T_PAD, hd), BF16)

        def new_tiles(g):
            hs = slice(g * hd, (g + 1) * hd)
            kn = jnp.concatenate([kn_ref[:, hs].astype(BF16), pad], axis=0)
            vn = jnp.concatenate([vn_ref[:, hs].astype(BF16), pad], axis=0)
            return [_nt(q_g[g], kn)], lambda p: jnp.dot(p, vn, preferred_element_type=F32)

        attend(new_tiles, [sel_mask(keyn_ref[0], n_steps * pps)], [bn_ref[...]])
        for g in range(n_kv):
            rs = slice(g * rows_g, (g + 1) * rows_g)
            og = acc_ref[rs] / l_ref[rs]
            for j in range(grp):
                h = g * grp + j
                o_ref[:, h * hd:(h + 1) * hd] = og[j * T_PAD:(j + 1) * T_PAD]


def _dsa_attend_decode(P, cache_k, cache_v, page_table, layer, keys, keyn, thr, pos,
                       bias_last, bias_new, *, B, pps):
    n_pages = page_table.shape[1]
    n_steps = n_pages // pps
    rows = bias_last.shape[0]

    def page_spec(i):
        return pl.BlockSpec((None, None) + cache_k.shape[2:],
                            lambda b, s, pt: (pt[b, s * pps + i], layer, 0, 0, 0))

    tile_spec = pl.BlockSpec((1, T_PAD, LANE), lambda b, s, pt: (b, 0, 0))
    grid_spec = pltpu.PrefetchScalarGridSpec(
        num_scalar_prefetch=1,
        grid=(B, n_steps),
        in_specs=[pl.BlockSpec((T_PAD, 4 * LANE), lambda b, s, pt: (b, C_DQ // 4)),
                  pl.BlockSpec((T_PAD, LANE), lambda b, s, pt: (b, C_DK)),
                  pl.BlockSpec((T_PAD, LANE), lambda b, s, pt: (b, C_DV)),
                  pl.BlockSpec((1, pps, T_PAD, LANE), lambda b, s, pt: (b, s, 0, 0)),
                  tile_spec, tile_spec, tile_spec,
                  pl.BlockSpec(bias_last.shape, lambda b, s, pt: (0, 0)),
                  pl.BlockSpec(bias_new.shape, lambda b, s, pt: (0, 0))]
        + [page_spec(i) for i in range(pps)] + [page_spec(i) for i in range(pps)],
        out_specs=pl.BlockSpec((T_PAD, 4 * LANE), lambda b, s, pt: (b, 0)),
        scratch_shapes=[pltpu.VMEM((rows, 1), F32), pltpu.VMEM((rows, 1), F32),
                        pltpu.VMEM((rows, HEAD_DIM), F32)])
    return pl.pallas_call(
        functools.partial(_dsa_att_body, pps=pps, n_steps=n_steps),
        grid_spec=grid_spec,
        out_shape=jax.ShapeDtypeStruct((B * T_PAD, 4 * LANE), F32),
        compiler_params=_cparams(("parallel", "arbitrary")),
        name="dsa_attend_decode")(page_table, P, P, P, keys, keyn, thr, pos, bias_last, bias_new,
                                  *([cache_k] * pps), *([cache_v] * pps))


def _bias_minus_far(tab, dist):
    onehot = jax.nn.one_hot(_rel_bucket(dist), N_BUCKETS, dtype=F32)
    bias = jnp.einsum("...k,km->...m", onehot, tab, precision=lax.Precision.HIGHEST)
    return bias - tab[N_BUCKETS - 1]


def _toeplitz_tiles(tab, t):
    r = jnp.arange(t)[:, None]
    c = jnp.arange(t)[None, :]
    tiles = jnp.stack([_bias_minus_far(tab, r - c), _bias_minus_far(tab, t + r - c)], axis=0)
    return jnp.transpose(tiles, (3, 0, 1, 2))


def _decode_bias(tab, past, t_real):
    tq = jnp.arange(T_PAD)[:, None]
    kc = jnp.arange(LANE)[None, :]
    last = _bias_minus_far(tab, past + tq - (past - PAGE + kc))
    new = _bias_minus_far(tab, tq - kc)
    valid = jnp.logical_and(kc <= tq, kc < t_real)
    new = jnp.where(valid[..., None], new, NEG_INF)
    return jnp.transpose(last, (2, 0, 1)), jnp.transpose(new, (2, 0, 1))


def _block_diag(w):
    n, d, e = w.shape
    eye = jnp.eye(n, dtype=w.dtype)
    return (eye[:, None, :, None] * w[:, :, None, :]).reshape(n * d, n * e)


def kernel(x_prompt, x_sample, cache_a_k, cache_a_v, cache_d_k, cache_d_v, cache_d_idx, state_b_h, state_b_conv, state_c_s, page_table, rel_bias, ln1, w_in, a_q_norm, a_k_norm, a_lam_q1, a_lam_k1, a_lam_q2, a_lam_k2, a_out_norm, b_conv_w, b_conv_b, b_wa, b_ba, b_wx, b_bx, b_lambda, c_lb_logits, c_out_norm, d_q_norm, d_k_norm, w_out, ln2, w_up, w_down):
    B, T, D = x_prompt.shape
    Bs, Ts, _ = x_sample.shape
    L = w_in.shape[0]
    n_pages = page_table.shape[1]
    past = n_pages * PAGE
    gw = D // 4
    a_heads = gw // (2 * HEAD_DIM)
    c_heads = c_lb_logits.shape[1] // LANE
    d_heads = gw // HEAD_DIM
    d_kv = cache_d_k.shape[3]
    d_grp = d_heads // d_kv
    n_pool = cache_a_k.shape[0]
    assert Ts <= T_PAD - 0 and Ts >= CONV_W - 1 and past > 0

    tab = rel_bias.astype(F32)
    n_a_maps = 2 * a_heads
    tab_a, tab_d = tab[:, :n_a_maps], tab[:, n_a_maps:]
    tq_a = min(128, T)
    tq_d = min(128, T)
    bias_a = jnp.transpose(_toeplitz_tiles(tab_a, tq_a).reshape(a_heads, 2, 2, tq_a, tq_a),
                           (0, 2, 3, 1, 4)).reshape(a_heads, 2, tq_a, 2 * tq_a)
    causal = jnp.where(jnp.arange(tq_a)[None, :] <= jnp.arange(tq_a)[:, None], 0.0, NEG_INF)
    causal = jnp.tile(causal, (1, 2)).astype(F32)
    bias_a = jnp.stack([jnp.full((a_heads, tq_a, 2 * tq_a), NEG_INF, F32), bias_a[:, 0] + causal,
                        bias_a[:, 1], jnp.zeros((a_heads, tq_a, 2 * tq_a), F32)], axis=1)
    bias_d = jnp.concatenate([_toeplitz_tiles(tab_d, tq_d),
                              jnp.zeros((d_heads, 1, tq_d, tq_d), F32)], axis=1)
    bias_d = jnp.transpose(bias_d.reshape(d_heads // 2, 2, 3, tq_d, tq_d),
                           (0, 2, 3, 1, 4)).reshape(d_heads // 2, 3, tq_d, 2 * tq_d)
    bl_a, bn_a = _decode_bias(tab_a, past, Ts)
    bl_a = jnp.repeat(bl_a.reshape(n_a_maps * T_PAD, LANE), a_heads, axis=1)
    bn_a = bn_a.reshape(n_a_maps * T_PAD, LANE)
    bl_d, bn_d = _decode_bias(tab_d, past, Ts)
    bl_d = bl_d.reshape(d_heads * T_PAD, LANE)
    bn_d = bn_d.reshape(d_heads * T_PAD, LANE)

    lb_cum = jnp.cumsum(jax.nn.softmax(c_lb_logits.astype(F32), axis=0), axis=0)
    lb_all = (lb_cum - lb_cum[0]).reshape(L, c_heads, 1, LANE)

    seg = jnp.kron(jnp.eye(LANE // HEAD_DIM, dtype=F32),
                   jnp.full((HEAD_DIM, HEAD_DIM), 1.0 / HEAD_DIM, F32)).astype(BF16)
    qscale = HEAD_DIM ** -0.5
    zeros = lambda n: jnp.zeros((n,), F32)
    ones = lambda n: jnp.ones((n,), F32)
    flag = jnp.concatenate([ones(2 * gw), zeros(7 * gw), ones(gw), ones(LANE),
                            zeros(P_WIDTH - 10 * gw - LANE)]).reshape(1, P_WIDTH)
    w_in_t = jnp.transpose(w_in, (0, 2, 1))
    n_in = w_in_t.shape[1]
    tail0 = (P_WIDTH // IN_PROJ_TN - 1) * IN_PROJ_TN
    assert tail0 < n_in <= P_WIDTH
    w_in_tail = jnp.pad(w_in_t[:, tail0:], ((0, 0), (0, tail0 + IN_PROJ_TN - n_in), (0, 0)))

    ca_k = cache_a_k.reshape(n_pool, L, PAGE * a_heads, 2 * HEAD_DIM)
    ca_v = cache_a_v.reshape(n_pool, L, PAGE * a_heads, 2 * HEAD_DIM)
    cd_k = jnp.transpose(cache_d_k, (0, 1, 3, 4, 2))
    cd_v = jnp.transpose(cache_d_v, (0, 1, 3, 4, 2))
    cd_idx = jnp.transpose(cache_d_idx, (0, 1, 3, 2))

    xp = x_prompt.reshape(B * T, D)
    xs = jnp.pad(x_sample, ((0, 0), (0, T_PAD - Ts), (0, 0))).reshape(Bs * T_PAD, D)
    s_conv = jnp.pad(state_b_conv, ((0, 0), (0, 0), (SUBLANE - (CONV_W - 1), 0), (0, 0)))

    n_top_p = min(TOPK_MAX, T // 4)
    n_top_s = min(TOPK_MAX, (past + Ts) // 4)
    tm_p = min(1024, B * T)
    tm_f = min(1024, B * T)
    tt_b = min(256, T)
    tt_c = min(512, T)
    ch_c = math.gcd(T, 64)
    pps_a = math.gcd(n_pages, 32)
    pps_d = math.gcd(n_pages, 32)
    pps_att = pps_d

    outs_p, outs_s = [], []
    for l in range(L):
        gain = jnp.concatenate(
            [jnp.tile(a_q_norm[l], 2 * a_heads) * qscale, jnp.tile(a_k_norm[l], 2 * a_heads),
             ones(7 * gw), jnp.tile(d_q_norm[l], d_heads) * qscale,
             jnp.tile(d_k_norm[l], d_kv), ones(P_WIDTH - 10 * gw - LANE)]).reshape(1, P_WIDTH)
        g1 = ln1[l].reshape(1, D)
        g2 = ln2[l].reshape(1, D)
        lam_init = 0.8 - 0.6 * math.exp(-0.3 * l)
        lam = (jnp.exp(jnp.sum(a_lam_q1[l].astype(F32) * a_lam_k1[l].astype(F32)))
               - jnp.exp(jnp.sum(a_lam_q2[l].astype(F32) * a_lam_k2[l].astype(F32))) + lam_init)
        scal = jnp.stack([lam, jnp.asarray(1.0 - lam_init, F32)]).astype(F32)
        g_a = a_out_norm[l].reshape(1, LANE)
        g_c = c_out_norm[l].reshape(1, LANE)
        b_weights = (b_conv_w[l], b_conv_b[l].reshape(1, gw), _block_diag(b_wa[l]).astype(BF16),
                     b_ba[l].reshape(1, gw), _block_diag(b_wx[l]).astype(BF16),
                     b_bx[l].reshape(1, gw), b_lambda[l].reshape(1, gw))

        def dense_tail(x, mixes, tm, tmf):
            x1 = _out_proj(mixes, w_out, l, x, tm=tm, tn=min(512, D))
            return _ffn(x1, g2, w_up, w_down, l, tm=tmf, tf=min(512, w_up.shape[2]))

        Pp, ak_p, av_p = _in_proj(xp, g1, w_in_t, w_in_tail, l, gain, flag, seg, tm=tm_p)
        mix_a = _attn_a_prompt(Pp, scal, bias_a, g_a, B=B, T=T, tq=tq_a)
        mix_b, hfin, buf = _rglru(Pp, None, b_weights, B=B, T=T, tt=tt_b, t_real=T, out_dtype=BF16)
        mix_c, sfin = _hgrn(Pp, None, lb_all[l], g_c, B=B, T=T, tt=tt_c, ch=ch_c,
                            sub=min(16, ch_c), t_real=T, out_dtype=BF16)
        mix_d = _dsa_prompt(Pp, bias_d, B=B, T=T, tq=tq_d, n_top=n_top_p)
        xp = dense_tail(xp, (mix_a, mix_b, mix_c, mix_d), tm_p, tm_f)
        P3 = Pp.reshape(B, T, P_WIDTH)
        outs_p.append((ak_p.reshape(B, T, a_heads, 2 * HEAD_DIM),
                       av_p.reshape(B, T, a_heads, 2 * HEAD_DIM),
                       P3[..., C_DK * LANE:C_DV * LANE].reshape(B, T, d_kv, HEAD_DIM),
                       P3[..., C_DV * LANE:(C_DV + 1) * LANE].reshape(B, T, d_kv, HEAD_DIM),
                       P3[..., C_TAIL * LANE:C_TAIL * LANE + HEAD_DIM],
                       hfin.reshape(B, gw), buf, sfin))

        Ps, ak_s, av_s = _in_proj(xs, g1, w_in_t, w_in_tail, l, gain, flag, seg, tm=Bs * T_PAD)
        smix_a = _attn_a_decode(Ps, ca_k, ca_v, page_table, l, scal, bl_a, bn_a, g_a, B=Bs, pps=pps_a)
        smix_b, shfin, sbuf = _rglru(
            Ps, (state_b_h[:, l].reshape(Bs, 1, gw), s_conv[:, l]), b_weights,
            B=Bs, T=T_PAD, tt=T_PAD, t_real=Ts, out_dtype=F32)
        smix_c, ssfin = _hgrn(Ps, state_c_s[:, l], lb_all[l], g_c, B=Bs, T=T_PAD, tt=T_PAD,
                              ch=T_PAD, sub=T_PAD, t_real=Ts, out_dtype=F32)
        keys, keyn, thr, pos = _dsa_select_decode(Ps, cd_idx, page_table, l, B=Bs, pps=pps_d,
                                                  n_top=n_top_s, t_real=Ts)
        smix_d = _dsa_attend_decode(Ps, cd_k, cd_v, page_table, l, keys, keyn, thr, pos,
                                    bl_d, bn_d, B=Bs, pps=pps_att)
        xs = dense_tail(xs, (smix_a, smix_b, smix_c, smix_d), Bs * T_PAD, Bs * T_PAD)
        S3 = Ps.reshape(Bs, T_PAD, P_WIDTH)[:, :Ts]
        outs_s.append((ak_s.reshape(Bs, T_PAD, a_heads, 2 * HEAD_DIM)[:, :Ts],
                       av_s.reshape(Bs, T_PAD, a_heads, 2 * HEAD_DIM)[:, :Ts],
                       S3[..., C_DK * LANE:C_DV * LANE].reshape(Bs, Ts, d_kv, HEAD_DIM),
                       S3[..., C_DV * LANE:(C_DV + 1) * LANE].reshape(Bs, Ts, d_kv, HEAD_DIM),
                       S3[..., C_TAIL * LANE:C_TAIL * LANE + HEAD_DIM],
                       shfin.reshape(Bs, gw), sbuf, ssfin))

    y_prompt = xp.reshape(B, T, D)
    y_sample = xs.reshape(Bs, T_PAD, D)[:, :Ts]
    stack = lambda outs: [jnp.stack(s, axis=1) for s in zip(*outs)]
    return (y_prompt, y_sample, *stack(outs_p), *stack(outs_s))
```
